```python
import jax, jax.numpy as jnp
from jax import lax
import numpy as np

D_MODEL = 2048
BATCH = 8
SEQ = 8192
DEPTH = 1

SSD_HEADS = 32
SSD_HEAD_DIM = 64
SSD_WIDTH = SSD_HEADS * SSD_HEAD_DIM
SSD_GROUPS = 4
SSD_STATE = 128
SSD_CONV = 4
SSD_CHUNK = 128
SSD_CONV_WIDTH = SSD_WIDTH + 2 * SSD_GROUPS * SSD_STATE
MLSTM_HEADS = 8
MLSTM_QK_DIM = 128
MLSTM_V_DIM = 256
MLSTM_WIDTH = MLSTM_HEADS * MLSTM_V_DIM
MLSTM_CHUNK = 64
GATE_SOFTCAP = 15.0
MIX_WIDTH = SSD_WIDTH + MLSTM_WIDTH
D_FF = -(-8 * D_MODEL // (3 * 256)) * 256
EPS = 1e-6
IN_SIZES = (SSD_WIDTH, SSD_CONV_WIDTH, SSD_HEADS,
            MLSTM_HEADS * MLSTM_QK_DIM, MLSTM_HEADS * MLSTM_QK_DIM,
            MLSTM_WIDTH, MLSTM_WIDTH, MLSTM_HEADS, MLSTM_HEADS)
IN_WIDTH = sum(IN_SIZES)
SPLIT_POINTS = tuple(int(s) for s in np.cumsum(IN_SIZES)[:-1])

kernel_name = "hymba_ssd_mlstm_swiglu_layer"


def rmsnorm(x, g):
    xf = x.astype(jnp.float32)
    y = xf * lax.rsqrt(jnp.mean(xf * xf, axis=-1, keepdims=True) + EPS)
    return (y * g.astype(jnp.float32)).astype(x.dtype)


def softcap(x, cap):
    return cap * jnp.tanh(x / cap)


def ssd_mixer(z, xbc, dt_raw, conv_w, conv_b, dt_bias, a_log, d_skip, norm_g):
    b, L, _ = xbc.shape
    G, R, P, N, CH = SSD_GROUPS, SSD_HEADS // SSD_GROUPS, SSD_HEAD_DIM, SSD_STATE, SSD_CHUNK
    nc = L // CH
    f32 = jnp.float32
    xbc = xbc.astype(f32)
    pad = jnp.pad(xbc, ((0, 0), (SSD_CONV - 1, 0), (0, 0)))
    conv = conv_b.astype(f32)
    for tap in range(SSD_CONV):
        conv = conv + pad[:, tap:tap + L] * conv_w[tap].astype(f32)
    xbc = jax.nn.silu(conv)
    xs, Bm, Cm = jnp.split(xbc, [SSD_WIDTH, SSD_WIDTH + G * N], axis=-1)
    xs = xs.reshape(b, nc, CH, G, R, P)
    Bm = Bm.reshape(b, nc, CH, G, N)
    Cm = Cm.reshape(b, nc, CH, G, N)
    dt = jax.nn.softplus(dt_raw.astype(f32) + dt_bias.astype(f32))
    A = -jnp.exp(a_log.astype(f32))
    dt_c = dt.reshape(b, nc, CH, G, R)
    a_dt = (dt_c * A.reshape(G, R)).transpose(0, 3, 4, 1, 2)
    a_cum = jnp.cumsum(a_dt, axis=-1)
    X = xs * dt_c[..., None]
    seg = a_cum[..., :, None] - a_cum[..., None, :]
    causal = jnp.tril(jnp.ones((CH, CH), dtype=bool))
    Lmat = jnp.exp(jnp.where(causal, seg, -jnp.inf))
    cb = jnp.einsum("bclgn,bcsgn->bcgls", Cm, Bm)
    y_diag = jnp.einsum("bcgls,bgrcls,bcsgrp->bclgrp", cb, Lmat, X)
    decay_states = jnp.exp(a_cum[..., -1:] - a_cum)
    states = jnp.einsum("bclgn,bgrcl,bclgrp->cbgrpn", Bm, decay_states, X)
    chunk_decay = jnp.exp(a_cum[..., -1]).transpose(3, 0, 1, 2)

    def step(carry, inp):
        st, dec = inp
        return carry * dec[..., None, None] + st, carry

    _, prev_states = lax.scan(step, jnp.zeros((b, G, R, P, N), f32), (states, chunk_decay))
    y_off = jnp.einsum("bclgn,cbgrpn,bgrcl->bclgrp", Cm, prev_states, jnp.exp(a_cum))
    y = y_diag + y_off + d_skip.astype(f32).reshape(G, R)[:, :, None] * xs
    y = y.reshape(b, L, SSD_WIDTH) * jax.nn.silu(z.astype(f32))
    yg = y.reshape(b, L, G, SSD_WIDTH // G)
    yg = yg * lax.rsqrt(jnp.mean(yg * yg, axis=-1, keepdims=True) + EPS)
    y = yg.reshape(b, L, SSD_WIDTH) * norm_g.astype(f32)
    return y.astype(z.dtype)


def mlstm_mixer(q, k, v, o_raw, i_raw, f_raw, i_bias, f_bias, norm_g):
    b, L, _ = q.shape
    H, DK, DV, CH = MLSTM_HEADS, MLSTM_QK_DIM, MLSTM_V_DIM, MLSTM_CHUNK
    nc = L // CH
    f32 = jnp.float32
    q = q.astype(f32).reshape(b, nc, CH, H, DK)
    k = k.astype(f32).reshape(b, nc, CH, H, DK) * (DK ** -0.5)
    v = v.astype(f32).reshape(b, nc, CH, H, DV)
    log_i = softcap(i_raw.astype(f32) + i_bias.astype(f32), GATE_SOFTCAP)
    log_f = jax.nn.log_sigmoid(softcap(f_raw.astype(f32) + f_bias.astype(f32), GATE_SOFTCAP))
    log_i = log_i.reshape(b, nc, CH, H).transpose(0, 3, 1, 2)
    log_f = log_f.reshape(b, nc, CH, H).transpose(0, 3, 1, 2)
    bcum = jnp.cumsum(log_f, axis=-1)
    b_last = bcum[..., -1]
    a = b_last[..., None] - bcum + log_i
    m_loc = jnp.max(a, axis=-1)
    w = jnp.exp(a - m_loc[..., None])
    C_loc = jnp.einsum("bhcs,bcshv,bcshk->cbhvk", w, v, k)
    n_loc = jnp.einsum("bhcs,bcshk->cbhk", w, k)

    def step(carry, inp):
        C, n, m = carry
        Cl, nl, bl, ml = inp
        m_new = jnp.maximum(bl + m, ml)
        s_old = jnp.exp(bl + m - m_new)
        s_new = jnp.exp(ml - m_new)
        C_new = s_old[..., None, None] * C + s_new[..., None, None] * Cl
        n_new = s_old[..., None] * n + s_new[..., None] * nl
        return (C_new, n_new, m_new), (C, n, m)

    init = (jnp.zeros((b, H, DV, DK), f32), jnp.zeros((b, H, DK), f32), jnp.zeros((b, H), f32))
    _, (prev_C, prev_n, prev_m) = lax.scan(
        step, init, (C_loc, n_loc, b_last.transpose(2, 0, 1), m_loc.transpose(2, 0, 1)))
    D = bcum[..., :, None] - bcum[..., None, :] + log_i[..., None, :]
    causal = jnp.tril(jnp.ones((CH, CH), dtype=bool))
    D = jnp.where(causal, D, -jnp.inf)
    m_intra = jnp.max(D, axis=-1)
    inter_log = bcum + prev_m.transpose(1, 2, 0)[..., None]
    m_t = jnp.maximum(inter_log, m_intra)
    S = jnp.einsum("bclhk,bcshk->bhcls", q, k) * jnp.exp(D - m_t[..., None])
    w_inter = jnp.exp(inter_log - m_t)
    w_inter_t = w_inter.transpose(0, 2, 3, 1)[..., None]
    num = (jnp.einsum("bhcls,bcshv->bclhv", S, v)
           + w_inter_t * jnp.einsum("bclhk,cbhvk->bclhv", q, prev_C))
    nq = jnp.sum(S, axis=-1) + w_inter * jnp.einsum("bclhk,cbhk->bhcl", q, prev_n)
    den = jnp.maximum(jnp.abs(nq), jnp.exp(-m_t))
    h = num / den.transpose(0, 2, 3, 1)[..., None]
    h = h * lax.rsqrt(jnp.mean(h * h, axis=-1, keepdims=True) + EPS)
    h = h.reshape(b, L, MLSTM_WIDTH) * norm_g.astype(f32)
    h = h * jax.nn.sigmoid(o_raw.astype(f32))
    return h.astype(o_raw.dtype)


def _fwd_setup_inputs(seed: int = 0) -> dict:
    key = jax.random.key(seed)
    ks = jax.random.split(key, 20)
    f32 = jnp.float32
    nrm = lambda k, shape, scale: jax.random.normal(k, shape, f32) * scale
    x = jax.random.normal(ks[0], (BATCH, SEQ, D_MODEL), f32)
    norm_mix_g = 1.0 + nrm(ks[1], (DEPTH, D_MODEL), 0.02)
    w_in = nrm(ks[2], (DEPTH, D_MODEL, IN_WIDTH), D_MODEL ** -0.5)
    conv_w = nrm(ks[3], (DEPTH, SSD_CONV, SSD_CONV_WIDTH), SSD_CONV ** -0.5)
    conv_b = nrm(ks[4], (DEPTH, SSD_CONV_WIDTH), 0.02)
    dt0 = jnp.exp(jax.random.uniform(ks[5], (DEPTH, SSD_HEADS), f32,
                                     jnp.log(0.001), jnp.log(0.1)))
    dt_bias = dt0 + jnp.log(-jnp.expm1(-dt0))
    a_log = jnp.log(jax.random.uniform(ks[6], (DEPTH, SSD_HEADS), f32, 1.0, 16.0))
    d_skip = 1.0 + nrm(ks[7], (DEPTH, SSD_HEADS), 0.02)
    ssd_norm_g = 1.0 + nrm(ks[8], (DEPTH, SSD_WIDTH), 0.02)
    i_bias = nrm(ks[9], (DEPTH, MLSTM_HEADS), 0.1)
    f_bias = jnp.linspace(3.0, 6.0, MLSTM_HEADS, dtype=f32)[None] + nrm(ks[10], (DEPTH, MLSTM_HEADS), 0.1)
    mlstm_norm_g = 1.0 + nrm(ks[11], (DEPTH, MLSTM_WIDTH), 0.02)
    w_out = nrm(ks[12], (DEPTH, MIX_WIDTH, D_MODEL), MIX_WIDTH ** -0.5)
    norm_ffn_g = 1.0 + nrm(ks[13], (DEPTH, D_MODEL), 0.02)
    w_gate = nrm(ks[14], (DEPTH, D_MODEL, D_FF), D_MODEL ** -0.5)
    w_up = nrm(ks[15], (DEPTH, D_MODEL, D_FF), D_MODEL ** -0.5)
    w_down = nrm(ks[16], (DEPTH, D_FF, D_MODEL), D_FF ** -0.5)
    final_norm_g = 1.0 + nrm(ks[17], (D_MODEL,), 0.02)
    return {"x": x, "norm_mix_g": norm_mix_g, "w_in": w_in, "conv_w": conv_w,
            "conv_b": conv_b, "dt_bias": dt_bias, "a_log": a_log, "d_skip": d_skip,
            "ssd_norm_g": ssd_norm_g, "i_bias": i_bias, "f_bias": f_bias,
            "mlstm_norm_g": mlstm_norm_g, "w_out": w_out, "norm_ffn_g": norm_ffn_g,
            "w_gate": w_gate, "w_up": w_up, "w_down": w_down, "final_norm_g": final_norm_g}


def _fwd_reference(x, norm_mix_g, w_in, conv_w, conv_b, dt_bias, a_log, d_skip, ssd_norm_g,
              i_bias, f_bias, mlstm_norm_g, w_out, norm_ffn_g, w_gate, w_up, w_down,
              final_norm_g):
    h = x
    for l in range(DEPTH):
        u = rmsnorm(h, norm_mix_g[l])
        proj = jnp.einsum("bsd,de->bse", u, w_in[l])
        z, xbc, dt_raw, q, k, v, o_raw, i_raw, f_raw = jnp.split(proj, SPLIT_POINTS, axis=-1)
        y_ssd = ssd_mixer(z, xbc, dt_raw, conv_w[l], conv_b[l], dt_bias[l], a_log[l],
                          d_skip[l], ssd_norm_g[l])
        y_ml = mlstm_mixer(q, k, v, o_raw, i_raw, f_raw, i_bias[l], f_bias[l], mlstm_norm_g[l])
        y_mix = jnp.concatenate([y_ssd, y_ml], axis=-1)
        h = h + jnp.einsum("bse,ed->bsd", y_mix, w_out[l])
        u = rmsnorm(h, norm_ffn_g[l])
        gate = jnp.einsum("bsd,df->bsf", u, w_gate[l])
        up = jnp.einsum("bsd,df->bsf", u, w_up[l])
        h = h + jnp.einsum("bsf,fd->bsd", jax.nn.silu(gate) * up, w_down[l])
    return rmsnorm(h, final_norm_g)


import jax as _jax
import jax.numpy as _jnp

TWIN_FORMAT = 'train_step'
FWD_PARAMS = ['x', 'norm_mix_g', 'w_in', 'conv_w', 'conv_b', 'dt_bias', 'a_log', 'd_skip', 'ssd_norm_g', 'i_bias', 'f_bias', 'mlstm_norm_g', 'w_out', 'norm_ffn_g', 'w_gate', 'w_up', 'w_down', 'final_norm_g']
TWIN_WEIGHTS = ['norm_mix_g', 'w_in', 'conv_w', 'conv_b', 'dt_bias', 'a_log', 'd_skip', 'ssd_norm_g', 'i_bias', 'f_bias', 'mlstm_norm_g', 'w_out', 'norm_ffn_g', 'w_gate', 'w_up', 'w_down', 'final_norm_g']
TWIN_DIFF_INPUT = 'x'
TWIN_INPUTS = ['x', 'norm_mix_g', 'w_in', 'conv_w', 'conv_b', 'dt_bias', 'a_log', 'd_skip', 'ssd_norm_g', 'i_bias', 'f_bias', 'mlstm_norm_g', 'w_out', 'norm_ffn_g', 'w_gate', 'w_up', 'w_down', 'final_norm_g', 'loss_target', 'm_norm_mix_g', 'm_w_in', 'm_conv_w', 'm_conv_b', 'm_dt_bias', 'm_a_log', 'm_d_skip', 'm_ssd_norm_g', 'm_i_bias', 'm_f_bias', 'm_mlstm_norm_g', 'm_w_out', 'm_norm_ffn_g', 'm_w_gate', 'm_w_up', 'm_w_down', 'm_final_norm_g', 'v_norm_mix_g', 'v_w_in', 'v_conv_w', 'v_conv_b', 'v_dt_bias', 'v_a_log', 'v_d_skip', 'v_ssd_norm_g', 'v_i_bias', 'v_f_bias', 'v_mlstm_norm_g', 'v_w_out', 'v_norm_ffn_g', 'v_w_gate', 'v_w_up', 'v_w_down', 'v_final_norm_g']
TWIN_OUTPUTS = ['loss', 'grad_x', 'grad_norm_mix_g', 'grad_w_in', 'grad_conv_w', 'grad_conv_b', 'grad_dt_bias', 'grad_a_log', 'grad_d_skip', 'grad_ssd_norm_g', 'grad_i_bias', 'grad_f_bias', 'grad_mlstm_norm_g', 'grad_w_out', 'grad_norm_ffn_g', 'grad_w_gate', 'grad_w_up', 'grad_w_down', 'grad_final_norm_g', 'delta_norm_mix_g', 'delta_w_in', 'delta_conv_w', 'delta_conv_b', 'delta_dt_bias', 'delta_a_log', 'delta_d_skip', 'delta_ssd_norm_g', 'delta_i_bias', 'delta_f_bias', 'delta_mlstm_norm_g', 'delta_w_out', 'delta_norm_ffn_g', 'delta_w_gate', 'delta_w_up', 'delta_w_down', 'delta_final_norm_g', 'new_m_norm_mix_g', 'new_m_w_in', 'new_m_conv_w', 'new_m_conv_b', 'new_m_dt_bias', 'new_m_a_log', 'new_m_d_skip', 'new_m_ssd_norm_g', 'new_m_i_bias', 'new_m_f_bias', 'new_m_mlstm_norm_g', 'new_m_w_out', 'new_m_norm_ffn_g', 'new_m_w_gate', 'new_m_w_up', 'new_m_w_down', 'new_m_final_norm_g', 'new_v_norm_mix_g', 'new_v_w_in', 'new_v_conv_w', 'new_v_conv_b', 'new_v_dt_bias', 'new_v_a_log', 'new_v_d_skip', 'new_v_ssd_norm_g', 'new_v_i_bias', 'new_v_f_bias', 'new_v_mlstm_norm_g', 'new_v_w_out', 'new_v_norm_ffn_g', 'new_v_w_gate', 'new_v_w_up', 'new_v_w_down', 'new_v_final_norm_g']
TWIN_LEAF_KINDS = {'loss': 'loss', 'grad_x': 'grad_x', 'grad_norm_mix_g': 'grad_w', 'grad_w_in': 'grad_w', 'grad_conv_w': 'grad_w', 'grad_conv_b': 'grad_w', 'grad_dt_bias': 'grad_w', 'grad_a_log': 'grad_w', 'grad_d_skip': 'grad_w', 'grad_ssd_norm_g': 'grad_w', 'grad_i_bias': 'grad_w', 'grad_f_bias': 'grad_w', 'grad_mlstm_norm_g': 'grad_w', 'grad_w_out': 'grad_w', 'grad_norm_ffn_g': 'grad_w', 'grad_w_gate': 'grad_w', 'grad_w_up': 'grad_w', 'grad_w_down': 'grad_w', 'grad_final_norm_g': 'grad_w', 'delta_norm_mix_g': 'delta_w', 'delta_w_in': 'delta_w', 'delta_conv_w': 'delta_w', 'delta_conv_b': 'delta_w', 'delta_dt_bias': 'delta_w', 'delta_a_log': 'delta_w', 'delta_d_skip': 'delta_w', 'delta_ssd_norm_g': 'delta_w', 'delta_i_bias': 'delta_w', 'delta_f_bias': 'delta_w', 'delta_mlstm_norm_g': 'delta_w', 'delta_w_out': 'delta_w', 'delta_norm_ffn_g': 'delta_w', 'delta_w_gate': 'delta_w', 'delta_w_up': 'delta_w', 'delta_w_down': 'delta_w', 'delta_final_norm_g': 'delta_w', 'new_m_norm_mix_g': 'new_m', 'new_m_w_in': 'new_m', 'new_m_conv_w': 'new_m', 'new_m_conv_b': 'new_m', 'new_m_dt_bias': 'new_m', 'new_m_a_log': 'new_m', 'new_m_d_skip': 'new_m', 'new_m_ssd_norm_g': 'new_m', 'new_m_i_bias': 'new_m', 'new_m_f_bias': 'new_m', 'new_m_mlstm_norm_g': 'new_m', 'new_m_w_out': 'new_m', 'new_m_norm_ffn_g': 'new_m', 'new_m_w_gate': 'new_m', 'new_m_w_up': 'new_m', 'new_m_w_down': 'new_m', 'new_m_final_norm_g': 'new_m', 'new_v_norm_mix_g': 'new_v', 'new_v_w_in': 'new_v', 'new_v_conv_w': 'new_v', 'new_v_conv_b': 'new_v', 'new_v_dt_bias': 'new_v', 'new_v_a_log': 'new_v', 'new_v_d_skip': 'new_v', 'new_v_ssd_norm_g': 'new_v', 'new_v_i_bias': 'new_v', 'new_v_f_bias': 'new_v', 'new_v_mlstm_norm_g': 'new_v', 'new_v_w_out': 'new_v', 'new_v_norm_ffn_g': 'new_v', 'new_v_w_gate': 'new_v', 'new_v_w_up': 'new_v', 'new_v_w_down': 'new_v', 'new_v_final_norm_g': 'new_v'}


def _forward(args):
    return _fwd_reference(*[args[k] for k in FWD_PARAMS])


def _output_shape():
    def fwd():
        inp = _fwd_setup_inputs(0)
        return _fwd_reference(*[inp[k] for k in FWD_PARAMS])
    out = _jax.eval_shape(fwd)
    return out.shape, out.dtype

N_MICROBATCH = 1
ADAM_LR = 0.001
ADAM_B1 = 0.9
ADAM_B2 = 0.999
ADAM_EPS = 1e-08
ADAM_WD = 0.01
ADAM_STEP = 10
PER_EXAMPLE_BATCH_AXIS = {'x': 0, 'loss_target': 0}
SHARED_INPUTS = []
_WEIGHT_DTYPES = {'norm_mix_g': _jnp.float32, 'w_in': _jnp.float32, 'conv_w': _jnp.float32, 'conv_b': _jnp.float32, 'dt_bias': _jnp.float32, 'a_log': _jnp.float32, 'd_skip': _jnp.float32, 'ssd_norm_g': _jnp.float32, 'i_bias': _jnp.float32, 'f_bias': _jnp.float32, 'mlstm_norm_g': _jnp.float32, 'w_out': _jnp.float32, 'norm_ffn_g': _jnp.float32, 'w_gate': _jnp.float32, 'w_up': _jnp.float32, 'w_down': _jnp.float32, 'final_norm_g': _jnp.float32}
MOMENT_SCALE = {'norm_mix_g': 1.409244e-01, 'w_in': 6.077068e-02, 'conv_w': 6.511063e-02, 'conv_b': 9.341805e-02, 'dt_bias': 4.089991e-01, 'a_log': 3.182670e-01, 'd_skip': 4.735322e-01, 'ssd_norm_g': 7.504997e-02, 'i_bias': 7.703361e-03, 'f_bias': 2.015588e-01, 'mlstm_norm_g': 4.129994e-02, 'w_out': 8.606876e-02, 'norm_ffn_g': 8.134408e-02, 'w_gate': 3.293877e-02, 'w_up': 3.184346e-02, 'w_down': 5.282012e-02, 'final_norm_g': 3.197223e+01}


def _to_microbatches(a, axis):
    t = _jnp.moveaxis(a, axis, 0)
    t = t.reshape((N_MICROBATCH, t.shape[0] // N_MICROBATCH) + t.shape[1:])
    return _jnp.moveaxis(t, 1, axis + 1)


def setup_inputs(seed: int = 0) -> dict:
    inp = _fwd_setup_inputs(seed)
    key = _jax.random.fold_in(_jax.random.key(seed), 7919)
    shape, _ = _output_shape()
    out = dict(inp)
    out["loss_target"] = _jax.random.normal(_jax.random.fold_in(key, 0), shape, _jnp.float32)
    for i, name in enumerate(TWIN_WEIGHTS):
        w = inp[name].astype(_jnp.float32)
        if MOMENT_SCALE is None:
            s = _jnp.sqrt(_jnp.mean(_jnp.square(w)) + 1e-30)
        else:
            s = MOMENT_SCALE[name]
        km, kv = _jax.random.split(_jax.random.fold_in(key, i + 1))
        out[name] = w
        out["m_" + name] = s * _jax.random.normal(km, w.shape, _jnp.float32)
        out["v_" + name] = (s * s) * _jax.random.uniform(kv, w.shape, _jnp.float32, 0.5, 1.5)
    if N_MICROBATCH > 1:
        for name, axis in PER_EXAMPLE_BATCH_AXIS.items():
            out[name] = _to_microbatches(out[name], axis)
    return {'x': out['x'], 'norm_mix_g': out['norm_mix_g'], 'w_in': out['w_in'], 'conv_w': out['conv_w'], 'conv_b': out['conv_b'], 'dt_bias': out['dt_bias'], 'a_log': out['a_log'], 'd_skip': out['d_skip'], 'ssd_norm_g': out['ssd_norm_g'], 'i_bias': out['i_bias'], 'f_bias': out['f_bias'], 'mlstm_norm_g': out['mlstm_norm_g'], 'w_out': out['w_out'], 'norm_ffn_g': out['norm_ffn_g'], 'w_gate': out['w_gate'], 'w_up': out['w_up'], 'w_down': out['w_down'], 'final_norm_g': out['final_norm_g'], 'loss_target': out['loss_target'], 'm_norm_mix_g': out['m_norm_mix_g'], 'm_w_in': out['m_w_in'], 'm_conv_w': out['m_conv_w'], 'm_conv_b': out['m_conv_b'], 'm_dt_bias': out['m_dt_bias'], 'm_a_log': out['m_a_log'], 'm_d_skip': out['m_d_skip'], 'm_ssd_norm_g': out['m_ssd_norm_g'], 'm_i_bias': out['m_i_bias'], 'm_f_bias': out['m_f_bias'], 'm_mlstm_norm_g': out['m_mlstm_norm_g'], 'm_w_out': out['m_w_out'], 'm_norm_ffn_g': out['m_norm_ffn_g'], 'm_w_gate': out['m_w_gate'], 'm_w_up': out['m_w_up'], 'm_w_down': out['m_w_down'], 'm_final_norm_g': out['m_final_norm_g'], 'v_norm_mix_g': out['v_norm_mix_g'], 'v_w_in': out['v_w_in'], 'v_conv_w': out['v_conv_w'], 'v_conv_b': out['v_conv_b'], 'v_dt_bias': out['v_dt_bias'], 'v_a_log': out['v_a_log'], 'v_d_skip': out['v_d_skip'], 'v_ssd_norm_g': out['v_ssd_norm_g'], 'v_i_bias': out['v_i_bias'], 'v_f_bias': out['v_f_bias'], 'v_mlstm_norm_g': out['v_mlstm_norm_g'], 'v_w_out': out['v_w_out'], 'v_norm_ffn_g': out['v_norm_ffn_g'], 'v_w_gate': out['v_w_gate'], 'v_w_up': out['v_w_up'], 'v_w_down': out['v_w_down'], 'v_final_norm_g': out['v_final_norm_g']}


def _loss(weights, diff, rest, loss_target):
    with _jax.named_scope("forward"):
        args = {**rest, TWIN_DIFF_INPUT: diff, **{k: w.astype(_WEIGHT_DTYPES[k]) for k, w in weights.items()}}
        y = _forward(args)
    with _jax.named_scope("loss_head"):
        err = _jnp.square(y.astype(_jnp.float32) - loss_target)
        return 0.5 * _jnp.sum(_jnp.mean(err, axis=-1)) if err.ndim else 0.5 * err


def _adamw(w, g, m, v):
    m = ADAM_B1 * m + (1.0 - ADAM_B1) * g
    v = ADAM_B2 * v + (1.0 - ADAM_B2) * _jnp.square(g)
    m_hat = m / (1.0 - ADAM_B1 ** ADAM_STEP)
    v_hat = v / (1.0 - ADAM_B2 ** ADAM_STEP)
    delta = -ADAM_LR * (m_hat / (_jnp.sqrt(v_hat) + ADAM_EPS) + ADAM_WD * w)
    return delta, m, v


def reference(x, norm_mix_g, w_in, conv_w, conv_b, dt_bias, a_log, d_skip, ssd_norm_g, i_bias, f_bias, mlstm_norm_g, w_out, norm_ffn_g, w_gate, w_up, w_down, final_norm_g, loss_target, m_norm_mix_g, m_w_in, m_conv_w, m_conv_b, m_dt_bias, m_a_log, m_d_skip, m_ssd_norm_g, m_i_bias, m_f_bias, m_mlstm_norm_g, m_w_out, m_norm_ffn_g, m_w_gate, m_w_up, m_w_down, m_final_norm_g, v_norm_mix_g, v_w_in, v_conv_w, v_conv_b, v_dt_bias, v_a_log, v_d_skip, v_ssd_norm_g, v_i_bias, v_f_bias, v_mlstm_norm_g, v_w_out, v_norm_ffn_g, v_w_gate, v_w_up, v_w_down, v_final_norm_g):
    given = dict(x=x, norm_mix_g=norm_mix_g, w_in=w_in, conv_w=conv_w, conv_b=conv_b, dt_bias=dt_bias, a_log=a_log, d_skip=d_skip, ssd_norm_g=ssd_norm_g, i_bias=i_bias, f_bias=f_bias, mlstm_norm_g=mlstm_norm_g, w_out=w_out, norm_ffn_g=norm_ffn_g, w_gate=w_gate, w_up=w_up, w_down=w_down, final_norm_g=final_norm_g, loss_target=loss_target, m_norm_mix_g=m_norm_mix_g, m_w_in=m_w_in, m_conv_w=m_conv_w, m_conv_b=m_conv_b, m_dt_bias=m_dt_bias, m_a_log=m_a_log, m_d_skip=m_d_skip, m_ssd_norm_g=m_ssd_norm_g, m_i_bias=m_i_bias, m_f_bias=m_f_bias, m_mlstm_norm_g=m_mlstm_norm_g, m_w_out=m_w_out, m_norm_ffn_g=m_norm_ffn_g, m_w_gate=m_w_gate, m_w_up=m_w_up, m_w_down=m_w_down, m_final_norm_g=m_final_norm_g, v_norm_mix_g=v_norm_mix_g, v_w_in=v_w_in, v_conv_w=v_conv_w, v_conv_b=v_conv_b, v_dt_bias=v_dt_bias, v_a_log=v_a_log, v_d_skip=v_d_skip, v_ssd_norm_g=v_ssd_norm_g, v_i_bias=v_i_bias, v_f_bias=v_f_bias, v_mlstm_norm_g=v_mlstm_norm_g, v_w_out=v_w_out, v_norm_ffn_g=v_norm_ffn_g, v_w_gate=v_w_gate, v_w_up=v_w_up, v_w_down=v_w_down, v_final_norm_g=v_final_norm_g)
    weights = {n: given[n] for n in TWIN_WEIGHTS}
    shared = {n: given[n] for n in SHARED_INPUTS}
    per_example = {n: given[n] for n in ['x']}
    grad_fn = _jax.value_and_grad(_loss, argnums=(0, 1))

    def one_microbatch(ex, loss_target):
        ex = dict(ex)
        diff = ex.pop(TWIN_DIFF_INPUT)
        return grad_fn(weights, diff, {**shared, **ex}, loss_target)

    if N_MICROBATCH == 1:
        loss, (grad_w, grad_x) = one_microbatch(per_example, given["loss_target"])
    else:
        def body(carry, xs):
            loss_sum, grad_sum = carry
            l_k, (gw_k, gx_k) = one_microbatch(xs[0], xs[1])
            with _jax.named_scope("update"):
                return (loss_sum + l_k, _jax.tree.map(_jnp.add, grad_sum, gw_k)), gx_k

        init = (_jnp.zeros((), _jnp.float32), _jax.tree.map(_jnp.zeros_like, weights))
        (loss, grad_w), grad_x = _jax.lax.scan(body, init, (per_example, given["loss_target"]))
    with _jax.named_scope("update"):
        delta_w, new_m, new_v = {}, {}, {}
        for n in TWIN_WEIGHTS:
            delta_w[n], new_m[n], new_v[n] = _adamw(weights[n], grad_w[n], given["m_" + n], given["v_" + n])
    return (loss, grad_x, *[grad_w[n] for n in TWIN_WEIGHTS], *[delta_w[n] for n in TWIN_WEIGHTS],
            *[new_m[n] for n in TWIN_WEIGHTS], *[new_v[n] for n in TWIN_WEIGHTS])
```

```python
import functools

import jax
import jax.numpy as jnp
import numpy as np
from jax import lax
from jax.experimental import pallas as pl
from jax.experimental.pallas import tpu as pltpu

F32 = jnp.float32
BF16 = jnp.bfloat16

D_MODEL = 2048
SSD_HEADS = 32
SSD_GROUPS = 4
SSD_STATE = 128
SSD_WIDTH = 2048
SSD_CONV = 4
SSD_GROUP_COLS = 1280
SSD_XBC_COLS = 768
ML_HEADS = 8
ML_HEAD_COLS = 768
ML_DK = 128
ML_DV = 256
CHUNK = 128
SMALL_COLS = 128
LANE_I = 32
LANE_F = 40
D_FF = 5632
GATE_SOFTCAP = 15.0
EPS = 1e-6
ADAM_LR, ADAM_B1, ADAM_B2, ADAM_EPS, ADAM_WD, ADAM_STEP = 0.001, 0.9, 0.999, 1e-8, 0.01, 10
MESH = pl.DeviceIdType.MESH
VMEM_LIMIT = 56 * 1024 * 1024


def _dg(a, b, ca, cb):
    return lax.dot_general(a.astype(BF16), b.astype(BF16), (((ca,), (cb,)), ((), ())), preferred_element_type=F32)


@jax.custom_vjp
def _mm(a, b):
    return _dg(a, b, 1, 0)


_mm.defvjp(lambda a, b: (_dg(a, b, 1, 0), (a, b)),
           lambda r, g: (_dg(g, r[1], 1, 1), _dg(r[0], g, 0, 0)))


@jax.custom_vjp
def _mm_nt(a, b):
    return _dg(a, b, 1, 1)


_mm_nt.defvjp(lambda a, b: (_dg(a, b, 1, 1), (a, b)),
              lambda r, g: (_dg(g, r[1], 1, 0), _dg(g, r[0], 0, 0)))


@jax.custom_vjp
def _mm_tn(a, b):
    return _dg(a, b, 0, 0)


_mm_tn.defvjp(lambda a, b: (_dg(a, b, 0, 0), (a, b)),
              lambda r, g: (_dg(r[1], g, 1, 1), _dg(r[0], g, 1, 0)))


def _tri(lower):
    r = lax.broadcasted_iota(jnp.int32, (CHUNK, CHUNK), 0)
    c = lax.broadcasted_iota(jnp.int32, (CHUNK, CHUNK), 1)
    return ((r >= c) if lower else (r <= c)).astype(F32)


def _dg32(t, x):
    return lax.dot_general(t, x, (((1,), (0,)), ((), ())), precision=lax.Precision.HIGHEST, preferred_element_type=F32)


@jax.custom_vjp
def _cumsum(x):
    return _dg32(_tri(True), x)


_cumsum.defvjp(lambda x: (_dg32(_tri(True), x), None), lambda r, g: (_dg32(_tri(False), g),))


def _silu(x):
    return x * jax.nn.sigmoid(x)


def _softplus(x):
    return jnp.maximum(x, 0.0) + jnp.log(1.0 + jnp.exp(-jnp.abs(x)))


def _lane_col(m, lane, h):
    return jnp.sum(jnp.where(lane == h, m, 0.0), axis=1, keepdims=True)


def _ssd_math(cx, cB, cC, z, smallb, S_in, dtb_row, alog_row, dskip, ng, g):
    lane = lax.broadcasted_iota(jnp.int32, (1, CHUNK), 1)
    row_i = lax.broadcasted_iota(jnp.int32, (CHUNK, CHUNK), 0)
    col_i = lax.broadcasted_iota(jnp.int32, (CHUNK, CHUNK), 1)
    causal = row_i >= col_i
    half = lane < 64
    rhalf = lax.broadcasted_iota(jnp.int32, (CHUNK, 1), 0) < 64
    xs, Bm, Cm = _silu(cx), _silu(cB), _silu(cC)
    dt_all = _softplus(smallb + dtb_row)
    a_all = dt_all * (-jnp.exp(alog_row))
    acum_all = _cumsum(a_all)
    alast_all = jnp.sum(a_all, axis=0, keepdims=True)
    cb = _mm_nt(Cm, Bm)

    def lmat(ac):
        acb = jnp.broadcast_to(ac, (CHUNK, CHUNK))
        return jnp.exp(jnp.where(causal, acb - acb.T, -jnp.inf))

    ys, S_out = [], []
    for j in range(4):
        h0 = 8 * g + 2 * j
        ac0, ac1 = _lane_col(acum_all, lane, h0), _lane_col(acum_all, lane, h0 + 1)
        dt0, dt1 = _lane_col(dt_all, lane, h0), _lane_col(dt_all, lane, h0 + 1)
        al0, al1 = _lane_col(alast_all, lane, h0), _lane_col(alast_all, lane, h0 + 1)
        Xp = xs[:, 128 * j:128 * (j + 1)]
        Xd = Xp * jnp.where(half, dt0, dt1)
        ac_sel = jnp.where(half, ac0, ac1)
        al_sel = jnp.where(half, al0, al1)
        Yd = jnp.where(half, _mm(cb * lmat(ac0), Xd), _mm(cb * lmat(ac1), Xd))
        Yoff = _mm_nt(Cm, S_in[j]) * jnp.exp(ac_sel)
        ys.append(Yd + Yoff + dskip[:, 128 * j:128 * (j + 1)] * Xp)
        S_new = _mm_tn(Xd * jnp.exp(al_sel - ac_sel), Bm)
        S_out.append(S_in[j] * jnp.exp(jnp.where(rhalf, al0, al1)) + S_new)
    y = jnp.concatenate(ys, axis=1)
    y = y * _silu(z)
    y = y * lax.rsqrt(jnp.mean(y * y, axis=1, keepdims=True) + EPS) * ng
    return y, tuple(S_out)


def _ssd_conv(blk_ref, halo_ref, cw_ref, cb_ref, ext, first):
    halo = halo_ref[:, 512:SSD_GROUP_COLS]
    ext[0:8, :] = jnp.where(first, jnp.zeros_like(halo), halo)
    ext[8:8 + CHUNK, :] = blk_ref[:, 512:SSD_GROUP_COLS]
    conv = jnp.broadcast_to(cb_ref[0], (CHUNK, SSD_XBC_COLS))
    for tap in range(SSD_CONV):
        conv = conv + cw_ref[0, tap:tap + 1, :] * ext[pl.ds(5 + tap, CHUNK), :]
    return conv


def _ssd_specs(nc, rev):
    cc = (lambda c: nc - 1 - c) if rev else (lambda c: c)
    return [
        pl.BlockSpec((CHUNK, SSD_GROUP_COLS), lambda c, g: (cc(c), g)),
        pl.BlockSpec((8, SSD_GROUP_COLS), lambda c, g: (jnp.maximum(cc(c) * (CHUNK // 8) - 1, 0), g)),
        pl.BlockSpec((CHUNK, SMALL_COLS), lambda c, g: (cc(c), 0)),
        pl.BlockSpec((1, SSD_CONV, SSD_XBC_COLS), lambda c, g: (g, 0, 0)),
        pl.BlockSpec((1, 1, SSD_XBC_COLS), lambda c, g: (g, 0, 0)),
        pl.BlockSpec((1, SMALL_COLS), lambda c, g: (0, 0)),
        pl.BlockSpec((1, SMALL_COLS), lambda c, g: (0, 0)),
        pl.BlockSpec((1, 512), lambda c, g: (0, g)),
        pl.BlockSpec((1, 512), lambda c, g: (0, g)),
    ]


def _ssd_forward(proj, small, cw, cb, dtb, alog, dskip, ng):
    L = proj.shape[0]
    nc = L // CHUNK

    def body(blk_ref, halo_ref, small_ref, cw_ref, cb_ref, dtb_ref, alog_ref, dskip_ref, ng_ref, y_ref, st_ref, carry, ext):
        c, g = pl.program_id(0), pl.program_id(1)

        @pl.when(c == 0)
        def _():
            carry[g] = jnp.zeros((4, CHUNK, CHUNK), F32)

        conv = _ssd_conv(blk_ref, halo_ref, cw_ref, cb_ref, ext, c == 0)
        S_in = tuple(carry[g, j] for j in range(4))
        st_ref[0, 0] = carry[g]
        y, S_out = _ssd_math(conv[:, 0:512], conv[:, 512:640], conv[:, 640:768], blk_ref[:, 0:512], small_ref[...],
                             S_in, dtb_ref[...], alog_ref[...], dskip_ref[...], ng_ref[...], g)
        y_ref[...] = y.astype(BF16)
        for j in range(4):
            carry[g, j] = S_out[j]

    return pl.pallas_call(
        body, name="ssd_fwd", grid=(nc, SSD_GROUPS),
        in_specs=_ssd_specs(nc, False),
        out_specs=[pl.BlockSpec((CHUNK, 512), lambda c, g: (c, g)),
                   pl.BlockSpec((1, 1, 4, CHUNK, CHUNK), lambda c, g: (c, g, 0, 0, 0))],
        out_shape=[jax.ShapeDtypeStruct((L, SSD_WIDTH), BF16), jax.ShapeDtypeStruct((nc, SSD_GROUPS, 4, CHUNK, CHUNK), F32)],
        scratch_shapes=[pltpu.VMEM((SSD_GROUPS, 4, CHUNK, CHUNK), F32), pltpu.VMEM((8 + CHUNK, SSD_XBC_COLS), F32)],
        compiler_params=pltpu.CompilerParams(dimension_semantics=("arbitrary", "arbitrary"), vmem_limit_bytes=VMEM_LIMIT),
    )(proj, proj, small, cw, cb, dtb, alog, dskip, ng)


def _ssd_backward(proj, small, cw, cb, dtb, alog, dskip, ng, states, dy):
    L = proj.shape[0]
    nc = L // CHUNK

    def body(blk_ref, halo_ref, small_ref, cw_ref, cb_ref, dtb_ref, alog_ref, dskip_ref, ng_ref, st_ref, dy_ref,
             dproj_ref, dsmall_ref, dcw_ref, dcb_ref, ddtb_ref, dalog_ref, ddskip_ref, dng_ref, dcarry, nxt, ext, dext):
        s, g = pl.program_id(0), pl.program_id(1)
        c = nc - 1 - s

        @pl.when(s == 0)
        def _():
            dcarry[g] = jnp.zeros((4, CHUNK, CHUNK), F32)
            nxt[g] = jnp.zeros((8, SSD_XBC_COLS), F32)
            dcw_ref[g] = jnp.zeros((SSD_CONV, SSD_XBC_COLS), F32)
            dcb_ref[g] = jnp.zeros((1, SSD_XBC_COLS), F32)
            ddskip_ref[g] = jnp.zeros((1, 512), F32)
            dng_ref[g] = jnp.zeros((1, 512), F32)

        @pl.when((s == 0) & (g == 0))
        def _():
            ddtb_ref[...] = jnp.zeros((1, SMALL_COLS), F32)
            dalog_ref[...] = jnp.zeros((1, SMALL_COLS), F32)

        conv = _ssd_conv(blk_ref, halo_ref, cw_ref, cb_ref, ext, c == 0)
        S_in = tuple(st_ref[0, 0, j] for j in range(4))
        dS_out = tuple(dcarry[g, j] for j in range(4))
        _, vjp = jax.vjp(functools.partial(_ssd_math, g=g), conv[:, 0:512], conv[:, 512:640], conv[:, 640:768],
                         blk_ref[:, 0:512], small_ref[...], S_in, dtb_ref[...], alog_ref[...], dskip_ref[...], ng_ref[...])
        d_cx, d_cB, d_cC, d_z, d_small, dS_in, d_dtb, d_alog, d_dskip, d_ng = vjp((dy_ref[...].astype(F32), dS_out))
        for j in range(4):
            dcarry[g, j] = dS_in[j]
        dext[0:8, :] = jnp.zeros((8, SSD_XBC_COLS), F32)
        dext[8:8 + CHUNK, 0:512] = d_cx
        dext[8:8 + CHUNK, 512:640] = d_cB
        dext[8:8 + CHUNK, 640:768] = d_cC
        dext[8 + CHUNK:16 + CHUNK, :] = nxt[g]
        nxt[g] = dext[8:16, :]
        dconv = dext[8:8 + CHUNK, :]
        d_xbc = jnp.zeros((CHUNK, SSD_XBC_COLS), F32)
        for tap in range(SSD_CONV):
            d_xbc = d_xbc + cw_ref[0, tap:tap + 1, :] * dext[pl.ds(8 + 3 - tap, CHUNK), :]
            dcw_ref[g, tap:tap + 1, :] += jnp.sum(dconv * ext[pl.ds(5 + tap, CHUNK), :], axis=0, keepdims=True)
        dcb_ref[g] += jnp.sum(dconv, axis=0, keepdims=True)
        dproj_ref[:, 0:512] = d_z.astype(BF16)
        dproj_ref[:, 512:SSD_GROUP_COLS] = d_xbc.astype(BF16)

        @pl.when(g == 0)
        def _():
            dsmall_ref[...] = d_small

        @pl.when(g != 0)
        def _():
            dsmall_ref[...] += d_small

        ddtb_ref[...] += d_dtb
        dalog_ref[...] += d_alog
        ddskip_ref[g] += d_dskip
        dng_ref[g] += d_ng

    whole = lambda shape: pl.BlockSpec(shape, lambda s, g: (0,) * len(shape))
    return pl.pallas_call(
        body, name="ssd_bwd", grid=(nc, SSD_GROUPS),
        in_specs=_ssd_specs(nc, True) + [
            pl.BlockSpec((1, 1, 4, CHUNK, CHUNK), lambda s, g: (nc - 1 - s, g, 0, 0, 0)),
            pl.BlockSpec((CHUNK, 512), lambda s, g: (nc - 1 - s, g))],
        out_specs=[pl.BlockSpec((CHUNK, SSD_GROUP_COLS), lambda s, g: (nc - 1 - s, g)),
                   pl.BlockSpec((CHUNK, SMALL_COLS), lambda s, g: (nc - 1 - s, 0)),
                   whole((SSD_GROUPS, SSD_CONV, SSD_XBC_COLS)), whole((SSD_GROUPS, 1, SSD_XBC_COLS)),
                   whole((1, SMALL_COLS)), whole((1, SMALL_COLS)),
                   whole((SSD_GROUPS, 1, 512)), whole((SSD_GROUPS, 1, 512))],
        out_shape=[jax.ShapeDtypeStruct((L, SSD_GROUPS * SSD_GROUP_COLS), BF16), jax.ShapeDtypeStruct((L, SMALL_COLS), F32),
                   jax.ShapeDtypeStruct((SSD_GROUPS, SSD_CONV, SSD_XBC_COLS), F32),
                   jax.ShapeDtypeStruct((SSD_GROUPS, 1, SSD_XBC_COLS), F32),
                   jax.ShapeDtypeStruct((1, SMALL_COLS), F32), jax.ShapeDtypeStruct((1, SMALL_COLS), F32),
                   jax.ShapeDtypeStruct((SSD_GROUPS, 1, 512), F32), jax.ShapeDtypeStruct((SSD_GROUPS, 1, 512), F32)],
        scratch_shapes=[pltpu.VMEM((SSD_GROUPS, 4, CHUNK, CHUNK), F32), pltpu.VMEM((SSD_GROUPS, 8, SSD_XBC_COLS), F32),
                        pltpu.VMEM((8 + CHUNK, SSD_XBC_COLS), F32), pltpu.VMEM((16 + CHUNK, SSD_XBC_COLS), F32)],
        compiler_params=pltpu.CompilerParams(dimension_semantics=("arbitrary", "arbitrary"), vmem_limit_bytes=VMEM_LIMIT),
    )(proj, proj, small, cw, cb, dtb, alog, dskip, ng, states, dy)


def _mlstm_math(q, k, v, o_raw, smallb, C_in, n_in, m_in_row, ib_row, fb_row, ng, h):
    lane = lax.broadcasted_iota(jnp.int32, (1, CHUNK), 1)
    row_i = lax.broadcasted_iota(jnp.int32, (CHUNK, CHUNK), 0)
    col_i = lax.broadcasted_iota(jnp.int32, (CHUNK, CHUNK), 1)
    causal = row_i >= col_i
    kk = k * (ML_DK ** -0.5)
    li_all = GATE_SOFTCAP * jnp.tanh((smallb + ib_row) / GATE_SOFTCAP)
    lf_all = -_softplus(-(GATE_SOFTCAP * jnp.tanh((smallb + fb_row) / GATE_SOFTCAP)))
    bcum_all = _cumsum(lf_all)
    li = _lane_col(li_all, lane, LANE_I + h)
    lf = _lane_col(lf_all, lane, LANE_F + h)
    bc = _lane_col(bcum_all, lane, LANE_F + h)
    b_last = jnp.sum(lf, axis=0, keepdims=True)
    m_in = _lane_col(m_in_row, lane, 0)
    a = b_last - bc + li
    m_loc = jnp.max(a, axis=0, keepdims=True)
    w = jnp.exp(a - m_loc)
    C_loc = _mm_tn(w * v, kk)
    n_loc = jnp.sum(w * kk, axis=0, keepdims=True)
    m_new = jnp.maximum(b_last + m_in, m_loc)
    s_old = jnp.exp(b_last + m_in - m_new)
    s_new = jnp.exp(m_loc - m_new)
    C_out = s_old * C_in + s_new * C_loc
    n_out = s_old * n_in + s_new * n_loc
    bc_b = jnp.broadcast_to(bc, (CHUNK, CHUNK))
    li_b = jnp.broadcast_to(li, (CHUNK, CHUNK))
    D = jnp.where(causal, bc_b - bc_b.T + li_b.T, -jnp.inf)
    m_intra = jnp.max(D, axis=1, keepdims=True)
    inter_log = bc + m_in
    m_t = jnp.maximum(inter_log, m_intra)
    S = _mm_nt(q, kk) * jnp.exp(D - m_t)
    w_inter = jnp.exp(inter_log - m_t)
    num = _mm(S, v) + w_inter * _mm_nt(q, C_in)
    nq = jnp.sum(S, axis=1, keepdims=True) + w_inter * jnp.sum(q * n_in, axis=1, keepdims=True)
    den = jnp.maximum(jnp.abs(nq), jnp.exp(-m_t))
    hh = num / den
    hh = hh * lax.rsqrt(jnp.mean(hh * hh, axis=1, keepdims=True) + EPS)
    hh = hh * ng * jax.nn.sigmoid(o_raw)
    return hh, C_out, n_out, jnp.broadcast_to(m_new, (1, CHUNK))


def _ml_specs(nc, rev):
    cc = (lambda c: nc - 1 - c) if rev else (lambda c: c)
    return [
        pl.BlockSpec((CHUNK, ML_HEAD_COLS), lambda c, h: (cc(c), h)),
        pl.BlockSpec((CHUNK, SMALL_COLS), lambda c, h: (cc(c), 0)),
        pl.BlockSpec((1, SMALL_COLS), lambda c, h: (0, 0)),
        pl.BlockSpec((1, SMALL_COLS), lambda c, h: (0, 0)),
        pl.BlockSpec((1, ML_DV), lambda c, h: (0, h)),
    ]


def _mlstm_forward(proj, small, ib, fb, ng):
    L = proj.shape[0]
    nc = L // CHUNK

    def body(blk_ref, small_ref, ib_ref, fb_ref, ng_ref, y_ref, cst_ref, nm_ref, c_carry, nm_carry):
        c, h = pl.program_id(0), pl.program_id(1)

        @pl.when(c == 0)
        def _():
            c_carry[h] = jnp.zeros((ML_DV, ML_DK), F32)
            nm_carry[h] = jnp.zeros((2, ML_DK), F32)

        cst_ref[0, 0] = c_carry[h]
        nm_ref[0, 0] = nm_carry[h]
        hh, C_out, n_out, m_out = _mlstm_math(
            blk_ref[:, 0:128], blk_ref[:, 128:256], blk_ref[:, 256:512], blk_ref[:, 512:768], small_ref[...],
            c_carry[h], nm_carry[h, 0:1, :], nm_carry[h, 1:2, :], ib_ref[...], fb_ref[...], ng_ref[...], h)
        y_ref[...] = hh.astype(BF16)
        c_carry[h] = C_out
        nm_carry[h, 0:1, :] = n_out
        nm_carry[h, 1:2, :] = m_out

    return pl.pallas_call(
        body, name="mlstm_fwd", grid=(nc, ML_HEADS),
        in_specs=_ml_specs(nc, False),
        out_specs=[pl.BlockSpec((CHUNK, ML_DV), lambda c, h: (c, h)),
                   pl.BlockSpec((1, 1, ML_DV, ML_DK), lambda c, h: (c, h, 0, 0)),
                   pl.BlockSpec((1, 1, 2, ML_DK), lambda c, h: (c, h, 0, 0))],
        out_shape=[jax.ShapeDtypeStruct((L, ML_HEADS * ML_DV), BF16),
                   jax.ShapeDtypeStruct((nc, ML_HEADS, ML_DV, ML_DK), F32),
                   jax.ShapeDtypeStruct((nc, ML_HEADS, 2, ML_DK), F32)],
        scratch_shapes=[pltpu.VMEM((ML_HEADS, ML_DV, ML_DK), F32), pltpu.VMEM((ML_HEADS, 2, ML_DK), F32)],
        compiler_params=pltpu.CompilerParams(dimension_semantics=("arbitrary", "arbitrary"), vmem_limit_bytes=VMEM_LIMIT),
    )(proj, small, ib, fb, ng)


def _mlstm_backward(proj, small, ib, fb, ng, cst, nmst, dy, dsmall_in):
    L = proj.shape[0]
    nc = L // CHUNK

    def body(blk_ref, small_ref, ib_ref, fb_ref, ng_ref, cst_ref, nm_ref, dy_ref, dsin_ref,
             dproj_ref, dsmall_ref, dib_ref, dfb_ref, dng_ref, dc_carry, dnm_carry):
        s, h = pl.program_id(0), pl.program_id(1)

        @pl.when(s == 0)
        def _():
            dc_carry[h] = jnp.zeros((ML_DV, ML_DK), F32)
            dnm_carry[h] = jnp.zeros((2, ML_DK), F32)
            dng_ref[h] = jnp.zeros((1, ML_DV), F32)

        @pl.when((s == 0) & (h == 0))
        def _():
            dib_ref[...] = jnp.zeros((1, SMALL_COLS), F32)
            dfb_ref[...] = jnp.zeros((1, SMALL_COLS), F32)

        _, vjp = jax.vjp(functools.partial(_mlstm_math, h=h),
                         blk_ref[:, 0:128], blk_ref[:, 128:256], blk_ref[:, 256:512], blk_ref[:, 512:768], small_ref[...],
                         cst_ref[0, 0], nm_ref[0, 0, 0:1, :], nm_ref[0, 0, 1:2, :], ib_ref[...], fb_ref[...], ng_ref[...])
        dq, dk, dv, do, d_small, dC, dn, dm, d_ib, d_fb, d_ng = vjp(
            (dy_ref[...].astype(F32), dc_carry[h], dnm_carry[h, 0:1, :], dnm_carry[h, 1:2, :]))
        dc_carry[h] = dC
        dnm_carry[h, 0:1, :] = dn
        dnm_carry[h, 1:2, :] = dm
        dproj_ref[:, 0:128] = dq.astype(BF16)
        dproj_ref[:, 128:256] = dk.astype(BF16)
        dproj_ref[:, 256:512] = dv.astype(BF16)
        dproj_ref[:, 512:768] = do.astype(BF16)

        @pl.when(h == 0)
        def _():
            dsmall_ref[...] = dsin_ref[...] + d_small

        @pl.when(h != 0)
        def _():
            dsmall_ref[...] += d_small

        dib_ref[...] += d_ib
        dfb_ref[...] += d_fb
        dng_ref[h] += d_ng

    whole = lambda shape: pl.BlockSpec(shape, lambda s, h: (0,) * len(shape))
    return pl.pallas_call(
        body, name="mlstm_bwd", grid=(nc, ML_HEADS),
        in_specs=_ml_specs(nc, True) + [
            pl.BlockSpec((1, 1, ML_DV, ML_DK), lambda s, h: (nc - 1 - s, h, 0, 0)),
            pl.BlockSpec((1, 1, 2, ML_DK), lambda s, h: (nc - 1 - s, h, 0, 0)),
            pl.BlockSpec((CHUNK, ML_DV), lambda s, h: (nc - 1 - s, SSD_WIDTH // ML_DV + h)),
            pl.BlockSpec((CHUNK, SMALL_COLS), lambda s, h: (nc - 1 - s, 0))],
        out_specs=[pl.BlockSpec((CHUNK, ML_HEAD_COLS), lambda s, h: (nc - 1 - s, h)),
                   pl.BlockSpec((CHUNK, SMALL_COLS), lambda s, h: (nc - 1 - s, 0)),
                   whole((1, SMALL_COLS)), whole((1, SMALL_COLS)), whole((ML_HEADS, 1, ML_DV))],
        out_shape=[jax.ShapeDtypeStruct((L, ML_HEADS * ML_HEAD_COLS), BF16), jax.ShapeDtypeStruct((L, SMALL_COLS), F32),
                   jax.ShapeDtypeStruct((1, SMALL_COLS), F32), jax.ShapeDtypeStruct((1, SMALL_COLS), F32),
                   jax.ShapeDtypeStruct((ML_HEADS, 1, ML_DV), F32)],
        scratch_shapes=[pltpu.VMEM((ML_HEADS, ML_DV, ML_DK), F32), pltpu.VMEM((ML_HEADS, 2, ML_DK), F32)],
        compiler_params=pltpu.CompilerParams(dimension_semantics=("arbitrary", "arbitrary"), vmem_limit_bytes=VMEM_LIMIT),
    )(proj, small, ib, fb, ng, cst, nmst, dy, dsmall_in)


_OFF_Z, _OFF_X, _OFF_B, _OFF_C, _OFF_DT = 0, 2048, 4096, 4608, 5120
_OFF_Q, _OFF_K, _OFF_V, _OFF_O, _OFF_I, _OFF_F, IN_WIDTH = 5152, 6176, 7200, 9248, 11296, 11304, 11312


def _ssd_segments():
    segs = []
    for g in range(SSD_GROUPS):
        segs += [(_OFF_Z + 512 * g, 512), (_OFF_X + 512 * g, 512), (_OFF_B + 128 * g, 128), (_OFF_C + 128 * g, 128)]
    return segs


def _ml_segments():
    segs = []
    for h in range(ML_HEADS):
        segs += [(_OFF_Q + 128 * h, 128), (_OFF_K + 128 * h, 128), (_OFF_V + 256 * h, 256), (_OFF_O + 256 * h, 256)]
    return segs


_SMALL_SEGMENTS = [(_OFF_DT, 32), (_OFF_I, 8), (_OFF_F, 8)]


def _take_cols(w, segs):
    return jnp.concatenate([w[:, s:s + n] for s, n in segs], axis=1)


def _split_w_in(w):
    small = _take_cols(w, _SMALL_SEGMENTS)
    small = jnp.concatenate([small, jnp.zeros((w.shape[0], SMALL_COLS - small.shape[1]), w.dtype)], axis=1)
    return _take_cols(w, _ssd_segments()), _take_cols(w, _ml_segments()), small


def _merge_w_in(d_ssd, d_ml, d_small):
    pieces = []
    for arr, segs in ((d_ssd, _ssd_segments()), (d_ml, _ml_segments()), (d_small, _SMALL_SEGMENTS)):
        pos = 0
        for s, n in segs:
            pieces.append((s, arr[:, pos:pos + n]))
            pos += n
    pieces.sort(key=lambda t: t[0])
    return jnp.concatenate([p for _, p in pieces], axis=1)


def _conv_to_groups(cw):
    return jnp.stack([jnp.concatenate([cw[:, 512 * g:512 * (g + 1)], cw[:, 2048 + 128 * g:2048 + 128 * (g + 1)],
                                       cw[:, 2560 + 128 * g:2560 + 128 * (g + 1)]], axis=1) for g in range(SSD_GROUPS)])


def _conv_from_groups(d):
    return jnp.concatenate([d[g, :, 0:512] for g in range(SSD_GROUPS)] + [d[g, :, 512:640] for g in range(SSD_GROUPS)]
                           + [d[g, :, 640:768] for g in range(SSD_GROUPS)], axis=1)


def _small_row(vec, lane0):
    n = vec.shape[1]
    return jnp.concatenate([jnp.zeros((1, lane0), F32), vec, jnp.zeros((1, SMALL_COLS - lane0 - n), F32)], axis=1)


def _matmul(a, b, mode, out_dtype, tm, tn, tk, name, addend=None):
    M, Kd = (a.shape[1], a.shape[0]) if mode == "tn" else a.shape
    N = b.shape[0] if mode == "nt" else b.shape[1]
    tm, tn, tk = min(tm, M), min(tn, N), min(tk, Kd)
    if mode == "nn":
        a_spec = pl.BlockSpec((tm, tk), lambda i, j, k: (i, k))
        b_spec = pl.BlockSpec((tk, tn), lambda i, j, k: (k, j))
        dims = (((1,), (0,)), ((), ()))
    elif mode == "nt":
        a_spec = pl.BlockSpec((tm, tk), lambda i, j, k: (i, k))
        b_spec = pl.BlockSpec((tn, tk), lambda i, j, k: (j, k))
        dims = (((1,), (1,)), ((), ()))
    else:
        a_spec = pl.BlockSpec((tk, tm), lambda i, j, k: (k, i))
        b_spec = pl.BlockSpec((tk, tn), lambda i, j, k: (k, j))
        dims = (((0,), (0,)), ((), ()))
    assert M % tm == 0 and N % tn == 0 and Kd % tk == 0, (name, M, N, Kd, tm, tn, tk)
    nk = Kd // tk
    has_add = addend is not None

    def body(*refs):
        a_ref, b_ref = refs[0], refs[1]
        add_ref = refs[2] if has_add else None
        o_ref, acc_ref = refs[2 + has_add], refs[3 + has_add]
        k = pl.program_id(2)
        part = lax.dot_general(a_ref[...].astype(BF16), b_ref[...].astype(BF16), dims, preferred_element_type=F32)

        @pl.when(k == 0)
        def _():
            acc_ref[...] = part

        @pl.when(k != 0)
        def _():
            acc_ref[...] += part

        @pl.when(k == nk - 1)
        def _():
            r = acc_ref[...]
            if has_add:
                r = r + add_ref[...].astype(F32)
            o_ref[...] = r.astype(out_dtype)

    o_spec = pl.BlockSpec((tm, tn), lambda i, j, k: (i, j))
    return pl.pallas_call(
        body, name=name, grid=(M // tm, N // tn, nk),
        in_specs=[a_spec, b_spec] + ([o_spec] if has_add else []), out_specs=o_spec,
        out_shape=jax.ShapeDtypeStruct((M, N), out_dtype),
        scratch_shapes=[pltpu.VMEM((tm, tn), F32)],
        compiler_params=pltpu.CompilerParams(dimension_semantics=("parallel", "parallel", "arbitrary"),
                                             vmem_limit_bytes=VMEM_LIMIT),
    )(*((a, b) + ((addend,) if has_add else ())))


ROW_TILE = 256


def _rmsnorm_fwd(x, g, name):
    L, D = x.shape

    def body(x_ref, g_ref, u_ref):
        xv = x_ref[...]
        r = lax.rsqrt(jnp.mean(xv * xv, axis=1, keepdims=True) + EPS)
        u_ref[...] = (xv * r * g_ref[...]).astype(BF16)

    return pl.pallas_call(
        body, name=name, grid=(L // ROW_TILE,),
        in_specs=[pl.BlockSpec((ROW_TILE, D), lambda i: (i, 0)), pl.BlockSpec((1, D), lambda i: (0, 0))],
        out_specs=pl.BlockSpec((ROW_TILE, D), lambda i: (i, 0)),
        out_shape=jax.ShapeDtypeStruct((L, D), BF16),
        compiler_params=pltpu.CompilerParams(dimension_semantics=("parallel",), vmem_limit_bytes=VMEM_LIMIT),
    )(x, g)


def _rmsnorm_bwd(du, x, g, dres, name):
    L, D = x.shape

    def body(du_ref, x_ref, g_ref, dres_ref, dx_ref, dg_ref):
        i = pl.program_id(0)
        xv, duv = x_ref[...], du_ref[...]
        r = lax.rsqrt(jnp.mean(xv * xv, axis=1, keepdims=True) + EPS)
        t = duv * g_ref[...]
        dx_ref[...] = dres_ref[...] + r * t - xv * (r * r * r) * jnp.mean(t * xv, axis=1, keepdims=True)
        dg = jnp.sum(duv * xv * r, axis=0, keepdims=True)

        @pl.when(i == 0)
        def _():
            dg_ref[...] = dg

        @pl.when(i != 0)
        def _():
            dg_ref[...] += dg

    row = pl.BlockSpec((ROW_TILE, D), lambda i: (i, 0))
    vec = pl.BlockSpec((1, D), lambda i: (0, 0))
    return pl.pallas_call(
        body, name=name, grid=(L // ROW_TILE,),
        in_specs=[row, row, vec, row], out_specs=[row, vec],
        out_shape=[jax.ShapeDtypeStruct((L, D), F32), jax.ShapeDtypeStruct((1, D), F32)],
        compiler_params=pltpu.CompilerParams(dimension_semantics=("arbitrary",), vmem_limit_bytes=VMEM_LIMIT),
    )(du, x, g, dres)


def _loss_head(h, target, g):
    L, D = h.shape

    def body(h_ref, t_ref, g_ref, dh_ref, loss_ref, dg_ref):
        i = pl.program_id(0)
        hv = h_ref[...]
        r = lax.rsqrt(jnp.mean(hv * hv, axis=1, keepdims=True) + EPS)
        diff = hv * r * g_ref[...] - t_ref[...]
        part = 0.5 * jnp.sum(jnp.mean(diff * diff, axis=1, keepdims=True), axis=0, keepdims=True)
        dy = diff * (1.0 / D)
        t = dy * g_ref[...]
        dh_ref[...] = r * t - hv * (r * r * r) * jnp.mean(t * hv, axis=1, keepdims=True)
        dg = jnp.sum(dy * hv * r, axis=0, keepdims=True)

        @pl.when(i == 0)
        def _():
            dg_ref[...] = dg
            loss_ref[...] = jnp.broadcast_to(part, (1, 128))

        @pl.when(i != 0)
        def _():
            dg_ref[...] += dg
            loss_ref[...] += jnp.broadcast_to(part, (1, 128))

    row = pl.BlockSpec((ROW_TILE, D), lambda i: (i, 0))
    vec = pl.BlockSpec((1, D), lambda i: (0, 0))
    return pl.pallas_call(
        body, name="loss_head", grid=(L // ROW_TILE,),
        in_specs=[row, row, vec], out_specs=[row, pl.BlockSpec((1, 128), lambda i: (0, 0)), vec],
        out_shape=[jax.ShapeDtypeStruct((L, D), F32), jax.ShapeDtypeStruct((1, 128), F32), jax.ShapeDtypeStruct((1, D), F32)],
        compiler_params=pltpu.CompilerParams(dimension_semantics=("arbitrary",), vmem_limit_bytes=VMEM_LIMIT),
    )(h, target, g)


FF_TILE = 1408


def _swiglu_fwd(gate, up):
    L, N = gate.shape

    def body(g_ref, u_ref, o_ref):
        o_ref[...] = (_silu(g_ref[...]) * u_ref[...]).astype(BF16)

    blk = pl.BlockSpec((ROW_TILE, FF_TILE), lambda i, j: (i, j))
    return pl.pallas_call(
        body, name="swiglu_fwd", grid=(L // ROW_TILE, N // FF_TILE), in_specs=[blk, blk], out_specs=blk,
        out_shape=jax.ShapeDtypeStruct((L, N), BF16),
        compiler_params=pltpu.CompilerParams(dimension_semantics=("parallel", "parallel"), vmem_limit_bytes=VMEM_LIMIT),
    )(gate, up)


def _swiglu_bwd(dact, gate, up):
    L, N = gate.shape

    def body(d_ref, g_ref, u_ref, dg_ref, du_ref):
        gv, d = g_ref[...], d_ref[...]
        s = jax.nn.sigmoid(gv)
        dg_ref[...] = (d * u_ref[...] * (s * (1.0 + gv * (1.0 - s)))).astype(BF16)
        du_ref[...] = (d * gv * s).astype(BF16)

    blk = pl.BlockSpec((ROW_TILE, FF_TILE), lambda i, j: (i, j))
    return pl.pallas_call(
        body, name="swiglu_bwd", grid=(L // ROW_TILE, N // FF_TILE), in_specs=[blk, blk, blk], out_specs=[blk, blk],
        out_shape=[jax.ShapeDtypeStruct((L, N), BF16), jax.ShapeDtypeStruct((L, N), BF16)],
        compiler_params=pltpu.CompilerParams(dimension_semantics=("parallel", "parallel"), vmem_limit_bytes=VMEM_LIMIT),
    )(dact, gate, up)


def _adamw(w, g, m, v, rows, name):
    R, C = w.shape
    assert R % rows == 0, (name, R, rows)

    def body(w_ref, g_ref, m_ref, v_ref, d_ref, nm_ref, nv_ref):
        gv = g_ref[...]
        mn = ADAM_B1 * m_ref[...] + (1.0 - ADAM_B1) * gv
        vn = ADAM_B2 * v_ref[...] + (1.0 - ADAM_B2) * (gv * gv)
        m_hat = mn / (1.0 - ADAM_B1 ** ADAM_STEP)
        v_hat = vn / (1.0 - ADAM_B2 ** ADAM_STEP)
        d_ref[...] = -ADAM_LR * (m_hat / (jnp.sqrt(v_hat) + ADAM_EPS) + ADAM_WD * w_ref[...])
        nm_ref[...] = mn
        nv_ref[...] = vn

    blk = pl.BlockSpec((rows, C), lambda i: (i, 0))
    out = jax.ShapeDtypeStruct((R, C), F32)
    return pl.pallas_call(
        body, name=name, grid=(R // rows,), in_specs=[blk] * 4, out_specs=[blk] * 3, out_shape=[out] * 3,
        compiler_params=pltpu.CompilerParams(dimension_semantics=("parallel",), vmem_limit_bytes=VMEM_LIMIT),
    )(w, g, m, v)


ANY = pl.BlockSpec(memory_space=pl.ANY)
N_DEV = 8


def _allgather_small(p, name):
    R, C = p.shape

    def body(p_ref, out_ref, send_sems, recv_sems, local_sem):
        x, y, c = lax.axis_index("x"), lax.axis_index("y"), lax.axis_index("c")
        me = 4 * x + 2 * y + c
        mine = pltpu.make_async_copy(p_ref, out_ref.at[me], local_sem)
        mine.start()

        def peer(d):
            return (x ^ ((d >> 2) & 1), y ^ ((d >> 1) & 1), c ^ (d & 1))

        def copy(d, block):
            return pltpu.make_async_remote_copy(src_ref=p_ref, dst_ref=out_ref.at[block], send_sem=send_sems.at[d - 1],
                                                recv_sem=recv_sems.at[d - 1], device_id=peer(d), device_id_type=MESH)

        sends = [copy(d, me) for d in range(1, N_DEV)]
        for cp in sends:
            cp.start()
        for d in range(1, N_DEV):
            px, py, pc = peer(d)
            copy(d, 4 * px + 2 * py + pc).wait_recv()
        for cp in sends:
            cp.wait_send()
        mine.wait()

    return pl.pallas_call(
        body, name=name, out_shape=jax.ShapeDtypeStruct((N_DEV, R, C), p.dtype),
        in_specs=[pl.BlockSpec(memory_space=pltpu.VMEM)], out_specs=pl.BlockSpec(memory_space=pltpu.VMEM),
        scratch_shapes=[pltpu.SemaphoreType.DMA((N_DEV - 1,)), pltpu.SemaphoreType.DMA((N_DEV - 1,)), pltpu.SemaphoreType.DMA],
    )(p)


def _other_chips(x, y):
    return [(1 - x, y), (x, 1 - y), (1 - x, 1 - y)]


def _allgather_big(shards):
    T = len(shards)
    halves = [s.shape[0] // 2 for s in shards]

    def body(*refs):
        ins, outs = refs[:T], refs[T:2 * T]
        send_sems, recv_sems, local_sems = refs[2 * T:]
        x, y, c = lax.axis_index("x"), lax.axis_index("y"), lax.axis_index("c")
        sibling = (x, y, 1 - c)
        chips = _other_chips(x, y)

        def rows(t, px, py, pc):
            return outs[t].at[2 * px + py, pl.ds(pc * halves[t], halves[t]), :]

        def copy(t, k, block, to, src=None):
            return pltpu.make_async_remote_copy(
                src_ref=rows(t, *block) if src is None else src, dst_ref=rows(t, *block),
                send_sem=send_sems.at[t, k], recv_sem=recv_sems.at[t, k], device_id=to, device_id_type=MESH)

        mine, first, passed = [], [], []
        for t in range(T):
            src = ins[t].at[pl.ds(c * halves[t], halves[t]), :]
            mine.append(pltpu.make_async_copy(src, rows(t, x, y, c), local_sems.at[t]))
            first.append([copy(t, 0, (x, y, c), sibling, src=src)]
                         + [copy(t, 1 + j, (x, y, c), (*chip, c), src=src) for j, chip in enumerate(chips)])
        for t in range(T):
            mine[t].start()
            for cp in first[t]:
                cp.start()
        for t in range(T):
            for j, chip in enumerate(chips):
                copy(t, 1 + j, (*chip, c), (x, y, c)).wait_recv()
                cp = copy(t, 4 + j, (*chip, c), sibling)
                cp.start()
                passed.append(cp)
        for t in range(T):
            copy(t, 0, (x, y, 1 - c), (x, y, c)).wait_recv()
            for j, chip in enumerate(chips):
                copy(t, 4 + j, (*chip, 1 - c), (x, y, c)).wait_recv()
        for t in range(T):
            for cp in first[t]:
                cp.wait_send()
            mine[t].wait()
        for cp in passed:
            cp.wait_send()

    return pl.pallas_call(
        body, name="allgather_weights",
        out_shape=[jax.ShapeDtypeStruct((4,) + s.shape, s.dtype) for s in shards],
        in_specs=[ANY] * T, out_specs=[ANY] * T,
        scratch_shapes=[pltpu.SemaphoreType.DMA((T, 7)), pltpu.SemaphoreType.DMA((T, 7)), pltpu.SemaphoreType.DMA((T,))],
    )(*shards)


def _pair_exchange(grads):
    T = len(grads)

    def body(*refs):
        ins, outs = refs[:T], refs[T:2 * T]
        send_sems, recv_sems = refs[2 * T:]
        x, y, c = lax.axis_index("x"), lax.axis_index("y"), lax.axis_index("c")
        cps = [pltpu.make_async_remote_copy(src_ref=ins[t].at[1 - c], dst_ref=outs[t], send_sem=send_sems.at[t],
                                            recv_sem=recv_sems.at[t], device_id=(x, y, 1 - c), device_id_type=MESH)
               for t in range(T)]
        for cp in cps:
            cp.start()
        for cp in cps:
            cp.wait()

    return pl.pallas_call(
        body, name="grad_pair_exchange",
        out_shape=[jax.ShapeDtypeStruct(g.shape[1:], g.dtype) for g in grads],
        in_specs=[ANY] * T, out_specs=[ANY] * T,
        scratch_shapes=[pltpu.SemaphoreType.DMA((T,)), pltpu.SemaphoreType.DMA((T,))],
    )(*grads)


def _chip_exchange(parts):
    T = len(parts)

    def body(*refs):
        ins, outs = refs[:T], refs[T:2 * T]
        send_sems, recv_sems, local_sems = refs[2 * T:]
        x, y, c = lax.axis_index("x"), lax.axis_index("y"), lax.axis_index("c")
        my_chip = 2 * x + y
        chips = _other_chips(x, y)
        mine, sends = [], []
        for t in range(T):
            mine.append(pltpu.make_async_copy(ins[t].at[my_chip], outs[t].at[my_chip], local_sems.at[t]))
            for j, (px, py) in enumerate(chips):
                sends.append(pltpu.make_async_remote_copy(
                    src_ref=ins[t].at[2 * px + py], dst_ref=outs[t].at[my_chip], send_sem=send_sems.at[t, j],
                    recv_sem=recv_sems.at[t, j], device_id=(px, py, c), device_id_type=MESH))
        for cp in mine + sends:
            cp.start()
        for t in range(T):
            for j, (px, py) in enumerate(chips):
                pltpu.make_async_remote_copy(
                    src_ref=ins[t].at[my_chip], dst_ref=outs[t].at[2 * px + py], send_sem=send_sems.at[t, j],
                    recv_sem=recv_sems.at[t, j], device_id=(px, py, c), device_id_type=MESH).wait_recv()
        for cp in sends:
            cp.wait_send()
        for cp in mine:
            cp.wait()

    return pl.pallas_call(
        body, name="grad_chip_exchange",
        out_shape=[jax.ShapeDtypeStruct(p.shape, p.dtype) for p in parts],
        in_specs=[ANY] * T, out_specs=[ANY] * T,
        scratch_shapes=[pltpu.SemaphoreType.DMA((T, 3)), pltpu.SemaphoreType.DMA((T, 3)), pltpu.SemaphoreType.DMA((T,))],
    )(*parts)


def _pair_share(reds):
    T = len(reds)

    def body(*refs):
        ins, outs = refs[:T], refs[T:2 * T]
        send_sems, recv_sems, local_sems = refs[2 * T:]
        x, y, c = lax.axis_index("x"), lax.axis_index("y"), lax.axis_index("c")
        mine = [pltpu.make_async_copy(ins[t], outs[t].at[c], local_sems.at[t]) for t in range(T)]
        sends = [pltpu.make_async_remote_copy(src_ref=ins[t], dst_ref=outs[t].at[c], send_sem=send_sems.at[t],
                                              recv_sem=recv_sems.at[t], device_id=(x, y, 1 - c), device_id_type=MESH)
                 for t in range(T)]
        for cp in mine + sends:
            cp.start()
        for t in range(T):
            pltpu.make_async_remote_copy(src_ref=ins[t], dst_ref=outs[t].at[1 - c], send_sem=send_sems.at[t],
                                         recv_sem=recv_sems.at[t], device_id=(x, y, 1 - c), device_id_type=MESH).wait_recv()
        for cp in sends:
            cp.wait_send()
        for cp in mine:
            cp.wait()

    return pl.pallas_call(
        body, name="grad_pair_share",
        out_shape=[jax.ShapeDtypeStruct((2,) + r.shape, r.dtype) for r in reds],
        in_specs=[ANY] * T, out_specs=[ANY] * T,
        scratch_shapes=[pltpu.SemaphoreType.DMA((T,)), pltpu.SemaphoreType.DMA((T,)), pltpu.SemaphoreType.DMA((T,))],
    )(*reds)


def _row_tile(hr):
    return next(r for r in (128, 64, 32, 16, 8) if hr % r == 0)


def _pair_add(g, got, c, name):
    _, _, hr, C = g.shape
    tr = _row_tile(hr)

    def body(c_ref, g_ref, r_ref, o_ref):
        o_ref[...] = g_ref[0] + r_ref[...]

    return pl.pallas_call(
        body, name=name,
        grid_spec=pltpu.PrefetchScalarGridSpec(
            num_scalar_prefetch=1, grid=(4, hr // tr),
            in_specs=[pl.BlockSpec((1, 1, tr, C), lambda k, i, c_ref: (c_ref[0], k, i, 0)),
                      pl.BlockSpec((1, tr, C), lambda k, i, c_ref: (k, i, 0))],
            out_specs=pl.BlockSpec((1, tr, C), lambda k, i, c_ref: (k, i, 0))),
        out_shape=jax.ShapeDtypeStruct(got.shape, F32),
        compiler_params=pltpu.CompilerParams(dimension_semantics=("parallel", "parallel"), vmem_limit_bytes=VMEM_LIMIT),
    )(jnp.reshape(c, (1,)).astype(jnp.int32), g, got)


def _sum_slots(parts, name):
    n, hr, C = parts.shape
    tr = _row_tile(hr)

    def body(p_ref, o_ref):
        acc = p_ref[0]
        for k in range(1, n):
            acc = acc + p_ref[k]
        o_ref[...] = acc

    return pl.pallas_call(
        body, name=name, grid=(hr // tr,),
        in_specs=[pl.BlockSpec((n, tr, C), lambda i: (0, i, 0))], out_specs=pl.BlockSpec((tr, C), lambda i: (i, 0)),
        out_shape=jax.ShapeDtypeStruct((hr, C), F32),
        compiler_params=pltpu.CompilerParams(dimension_semantics=("parallel",), vmem_limit_bytes=VMEM_LIMIT),
    )(parts)


PACK_ROWS, PACK_COLS = 16, 3072
_PACK = [("norm_mix_g", 0, 1, 2048), ("conv_w", 1, 4, 3072), ("conv_b", 5, 1, 3072), ("dt_bias", 6, 1, 32),
         ("a_log", 7, 1, 32), ("d_skip", 8, 1, 32), ("ssd_norm_g", 9, 1, 2048), ("i_bias", 10, 1, 8), ("f_bias", 11, 1, 8),
         ("mlstm_norm_g", 12, 1, 2048), ("norm_ffn_g", 13, 1, 2048), ("final_norm_g", 14, 1, 2048), ("loss", 15, 1, 1)]
_WEIGHTS = ["norm_mix_g", "w_in", "conv_w", "conv_b", "dt_bias", "a_log", "d_skip", "ssd_norm_g", "i_bias", "f_bias",
            "mlstm_norm_g", "w_out", "norm_ffn_g", "w_gate", "w_up", "w_down", "final_norm_g"]
_BIG = ["w_in", "w_out", "w_gate", "w_up", "w_down"]


def _pack(vals):
    rows = []
    for name, _, r, w in _PACK:
        v = vals.get(name)
        v = jnp.zeros((r, w), F32) if v is None else v.reshape(r, w).astype(F32)
        rows.append(jnp.concatenate([v, jnp.zeros((r, PACK_COLS - w), F32)], axis=1) if w < PACK_COLS else v)
    return jnp.concatenate(rows, axis=0)


def _unpack(p):
    return {name: p[r0:r0 + r, :w] for name, r0, r, w in _PACK}


def _to_halves_cols(dw, cols):
    R = dw.shape[0]
    return dw.reshape(2, R // 2, 4, cols).transpose(0, 2, 1, 3)


def _to_halves_rows(dw, rows):
    return dw.reshape(4, 2, rows // 2, dw.shape[1]).transpose(1, 0, 2, 3)


def kernel(x, norm_mix_g, w_in, conv_w, conv_b, dt_bias, a_log, d_skip, ssd_norm_g, i_bias, f_bias, mlstm_norm_g, w_out, norm_ffn_g, w_gate, w_up, w_down, final_norm_g, loss_target, m_norm_mix_g, m_w_in, m_conv_w, m_conv_b, m_dt_bias, m_a_log, m_d_skip, m_ssd_norm_g, m_i_bias, m_f_bias, m_mlstm_norm_g, m_w_out, m_norm_ffn_g, m_w_gate, m_w_up, m_w_down, m_final_norm_g, v_norm_mix_g, v_w_in, v_conv_w, v_conv_b, v_dt_bias, v_a_log, v_d_skip, v_ssd_norm_g, v_i_bias, v_f_bias, v_mlstm_norm_g, v_w_out, v_norm_ffn_g, v_w_gate, v_w_up, v_w_down, v_final_norm_g):
    weights = dict(norm_mix_g=norm_mix_g, w_in=w_in, conv_w=conv_w, conv_b=conv_b, dt_bias=dt_bias, a_log=a_log, d_skip=d_skip,
                   ssd_norm_g=ssd_norm_g, i_bias=i_bias, f_bias=f_bias, mlstm_norm_g=mlstm_norm_g, w_out=w_out,
                   norm_ffn_g=norm_ffn_g, w_gate=w_gate, w_up=w_up, w_down=w_down, final_norm_g=final_norm_g)
    mom1 = dict(norm_mix_g=m_norm_mix_g, w_in=m_w_in, conv_w=m_conv_w, conv_b=m_conv_b, dt_bias=m_dt_bias, a_log=m_a_log,
                d_skip=m_d_skip, ssd_norm_g=m_ssd_norm_g, i_bias=m_i_bias, f_bias=m_f_bias, mlstm_norm_g=m_mlstm_norm_g,
                w_out=m_w_out, norm_ffn_g=m_norm_ffn_g, w_gate=m_w_gate, w_up=m_w_up, w_down=m_w_down,
                final_norm_g=m_final_norm_g)
    mom2 = dict(norm_mix_g=v_norm_mix_g, w_in=v_w_in, conv_w=v_conv_w, conv_b=v_conv_b, dt_bias=v_dt_bias, a_log=v_a_log,
                d_skip=v_d_skip, ssd_norm_g=v_ssd_norm_g, i_bias=v_i_bias, f_bias=v_f_bias, mlstm_norm_g=v_mlstm_norm_g,
                w_out=v_w_out, norm_ffn_g=v_norm_ffn_g, w_gate=v_w_gate, w_up=v_w_up, w_down=v_w_down,
                final_norm_g=v_final_norm_g)
    xi, yi, ci = lax.axis_index("x"), lax.axis_index("y"), lax.axis_index("c")
    chip = 2 * xi + yi
    xs, tgt = x[0], loss_target[0]

    g_in, g_out, g_gate, g_up, g_down = _allgather_big([weights[n][0].astype(BF16) for n in _BIG])
    W_ssd, W_ml, W_small = _split_w_in(jnp.transpose(g_in, (1, 0, 2)).reshape(D_MODEL, IN_WIDTH))
    W_out = g_out.reshape(2 * SSD_WIDTH, D_MODEL)
    W_gate = jnp.transpose(g_gate, (1, 0, 2)).reshape(D_MODEL, D_FF)
    W_up = jnp.transpose(g_up, (1, 0, 2)).reshape(D_MODEL, D_FF)
    W_down = g_down.reshape(D_FF, D_MODEL)
    cw_all = _allgather_small(jnp.concatenate([conv_w[0], jnp.zeros((4, 768), F32)], axis=0), "allgather_conv_w")
    conv_w_full = jnp.concatenate([cw_all[2 * k, :SSD_CONV] for k in range(4)], axis=1)
    cwg, cbg = _conv_to_groups(conv_w_full), _conv_to_groups(conv_b)
    dtb_row, al_row = _small_row(dt_bias, 0), _small_row(a_log, 0)
    ib_row, fb_row = _small_row(i_bias, LANE_I), _small_row(f_bias, LANE_F)
    dskip_lane = jnp.repeat(d_skip, 64, axis=1)
    fng = final_norm_g[None]

    u1 = _rmsnorm_fwd(xs, norm_mix_g, "norm_mix_fwd")
    p_ssd = _matmul(u1, W_ssd, "nn", F32, 1024, 1024, 2048, "proj_ssd")
    p_ml = _matmul(u1, W_ml, "nn", F32, 1024, 1024, 2048, "proj_ml")
    p_small = _matmul(u1, W_small, "nn", F32, 1024, 128, 2048, "proj_small")
    y_ssd, ssd_st = _ssd_forward(p_ssd, p_small, cwg, cbg, dtb_row, al_row, dskip_lane, ssd_norm_g)
    y_ml, ml_c, ml_nm = _mlstm_forward(p_ml, p_small, ib_row, fb_row, mlstm_norm_g)
    y_mix = jnp.concatenate([y_ssd, y_ml], axis=1)
    h1 = _matmul(y_mix, W_out, "nn", F32, 1024, 1024, 2048, "out_proj", addend=xs)
    u2 = _rmsnorm_fwd(h1, norm_ffn_g, "norm_ffn_fwd")
    gate = _matmul(u2, W_gate, "nn", F32, 1024, FF_TILE, 2048, "ffn_gate")
    up = _matmul(u2, W_up, "nn", F32, 1024, FF_TILE, 2048, "ffn_up")
    act = _swiglu_fwd(gate, up)
    h2 = _matmul(act, W_down, "nn", F32, 1024, 1024, FF_TILE, "ffn_down", addend=h1)
    dh2, loss_row, d_fng = _loss_head(h2, tgt, fng)

    dact = _matmul(dh2, W_down, "nt", F32, 512, FF_TILE, 2048, "d_act")
    dW_down = _matmul(act, dh2, "tn", F32, FF_TILE, 1024, 1024, "dw_down")
    d_gate, d_up = _swiglu_bwd(dact, gate, up)
    du2 = _matmul(d_gate, W_gate, "nt", F32, 1024, 1024, FF_TILE, "du2_gate")
    du2 = _matmul(d_up, W_up, "nt", F32, 1024, 1024, FF_TILE, "du2_up", addend=du2)
    dW_gate = _matmul(u2, d_gate, "tn", F32, 1024, FF_TILE, 1024, "dw_gate")
    dW_up = _matmul(u2, d_up, "tn", F32, 1024, FF_TILE, 1024, "dw_up")
    dh1, d_ffn_g = _rmsnorm_bwd(du2, h1, norm_ffn_g, dh2, "norm_ffn_bwd")
    dy_mix = _matmul(dh1, W_out, "nt", F32, 512, 1024, 2048, "d_mix")
    dW_out = _matmul(y_mix, dh1, "tn", F32, 1024, 1024, 1024, "dw_out")
    d_ssd, d_small, d_cw, d_cb, d_dtb, d_alog, d_dskip, d_sng = _ssd_backward(
        p_ssd, p_small, cwg, cbg, dtb_row, al_row, dskip_lane, ssd_norm_g, ssd_st, dy_mix)
    d_ml, d_small, d_ib, d_fb, d_mng = _mlstm_backward(p_ml, p_small, ib_row, fb_row, mlstm_norm_g, ml_c, ml_nm, dy_mix, d_small)
    du1 = _matmul(d_ssd, W_ssd, "nt", F32, 1024, 1024, 1280, "du1_ssd")
    du1 = _matmul(d_ml, W_ml, "nt", F32, 1024, 1024, 1024, "du1_ml", addend=du1)
    du1 = _matmul(d_small, W_small, "nt", F32, 1024, 1024, 128, "du1_small", addend=du1)
    dW_ssd = _matmul(u1, d_ssd, "tn", F32, 1024, 1280, 1024, "dw_ssd")
    dW_ml = _matmul(u1, d_ml, "tn", F32, 1024, 1024, 1024, "dw_ml")
    dW_small = _matmul(u1, d_small, "tn", F32, 1024, 128, 1024, "dw_small")
    grad_x, d_mix_g = _rmsnorm_bwd(du1, xs, norm_mix_g, dh1, "norm_mix_bwd")
    dW_in = _merge_w_in(dW_ssd, dW_ml, dW_small)

    big = [_to_halves_cols(dW_in, 2828), _to_halves_rows(dW_out, 1024), _to_halves_cols(dW_gate, 1408),
           _to_halves_cols(dW_up, 1408), _to_halves_rows(dW_down, 1408)]
    got = _pair_exchange(big)
    parts = [_pair_add(g, r, ci, "grad_pair_add_" + n) for g, r, n in zip(big, got, _BIG)]
    slots = _chip_exchange(parts)
    reds = [_sum_slots(s, "grad_chip_sum_" + n) for s, n in zip(slots, _BIG)]
    fulls = _pair_share(reds)
    grads = {n: f.reshape(weights[n].shape[1:]) for n, f in zip(_BIG, fulls)}

    small = _pack(dict(norm_mix_g=d_mix_g, conv_w=_conv_from_groups(d_cw), conv_b=_conv_from_groups(d_cb),
                       dt_bias=d_dtb[:, 0:32], a_log=d_alog[:, 0:32], d_skip=d_dskip.reshape(SSD_HEADS, 64).sum(axis=1),
                       ssd_norm_g=d_sng, i_bias=d_ib[:, LANE_I:LANE_I + 8], f_bias=d_fb[:, LANE_F:LANE_F + 8],
                       mlstm_norm_g=d_mng, norm_ffn_g=d_ffn_g, final_norm_g=d_fng, loss=loss_row[:, 0:1]))
    total = _sum_slots(_allgather_small(small, "allgather_small_grads"), "small_grad_sum")
    small_w = {n: weights[n] for n in _WEIGHTS if n not in _BIG and n != "conv_w"}
    sd, sm, sv = _adamw(_pack(small_w), total, _pack({n: mom1[n] for n in small_w}), _pack({n: mom2[n] for n in small_w}),
                        PACK_ROWS, "adamw_small")
    tot, sd, sm, sv = _unpack(total), _unpack(sd), _unpack(sm), _unpack(sv)
    for n in small_w:
        grads[n] = tot[n].reshape(weights[n].shape)
    grads["conv_w"] = lax.dynamic_slice_in_dim(tot["conv_w"], chip * 768, 768, axis=1)
    loss = tot["loss"][0, 0]

    delta, new_m, new_v = {}, {}, {}
    for n in _BIG + ["conv_w"]:
        w2 = weights[n][0]
        d, nm, nv = _adamw(w2, grads[n], mom1[n][0], mom2[n][0], _row_tile(w2.shape[0]) if n != "conv_w" else SSD_CONV,
                           "adamw_" + n)
        delta[n], new_m[n], new_v[n] = d[None], nm[None], nv[None]
        grads[n] = grads[n][None]
    for n in small_w:
        delta[n], new_m[n], new_v[n] = (t[n].reshape(weights[n].shape) for t in (sd, sm, sv))
    return (loss, grad_x[None], *[grads[n] for n in _WEIGHTS], *[delta[n] for n in _WEIGHTS],
            *[new_m[n] for n in _WEIGHTS], *[new_v[n] for n in _WEIGHTS])
```

```python
import functools

import jax
import jax.numpy as jnp
import numpy as np
from jax import lax
from jax.experimental import pallas as pl
from jax.experimental.pallas import tpu as pltpu

F32 = jnp.float32
BF16 = jnp.bfloat16

D_MODEL = 2048
SSD_HEADS = 32
SSD_GROUPS = 4
SSD_STATE = 128
SSD_WIDTH = 2048
SSD_CONV = 4
SSD_GROUP_COLS = 1280
SSD_XBC_COLS = 768
ML_HEADS = 8
ML_HEAD_COLS = 768
ML_DK = 128
ML_DV = 256
CHUNK = 128
SMALL_COLS = 128
LANE_I = 32
LANE_F = 40
D_FF = 5632
GATE_SOFTCAP = 15.0
EPS = 1e-6
ADAM_LR, ADAM_B1, ADAM_B2, ADAM_EPS, ADAM_WD, ADAM_STEP = 0.001, 0.9, 0.999, 1e-8, 0.01, 10
MESH = pl.DeviceIdType.MESH
VMEM_LIMIT = 56 * 1024 * 1024


def _dg(a, b, ca, cb):
    return lax.dot_general(a.astype(BF16), b.astype(BF16), (((ca,), (cb,)), ((), ())), preferred_element_type=F32)


@jax.custom_vjp
def _mm(a, b):
    return _dg(a, b, 1, 0)


_mm.defvjp(lambda a, b: (_dg(a, b, 1, 0), (a, b)),
           lambda r, g: (_dg(g, r[1], 1, 1), _dg(r[0], g, 0, 0)))


@jax.custom_vjp
def _mm_nt(a, b):
    return _dg(a, b, 1, 1)


_mm_nt.defvjp(lambda a, b: (_dg(a, b, 1, 1), (a, b)),
              lambda r, g: (_dg(g, r[1], 1, 0), _dg(g, r[0], 0, 0)))


@jax.custom_vjp
def _mm_tn(a, b):
    return _dg(a, b, 0, 0)


_mm_tn.defvjp(lambda a, b: (_dg(a, b, 0, 0), (a, b)),
              lambda r, g: (_dg(r[1], g, 1, 1), _dg(r[0], g, 1, 0)))


def _tri(lower):
    r = lax.broadcasted_iota(jnp.int32, (CHUNK, CHUNK), 0)
    c = lax.broadcasted_iota(jnp.int32, (CHUNK, CHUNK), 1)
    return ((r >= c) if lower else (r <= c)).astype(F32)


def _dg32(t, x):
    return lax.dot_general(t, x, (((1,), (0,)), ((), ())), precision=lax.Precision.HIGHEST, preferred_element_type=F32)


@jax.custom_vjp
def _cumsum(x):
    return _dg32(_tri(True), x)


_cumsum.defvjp(lambda x: (_dg32(_tri(True), x), None), lambda r, g: (_dg32(_tri(False), g),))


_sigmoid = jax.nn.sigmoid


def _silu(x):
    return x * _sigmoid(x)


def _softplus(x):
    return jnp.maximum(x, 0.0) + jnp.log(1.0 + jnp.exp(-jnp.abs(x)))


def _lane_col(m, lane, h):
    return jnp.sum(jnp.where(lane == h, m, 0.0), axis=1, keepdims=True)


def _ssd_math(cx, cB, cC, z, smallb, S_in, dtb_row, alog_row, dskip, ng, g):
    lane = lax.broadcasted_iota(jnp.int32, (1, CHUNK), 1)
    row_i = lax.broadcasted_iota(jnp.int32, (CHUNK, CHUNK), 0)
    col_i = lax.broadcasted_iota(jnp.int32, (CHUNK, CHUNK), 1)
    causal = row_i >= col_i
    half = lane < 64
    rhalf = lax.broadcasted_iota(jnp.int32, (CHUNK, 1), 0) < 64
    xs, Bm, Cm = _silu(cx), _silu(cB), _silu(cC)
    dt_all = _softplus(smallb + dtb_row)
    a_all = dt_all * (-jnp.exp(alog_row))
    acum_all = _cumsum(a_all)
    alast_all = jnp.sum(a_all, axis=0, keepdims=True)
    cb = _mm_nt(Cm, Bm)

    def lmat(ac):
        acb = jnp.broadcast_to(ac, (CHUNK, CHUNK))
        return jnp.exp(jnp.where(causal, acb - acb.T, -jnp.inf))

    ys, S_out = [], []
    for j in range(4):
        h0 = 8 * g + 2 * j
        ac0, ac1 = _lane_col(acum_all, lane, h0), _lane_col(acum_all, lane, h0 + 1)
        dt0, dt1 = _lane_col(dt_all, lane, h0), _lane_col(dt_all, lane, h0 + 1)
        al0, al1 = _lane_col(alast_all, lane, h0), _lane_col(alast_all, lane, h0 + 1)
        Xp = xs[:, 128 * j:128 * (j + 1)]
        Xd = Xp * jnp.where(half, dt0, dt1)
        ac_sel = jnp.where(half, ac0, ac1)
        al_sel = jnp.where(half, al0, al1)
        Yd = jnp.where(half, _mm(cb * lmat(ac0), Xd), _mm(cb * lmat(ac1), Xd))
        Yoff = _mm_nt(Cm, S_in[j]) * jnp.exp(ac_sel)
        ys.append(Yd + Yoff + dskip[:, 128 * j:128 * (j + 1)] * Xp)
        S_new = _mm_tn(Xd * jnp.exp(al_sel - ac_sel), Bm)
        S_out.append(S_in[j] * jnp.exp(jnp.where(rhalf, al0, al1)) + S_new)
    y = jnp.concatenate(ys, axis=1)
    y = y * _silu(z)
    y = y * lax.rsqrt(jnp.mean(y * y, axis=1, keepdims=True) + EPS) * ng
    return y, tuple(S_out)


def _ssd_conv(blk_ref, halo_ref, cw_ref, cb_ref, ext, first):
    halo = halo_ref[:, 512:SSD_GROUP_COLS]
    ext[0:8, :] = jnp.where(first, jnp.zeros_like(halo), halo)
    ext[8:8 + CHUNK, :] = blk_ref[:, 512:SSD_GROUP_COLS]
    conv = jnp.broadcast_to(cb_ref[0], (CHUNK, SSD_XBC_COLS))
    for tap in range(SSD_CONV):
        conv = conv + cw_ref[0, tap:tap + 1, :] * ext[pl.ds(5 + tap, CHUNK), :]
    return conv


def _ssd_specs(nc, rev):
    cc = (lambda c: nc - 1 - c) if rev else (lambda c: c)
    return [
        pl.BlockSpec((CHUNK, SSD_GROUP_COLS), lambda c, g: (cc(c), g)),
        pl.BlockSpec((8, SSD_GROUP_COLS), lambda c, g: (jnp.maximum(cc(c) * (CHUNK // 8) - 1, 0), g)),
        pl.BlockSpec((CHUNK, SMALL_COLS), lambda c, g: (cc(c), 0)),
        pl.BlockSpec((1, SSD_CONV, SSD_XBC_COLS), lambda c, g: (g, 0, 0)),
        pl.BlockSpec((1, 1, SSD_XBC_COLS), lambda c, g: (g, 0, 0)),
        pl.BlockSpec((1, SMALL_COLS), lambda c, g: (0, 0)),
        pl.BlockSpec((1, SMALL_COLS), lambda c, g: (0, 0)),
        pl.BlockSpec((1, 512), lambda c, g: (0, g)),
        pl.BlockSpec((1, 512), lambda c, g: (0, g)),
    ]


def _ssd_forward(proj, small, cw, cb, dtb, alog, dskip, ng):
    L = proj.shape[0]
    nc = L // CHUNK

    def body(blk_ref, halo_ref, small_ref, cw_ref, cb_ref, dtb_ref, alog_ref, dskip_ref, ng_ref, y_ref, st_ref, carry, ext):
        c, g = pl.program_id(0), pl.program_id(1)

        @pl.when(c == 0)
        def _():
            carry[g] = jnp.zeros((4, CHUNK, CHUNK), F32)

        conv = _ssd_conv(blk_ref, halo_ref, cw_ref, cb_ref, ext, c == 0)
        S_in = tuple(carry[g, j] for j in range(4))
        st_ref[0, 0] = carry[g]
        y, S_out = _ssd_math(conv[:, 0:512], conv[:, 512:640], conv[:, 640:768], blk_ref[:, 0:512], small_ref[...],
                             S_in, dtb_ref[...], alog_ref[...], dskip_ref[...], ng_ref[...], g)
        y_ref[...] = y.astype(BF16)
        for j in range(4):
            carry[g, j] = S_out[j]

    return pl.pallas_call(
        body, name="ssd_fwd", grid=(nc, SSD_GROUPS),
        in_specs=_ssd_specs(nc, False),
        out_specs=[pl.BlockSpec((CHUNK, 512), lambda c, g: (c, g)),
                   pl.BlockSpec((1, 1, 4, CHUNK, CHUNK), lambda c, g: (c, g, 0, 0, 0))],
        out_shape=[jax.ShapeDtypeStruct((L, SSD_WIDTH), BF16), jax.ShapeDtypeStruct((nc, SSD_GROUPS, 4, CHUNK, CHUNK), F32)],
        scratch_shapes=[pltpu.VMEM((SSD_GROUPS, 4, CHUNK, CHUNK), F32), pltpu.VMEM((8 + CHUNK, SSD_XBC_COLS), F32)],
        compiler_params=pltpu.CompilerParams(dimension_semantics=("arbitrary", "arbitrary"), vmem_limit_bytes=VMEM_LIMIT),
    )(proj, proj, small, cw, cb, dtb, alog, dskip, ng)


def _ssd_backward(proj, small, cw, cb, dtb, alog, dskip, ng, states, dy):
    L = proj.shape[0]
    nc = L // CHUNK

    def body(blk_ref, halo_ref, small_ref, cw_ref, cb_ref, dtb_ref, alog_ref, dskip_ref, ng_ref, st_ref, dy_ref,
             dproj_ref, dsmall_ref, dcw_ref, dcb_ref, ddtb_ref, dalog_ref, ddskip_ref, dng_ref, dcarry, nxt, ext, dext):
        s, g = pl.program_id(0), pl.program_id(1)
        c = nc - 1 - s

        @pl.when(s == 0)
        def _():
            dcarry[g] = jnp.zeros((4, CHUNK, CHUNK), F32)
            nxt[g] = jnp.zeros((8, SSD_XBC_COLS), F32)
            dcw_ref[g] = jnp.zeros((SSD_CONV, SSD_XBC_COLS), F32)
            dcb_ref[g] = jnp.zeros((1, SSD_XBC_COLS), F32)
            ddskip_ref[g] = jnp.zeros((1, 512), F32)
            dng_ref[g] = jnp.zeros((1, 512), F32)

        @pl.when((s == 0) & (g == 0))
        def _():
            ddtb_ref[...] = jnp.zeros((1, SMALL_COLS), F32)
            dalog_ref[...] = jnp.zeros((1, SMALL_COLS), F32)

        conv = _ssd_conv(blk_ref, halo_ref, cw_ref, cb_ref, ext, c == 0)
        S_in = tuple(st_ref[0, 0, j] for j in range(4))
        dS_out = tuple(dcarry[g, j] for j in range(4))
        _, vjp = jax.vjp(functools.partial(_ssd_math, g=g), conv[:, 0:512], conv[:, 512:640], conv[:, 640:768],
                         blk_ref[:, 0:512], small_ref[...], S_in, dtb_ref[...], alog_ref[...], dskip_ref[...], ng_ref[...])
        d_cx, d_cB, d_cC, d_z, d_small, dS_in, d_dtb, d_alog, d_dskip, d_ng = vjp((dy_ref[...].astype(F32), dS_out))
        for j in range(4):
            dcarry[g, j] = dS_in[j]
        dext[0:8, :] = jnp.zeros((8, SSD_XBC_COLS), F32)
        dext[8:8 + CHUNK, 0:512] = d_cx
        dext[8:8 + CHUNK, 512:640] = d_cB
        dext[8:8 + CHUNK, 640:768] = d_cC
        dext[8 + CHUNK:16 + CHUNK, :] = nxt[g]
        nxt[g] = dext[8:16, :]
        dconv = dext[8:8 + CHUNK, :]
        d_xbc = jnp.zeros((CHUNK, SSD_XBC_COLS), F32)
        for tap in range(SSD_CONV):
            d_xbc = d_xbc + cw_ref[0, tap:tap + 1, :] * dext[pl.ds(8 + 3 - tap, CHUNK), :]
            dcw_ref[g, tap:tap + 1, :] += jnp.sum(dconv * ext[pl.ds(5 + tap, CHUNK), :], axis=0, keepdims=True)
        dcb_ref[g] += jnp.sum(dconv, axis=0, keepdims=True)
        dproj_ref[:, 0:512] = d_z.astype(BF16)
        dproj_ref[:, 512:SSD_GROUP_COLS] = d_xbc.astype(BF16)

        @pl.when(g == 0)
        def _():
            dsmall_ref[...] = d_small

        @pl.when(g != 0)
        def _():
            dsmall_ref[...] += d_small

        ddtb_ref[...] += d_dtb
        dalog_ref[...] += d_alog
        ddskip_ref[g] += d_dskip
        dng_ref[g] += d_ng

    whole = lambda shape: pl.BlockSpec(shape, lambda s, g: (0,) * len(shape))
    return pl.pallas_call(
        body, name="ssd_bwd", grid=(nc, SSD_GROUPS),
        in_specs=_ssd_specs(nc, True) + [
            pl.BlockSpec((1, 1, 4, CHUNK, CHUNK), lambda s, g: (nc - 1 - s, g, 0, 0, 0)),
            pl.BlockSpec((CHUNK, 512), lambda s, g: (nc - 1 - s, g))],
        out_specs=[pl.BlockSpec((CHUNK, SSD_GROUP_COLS), lambda s, g: (nc - 1 - s, g)),
                   pl.BlockSpec((CHUNK, SMALL_COLS), lambda s, g: (nc - 1 - s, 0)),
                   whole((SSD_GROUPS, SSD_CONV, SSD_XBC_COLS)), whole((SSD_GROUPS, 1, SSD_XBC_COLS)),
                   whole((1, SMALL_COLS)), whole((1, SMALL_COLS)),
                   whole((SSD_GROUPS, 1, 512)), whole((SSD_GROUPS, 1, 512))],
        out_shape=[jax.ShapeDtypeStruct((L, SSD_GROUPS * SSD_GROUP_COLS), BF16), jax.ShapeDtypeStruct((L, SMALL_COLS), F32),
                   jax.ShapeDtypeStruct((SSD_GROUPS, SSD_CONV, SSD_XBC_COLS), F32),
                   jax.ShapeDtypeStruct((SSD_GROUPS, 1, SSD_XBC_COLS), F32),
                   jax.ShapeDtypeStruct((1, SMALL_COLS), F32), jax.ShapeDtypeStruct((1, SMALL_COLS), F32),
                   jax.ShapeDtypeStruct((SSD_GROUPS, 1, 512), F32), jax.ShapeDtypeStruct((SSD_GROUPS, 1, 512), F32)],
        scratch_shapes=[pltpu.VMEM((SSD_GROUPS, 4, CHUNK, CHUNK), F32), pltpu.VMEM((SSD_GROUPS, 8, SSD_XBC_COLS), F32),
                        pltpu.VMEM((8 + CHUNK, SSD_XBC_COLS), F32), pltpu.VMEM((16 + CHUNK, SSD_XBC_COLS), F32)],
        compiler_params=pltpu.CompilerParams(dimension_semantics=("arbitrary", "arbitrary"), vmem_limit_bytes=VMEM_LIMIT),
    )(proj, proj, small, cw, cb, dtb, alog, dskip, ng, states, dy)


def _mlstm_math(q, k, v, o_raw, smallb, C_in, n_in, m_in_row, ib_row, fb_row, ng, h):
    lane = lax.broadcasted_iota(jnp.int32, (1, CHUNK), 1)
    row_i = lax.broadcasted_iota(jnp.int32, (CHUNK, CHUNK), 0)
    col_i = lax.broadcasted_iota(jnp.int32, (CHUNK, CHUNK), 1)
    causal = row_i >= col_i
    kk = k * (ML_DK ** -0.5)
    li_all = GATE_SOFTCAP * jnp.tanh((smallb + ib_row) / GATE_SOFTCAP)
    lf_all = -_softplus(-(GATE_SOFTCAP * jnp.tanh((smallb + fb_row) / GATE_SOFTCAP)))
    bcum_all = _cumsum(lf_all)
    li = _lane_col(li_all, lane, LANE_I + h)
    lf = _lane_col(lf_all, lane, LANE_F + h)
    bc = _lane_col(bcum_all, lane, LANE_F + h)
    b_last = jnp.sum(lf, axis=0, keepdims=True)
    m_in = _lane_col(m_in_row, lane, 0)
    a = b_last - bc + li
    m_loc = jnp.max(a, axis=0, keepdims=True)
    w = jnp.exp(a - m_loc)
    C_loc = _mm_tn(w * v, kk)
    n_loc = jnp.sum(w * kk, axis=0, keepdims=True)
    m_new = jnp.maximum(b_last + m_in, m_loc)
    s_old = jnp.exp(b_last + m_in - m_new)
    s_new = jnp.exp(m_loc - m_new)
    C_out = s_old * C_in + s_new * C_loc
    n_out = s_old * n_in + s_new * n_loc
    bc_b = jnp.broadcast_to(bc, (CHUNK, CHUNK))
    li_b = jnp.broadcast_to(li, (CHUNK, CHUNK))
    D = jnp.where(causal, bc_b - bc_b.T + li_b.T, -jnp.inf)
    m_intra = jnp.max(D, axis=1, keepdims=True)
    inter_log = bc + m_in
    m_t = jnp.maximum(inter_log, m_intra)
    S = _mm_nt(q, kk) * jnp.exp(D - m_t)
    w_inter = jnp.exp(inter_log - m_t)
    num = _mm(S, v) + w_inter * _mm_nt(q, C_in)
    nq = jnp.sum(S, axis=1, keepdims=True) + w_inter * jnp.sum(q * n_in, axis=1, keepdims=True)
    den = jnp.maximum(jnp.abs(nq), jnp.exp(-m_t))
    hh = num / den
    hh = hh * lax.rsqrt(jnp.mean(hh * hh, axis=1, keepdims=True) + EPS)
    hh = hh * ng * _sigmoid(o_raw)
    return hh, C_out, n_out, jnp.broadcast_to(m_new, (1, CHUNK))


def _ml_specs(nc, rev):
    cc = (lambda c: nc - 1 - c) if rev else (lambda c: c)
    return [
        pl.BlockSpec((CHUNK, ML_HEAD_COLS), lambda c, h: (cc(c), h)),
        pl.BlockSpec((CHUNK, SMALL_COLS), lambda c, h: (cc(c), 0)),
        pl.BlockSpec((1, SMALL_COLS), lambda c, h: (0, 0)),
        pl.BlockSpec((1, SMALL_COLS), lambda c, h: (0, 0)),
        pl.BlockSpec((1, ML_DV), lambda c, h: (0, h)),
    ]


def _mlstm_forward(proj, small, ib, fb, ng):
    L = proj.shape[0]
    nc = L // CHUNK

    def body(blk_ref, small_ref, ib_ref, fb_ref, ng_ref, y_ref, cst_ref, nm_ref, c_carry, nm_carry):
        c, h = pl.program_id(0), pl.program_id(1)

        @pl.when(c == 0)
        def _():
            c_carry[h] = jnp.zeros((ML_DV, ML_DK), F32)
            nm_carry[h] = jnp.zeros((2, ML_DK), F32)

        cst_ref[0, 0] = c_carry[h]
        nm_ref[0, 0] = nm_carry[h]
        hh, C_out, n_out, m_out = _mlstm_math(
            blk_ref[:, 0:128], blk_ref[:, 128:256], blk_ref[:, 256:512], blk_ref[:, 512:768], small_ref[...],
            c_carry[h], nm_carry[h, 0:1, :], nm_carry[h, 1:2, :], ib_ref[...], fb_ref[...], ng_ref[...], h)
        y_ref[...] = hh.astype(BF16)
        c_carry[h] = C_out
        nm_carry[h, 0:1, :] = n_out
        nm_carry[h, 1:2, :] = m_out

    return pl.pallas_call(
        body, name="mlstm_fwd", grid=(nc, ML_HEADS),
        in_specs=_ml_specs(nc, False),
        out_specs=[pl.BlockSpec((CHUNK, ML_DV), lambda c, h: (c, h)),
                   pl.BlockSpec((1, 1, ML_DV, ML_DK), lambda c, h: (c, h, 0, 0)),
                   pl.BlockSpec((1, 1, 2, ML_DK), lambda c, h: (c, h, 0, 0))],
        out_shape=[jax.ShapeDtypeStruct((L, ML_HEADS * ML_DV), BF16),
                   jax.ShapeDtypeStruct((nc, ML_HEADS, ML_DV, ML_DK), F32),
                   jax.ShapeDtypeStruct((nc, ML_HEADS, 2, ML_DK), F32)],
        scratch_shapes=[pltpu.VMEM((ML_HEADS, ML_DV, ML_DK), F32), pltpu.VMEM((ML_HEADS, 2, ML_DK), F32)],
        compiler_params=pltpu.CompilerParams(dimension_semantics=("arbitrary", "arbitrary"), vmem_limit_bytes=VMEM_LIMIT),
    )(proj, small, ib, fb, ng)


def _mlstm_backward(proj, small, ib, fb, ng, cst, nmst, dy, dsmall_in):
    L = proj.shape[0]
    nc = L // CHUNK

    def body(blk_ref, small_ref, ib_ref, fb_ref, ng_ref, cst_ref, nm_ref, dy_ref, dsin_ref,
             dproj_ref, dsmall_ref, dib_ref, dfb_ref, dng_ref, dc_carry, dnm_carry):
        s, h = pl.program_id(0), pl.program_id(1)

        @pl.when(s == 0)
        def _():
            dc_carry[h] = jnp.zeros((ML_DV, ML_DK), F32)
            dnm_carry[h] = jnp.zeros((2, ML_DK), F32)
            dng_ref[h] = jnp.zeros((1, ML_DV), F32)

        @pl.when((s == 0) & (h == 0))
        def _():
            dib_ref[...] = jnp.zeros((1, SMALL_COLS), F32)
            dfb_ref[...] = jnp.zeros((1, SMALL_COLS), F32)

        _, vjp = jax.vjp(functools.partial(_mlstm_math, h=h),
                         blk_ref[:, 0:128], blk_ref[:, 128:256], blk_ref[:, 256:512], blk_ref[:, 512:768], small_ref[...],
                         cst_ref[0, 0], nm_ref[0, 0, 0:1, :], nm_ref[0, 0, 1:2, :], ib_ref[...], fb_ref[...], ng_ref[...])
        dq, dk, dv, do, d_small, dC, dn, dm, d_ib, d_fb, d_ng = vjp(
            (dy_ref[...].astype(F32), dc_carry[h], dnm_carry[h, 0:1, :], dnm_carry[h, 1:2, :]))
        dc_carry[h] = dC
        dnm_carry[h, 0:1, :] = dn
        dnm_carry[h, 1:2, :] = dm
        dproj_ref[:, 0:128] = dq.astype(BF16)
        dproj_ref[:, 128:256] = dk.astype(BF16)
        dproj_ref[:, 256:512] = dv.astype(BF16)
        dproj_ref[:, 512:768] = do.astype(BF16)

        @pl.when(h == 0)
        def _():
            dsmall_ref[...] = dsin_ref[...] + d_small

        @pl.when(h != 0)
        def _():
            dsmall_ref[...] += d_small

        dib_ref[...] += d_ib
        dfb_ref[...] += d_fb
        dng_ref[h] += d_ng

    whole = lambda shape: pl.BlockSpec(shape, lambda s, h: (0,) * len(shape))
    return pl.pallas_call(
        body, name="mlstm_bwd", grid=(nc, ML_HEADS),
        in_specs=_ml_specs(nc, True) + [
            pl.BlockSpec((1, 1, ML_DV, ML_DK), lambda s, h: (nc - 1 - s, h, 0, 0)),
            pl.BlockSpec((1, 1, 2, ML_DK), lambda s, h: (nc - 1 - s, h, 0, 0)),
            pl.BlockSpec((CHUNK, ML_DV), lambda s, h: (nc - 1 - s, SSD_WIDTH // ML_DV + h)),
            pl.BlockSpec((CHUNK, SMALL_COLS), lambda s, h: (nc - 1 - s, 0))],
        out_specs=[pl.BlockSpec((CHUNK, ML_HEAD_COLS), lambda s, h: (nc - 1 - s, h)),
                   pl.BlockSpec((CHUNK, SMALL_COLS), lambda s, h: (nc - 1 - s, 0)),
                   whole((1, SMALL_COLS)), whole((1, SMALL_COLS)), whole((ML_HEADS, 1, ML_DV))],
        out_shape=[jax.ShapeDtypeStruct((L, ML_HEADS * ML_HEAD_COLS), BF16), jax.ShapeDtypeStruct((L, SMALL_COLS), F32),
                   jax.ShapeDtypeStruct((1, SMALL_COLS), F32), jax.ShapeDtypeStruct((1, SMALL_COLS), F32),
                   jax.ShapeDtypeStruct((ML_HEADS, 1, ML_DV), F32)],
        scratch_shapes=[pltpu.VMEM((ML_HEADS, ML_DV, ML_DK), F32), pltpu.VMEM((ML_HEADS, 2, ML_DK), F32)],
        compiler_params=pltpu.CompilerParams(dimension_semantics=("arbitrary", "arbitrary"), vmem_limit_bytes=VMEM_LIMIT),
    )(proj, small, ib, fb, ng, cst, nmst, dy, dsmall_in)


_OFF_Z, _OFF_X, _OFF_B, _OFF_C, _OFF_DT = 0, 2048, 4096, 4608, 5120
_OFF_Q, _OFF_K, _OFF_V, _OFF_O, _OFF_I, _OFF_F, IN_WIDTH = 5152, 6176, 7200, 9248, 11296, 11304, 11312


def _ssd_segments():
    segs = []
    for g in range(SSD_GROUPS):
        segs += [(_OFF_Z + 512 * g, 512), (_OFF_X + 512 * g, 512), (_OFF_B + 128 * g, 128), (_OFF_C + 128 * g, 128)]
    return segs


def _ml_segments():
    segs = []
    for h in range(ML_HEADS):
        segs += [(_OFF_Q + 128 * h, 128), (_OFF_K + 128 * h, 128), (_OFF_V + 256 * h, 256), (_OFF_O + 256 * h, 256)]
    return segs


_SMALL_SEGMENTS = [(_OFF_DT, 32), (_OFF_I, 8), (_OFF_F, 8)]


def _take_cols(w, segs):
    return jnp.concatenate([w[:, s:s + n] for s, n in segs], axis=1)


def _split_w_in(w):
    small = _take_cols(w, _SMALL_SEGMENTS)
    small = jnp.concatenate([small, jnp.zeros((w.shape[0], SMALL_COLS - small.shape[1]), w.dtype)], axis=1)
    return _take_cols(w, _ssd_segments()), _take_cols(w, _ml_segments()), small


def _merge_w_in(d_ssd, d_ml, d_small):
    pieces = []
    for arr, segs in ((d_ssd, _ssd_segments()), (d_ml, _ml_segments()), (d_small, _SMALL_SEGMENTS)):
        pos = 0
        for s, n in segs:
            pieces.append((s, arr[:, pos:pos + n]))
            pos += n
    pieces.sort(key=lambda t: t[0])
    return jnp.concatenate([p for _, p in pieces], axis=1)


def _conv_to_groups(cw):
    return jnp.stack([jnp.concatenate([cw[:, 512 * g:512 * (g + 1)], cw[:, 2048 + 128 * g:2048 + 128 * (g + 1)],
                                       cw[:, 2560 + 128 * g:2560 + 128 * (g + 1)]], axis=1) for g in range(SSD_GROUPS)])


def _conv_from_groups(d):
    return jnp.concatenate([d[g, :, 0:512] for g in range(SSD_GROUPS)] + [d[g, :, 512:640] for g in range(SSD_GROUPS)]
                           + [d[g, :, 640:768] for g in range(SSD_GROUPS)], axis=1)


def _small_row(vec, lane0):
    n = vec.shape[1]
    return jnp.concatenate([jnp.zeros((1, lane0), F32), vec, jnp.zeros((1, SMALL_COLS - lane0 - n), F32)], axis=1)


def _matmul(a, b, mode, out_dtype, tm, tn, tk, name, addend=None, layout=None):
    M, Kd = (a.shape[1], a.shape[0]) if mode == "tn" else a.shape
    N = b.shape[0] if mode == "nt" else b.shape[1]
    if layout == "cols4":
        tm, tn = M // 2, N // 4
    elif layout == "rows4":
        tm = M // 4
    tm, tn, tk = min(tm, M), min(tn, N), min(tk, Kd)
    if mode == "nn":
        a_spec = pl.BlockSpec((tm, tk), lambda i, j, k: (i, k))
        b_spec = pl.BlockSpec((tk, tn), lambda i, j, k: (k, j))
        dims = (((1,), (0,)), ((), ()))
    elif mode == "nt":
        a_spec = pl.BlockSpec((tm, tk), lambda i, j, k: (i, k))
        b_spec = pl.BlockSpec((tn, tk), lambda i, j, k: (j, k))
        dims = (((1,), (1,)), ((), ()))
    else:
        a_spec = pl.BlockSpec((tk, tm), lambda i, j, k: (k, i))
        b_spec = pl.BlockSpec((tk, tn), lambda i, j, k: (k, j))
        dims = (((0,), (0,)), ((), ()))
    assert M % tm == 0 and N % tn == 0 and Kd % tk == 0, (name, M, N, Kd, tm, tn, tk)
    nk = Kd // tk
    has_add = addend is not None

    def body(*refs):
        a_ref, b_ref = refs[0], refs[1]
        add_ref = refs[2] if has_add else None
        o_ref, acc_ref = refs[2 + has_add], refs[3 + has_add]
        k = pl.program_id(2)
        part = lax.dot_general(a_ref[...].astype(BF16), b_ref[...].astype(BF16), dims, preferred_element_type=F32)

        @pl.when(k == 0)
        def _():
            acc_ref[...] = part

        @pl.when(k != 0)
        def _():
            acc_ref[...] += part

        @pl.when(k == nk - 1)
        def _():
            r = acc_ref[...]
            if has_add:
                r = r + add_ref[...].astype(F32)
            if layout == "cols4":
                o_ref[0, 0] = r.astype(out_dtype)
            elif layout == "rows4":
                o_ref[0, 0] = r[:tm // 2].astype(out_dtype)
                o_ref[1, 0] = r[tm // 2:].astype(out_dtype)
            else:
                o_ref[...] = r.astype(out_dtype)

    o_spec = pl.BlockSpec((tm, tn), lambda i, j, k: (i, j))
    out_spec, out_dims = o_spec, (M, N)
    if layout == "cols4":
        out_spec, out_dims = pl.BlockSpec((1, 1, tm, tn), lambda i, j, k: (i, j, 0, 0)), (2, 4, tm, tn)
    elif layout == "rows4":
        out_spec, out_dims = pl.BlockSpec((2, 1, tm // 2, tn), lambda i, j, k: (0, i, 0, j)), (2, 4, tm // 2, N)
    return pl.pallas_call(
        body, name=name, grid=(M // tm, N // tn, nk),
        in_specs=[a_spec, b_spec] + ([o_spec] if has_add else []), out_specs=out_spec,
        out_shape=jax.ShapeDtypeStruct(out_dims, out_dtype),
        scratch_shapes=[pltpu.VMEM((tm, tn), F32)],
        compiler_params=pltpu.CompilerParams(dimension_semantics=("parallel", "parallel", "arbitrary"),
                                             vmem_limit_bytes=VMEM_LIMIT),
    )(*((a, b) + ((addend,) if has_add else ())))


ROW_TILE = 256


def _rmsnorm_fwd(x, g, name):
    L, D = x.shape

    def body(x_ref, g_ref, u_ref):
        xv = x_ref[...]
        r = lax.rsqrt(jnp.mean(xv * xv, axis=1, keepdims=True) + EPS)
        u_ref[...] = (xv * r * g_ref[...]).astype(BF16)

    return pl.pallas_call(
        body, name=name, grid=(L // ROW_TILE,),
        in_specs=[pl.BlockSpec((ROW_TILE, D), lambda i: (i, 0)), pl.BlockSpec((1, D), lambda i: (0, 0))],
        out_specs=pl.BlockSpec((ROW_TILE, D), lambda i: (i, 0)),
        out_shape=jax.ShapeDtypeStruct((L, D), BF16),
        compiler_params=pltpu.CompilerParams(dimension_semantics=("parallel",), vmem_limit_bytes=VMEM_LIMIT),
    )(x, g)


def _rmsnorm_bwd(du, x, g, dres, name):
    L, D = x.shape

    def body(du_ref, x_ref, g_ref, dres_ref, dx_ref, dxb_ref, dg_ref):
        i = pl.program_id(0)
        xv, duv = x_ref[...], du_ref[...]
        r = lax.rsqrt(jnp.mean(xv * xv, axis=1, keepdims=True) + EPS)
        t = duv * g_ref[...]
        dx = dres_ref[...] + r * t - xv * (r * r * r) * jnp.mean(t * xv, axis=1, keepdims=True)
        dx_ref[...] = dx
        dxb_ref[...] = dx.astype(BF16)
        dg = jnp.sum(duv * xv * r, axis=0, keepdims=True)

        @pl.when(i == 0)
        def _():
            dg_ref[...] = dg

        @pl.when(i != 0)
        def _():
            dg_ref[...] += dg

    row = pl.BlockSpec((ROW_TILE, D), lambda i: (i, 0))
    vec = pl.BlockSpec((1, D), lambda i: (0, 0))
    return pl.pallas_call(
        body, name=name, grid=(L // ROW_TILE,),
        in_specs=[row, row, vec, row], out_specs=[row, row, vec],
        out_shape=[jax.ShapeDtypeStruct((L, D), F32), jax.ShapeDtypeStruct((L, D), BF16), jax.ShapeDtypeStruct((1, D), F32)],
        compiler_params=pltpu.CompilerParams(dimension_semantics=("arbitrary",), vmem_limit_bytes=VMEM_LIMIT),
    )(du, x, g, dres)


def _loss_head(h, target, g):
    L, D = h.shape

    def body(h_ref, t_ref, g_ref, dh_ref, dhb_ref, loss_ref, dg_ref):
        i = pl.program_id(0)
        hv = h_ref[...]
        r = lax.rsqrt(jnp.mean(hv * hv, axis=1, keepdims=True) + EPS)
        diff = hv * r * g_ref[...] - t_ref[...]
        part = 0.5 * jnp.sum(jnp.mean(diff * diff, axis=1, keepdims=True), axis=0, keepdims=True)
        dy = diff * (1.0 / D)
        t = dy * g_ref[...]
        dh = r * t - hv * (r * r * r) * jnp.mean(t * hv, axis=1, keepdims=True)
        dh_ref[...] = dh
        dhb_ref[...] = dh.astype(BF16)
        dg = jnp.sum(dy * hv * r, axis=0, keepdims=True)

        @pl.when(i == 0)
        def _():
            dg_ref[...] = dg
            loss_ref[...] = jnp.broadcast_to(part, (1, 128))

        @pl.when(i != 0)
        def _():
            dg_ref[...] += dg
            loss_ref[...] += jnp.broadcast_to(part, (1, 128))

    row = pl.BlockSpec((ROW_TILE, D), lambda i: (i, 0))
    vec = pl.BlockSpec((1, D), lambda i: (0, 0))
    return pl.pallas_call(
        body, name="loss_head", grid=(L // ROW_TILE,),
        in_specs=[row, row, vec], out_specs=[row, row, pl.BlockSpec((1, 128), lambda i: (0, 0)), vec],
        out_shape=[jax.ShapeDtypeStruct((L, D), F32), jax.ShapeDtypeStruct((L, D), BF16), jax.ShapeDtypeStruct((1, 128), F32),
                   jax.ShapeDtypeStruct((1, D), F32)],
        compiler_params=pltpu.CompilerParams(dimension_semantics=("arbitrary",), vmem_limit_bytes=VMEM_LIMIT),
    )(h, target, g)


FF_TILE = 1408


def _swiglu_fwd(gate, up):
    L, N = gate.shape

    def body(g_ref, u_ref, o_ref):
        o_ref[...] = (_silu(g_ref[...]) * u_ref[...]).astype(BF16)

    blk = pl.BlockSpec((ROW_TILE, FF_TILE), lambda i, j: (i, j))
    return pl.pallas_call(
        body, name="swiglu_fwd", grid=(L // ROW_TILE, N // FF_TILE), in_specs=[blk, blk], out_specs=blk,
        out_shape=jax.ShapeDtypeStruct((L, N), BF16),
        compiler_params=pltpu.CompilerParams(dimension_semantics=("parallel", "parallel"), vmem_limit_bytes=VMEM_LIMIT),
    )(gate, up)


def _swiglu_bwd(dact, gate, up):
    L, N = gate.shape

    def body(d_ref, g_ref, u_ref, dg_ref, du_ref):
        gv, d = g_ref[...], d_ref[...]
        s = _sigmoid(gv)
        dg_ref[...] = (d * u_ref[...] * (s * (1.0 + gv * (1.0 - s)))).astype(BF16)
        du_ref[...] = (d * gv * s).astype(BF16)

    blk = pl.BlockSpec((ROW_TILE, FF_TILE), lambda i, j: (i, j))
    return pl.pallas_call(
        body, name="swiglu_bwd", grid=(L // ROW_TILE, N // FF_TILE), in_specs=[blk, blk, blk], out_specs=[blk, blk],
        out_shape=[jax.ShapeDtypeStruct((L, N), BF16), jax.ShapeDtypeStruct((L, N), BF16)],
        compiler_params=pltpu.CompilerParams(dimension_semantics=("parallel", "parallel"), vmem_limit_bytes=VMEM_LIMIT),
    )(dact, gate, up)


def _adamw(w, g, m, v, rows, name):
    R, C = w.shape
    assert R % rows == 0, (name, R, rows)

    def body(w_ref, g_ref, m_ref, v_ref, d_ref, nm_ref, nv_ref):
        gv = g_ref[...]
        mn = ADAM_B1 * m_ref[...] + (1.0 - ADAM_B1) * gv
        vn = ADAM_B2 * v_ref[...] + (1.0 - ADAM_B2) * (gv * gv)
        m_hat = mn / (1.0 - ADAM_B1 ** ADAM_STEP)
        v_hat = vn / (1.0 - ADAM_B2 ** ADAM_STEP)
        d_ref[...] = -ADAM_LR * (m_hat / (jnp.sqrt(v_hat) + ADAM_EPS) + ADAM_WD * w_ref[...])
        nm_ref[...] = mn
        nv_ref[...] = vn

    blk = pl.BlockSpec((rows, C), lambda i: (i, 0))
    out = jax.ShapeDtypeStruct((R, C), F32)
    return pl.pallas_call(
        body, name=name, grid=(R // rows,), in_specs=[blk] * 4, out_specs=[blk] * 3, out_shape=[out] * 3,
        compiler_params=pltpu.CompilerParams(dimension_semantics=("parallel",), vmem_limit_bytes=VMEM_LIMIT),
    )(w, g, m, v)


ANY = pl.BlockSpec(memory_space=pl.ANY)
N_DEV = 8


def _allgather_small(p, name):
    R, C = p.shape

    def body(p_ref, out_ref, send_sems, recv_sems, local_sem):
        x, y, c = lax.axis_index("x"), lax.axis_index("y"), lax.axis_index("c")
        me = 4 * x + 2 * y + c
        mine = pltpu.make_async_copy(p_ref, out_ref.at[me], local_sem)
        mine.start()

        def peer(d):
            return (x ^ ((d >> 2) & 1), y ^ ((d >> 1) & 1), c ^ (d & 1))

        def copy(d, block):
            return pltpu.make_async_remote_copy(src_ref=p_ref, dst_ref=out_ref.at[block], send_sem=send_sems.at[d - 1],
                                                recv_sem=recv_sems.at[d - 1], device_id=peer(d), device_id_type=MESH)

        sends = [copy(d, me) for d in range(1, N_DEV)]
        for cp in sends:
            cp.start()
        for d in range(1, N_DEV):
            px, py, pc = peer(d)
            copy(d, 4 * px + 2 * py + pc).wait_recv()
        for cp in sends:
            cp.wait_send()
        mine.wait()

    return pl.pallas_call(
        body, name=name, out_shape=jax.ShapeDtypeStruct((N_DEV, R, C), p.dtype),
        in_specs=[pl.BlockSpec(memory_space=pltpu.VMEM)], out_specs=pl.BlockSpec(memory_space=pltpu.VMEM),
        scratch_shapes=[pltpu.SemaphoreType.DMA((N_DEV - 1,)), pltpu.SemaphoreType.DMA((N_DEV - 1,)), pltpu.SemaphoreType.DMA],
    )(p)


def _other_chips(x, y):
    return [(1 - x, y), (x, 1 - y), (1 - x, 1 - y)]


def _allgather_big(shards):
    T = len(shards)
    halves = [s.shape[0] // 2 for s in shards]

    def body(*refs):
        ins, outs = refs[:T], refs[T:2 * T]
        send_sems, recv_sems, local_sems = refs[2 * T:]
        x, y, c = lax.axis_index("x"), lax.axis_index("y"), lax.axis_index("c")
        sibling = (x, y, 1 - c)
        chips = _other_chips(x, y)

        def rows(t, px, py, pc):
            return outs[t].at[2 * px + py, pl.ds(pc * halves[t], halves[t]), :]

        def copy(t, k, block, to, src=None):
            return pltpu.make_async_remote_copy(
                src_ref=rows(t, *block) if src is None else src, dst_ref=rows(t, *block),
                send_sem=send_sems.at[t, k], recv_sem=recv_sems.at[t, k], device_id=to, device_id_type=MESH)

        mine, first, passed = [], [], []
        for t in range(T):
            src = ins[t].at[pl.ds(c * halves[t], halves[t]), :]
            mine.append(pltpu.make_async_copy(src, rows(t, x, y, c), local_sems.at[t]))
            first.append([copy(t, 0, (x, y, c), sibling, src=src)]
                         + [copy(t, 1 + j, (x, y, c), (*chip, c), src=src) for j, chip in enumerate(chips)])
        for t in range(T):
            mine[t].start()
            for cp in first[t]:
                cp.start()
        for t in range(T):
            for j, chip in enumerate(chips):
                copy(t, 1 + j, (*chip, c), (x, y, c)).wait_recv()
                cp = copy(t, 4 + j, (*chip, c), sibling)
                cp.start()
                passed.append(cp)
        for t in range(T):
            copy(t, 0, (x, y, 1 - c), (x, y, c)).wait_recv()
            for j, chip in enumerate(chips):
                copy(t, 4 + j, (*chip, 1 - c), (x, y, c)).wait_recv()
        for t in range(T):
            for cp in first[t]:
                cp.wait_send()
            mine[t].wait()
        for cp in passed:
            cp.wait_send()

    return pl.pallas_call(
        body, name="allgather_weights",
        out_shape=[jax.ShapeDtypeStruct((4,) + s.shape, s.dtype) for s in shards],
        in_specs=[ANY] * T, out_specs=[ANY] * T,
        scratch_shapes=[pltpu.SemaphoreType.DMA((T, 7)), pltpu.SemaphoreType.DMA((T, 7)), pltpu.SemaphoreType.DMA((T,))],
    )(*shards)


def _pair_exchange(grads):
    T = len(grads)

    def body(*refs):
        ins, outs = refs[:T], refs[T:2 * T]
        send_sems, recv_sems = refs[2 * T:]
        x, y, c = lax.axis_index("x"), lax.axis_index("y"), lax.axis_index("c")
        cps = [pltpu.make_async_remote_copy(src_ref=ins[t].at[1 - c], dst_ref=outs[t], send_sem=send_sems.at[t],
                                            recv_sem=recv_sems.at[t], device_id=(x, y, 1 - c), device_id_type=MESH)
               for t in range(T)]
        for cp in cps:
            cp.start()
        for cp in cps:
            cp.wait()

    return pl.pallas_call(
        body, name="grad_pair_exchange",
        out_shape=[jax.ShapeDtypeStruct(g.shape[1:], g.dtype) for g in grads],
        in_specs=[ANY] * T, out_specs=[ANY] * T,
        scratch_shapes=[pltpu.SemaphoreType.DMA((T,)), pltpu.SemaphoreType.DMA((T,))],
    )(*grads)


def _chip_exchange(parts):
    T = len(parts)

    def body(*refs):
        ins, outs = refs[:T], refs[T:2 * T]
        send_sems, recv_sems, local_sems = refs[2 * T:]
        x, y, c = lax.axis_index("x"), lax.axis_index("y"), lax.axis_index("c")
        my_chip = 2 * x + y
        chips = _other_chips(x, y)
        mine, sends = [], []
        for t in range(T):
            mine.append(pltpu.make_async_copy(ins[t].at[my_chip], outs[t].at[my_chip], local_sems.at[t]))
            for j, (px, py) in enumerate(chips):
                sends.append(pltpu.make_async_remote_copy(
                    src_ref=ins[t].at[2 * px + py], dst_ref=outs[t].at[my_chip], send_sem=send_sems.at[t, j],
                    recv_sem=recv_sems.at[t, j], device_id=(px, py, c), device_id_type=MESH))
        for cp in mine + sends:
            cp.start()
        for t in range(T):
            for j, (px, py) in enumerate(chips):
                pltpu.make_async_remote_copy(
                    src_ref=ins[t].at[my_chip], dst_ref=outs[t].at[2 * px + py], send_sem=send_sems.at[t, j],
                    recv_sem=recv_sems.at[t, j], device_id=(px, py, c), device_id_type=MESH).wait_recv()
        for cp in sends:
            cp.wait_send()
        for cp in mine:
            cp.wait()

    return pl.pallas_call(
        body, name="grad_chip_exchange",
        out_shape=[jax.ShapeDtypeStruct(p.shape, p.dtype) for p in parts],
        in_specs=[ANY] * T, out_specs=[ANY] * T,
        scratch_shapes=[pltpu.SemaphoreType.DMA((T, 3)), pltpu.SemaphoreType.DMA((T, 3)), pltpu.SemaphoreType.DMA((T,))],
    )(*parts)


def _pair_share(reds):
    T = len(reds)

    def body(*refs):
        ins, outs = refs[:T], refs[T:2 * T]
        send_sems, recv_sems, local_sems = refs[2 * T:]
        x, y, c = lax.axis_index("x"), lax.axis_index("y"), lax.axis_index("c")
        mine = [pltpu.make_async_copy(ins[t], outs[t].at[c], local_sems.at[t]) for t in range(T)]
        sends = [pltpu.make_async_remote_copy(src_ref=ins[t], dst_ref=outs[t].at[c], send_sem=send_sems.at[t],
                                              recv_sem=recv_sems.at[t], device_id=(x, y, 1 - c), device_id_type=MESH)
                 for t in range(T)]
        for cp in mine + sends:
            cp.start()
        for t in range(T):
            pltpu.make_async_remote_copy(src_ref=ins[t], dst_ref=outs[t].at[1 - c], send_sem=send_sems.at[t],
                                         recv_sem=recv_sems.at[t], device_id=(x, y, 1 - c), device_id_type=MESH).wait_recv()
        for cp in sends:
            cp.wait_send()
        for cp in mine:
            cp.wait()

    return pl.pallas_call(
        body, name="grad_pair_share",
        out_shape=[jax.ShapeDtypeStruct((2,) + r.shape, r.dtype) for r in reds],
        in_specs=[ANY] * T, out_specs=[ANY] * T,
        scratch_shapes=[pltpu.SemaphoreType.DMA((T,)), pltpu.SemaphoreType.DMA((T,)), pltpu.SemaphoreType.DMA((T,))],
    )(*reds)


def _row_tile(hr):
    return next(r for r in (128, 64, 32, 16, 8) if hr % r == 0)


def _pair_add(g, got, c, name):
    _, _, hr, C = g.shape
    tr = _row_tile(hr)

    def body(c_ref, g_ref, r_ref, o_ref):
        o_ref[...] = (g_ref[0].astype(F32) + r_ref[...].astype(F32)).astype(o_ref.dtype)

    return pl.pallas_call(
        body, name=name,
        grid_spec=pltpu.PrefetchScalarGridSpec(
            num_scalar_prefetch=1, grid=(4, hr // tr),
            in_specs=[pl.BlockSpec((1, 1, tr, C), lambda k, i, c_ref: (c_ref[0], k, i, 0)),
                      pl.BlockSpec((1, tr, C), lambda k, i, c_ref: (k, i, 0))],
            out_specs=pl.BlockSpec((1, tr, C), lambda k, i, c_ref: (k, i, 0))),
        out_shape=jax.ShapeDtypeStruct(got.shape, g.dtype),
        compiler_params=pltpu.CompilerParams(dimension_semantics=("parallel", "parallel"), vmem_limit_bytes=VMEM_LIMIT),
    )(jnp.reshape(c, (1,)).astype(jnp.int32), g, got)


def _sum_slots(parts, name):
    n, hr, C = parts.shape
    tr = _row_tile(hr)

    def body(p_ref, o_ref):
        acc = p_ref[0].astype(F32)
        for k in range(1, n):
            acc = acc + p_ref[k].astype(F32)
        o_ref[...] = acc

    return pl.pallas_call(
        body, name=name, grid=(hr // tr,),
        in_specs=[pl.BlockSpec((n, tr, C), lambda i: (0, i, 0))], out_specs=pl.BlockSpec((tr, C), lambda i: (i, 0)),
        out_shape=jax.ShapeDtypeStruct((hr, C), F32),
        compiler_params=pltpu.CompilerParams(dimension_semantics=("parallel",), vmem_limit_bytes=VMEM_LIMIT),
    )(parts)


PACK_ROWS, PACK_COLS = 16, 3072
_PACK = [("norm_mix_g", 0, 1, 2048), ("conv_w", 1, 4, 3072), ("conv_b", 5, 1, 3072), ("dt_bias", 6, 1, 32),
         ("a_log", 7, 1, 32), ("d_skip", 8, 1, 32), ("ssd_norm_g", 9, 1, 2048), ("i_bias", 10, 1, 8), ("f_bias", 11, 1, 8),
         ("mlstm_norm_g", 12, 1, 2048), ("norm_ffn_g", 13, 1, 2048), ("final_norm_g", 14, 1, 2048), ("loss", 15, 1, 1)]
_WEIGHTS = ["norm_mix_g", "w_in", "conv_w", "conv_b", "dt_bias", "a_log", "d_skip", "ssd_norm_g", "i_bias", "f_bias",
            "mlstm_norm_g", "w_out", "norm_ffn_g", "w_gate", "w_up", "w_down", "final_norm_g"]
_BIG = ["w_in", "w_out", "w_gate", "w_up", "w_down"]


def _pack(vals):
    rows = []
    for name, _, r, w in _PACK:
        v = vals.get(name)
        v = jnp.zeros((r, w), F32) if v is None else v.reshape(r, w).astype(F32)
        rows.append(jnp.concatenate([v, jnp.zeros((r, PACK_COLS - w), F32)], axis=1) if w < PACK_COLS else v)
    return jnp.concatenate(rows, axis=0)


def _unpack(p):
    return {name: p[r0:r0 + r, :w] for name, r0, r, w in _PACK}


def _to_halves_cols(dw, cols):
    R = dw.shape[0]
    return dw.reshape(2, R // 2, 4, cols).transpose(0, 2, 1, 3)


def kernel(x, norm_mix_g, w_in, conv_w, conv_b, dt_bias, a_log, d_skip, ssd_norm_g, i_bias, f_bias, mlstm_norm_g, w_out, norm_ffn_g, w_gate, w_up, w_down, final_norm_g, loss_target, m_norm_mix_g, m_w_in, m_conv_w, m_conv_b, m_dt_bias, m_a_log, m_d_skip, m_ssd_norm_g, m_i_bias, m_f_bias, m_mlstm_norm_g, m_w_out, m_norm_ffn_g, m_w_gate, m_w_up, m_w_down, m_final_norm_g, v_norm_mix_g, v_w_in, v_conv_w, v_conv_b, v_dt_bias, v_a_log, v_d_skip, v_ssd_norm_g, v_i_bias, v_f_bias, v_mlstm_norm_g, v_w_out, v_norm_ffn_g, v_w_gate, v_w_up, v_w_down, v_final_norm_g):
    weights = dict(norm_mix_g=norm_mix_g, w_in=w_in, conv_w=conv_w, conv_b=conv_b, dt_bias=dt_bias, a_log=a_log, d_skip=d_skip,
                   ssd_norm_g=ssd_norm_g, i_bias=i_bias, f_bias=f_bias, mlstm_norm_g=mlstm_norm_g, w_out=w_out,
                   norm_ffn_g=norm_ffn_g, w_gate=w_gate, w_up=w_up, w_down=w_down, final_norm_g=final_norm_g)
    mom1 = dict(norm_mix_g=m_norm_mix_g, w_in=m_w_in, conv_w=m_conv_w, conv_b=m_conv_b, dt_bias=m_dt_bias, a_log=m_a_log,
                d_skip=m_d_skip, ssd_norm_g=m_ssd_norm_g, i_bias=m_i_bias, f_bias=m_f_bias, mlstm_norm_g=m_mlstm_norm_g,
                w_out=m_w_out, norm_ffn_g=m_norm_ffn_g, w_gate=m_w_gate, w_up=m_w_up, w_down=m_w_down,
                final_norm_g=m_final_norm_g)
    mom2 = dict(norm_mix_g=v_norm_mix_g, w_in=v_w_in, conv_w=v_conv_w, conv_b=v_conv_b, dt_bias=v_dt_bias, a_log=v_a_log,
                d_skip=v_d_skip, ssd_norm_g=v_ssd_norm_g, i_bias=v_i_bias, f_bias=v_f_bias, mlstm_norm_g=v_mlstm_norm_g,
                w_out=v_w_out, norm_ffn_g=v_norm_ffn_g, w_gate=v_w_gate, w_up=v_w_up, w_down=v_w_down,
                final_norm_g=v_final_norm_g)
    xi, yi, ci = lax.axis_index("x"), lax.axis_index("y"), lax.axis_index("c")
    chip = 2 * xi + yi
    xs, tgt = x[0], loss_target[0]

    g_in, g_out, g_gate, g_up, g_down = _allgather_big([weights[n][0].astype(BF16) for n in _BIG])
    W_ssd, W_ml, W_small = _split_w_in(jnp.transpose(g_in, (1, 0, 2)).reshape(D_MODEL, IN_WIDTH))
    W_out = g_out.reshape(2 * SSD_WIDTH, D_MODEL)
    W_gate = jnp.transpose(g_gate, (1, 0, 2)).reshape(D_MODEL, D_FF)
    W_up = jnp.transpose(g_up, (1, 0, 2)).reshape(D_MODEL, D_FF)
    W_down = g_down.reshape(D_FF, D_MODEL)
    cw_all = _allgather_small(jnp.concatenate([conv_w[0], jnp.zeros((4, 768), F32)], axis=0), "allgather_conv_w")
    conv_w_full = jnp.concatenate([cw_all[2 * k, :SSD_CONV] for k in range(4)], axis=1)
    cwg, cbg = _conv_to_groups(conv_w_full), _conv_to_groups(conv_b)
    dtb_row, al_row = _small_row(dt_bias, 0), _small_row(a_log, 0)
    ib_row, fb_row = _small_row(i_bias, LANE_I), _small_row(f_bias, LANE_F)
    dskip_lane = jnp.repeat(d_skip, 64, axis=1)
    fng = final_norm_g[None]

    u1 = _rmsnorm_fwd(xs, norm_mix_g, "norm_mix_fwd")
    p_ssd = _matmul(u1, W_ssd, "nn", F32, 1024, 1024, 2048, "proj_ssd")
    p_ml = _matmul(u1, W_ml, "nn", F32, 1024, 1024, 2048, "proj_ml")
    p_small = _matmul(u1, W_small, "nn", F32, 1024, 128, 2048, "proj_small")
    y_ssd, ssd_st = _ssd_forward(p_ssd, p_small, cwg, cbg, dtb_row, al_row, dskip_lane, ssd_norm_g)
    y_ml, ml_c, ml_nm = _mlstm_forward(p_ml, p_small, ib_row, fb_row, mlstm_norm_g)
    y_mix = jnp.concatenate([y_ssd, y_ml], axis=1)
    h1 = _matmul(y_mix, W_out, "nn", F32, 1024, 1024, 2048, "out_proj", addend=xs)
    u2 = _rmsnorm_fwd(h1, norm_ffn_g, "norm_ffn_fwd")
    gate = _matmul(u2, W_gate, "nn", F32, 1024, FF_TILE, 2048, "ffn_gate")
    up = _matmul(u2, W_up, "nn", F32, 1024, FF_TILE, 2048, "ffn_up")
    act = _swiglu_fwd(gate, up)
    h2 = _matmul(act, W_down, "nn", F32, 1024, 1024, FF_TILE, "ffn_down", addend=h1)
    dh2, dh2_b, loss_row, d_fng = _loss_head(h2, tgt, fng)

    dact = _matmul(dh2_b, W_down, "nt", F32, 1024, FF_TILE, 2048, "d_act")
    dW_down = _matmul(act, dh2_b, "tn", BF16, FF_TILE, 1024, 1024, "dw_down", layout="rows4")
    d_gate, d_up = _swiglu_bwd(dact, gate, up)
    du2 = _matmul(d_gate, W_gate, "nt", F32, 1024, 1024, FF_TILE, "du2_gate")
    du2 = _matmul(d_up, W_up, "nt", F32, 1024, 1024, FF_TILE, "du2_up", addend=du2)
    dW_gate = _matmul(u2, d_gate, "tn", BF16, 1024, FF_TILE, 1024, "dw_gate", layout="cols4")
    dW_up = _matmul(u2, d_up, "tn", BF16, 1024, FF_TILE, 1024, "dw_up", layout="cols4")
    dh1, dh1_b, d_ffn_g = _rmsnorm_bwd(du2, h1, norm_ffn_g, dh2, "norm_ffn_bwd")
    dy_mix = _matmul(dh1_b, W_out, "nt", F32, 1024, 1024, 2048, "d_mix")
    dW_out = _matmul(y_mix, dh1_b, "tn", BF16, 1024, 1024, 1024, "dw_out", layout="rows4")
    d_ssd, d_small, d_cw, d_cb, d_dtb, d_alog, d_dskip, d_sng = _ssd_backward(
        p_ssd, p_small, cwg, cbg, dtb_row, al_row, dskip_lane, ssd_norm_g, ssd_st, dy_mix)
    d_ml, d_small, d_ib, d_fb, d_mng = _mlstm_backward(p_ml, p_small, ib_row, fb_row, mlstm_norm_g, ml_c, ml_nm, dy_mix, d_small)
    du1 = _matmul(d_ssd, W_ssd, "nt", F32, 1024, 1024, 1280, "du1_ssd")
    du1 = _matmul(d_ml, W_ml, "nt", F32, 1024, 1024, 1024, "du1_ml", addend=du1)
    du1 = _matmul(d_small, W_small, "nt", F32, 1024, 1024, 128, "du1_small", addend=du1)
    dW_ssd = _matmul(u1, d_ssd, "tn", BF16, 1024, 1280, 1024, "dw_ssd")
    dW_ml = _matmul(u1, d_ml, "tn", BF16, 1024, 1024, 1024, "dw_ml")
    dW_small = _matmul(u1, d_small, "tn", BF16, 1024, 128, 1024, "dw_small")
    grad_x, _, d_mix_g = _rmsnorm_bwd(du1, xs, norm_mix_g, dh1, "norm_mix_bwd")
    dW_in = _merge_w_in(dW_ssd, dW_ml, dW_small)

    big = [_to_halves_cols(dW_in, 2828), dW_out, dW_gate, dW_up, dW_down]
    got = _pair_exchange(big)
    parts = [_pair_add(g, r, ci, "grad_pair_add_" + n) for g, r, n in zip(big, got, _BIG)]
    slots = _chip_exchange(parts)
    reds = [_sum_slots(s, "grad_chip_sum_" + n) for s, n in zip(slots, _BIG)]
    fulls = _pair_share(reds)
    grads = {n: f.reshape(weights[n].shape[1:]) for n, f in zip(_BIG, fulls)}

    small = _pack(dict(norm_mix_g=d_mix_g, conv_w=_conv_from_groups(d_cw), conv_b=_conv_from_groups(d_cb),
                       dt_bias=d_dtb[:, 0:32], a_log=d_alog[:, 0:32], d_skip=d_dskip.reshape(SSD_HEADS, 64).sum(axis=1),
                       ssd_norm_g=d_sng, i_bias=d_ib[:, LANE_I:LANE_I + 8], f_bias=d_fb[:, LANE_F:LANE_F + 8],
                       mlstm_norm_g=d_mng, norm_ffn_g=d_ffn_g, final_norm_g=d_fng, loss=loss_row[:, 0:1]))
    total = _sum_slots(_allgather_small(small, "allgather_small_grads"), "small_grad_sum")
    small_w = {n: weights[n] for n in _WEIGHTS if n not in _BIG and n != "conv_w"}
    sd, sm, sv = _adamw(_pack(small_w), total, _pack({n: mom1[n] for n in small_w}), _pack({n: mom2[n] for n in small_w}),
                        PACK_ROWS, "adamw_small")
    tot, sd, sm, sv = _unpack(total), _unpack(sd), _unpack(sm), _unpack(sv)
    for n in small_w:
        grads[n] = tot[n].reshape(weights[n].shape)
    grads["conv_w"] = lax.dynamic_slice_in_dim(tot["conv_w"], chip * 768, 768, axis=1)
    loss = tot["loss"][0, 0]

    delta, new_m, new_v = {}, {}, {}
    for n in _BIG + ["conv_w"]:
        w2 = weights[n][0]
        d, nm, nv = _adamw(w2, grads[n], mom1[n][0], mom2[n][0], _row_tile(w2.shape[0]) if n != "conv_w" else SSD_CONV,
                           "adamw_" + n)
        delta[n], new_m[n], new_v[n] = d[None], nm[None], nv[None]
        grads[n] = grads[n][None]
    for n in small_w:
        delta[n], new_m[n], new_v[n] = (t[n].reshape(weights[n].shape) for t in (sd, sm, sv))
    return (loss, grad_x[None], *[grads[n] for n in _WEIGHTS], *[delta[n] for n in _WEIGHTS],
            *[new_m[n] for n in _WEIGHTS], *[new_v[n] for n in _WEIGHTS])
```

```python
import functools

import jax
import jax.numpy as jnp
import numpy as np
from jax import lax
from jax.experimental import pallas as pl
from jax.experimental.pallas import tpu as pltpu

F32 = jnp.float32
BF16 = jnp.bfloat16

D_MODEL = 2048
SSD_HEADS = 32
SSD_GROUPS = 4
SSD_STATE = 128
SSD_WIDTH = 2048
SSD_CONV = 4
SSD_GROUP_COLS = 1280
SSD_XBC_COLS = 768
ML_HEADS = 8
ML_HEAD_COLS = 768
ML_DK = 128
ML_DV = 256
CHUNK = 128
SMALL_COLS = 128
LANE_I = 32
LANE_F = 40
D_FF = 5632
GATE_SOFTCAP = 15.0
EPS = 1e-6
ADAM_LR, ADAM_B1, ADAM_B2, ADAM_EPS, ADAM_WD, ADAM_STEP = 0.001, 0.9, 0.999, 1e-8, 0.01, 10
MESH = pl.DeviceIdType.MESH
VMEM_LIMIT = 56 * 1024 * 1024


def _dg(a, b, ca, cb):
    return lax.dot_general(a.astype(BF16), b.astype(BF16), (((ca,), (cb,)), ((), ())), preferred_element_type=F32)


@jax.custom_vjp
def _mm(a, b):
    return _dg(a, b, 1, 0)


_mm.defvjp(lambda a, b: (_dg(a, b, 1, 0), (a, b)),
           lambda r, g: (_dg(g, r[1], 1, 1), _dg(r[0], g, 0, 0)))


@jax.custom_vjp
def _mm_nt(a, b):
    return _dg(a, b, 1, 1)


_mm_nt.defvjp(lambda a, b: (_dg(a, b, 1, 1), (a, b)),
              lambda r, g: (_dg(g, r[1], 1, 0), _dg(g, r[0], 0, 0)))


@jax.custom_vjp
def _mm_tn(a, b):
    return _dg(a, b, 0, 0)


_mm_tn.defvjp(lambda a, b: (_dg(a, b, 0, 0), (a, b)),
              lambda r, g: (_dg(r[1], g, 1, 1), _dg(r[0], g, 1, 0)))


def _tri(lower):
    r = lax.broadcasted_iota(jnp.int32, (CHUNK, CHUNK), 0)
    c = lax.broadcasted_iota(jnp.int32, (CHUNK, CHUNK), 1)
    return ((r >= c) if lower else (r <= c)).astype(F32)


def _dg32(t, x):
    return lax.dot_general(t, x, (((1,), (0,)), ((), ())), precision=lax.Precision.HIGHEST, preferred_element_type=F32)


@jax.custom_vjp
def _cumsum(x):
    return _dg32(_tri(True), x)


_cumsum.defvjp(lambda x: (_dg32(_tri(True), x), None), lambda r, g: (_dg32(_tri(False), g),))


_sigmoid = jax.nn.sigmoid


def _silu(x):
    return x * _sigmoid(x)


def _softplus(x):
    return jnp.maximum(x, 0.0) + jnp.log(1.0 + jnp.exp(-jnp.abs(x)))


def _lane_col(m, lane, h):
    return jnp.sum(jnp.where(lane == h, m, 0.0), axis=1, keepdims=True)


def _ssd_math(cx, cB, cC, z, smallb, S_in, dtb_row, alog_row, dskip, ng, g):
    lane = lax.broadcasted_iota(jnp.int32, (1, CHUNK), 1)
    row_i = lax.broadcasted_iota(jnp.int32, (CHUNK, CHUNK), 0)
    col_i = lax.broadcasted_iota(jnp.int32, (CHUNK, CHUNK), 1)
    causal = row_i >= col_i
    half = lane < 64
    rhalf = lax.broadcasted_iota(jnp.int32, (CHUNK, 1), 0) < 64
    xs, Bm, Cm = _silu(cx), _silu(cB), _silu(cC)
    dt_all = _softplus(smallb + dtb_row)
    a_all = dt_all * (-jnp.exp(alog_row))
    acum_all = _cumsum(a_all)
    alast_all = jnp.sum(a_all, axis=0, keepdims=True)
    cb = _mm_nt(Cm, Bm)

    def lmat(ac):
        acb = jnp.broadcast_to(ac, (CHUNK, CHUNK))
        return jnp.exp(jnp.where(causal, acb - acb.T, -jnp.inf))

    ys, S_out = [], []
    for j in range(4):
        h0 = 8 * g + 2 * j
        ac0, ac1 = _lane_col(acum_all, lane, h0), _lane_col(acum_all, lane, h0 + 1)
        dt0, dt1 = _lane_col(dt_all, lane, h0), _lane_col(dt_all, lane, h0 + 1)
        al0, al1 = _lane_col(alast_all, lane, h0), _lane_col(alast_all, lane, h0 + 1)
        Xp = xs[:, 128 * j:128 * (j + 1)]
        Xd = Xp * jnp.where(half, dt0, dt1)
        ac_sel = jnp.where(half, ac0, ac1)
        al_sel = jnp.where(half, al0, al1)
        Yd = jnp.where(half, _mm(cb * lmat(ac0), Xd), _mm(cb * lmat(ac1), Xd))
        Yoff = _mm_nt(Cm, S_in[j]) * jnp.exp(ac_sel)
        ys.append(Yd + Yoff + dskip[:, 128 * j:128 * (j + 1)] * Xp)
        S_new = _mm_tn(Xd * jnp.exp(al_sel - ac_sel), Bm)
        S_out.append(S_in[j] * jnp.exp(jnp.where(rhalf, al0, al1)) + S_new)
    y = jnp.concatenate(ys, axis=1)
    y = y * _silu(z)
    y = y * lax.rsqrt(jnp.mean(y * y, axis=1, keepdims=True) + EPS) * ng
    return y, tuple(S_out)


def _ssd_conv(blk_ref, halo_ref, cw_ref, cb_ref, ext, first):
    halo = halo_ref[:, 512:SSD_GROUP_COLS]
    ext[0:8, :] = jnp.where(first, jnp.zeros_like(halo), halo)
    ext[8:8 + CHUNK, :] = blk_ref[:, 512:SSD_GROUP_COLS]
    conv = jnp.broadcast_to(cb_ref[0], (CHUNK, SSD_XBC_COLS))
    for tap in range(SSD_CONV):
        conv = conv + cw_ref[0, tap:tap + 1, :] * ext[pl.ds(5 + tap, CHUNK), :]
    return conv


def _ssd_specs(nc, rev):
    cc = (lambda c: nc - 1 - c) if rev else (lambda c: c)
    return [
        pl.BlockSpec((CHUNK, SSD_GROUP_COLS), lambda c, g: (cc(c), g)),
        pl.BlockSpec((8, SSD_GROUP_COLS), lambda c, g: (jnp.maximum(cc(c) * (CHUNK // 8) - 1, 0), g)),
        pl.BlockSpec((CHUNK, SMALL_COLS), lambda c, g: (cc(c), 0)),
        pl.BlockSpec((1, SSD_CONV, SSD_XBC_COLS), lambda c, g: (g, 0, 0)),
        pl.BlockSpec((1, 1, SSD_XBC_COLS), lambda c, g: (g, 0, 0)),
        pl.BlockSpec((1, SMALL_COLS), lambda c, g: (0, 0)),
        pl.BlockSpec((1, SMALL_COLS), lambda c, g: (0, 0)),
        pl.BlockSpec((1, 512), lambda c, g: (0, g)),
        pl.BlockSpec((1, 512), lambda c, g: (0, g)),
    ]


def _ssd_forward(proj, small, cw, cb, dtb, alog, dskip, ng):
    L = proj.shape[0]
    nc = L // CHUNK

    def body(blk_ref, halo_ref, small_ref, cw_ref, cb_ref, dtb_ref, alog_ref, dskip_ref, ng_ref, y_ref, st_ref, carry, ext):
        c, g = pl.program_id(0), pl.program_id(1)

        @pl.when(c == 0)
        def _():
            carry[g] = jnp.zeros((4, CHUNK, CHUNK), F32)

        conv = _ssd_conv(blk_ref, halo_ref, cw_ref, cb_ref, ext, c == 0)
        S_in = tuple(carry[g, j] for j in range(4))
        st_ref[0, 0] = carry[g]
        y, S_out = _ssd_math(conv[:, 0:512], conv[:, 512:640], conv[:, 640:768], blk_ref[:, 0:512], small_ref[...],
                             S_in, dtb_ref[...], alog_ref[...], dskip_ref[...], ng_ref[...], g)
        y_ref[...] = y.astype(BF16)
        for j in range(4):
            carry[g, j] = S_out[j]

    return pl.pallas_call(
        body, name="ssd_fwd", grid=(nc, SSD_GROUPS),
        in_specs=_ssd_specs(nc, False),
        out_specs=[pl.BlockSpec((CHUNK, 512), lambda c, g: (c, g)),
                   pl.BlockSpec((1, 1, 4, CHUNK, CHUNK), lambda c, g: (c, g, 0, 0, 0))],
        out_shape=[jax.ShapeDtypeStruct((L, SSD_WIDTH), BF16), jax.ShapeDtypeStruct((nc, SSD_GROUPS, 4, CHUNK, CHUNK), F32)],
        scratch_shapes=[pltpu.VMEM((SSD_GROUPS, 4, CHUNK, CHUNK), F32), pltpu.VMEM((8 + CHUNK, SSD_XBC_COLS), F32)],
        compiler_params=pltpu.CompilerParams(dimension_semantics=("arbitrary", "arbitrary"), vmem_limit_bytes=VMEM_LIMIT),
    )(proj, proj, small, cw, cb, dtb, alog, dskip, ng)


def _ssd_backward(proj, small, cw, cb, dtb, alog, dskip, ng, states, dy, side=None):
    L = proj.shape[0]
    nc = L // CHUNK
    s_in, s_out = (len(side.operands), len(side.out_shapes)) if side is not None else (0, 0)
    grid = (nc, SSD_GROUPS)

    def body(*refs):
        blk_ref, halo_ref, small_ref, cw_ref, cb_ref, dtb_ref, alog_ref, dskip_ref, ng_ref, st_ref, dy_ref = refs[:11]
        dproj_ref, dsmall_ref, dcw_ref, dcb_ref, ddtb_ref, dalog_ref, ddskip_ref, dng_ref = refs[11 + s_in:19 + s_in]
        dcarry, nxt, ext, dext = refs[19 + s_in + s_out:23 + s_in + s_out]
        side_refs = (refs[11:11 + s_in], refs[19 + s_in:19 + s_in + s_out], refs[23 + s_in + s_out:])
        s, g = pl.program_id(0), pl.program_id(1)
        c = nc - 1 - s
        if side is not None:
            pl.when(_grid_edge(grid, first=True))(lambda: side.start(*side_refs))

        @pl.when(s == 0)
        def _():
            dcarry[g] = jnp.zeros((4, CHUNK, CHUNK), F32)
            nxt[g] = jnp.zeros((8, SSD_XBC_COLS), F32)
            dcw_ref[g] = jnp.zeros((SSD_CONV, SSD_XBC_COLS), F32)
            dcb_ref[g] = jnp.zeros((1, SSD_XBC_COLS), F32)
            ddskip_ref[g] = jnp.zeros((1, 512), F32)
            dng_ref[g] = jnp.zeros((1, 512), F32)

        @pl.when((s == 0) & (g == 0))
        def _():
            ddtb_ref[...] = jnp.zeros((1, SMALL_COLS), F32)
            dalog_ref[...] = jnp.zeros((1, SMALL_COLS), F32)

        conv = _ssd_conv(blk_ref, halo_ref, cw_ref, cb_ref, ext, c == 0)
        S_in = tuple(st_ref[0, 0, j] for j in range(4))
        dS_out = tuple(dcarry[g, j] for j in range(4))
        _, vjp = jax.vjp(functools.partial(_ssd_math, g=g), conv[:, 0:512], conv[:, 512:640], conv[:, 640:768],
                         blk_ref[:, 0:512], small_ref[...], S_in, dtb_ref[...], alog_ref[...], dskip_ref[...], ng_ref[...])
        d_cx, d_cB, d_cC, d_z, d_small, dS_in, d_dtb, d_alog, d_dskip, d_ng = vjp((dy_ref[...].astype(F32), dS_out))
        for j in range(4):
            dcarry[g, j] = dS_in[j]
        dext[0:8, :] = jnp.zeros((8, SSD_XBC_COLS), F32)
        dext[8:8 + CHUNK, 0:512] = d_cx
        dext[8:8 + CHUNK, 512:640] = d_cB
        dext[8:8 + CHUNK, 640:768] = d_cC
        dext[8 + CHUNK:16 + CHUNK, :] = nxt[g]
        nxt[g] = dext[8:16, :]
        dconv = dext[8:8 + CHUNK, :]
        d_xbc = jnp.zeros((CHUNK, SSD_XBC_COLS), F32)
        for tap in range(SSD_CONV):
            d_xbc = d_xbc + cw_ref[0, tap:tap + 1, :] * dext[pl.ds(8 + 3 - tap, CHUNK), :]
            dcw_ref[g, tap:tap + 1, :] += jnp.sum(dconv * ext[pl.ds(5 + tap, CHUNK), :], axis=0, keepdims=True)
        dcb_ref[g] += jnp.sum(dconv, axis=0, keepdims=True)
        dproj_ref[:, 0:512] = d_z.astype(BF16)
        dproj_ref[:, 512:SSD_GROUP_COLS] = d_xbc.astype(BF16)

        @pl.when(g == 0)
        def _():
            dsmall_ref[...] = d_small

        @pl.when(g != 0)
        def _():
            dsmall_ref[...] += d_small

        ddtb_ref[...] += d_dtb
        dalog_ref[...] += d_alog
        ddskip_ref[g] += d_dskip
        dng_ref[g] += d_ng
        if side is not None:
            pl.when(_grid_edge(grid, first=False))(lambda: side.finish(*side_refs))

    whole = lambda shape: pl.BlockSpec(shape, lambda s, g: (0,) * len(shape))
    side_ops = tuple(side.operands) if side is not None else ()
    res = pl.pallas_call(
        body, name="ssd_bwd", grid=grid,
        in_specs=_ssd_specs(nc, True) + [
            pl.BlockSpec((1, 1, 4, CHUNK, CHUNK), lambda s, g: (nc - 1 - s, g, 0, 0, 0)),
            pl.BlockSpec((CHUNK, 512), lambda s, g: (nc - 1 - s, g))] + [ANY] * s_in,
        out_specs=[pl.BlockSpec((CHUNK, SSD_GROUP_COLS), lambda s, g: (nc - 1 - s, g)),
                   pl.BlockSpec((CHUNK, SMALL_COLS), lambda s, g: (nc - 1 - s, 0)),
                   whole((SSD_GROUPS, SSD_CONV, SSD_XBC_COLS)), whole((SSD_GROUPS, 1, SSD_XBC_COLS)),
                   whole((1, SMALL_COLS)), whole((1, SMALL_COLS)),
                   whole((SSD_GROUPS, 1, 512)), whole((SSD_GROUPS, 1, 512))] + [ANY] * s_out,
        out_shape=[jax.ShapeDtypeStruct((L, SSD_GROUPS * SSD_GROUP_COLS), BF16), jax.ShapeDtypeStruct((L, SMALL_COLS), F32),
                   jax.ShapeDtypeStruct((SSD_GROUPS, SSD_CONV, SSD_XBC_COLS), F32),
                   jax.ShapeDtypeStruct((SSD_GROUPS, 1, SSD_XBC_COLS), F32),
                   jax.ShapeDtypeStruct((1, SMALL_COLS), F32), jax.ShapeDtypeStruct((1, SMALL_COLS), F32),
                   jax.ShapeDtypeStruct((SSD_GROUPS, 1, 512), F32), jax.ShapeDtypeStruct((SSD_GROUPS, 1, 512), F32)]
        + (list(side.out_shapes) if side is not None else []),
        scratch_shapes=[pltpu.VMEM((SSD_GROUPS, 4, CHUNK, CHUNK), F32), pltpu.VMEM((SSD_GROUPS, 8, SSD_XBC_COLS), F32),
                        pltpu.VMEM((8 + CHUNK, SSD_XBC_COLS), F32), pltpu.VMEM((16 + CHUNK, SSD_XBC_COLS), F32)]
        + (list(side.sems) if side is not None else []),
        compiler_params=pltpu.CompilerParams(dimension_semantics=("arbitrary", "arbitrary"), vmem_limit_bytes=VMEM_LIMIT),
    )(proj, proj, small, cw, cb, dtb, alog, dskip, ng, states, dy, *side_ops)
    return res if side is None else (res[:8], res[8:])


def _mlstm_math(q, k, v, o_raw, smallb, C_in, n_in, m_in_row, ib_row, fb_row, ng, h):
    lane = lax.broadcasted_iota(jnp.int32, (1, CHUNK), 1)
    row_i = lax.broadcasted_iota(jnp.int32, (CHUNK, CHUNK), 0)
    col_i = lax.broadcasted_iota(jnp.int32, (CHUNK, CHUNK), 1)
    causal = row_i >= col_i
    kk = k * (ML_DK ** -0.5)
    li_all = GATE_SOFTCAP * jnp.tanh((smallb + ib_row) / GATE_SOFTCAP)
    lf_all = -_softplus(-(GATE_SOFTCAP * jnp.tanh((smallb + fb_row) / GATE_SOFTCAP)))
    bcum_all = _cumsum(lf_all)
    li = _lane_col(li_all, lane, LANE_I + h)
    lf = _lane_col(lf_all, lane, LANE_F + h)
    bc = _lane_col(bcum_all, lane, LANE_F + h)
    b_last = jnp.sum(lf, axis=0, keepdims=True)
    m_in = _lane_col(m_in_row, lane, 0)
    a = b_last - bc + li
    m_loc = jnp.max(a, axis=0, keepdims=True)
    w = jnp.exp(a - m_loc)
    C_loc = _mm_tn(w * v, kk)
    n_loc = jnp.sum(w * kk, axis=0, keepdims=True)
    m_new = jnp.maximum(b_last + m_in, m_loc)
    s_old = jnp.exp(b_last + m_in - m_new)
    s_new = jnp.exp(m_loc - m_new)
    C_out = s_old * C_in + s_new * C_loc
    n_out = s_old * n_in + s_new * n_loc
    bc_b = jnp.broadcast_to(bc, (CHUNK, CHUNK))
    li_b = jnp.broadcast_to(li, (CHUNK, CHUNK))
    D = jnp.where(causal, bc_b - bc_b.T + li_b.T, -jnp.inf)
    m_intra = jnp.max(D, axis=1, keepdims=True)
    inter_log = bc + m_in
    m_t = jnp.maximum(inter_log, m_intra)
    S = _mm_nt(q, kk) * jnp.exp(D - m_t)
    w_inter = jnp.exp(inter_log - m_t)
    num = _mm(S, v) + w_inter * _mm_nt(q, C_in)
    nq = jnp.sum(S, axis=1, keepdims=True) + w_inter * jnp.sum(q * n_in, axis=1, keepdims=True)
    den = jnp.maximum(jnp.abs(nq), jnp.exp(-m_t))
    hh = num / den
    hh = hh * lax.rsqrt(jnp.mean(hh * hh, axis=1, keepdims=True) + EPS)
    hh = hh * ng * _sigmoid(o_raw)
    return hh, C_out, n_out, jnp.broadcast_to(m_new, (1, CHUNK))


def _ml_specs(nc, rev):
    cc = (lambda c: nc - 1 - c) if rev else (lambda c: c)
    return [
        pl.BlockSpec((CHUNK, ML_HEAD_COLS), lambda c, h: (cc(c), h)),
        pl.BlockSpec((CHUNK, SMALL_COLS), lambda c, h: (cc(c), 0)),
        pl.BlockSpec((1, SMALL_COLS), lambda c, h: (0, 0)),
        pl.BlockSpec((1, SMALL_COLS), lambda c, h: (0, 0)),
        pl.BlockSpec((1, ML_DV), lambda c, h: (0, h)),
    ]


def _mlstm_forward(proj, small, ib, fb, ng):
    L = proj.shape[0]
    nc = L // CHUNK

    def body(blk_ref, small_ref, ib_ref, fb_ref, ng_ref, y_ref, cst_ref, nm_ref, c_carry, nm_carry):
        c, h = pl.program_id(0), pl.program_id(1)

        @pl.when(c == 0)
        def _():
            c_carry[h] = jnp.zeros((ML_DV, ML_DK), F32)
            nm_carry[h] = jnp.zeros((2, ML_DK), F32)

        cst_ref[0, 0] = c_carry[h]
        nm_ref[0, 0] = nm_carry[h]
        hh, C_out, n_out, m_out = _mlstm_math(
            blk_ref[:, 0:128], blk_ref[:, 128:256], blk_ref[:, 256:512], blk_ref[:, 512:768], small_ref[...],
            c_carry[h], nm_carry[h, 0:1, :], nm_carry[h, 1:2, :], ib_ref[...], fb_ref[...], ng_ref[...], h)
        y_ref[...] = hh.astype(BF16)
        c_carry[h] = C_out
        nm_carry[h, 0:1, :] = n_out
        nm_carry[h, 1:2, :] = m_out

    return pl.pallas_call(
        body, name="mlstm_fwd", grid=(nc, ML_HEADS),
        in_specs=_ml_specs(nc, False),
        out_specs=[pl.BlockSpec((CHUNK, ML_DV), lambda c, h: (c, h)),
                   pl.BlockSpec((1, 1, ML_DV, ML_DK), lambda c, h: (c, h, 0, 0)),
                   pl.BlockSpec((1, 1, 2, ML_DK), lambda c, h: (c, h, 0, 0))],
        out_shape=[jax.ShapeDtypeStruct((L, ML_HEADS * ML_DV), BF16),
                   jax.ShapeDtypeStruct((nc, ML_HEADS, ML_DV, ML_DK), F32),
                   jax.ShapeDtypeStruct((nc, ML_HEADS, 2, ML_DK), F32)],
        scratch_shapes=[pltpu.VMEM((ML_HEADS, ML_DV, ML_DK), F32), pltpu.VMEM((ML_HEADS, 2, ML_DK), F32)],
        compiler_params=pltpu.CompilerParams(dimension_semantics=("arbitrary", "arbitrary"), vmem_limit_bytes=VMEM_LIMIT),
    )(proj, small, ib, fb, ng)


def _mlstm_backward(proj, small, ib, fb, ng, cst, nmst, dy, dsmall_in):
    L = proj.shape[0]
    nc = L // CHUNK

    def body(blk_ref, small_ref, ib_ref, fb_ref, ng_ref, cst_ref, nm_ref, dy_ref, dsin_ref,
             dproj_ref, dsmall_ref, dib_ref, dfb_ref, dng_ref, dc_carry, dnm_carry):
        s, h = pl.program_id(0), pl.program_id(1)

        @pl.when(s == 0)
        def _():
            dc_carry[h] = jnp.zeros((ML_DV, ML_DK), F32)
            dnm_carry[h] = jnp.zeros((2, ML_DK), F32)
            dng_ref[h] = jnp.zeros((1, ML_DV), F32)

        @pl.when((s == 0) & (h == 0))
        def _():
            dib_ref[...] = jnp.zeros((1, SMALL_COLS), F32)
            dfb_ref[...] = jnp.zeros((1, SMALL_COLS), F32)

        _, vjp = jax.vjp(functools.partial(_mlstm_math, h=h),
                         blk_ref[:, 0:128], blk_ref[:, 128:256], blk_ref[:, 256:512], blk_ref[:, 512:768], small_ref[...],
                         cst_ref[0, 0], nm_ref[0, 0, 0:1, :], nm_ref[0, 0, 1:2, :], ib_ref[...], fb_ref[...], ng_ref[...])
        dq, dk, dv, do, d_small, dC, dn, dm, d_ib, d_fb, d_ng = vjp(
            (dy_ref[...].astype(F32), dc_carry[h], dnm_carry[h, 0:1, :], dnm_carry[h, 1:2, :]))
        dc_carry[h] = dC
        dnm_carry[h, 0:1, :] = dn
        dnm_carry[h, 1:2, :] = dm
        dproj_ref[:, 0:128] = dq.astype(BF16)
        dproj_ref[:, 128:256] = dk.astype(BF16)
        dproj_ref[:, 256:512] = dv.astype(BF16)
        dproj_ref[:, 512:768] = do.astype(BF16)

        @pl.when(h == 0)
        def _():
            dsmall_ref[...] = dsin_ref[...] + d_small

        @pl.when(h != 0)
        def _():
            dsmall_ref[...] += d_small

        dib_ref[...] += d_ib
        dfb_ref[...] += d_fb
        dng_ref[h] += d_ng

    whole = lambda shape: pl.BlockSpec(shape, lambda s, h: (0,) * len(shape))
    return pl.pallas_call(
        body, name="mlstm_bwd", grid=(nc, ML_HEADS),
        in_specs=_ml_specs(nc, True) + [
            pl.BlockSpec((1, 1, ML_DV, ML_DK), lambda s, h: (nc - 1 - s, h, 0, 0)),
            pl.BlockSpec((1, 1, 2, ML_DK), lambda s, h: (nc - 1 - s, h, 0, 0)),
            pl.BlockSpec((CHUNK, ML_DV), lambda s, h: (nc - 1 - s, SSD_WIDTH // ML_DV + h)),
            pl.BlockSpec((CHUNK, SMALL_COLS), lambda s, h: (nc - 1 - s, 0))],
        out_specs=[pl.BlockSpec((CHUNK, ML_HEAD_COLS), lambda s, h: (nc - 1 - s, h)),
                   pl.BlockSpec((CHUNK, SMALL_COLS), lambda s, h: (nc - 1 - s, 0)),
                   whole((1, SMALL_COLS)), whole((1, SMALL_COLS)), whole((ML_HEADS, 1, ML_DV))],
        out_shape=[jax.ShapeDtypeStruct((L, ML_HEADS * ML_HEAD_COLS), BF16), jax.ShapeDtypeStruct((L, SMALL_COLS), F32),
                   jax.ShapeDtypeStruct((1, SMALL_COLS), F32), jax.ShapeDtypeStruct((1, SMALL_COLS), F32),
                   jax.ShapeDtypeStruct((ML_HEADS, 1, ML_DV), F32)],
        scratch_shapes=[pltpu.VMEM((ML_HEADS, ML_DV, ML_DK), F32), pltpu.VMEM((ML_HEADS, 2, ML_DK), F32)],
        compiler_params=pltpu.CompilerParams(dimension_semantics=("arbitrary", "arbitrary"), vmem_limit_bytes=VMEM_LIMIT),
    )(proj, small, ib, fb, ng, cst, nmst, dy, dsmall_in)


_OFF_Z, _OFF_X, _OFF_B, _OFF_C, _OFF_DT = 0, 2048, 4096, 4608, 5120
_OFF_Q, _OFF_K, _OFF_V, _OFF_O, _OFF_I, _OFF_F, IN_WIDTH = 5152, 6176, 7200, 9248, 11296, 11304, 11312


def _ssd_segments():
    segs = []
    for g in range(SSD_GROUPS):
        segs += [(_OFF_Z + 512 * g, 512), (_OFF_X + 512 * g, 512), (_OFF_B + 128 * g, 128), (_OFF_C + 128 * g, 128)]
    return segs


def _ml_segments():
    segs = []
    for h in range(ML_HEADS):
        segs += [(_OFF_Q + 128 * h, 128), (_OFF_K + 128 * h, 128), (_OFF_V + 256 * h, 256), (_OFF_O + 256 * h, 256)]
    return segs


_SMALL_SEGMENTS = [(_OFF_DT, 32), (_OFF_I, 8), (_OFF_F, 8)]


def _take_cols(w, segs):
    return jnp.concatenate([w[:, s:s + n] for s, n in segs], axis=1)


def _split_w_in(w):
    small = _take_cols(w, _SMALL_SEGMENTS)
    small = jnp.concatenate([small, jnp.zeros((w.shape[0], SMALL_COLS - small.shape[1]), w.dtype)], axis=1)
    return _take_cols(w, _ssd_segments()), _take_cols(w, _ml_segments()), small


def _merge_w_in(d_ssd, d_ml, d_small):
    pieces = []
    for arr, segs in ((d_ssd, _ssd_segments()), (d_ml, _ml_segments()), (d_small, _SMALL_SEGMENTS)):
        pos = 0
        for s, n in segs:
            pieces.append((s, arr[:, pos:pos + n]))
            pos += n
    pieces.sort(key=lambda t: t[0])
    return jnp.concatenate([p for _, p in pieces], axis=1)


def _conv_to_groups(cw):
    return jnp.stack([jnp.concatenate([cw[:, 512 * g:512 * (g + 1)], cw[:, 2048 + 128 * g:2048 + 128 * (g + 1)],
                                       cw[:, 2560 + 128 * g:2560 + 128 * (g + 1)]], axis=1) for g in range(SSD_GROUPS)])


def _conv_from_groups(d):
    return jnp.concatenate([d[g, :, 0:512] for g in range(SSD_GROUPS)] + [d[g, :, 512:640] for g in range(SSD_GROUPS)]
                           + [d[g, :, 640:768] for g in range(SSD_GROUPS)], axis=1)


def _small_row(vec, lane0):
    n = vec.shape[1]
    return jnp.concatenate([jnp.zeros((1, lane0), F32), vec, jnp.zeros((1, SMALL_COLS - lane0 - n), F32)], axis=1)


class _Side:
    def __init__(self, operands, out_shapes, sems, start, finish):
        self.operands, self.out_shapes, self.sems, self.start, self.finish = operands, out_shapes, sems, start, finish


def _grid_edge(grid, first):
    hit = None
    for axis, n in enumerate(grid):
        here = pl.program_id(axis) == (0 if first else n - 1)
        hit = here if hit is None else hit & here
    return hit


def _matmul(a, b, mode, out_dtype, tm, tn, tk, name, addend=None, layout=None, side=None):
    M, Kd = (a.shape[1], a.shape[0]) if mode == "tn" else a.shape
    N = b.shape[0] if mode == "nt" else b.shape[1]
    if layout == "cols4":
        tm, tn = M // 2, N // 4
    elif layout == "rows4":
        tm = M // 4
    tm, tn, tk = min(tm, M), min(tn, N), min(tk, Kd)
    if mode == "nn":
        a_spec = pl.BlockSpec((tm, tk), lambda i, j, k: (i, k))
        b_spec = pl.BlockSpec((tk, tn), lambda i, j, k: (k, j))
        dims = (((1,), (0,)), ((), ()))
    elif mode == "nt":
        a_spec = pl.BlockSpec((tm, tk), lambda i, j, k: (i, k))
        b_spec = pl.BlockSpec((tn, tk), lambda i, j, k: (j, k))
        dims = (((1,), (1,)), ((), ()))
    else:
        a_spec = pl.BlockSpec((tk, tm), lambda i, j, k: (k, i))
        b_spec = pl.BlockSpec((tk, tn), lambda i, j, k: (k, j))
        dims = (((0,), (0,)), ((), ()))
    assert M % tm == 0 and N % tn == 0 and Kd % tk == 0, (name, M, N, Kd, tm, tn, tk)
    nk = Kd // tk
    has_add = addend is not None
    n_in = 2 + has_add
    s_in, s_out = (len(side.operands), len(side.out_shapes)) if side is not None else (0, 0)
    grid = (M // tm, N // tn, nk)

    def body(*refs):
        a_ref, b_ref = refs[0], refs[1]
        add_ref = refs[2] if has_add else None
        o_ref, acc_ref = refs[n_in + s_in], refs[n_in + s_in + 1 + s_out]
        k = pl.program_id(2)
        side_refs = (refs[n_in:n_in + s_in], refs[n_in + s_in + 1:n_in + s_in + 1 + s_out], refs[n_in + s_in + 2 + s_out:])
        if side is not None:
            pl.when(_grid_edge(grid, first=True))(lambda: side.start(*side_refs))
        part = lax.dot_general(a_ref[...].astype(BF16), b_ref[...].astype(BF16), dims, preferred_element_type=F32)

        @pl.when(k == 0)
        def _():
            acc_ref[...] = part

        @pl.when(k != 0)
        def _():
            acc_ref[...] += part

        @pl.when(k == nk - 1)
        def _():
            r = acc_ref[...]
            if has_add:
                r = r + add_ref[...].astype(F32)
            if layout == "cols4":
                o_ref[0, 0] = r.astype(out_dtype)
            elif layout == "rows4":
                o_ref[0, 0] = r[:tm // 2].astype(out_dtype)
                o_ref[1, 0] = r[tm // 2:].astype(out_dtype)
            else:
                o_ref[...] = r.astype(out_dtype)

        if side is not None:
            pl.when(_grid_edge(grid, first=False))(lambda: side.finish(*side_refs))

    o_spec = pl.BlockSpec((tm, tn), lambda i, j, k: (i, j))
    out_spec, out_dims = o_spec, (M, N)
    if layout == "cols4":
        out_spec, out_dims = pl.BlockSpec((1, 1, tm, tn), lambda i, j, k: (i, j, 0, 0)), (2, 4, tm, tn)
    elif layout == "rows4":
        out_spec, out_dims = pl.BlockSpec((2, 1, tm // 2, tn), lambda i, j, k: (0, i, 0, j)), (2, 4, tm // 2, N)
    out_shape = jax.ShapeDtypeStruct(out_dims, out_dtype)
    semantics = ("parallel", "parallel", "arbitrary")
    if side is None:
        in_specs, operands, out_specs, scratch = [], (), out_spec, []
    else:
        in_specs, operands, scratch = [ANY] * s_in, tuple(side.operands), list(side.sems)
        out_specs, out_shape = [out_spec] + [ANY] * s_out, [out_shape] + list(side.out_shapes)
        semantics = ("arbitrary",) * 3
    res = pl.pallas_call(
        body, name=name, grid=grid,
        in_specs=[a_spec, b_spec] + ([o_spec] if has_add else []) + in_specs, out_specs=out_specs,
        out_shape=out_shape, scratch_shapes=[pltpu.VMEM((tm, tn), F32)] + scratch,
        compiler_params=pltpu.CompilerParams(dimension_semantics=semantics, vmem_limit_bytes=VMEM_LIMIT),
    )(*((a, b) + ((addend,) if has_add else ()) + operands))
    return res if side is None else (res[0], res[1:])


ROW_TILE = 256


def _rmsnorm_fwd(x, g, name):
    L, D = x.shape

    def body(x_ref, g_ref, u_ref):
        xv = x_ref[...]
        r = lax.rsqrt(jnp.mean(xv * xv, axis=1, keepdims=True) + EPS)
        u_ref[...] = (xv * r * g_ref[...]).astype(BF16)

    return pl.pallas_call(
        body, name=name, grid=(L // ROW_TILE,),
        in_specs=[pl.BlockSpec((ROW_TILE, D), lambda i: (i, 0)), pl.BlockSpec((1, D), lambda i: (0, 0))],
        out_specs=pl.BlockSpec((ROW_TILE, D), lambda i: (i, 0)),
        out_shape=jax.ShapeDtypeStruct((L, D), BF16),
        compiler_params=pltpu.CompilerParams(dimension_semantics=("parallel",), vmem_limit_bytes=VMEM_LIMIT),
    )(x, g)


def _rmsnorm_bwd(du, x, g, dres, name):
    L, D = x.shape

    def body(du_ref, x_ref, g_ref, dres_ref, dx_ref, dxb_ref, dg_ref):
        i = pl.program_id(0)
        xv, duv = x_ref[...], du_ref[...]
        r = lax.rsqrt(jnp.mean(xv * xv, axis=1, keepdims=True) + EPS)
        t = duv * g_ref[...]
        dx = dres_ref[...] + r * t - xv * (r * r * r) * jnp.mean(t * xv, axis=1, keepdims=True)
        dx_ref[...] = dx
        dxb_ref[...] = dx.astype(BF16)
        dg = jnp.sum(duv * xv * r, axis=0, keepdims=True)

        @pl.when(i == 0)
        def _():
            dg_ref[...] = dg

        @pl.when(i != 0)
        def _():
            dg_ref[...] += dg

    row = pl.BlockSpec((ROW_TILE, D), lambda i: (i, 0))
    vec = pl.BlockSpec((1, D), lambda i: (0, 0))
    return pl.pallas_call(
        body, name=name, grid=(L // ROW_TILE,),
        in_specs=[row, row, vec, row], out_specs=[row, row, vec],
        out_shape=[jax.ShapeDtypeStruct((L, D), F32), jax.ShapeDtypeStruct((L, D), BF16), jax.ShapeDtypeStruct((1, D), F32)],
        compiler_params=pltpu.CompilerParams(dimension_semantics=("arbitrary",), vmem_limit_bytes=VMEM_LIMIT),
    )(du, x, g, dres)


def _loss_head(h, target, g):
    L, D = h.shape

    def body(h_ref, t_ref, g_ref, dh_ref, dhb_ref, loss_ref, dg_ref):
        i = pl.program_id(0)
        hv = h_ref[...]
        r = lax.rsqrt(jnp.mean(hv * hv, axis=1, keepdims=True) + EPS)
        diff = hv * r * g_ref[...] - t_ref[...]
        part = 0.5 * jnp.sum(jnp.mean(diff * diff, axis=1, keepdims=True), axis=0, keepdims=True)
        dy = diff * (1.0 / D)
        t = dy * g_ref[...]
        dh = r * t - hv * (r * r * r) * jnp.mean(t * hv, axis=1, keepdims=True)
        dh_ref[...] = dh
        dhb_ref[...] = dh.astype(BF16)
        dg = jnp.sum(dy * hv * r, axis=0, keepdims=True)

        @pl.when(i == 0)
        def _():
            dg_ref[...] = dg
            loss_ref[...] = jnp.broadcast_to(part, (1, 128))

        @pl.when(i != 0)
        def _():
            dg_ref[...] += dg
            loss_ref[...] += jnp.broadcast_to(part, (1, 128))

    row = pl.BlockSpec((ROW_TILE, D), lambda i: (i, 0))
    vec = pl.BlockSpec((1, D), lambda i: (0, 0))
    return pl.pallas_call(
        body, name="loss_head", grid=(L // ROW_TILE,),
        in_specs=[row, row, vec], out_specs=[row, row, pl.BlockSpec((1, 128), lambda i: (0, 0)), vec],
        out_shape=[jax.ShapeDtypeStruct((L, D), F32), jax.ShapeDtypeStruct((L, D), BF16), jax.ShapeDtypeStruct((1, 128), F32),
                   jax.ShapeDtypeStruct((1, D), F32)],
        compiler_params=pltpu.CompilerParams(dimension_semantics=("arbitrary",), vmem_limit_bytes=VMEM_LIMIT),
    )(h, target, g)


FF_TILE = 1408


def _swiglu_fwd(gate, up):
    L, N = gate.shape

    def body(g_ref, u_ref, o_ref):
        o_ref[...] = (_silu(g_ref[...]) * u_ref[...]).astype(BF16)

    blk = pl.BlockSpec((ROW_TILE, FF_TILE), lambda i, j: (i, j))
    return pl.pallas_call(
        body, name="swiglu_fwd", grid=(L // ROW_TILE, N // FF_TILE), in_specs=[blk, blk], out_specs=blk,
        out_shape=jax.ShapeDtypeStruct((L, N), BF16),
        compiler_params=pltpu.CompilerParams(dimension_semantics=("parallel", "parallel"), vmem_limit_bytes=VMEM_LIMIT),
    )(gate, up)


def _swiglu_bwd(dact, gate, up):
    L, N = gate.shape

    def body(d_ref, g_ref, u_ref, dg_ref, du_ref):
        gv, d = g_ref[...], d_ref[...]
        s = _sigmoid(gv)
        dg_ref[...] = (d * u_ref[...] * (s * (1.0 + gv * (1.0 - s)))).astype(BF16)
        du_ref[...] = (d * gv * s).astype(BF16)

    blk = pl.BlockSpec((ROW_TILE, FF_TILE), lambda i, j: (i, j))
    return pl.pallas_call(
        body, name="swiglu_bwd", grid=(L // ROW_TILE, N // FF_TILE), in_specs=[blk, blk, blk], out_specs=[blk, blk],
        out_shape=[jax.ShapeDtypeStruct((L, N), BF16), jax.ShapeDtypeStruct((L, N), BF16)],
        compiler_params=pltpu.CompilerParams(dimension_semantics=("parallel", "parallel"), vmem_limit_bytes=VMEM_LIMIT),
    )(dact, gate, up)


def _adamw(w, g, m, v, rows, name):
    R, C = w.shape
    assert R % rows == 0, (name, R, rows)

    def body(w_ref, g_ref, m_ref, v_ref, d_ref, nm_ref, nv_ref):
        gv = g_ref[...]
        mn = ADAM_B1 * m_ref[...] + (1.0 - ADAM_B1) * gv
        vn = ADAM_B2 * v_ref[...] + (1.0 - ADAM_B2) * (gv * gv)
        m_hat = mn / (1.0 - ADAM_B1 ** ADAM_STEP)
        v_hat = vn / (1.0 - ADAM_B2 ** ADAM_STEP)
        d_ref[...] = -ADAM_LR * (m_hat / (jnp.sqrt(v_hat) + ADAM_EPS) + ADAM_WD * w_ref[...])
        nm_ref[...] = mn
        nv_ref[...] = vn

    blk = pl.BlockSpec((rows, C), lambda i: (i, 0))
    out = jax.ShapeDtypeStruct((R, C), F32)
    return pl.pallas_call(
        body, name=name, grid=(R // rows,), in_specs=[blk] * 4, out_specs=[blk] * 3, out_shape=[out] * 3,
        compiler_params=pltpu.CompilerParams(dimension_semantics=("parallel",), vmem_limit_bytes=VMEM_LIMIT),
    )(w, g, m, v)


ANY = pl.BlockSpec(memory_space=pl.ANY)
N_DEV = 8


def _allgather_small(p, name):
    R, C = p.shape

    def body(p_ref, out_ref, send_sems, recv_sems, local_sem):
        x, y, c = lax.axis_index("x"), lax.axis_index("y"), lax.axis_index("c")
        me = 4 * x + 2 * y + c
        mine = pltpu.make_async_copy(p_ref, out_ref.at[me], local_sem)
        mine.start()

        def peer(d):
            return (x ^ ((d >> 2) & 1), y ^ ((d >> 1) & 1), c ^ (d & 1))

        def copy(d, block):
            return pltpu.make_async_remote_copy(src_ref=p_ref, dst_ref=out_ref.at[block], send_sem=send_sems.at[d - 1],
                                                recv_sem=recv_sems.at[d - 1], device_id=peer(d), device_id_type=MESH)

        sends = [copy(d, me) for d in range(1, N_DEV)]
        for cp in sends:
            cp.start()
        for d in range(1, N_DEV):
            px, py, pc = peer(d)
            copy(d, 4 * px + 2 * py + pc).wait_recv()
        for cp in sends:
            cp.wait_send()
        mine.wait()

    return pl.pallas_call(
        body, name=name, out_shape=jax.ShapeDtypeStruct((N_DEV, R, C), p.dtype),
        in_specs=[pl.BlockSpec(memory_space=pltpu.VMEM)], out_specs=pl.BlockSpec(memory_space=pltpu.VMEM),
        scratch_shapes=[pltpu.SemaphoreType.DMA((N_DEV - 1,)), pltpu.SemaphoreType.DMA((N_DEV - 1,)), pltpu.SemaphoreType.DMA],
    )(p)


def _other_chips(x, y):
    return [(1 - x, y), (x, 1 - y), (1 - x, 1 - y)]


def _run_side(side, name):
    s_in, s_out = len(side.operands), len(side.out_shapes)

    def body(*refs):
        parts = (refs[:s_in], refs[s_in:s_in + s_out], refs[s_in + s_out:])
        side.start(*parts)
        side.finish(*parts)

    return pl.pallas_call(body, name=name, out_shape=list(side.out_shapes), in_specs=[ANY] * s_in, out_specs=[ANY] * s_out,
                          scratch_shapes=list(side.sems))(*side.operands)


def _gather_side(shards):
    T = len(shards)
    halves = [s.shape[0] // 2 for s in shards]

    def tools(ins, outs, sems):
        send_sems, recv_sems = sems
        x, y, c = lax.axis_index("x"), lax.axis_index("y"), lax.axis_index("c")

        def rows(t, px, py, pc):
            return outs[t].at[2 * px + py, pl.ds(pc * halves[t], halves[t]), :]

        def copy(t, k, block, to, own=False):
            src = ins[t].at[pl.ds(c * halves[t], halves[t]), :] if own else rows(t, *block)
            return pltpu.make_async_remote_copy(src_ref=src, dst_ref=rows(t, *block), send_sem=send_sems.at[t, k],
                                                recv_sem=recv_sems.at[t, k], device_id=to, device_id_type=MESH)

        return (x, y, c), _other_chips(x, y), copy

    def start(ins, outs, sems):
        (x, y, c), chips, copy = tools(ins, outs, sems)
        for t in range(T):
            for j, chip in enumerate(chips):
                copy(t, j, (x, y, c), (*chip, c), own=True).start()

    def finish(ins, outs, sems):
        (x, y, c), chips, copy = tools(ins, outs, sems)
        for t in range(T):
            for j, chip in enumerate(chips):
                copy(t, j, (*chip, c), (x, y, c)).wait_recv()
                copy(t, 3 + j, (*chip, c), (x, y, 1 - c)).start()
        for t in range(T):
            for j, chip in enumerate(chips):
                copy(t, 3 + j, (*chip, 1 - c), (x, y, c)).wait_recv()
        for t in range(T):
            for j, chip in enumerate(chips):
                copy(t, j, (x, y, c), (*chip, c), own=True).wait_send()
                copy(t, 3 + j, (*chip, c), (x, y, 1 - c)).wait_send()

    return _Side(shards, [jax.ShapeDtypeStruct((4,) + s.shape, s.dtype) for s in shards],
                 [pltpu.SemaphoreType.DMA((T, 6)), pltpu.SemaphoreType.DMA((T, 6))], start, finish)


def _with_own_shard(gathered, shard, chip):
    return lax.dynamic_update_slice(gathered, shard[None], (chip,) + (0,) * shard.ndim)


def _pair_exchange(grads, name):
    T = len(grads)

    def body(*refs):
        ins, outs = refs[:T], refs[T:2 * T]
        send_sems, recv_sems = refs[2 * T:]
        x, y, c = lax.axis_index("x"), lax.axis_index("y"), lax.axis_index("c")
        cps = [pltpu.make_async_remote_copy(src_ref=ins[t].at[1 - c], dst_ref=outs[t], send_sem=send_sems.at[t],
                                            recv_sem=recv_sems.at[t], device_id=(x, y, 1 - c), device_id_type=MESH)
               for t in range(T)]
        for cp in cps:
            cp.start()
        for cp in cps:
            cp.wait()

    return pl.pallas_call(
        body, name=name,
        out_shape=[jax.ShapeDtypeStruct(g.shape[1:], g.dtype) for g in grads],
        in_specs=[ANY] * T, out_specs=[ANY] * T,
        scratch_shapes=[pltpu.SemaphoreType.DMA((T,)), pltpu.SemaphoreType.DMA((T,))],
    )(*grads)


def _scatter_side(parts):
    T = len(parts)

    def tools(ins, outs, sems):
        send_sems, recv_sems = sems
        x, y, c = lax.axis_index("x"), lax.axis_index("y"), lax.axis_index("c")

        def copy(t, j, src_slot, dst_slot, chip):
            return pltpu.make_async_remote_copy(src_ref=ins[t].at[src_slot], dst_ref=outs[t].at[dst_slot],
                                                send_sem=send_sems.at[t, j], recv_sem=recv_sems.at[t, j],
                                                device_id=(*chip, c), device_id_type=MESH)

        return 2 * x + y, _other_chips(x, y), copy

    def start(ins, outs, sems):
        my_chip, chips, copy = tools(ins, outs, sems)
        for t in range(T):
            for j, (px, py) in enumerate(chips):
                copy(t, j, 2 * px + py, my_chip, (px, py)).start()

    def finish(ins, outs, sems):
        my_chip, chips, copy = tools(ins, outs, sems)
        for t in range(T):
            for j, (px, py) in enumerate(chips):
                copy(t, j, my_chip, 2 * px + py, (px, py)).wait_recv()
        for t in range(T):
            for j, (px, py) in enumerate(chips):
                copy(t, j, 2 * px + py, my_chip, (px, py)).wait_send()

    return _Side(parts, [jax.ShapeDtypeStruct(p.shape, p.dtype) for p in parts],
                 [pltpu.SemaphoreType.DMA((T, 3)), pltpu.SemaphoreType.DMA((T, 3))], start, finish)


def _pair_share(fulls):
    T = len(fulls)

    def body(*refs):
        outs = refs[T:2 * T]
        send_sems, recv_sems = refs[2 * T:]
        x, y, c = lax.axis_index("x"), lax.axis_index("y"), lax.axis_index("c")

        def copy(t, half):
            return pltpu.make_async_remote_copy(src_ref=outs[t].at[half], dst_ref=outs[t].at[half], send_sem=send_sems.at[t],
                                                recv_sem=recv_sems.at[t], device_id=(x, y, 1 - c), device_id_type=MESH)

        for t in range(T):
            copy(t, c).start()
        for t in range(T):
            copy(t, 1 - c).wait_recv()
        for t in range(T):
            copy(t, c).wait_send()

    return pl.pallas_call(
        body, name="grad_pair_share", out_shape=[jax.ShapeDtypeStruct(f.shape, f.dtype) for f in fulls],
        in_specs=[ANY] * T, out_specs=[ANY] * T, input_output_aliases={t: t for t in range(T)},
        scratch_shapes=[pltpu.SemaphoreType.DMA((T,)), pltpu.SemaphoreType.DMA((T,))],
    )(*fulls)


def _row_tile(hr):
    return next(r for r in (128, 64, 32, 16, 8) if hr % r == 0)


def _pair_add(g, got, c, name):
    _, _, hr, C = g.shape
    tr = _row_tile(hr)

    def body(c_ref, g_ref, r_ref, o_ref):
        o_ref[...] = (g_ref[0].astype(F32) + r_ref[...].astype(F32)).astype(o_ref.dtype)

    return pl.pallas_call(
        body, name=name,
        grid_spec=pltpu.PrefetchScalarGridSpec(
            num_scalar_prefetch=1, grid=(4, hr // tr),
            in_specs=[pl.BlockSpec((1, 1, tr, C), lambda k, i, c_ref: (c_ref[0], k, i, 0)),
                      pl.BlockSpec((1, tr, C), lambda k, i, c_ref: (k, i, 0))],
            out_specs=pl.BlockSpec((1, tr, C), lambda k, i, c_ref: (k, i, 0))),
        out_shape=jax.ShapeDtypeStruct(got.shape, g.dtype),
        compiler_params=pltpu.CompilerParams(dimension_semantics=("parallel", "parallel"), vmem_limit_bytes=VMEM_LIMIT),
    )(jnp.reshape(c, (1,)).astype(jnp.int32), g, got)


def _chip_sum(parts, slots, chip, c, name):
    _, hr, C = parts.shape
    tr = _row_tile(hr)

    def body(chip_ref, half_ref, own_ref, a_ref, b_ref, c_ref, o_ref):
        acc = own_ref[...].astype(F32)
        for r in (a_ref, b_ref, c_ref):
            acc = acc + r[...].astype(F32)
        o_ref[...] = acc

    other = lambda j: pl.BlockSpec((1, tr, C), lambda i, chip_ref, half_ref: ((chip_ref[0] + j) % 4, i, 0))
    return pl.pallas_call(
        body, name=name,
        grid_spec=pltpu.PrefetchScalarGridSpec(
            num_scalar_prefetch=2, grid=(hr // tr,),
            in_specs=[pl.BlockSpec((1, tr, C), lambda i, chip_ref, half_ref: (chip_ref[0], i, 0)),
                      other(1), other(2), other(3)],
            out_specs=pl.BlockSpec((1, tr, C), lambda i, chip_ref, half_ref: (half_ref[0], i, 0))),
        out_shape=jax.ShapeDtypeStruct((2, hr, C), F32),
        compiler_params=pltpu.CompilerParams(dimension_semantics=("parallel",), vmem_limit_bytes=VMEM_LIMIT),
    )(jnp.reshape(chip, (1,)).astype(jnp.int32), jnp.reshape(c, (1,)).astype(jnp.int32), parts, slots, slots, slots)


def _sum_slots(parts, name):
    n, hr, C = parts.shape
    tr = _row_tile(hr)

    def body(p_ref, o_ref):
        acc = p_ref[0].astype(F32)
        for k in range(1, n):
            acc = acc + p_ref[k].astype(F32)
        o_ref[...] = acc

    return pl.pallas_call(
        body, name=name, grid=(hr // tr,),
        in_specs=[pl.BlockSpec((n, tr, C), lambda i: (0, i, 0))], out_specs=pl.BlockSpec((tr, C), lambda i: (i, 0)),
        out_shape=jax.ShapeDtypeStruct((hr, C), F32),
        compiler_params=pltpu.CompilerParams(dimension_semantics=("parallel",), vmem_limit_bytes=VMEM_LIMIT),
    )(parts)


PACK_ROWS, PACK_COLS = 16, 3072
_PACK = [("norm_mix_g", 0, 1, 2048), ("conv_w", 1, 4, 3072), ("conv_b", 5, 1, 3072), ("dt_bias", 6, 1, 32),
         ("a_log", 7, 1, 32), ("d_skip", 8, 1, 32), ("ssd_norm_g", 9, 1, 2048), ("i_bias", 10, 1, 8), ("f_bias", 11, 1, 8),
         ("mlstm_norm_g", 12, 1, 2048), ("norm_ffn_g", 13, 1, 2048), ("final_norm_g", 14, 1, 2048), ("loss", 15, 1, 1)]
_WEIGHTS = ["norm_mix_g", "w_in", "conv_w", "conv_b", "dt_bias", "a_log", "d_skip", "ssd_norm_g", "i_bias", "f_bias",
            "mlstm_norm_g", "w_out", "norm_ffn_g", "w_gate", "w_up", "w_down", "final_norm_g"]
_BIG = ["w_in", "w_out", "w_gate", "w_up", "w_down"]


def _pack(vals):
    rows = []
    for name, _, r, w in _PACK:
        v = vals.get(name)
        v = jnp.zeros((r, w), F32) if v is None else v.reshape(r, w).astype(F32)
        rows.append(jnp.concatenate([v, jnp.zeros((r, PACK_COLS - w), F32)], axis=1) if w < PACK_COLS else v)
    return jnp.concatenate(rows, axis=0)


def _unpack(p):
    return {name: p[r0:r0 + r, :w] for name, r0, r, w in _PACK}


def _to_halves_cols(dw, cols):
    R = dw.shape[0]
    return dw.reshape(2, R // 2, 4, cols).transpose(0, 2, 1, 3)


def kernel(x, norm_mix_g, w_in, conv_w, conv_b, dt_bias, a_log, d_skip, ssd_norm_g, i_bias, f_bias, mlstm_norm_g, w_out, norm_ffn_g, w_gate, w_up, w_down, final_norm_g, loss_target, m_norm_mix_g, m_w_in, m_conv_w, m_conv_b, m_dt_bias, m_a_log, m_d_skip, m_ssd_norm_g, m_i_bias, m_f_bias, m_mlstm_norm_g, m_w_out, m_norm_ffn_g, m_w_gate, m_w_up, m_w_down, m_final_norm_g, v_norm_mix_g, v_w_in, v_conv_w, v_conv_b, v_dt_bias, v_a_log, v_d_skip, v_ssd_norm_g, v_i_bias, v_f_bias, v_mlstm_norm_g, v_w_out, v_norm_ffn_g, v_w_gate, v_w_up, v_w_down, v_final_norm_g):
    weights = dict(norm_mix_g=norm_mix_g, w_in=w_in, conv_w=conv_w, conv_b=conv_b, dt_bias=dt_bias, a_log=a_log, d_skip=d_skip,
                   ssd_norm_g=ssd_norm_g, i_bias=i_bias, f_bias=f_bias, mlstm_norm_g=mlstm_norm_g, w_out=w_out,
                   norm_ffn_g=norm_ffn_g, w_gate=w_gate, w_up=w_up, w_down=w_down, final_norm_g=final_norm_g)
    mom1 = dict(norm_mix_g=m_norm_mix_g, w_in=m_w_in, conv_w=m_conv_w, conv_b=m_conv_b, dt_bias=m_dt_bias, a_log=m_a_log,
                d_skip=m_d_skip, ssd_norm_g=m_ssd_norm_g, i_bias=m_i_bias, f_bias=m_f_bias, mlstm_norm_g=m_mlstm_norm_g,
                w_out=m_w_out, norm_ffn_g=m_norm_ffn_g, w_gate=m_w_gate, w_up=m_w_up, w_down=m_w_down,
                final_norm_g=m_final_norm_g)
    mom2 = dict(norm_mix_g=v_norm_mix_g, w_in=v_w_in, conv_w=v_conv_w, conv_b=v_conv_b, dt_bias=v_dt_bias, a_log=v_a_log,
                d_skip=v_d_skip, ssd_norm_g=v_ssd_norm_g, i_bias=v_i_bias, f_bias=v_f_bias, mlstm_norm_g=v_mlstm_norm_g,
                w_out=v_w_out, norm_ffn_g=v_norm_ffn_g, w_gate=v_w_gate, w_up=v_w_up, w_down=v_w_down,
                final_norm_g=v_final_norm_g)
    xi, yi, ci = lax.axis_index("x"), lax.axis_index("y"), lax.axis_index("c")
    chip = 2 * xi + yi
    xs, tgt = x[0], loss_target[0]

    shards = {n: weights[n][0].astype(BF16) for n in _BIG}
    gathered = lambda n, g: _with_own_shard(g, shards[n], chip)
    by_cols = lambda g, width: jnp.transpose(g, (1, 0, 2)).reshape(D_MODEL, width)
    (g_in,) = _run_side(_gather_side([shards["w_in"]]), "allgather_w_in")
    W_ssd, W_ml, W_small = _split_w_in(by_cols(gathered("w_in", g_in), IN_WIDTH))
    cw_all = _allgather_small(jnp.concatenate([conv_w[0], jnp.zeros((4, 768), F32)], axis=0), "allgather_conv_w")
    conv_w_full = jnp.concatenate([cw_all[2 * k, :SSD_CONV] for k in range(4)], axis=1)
    cwg, cbg = _conv_to_groups(conv_w_full), _conv_to_groups(conv_b)
    dtb_row, al_row = _small_row(dt_bias, 0), _small_row(a_log, 0)
    ib_row, fb_row = _small_row(i_bias, LANE_I), _small_row(f_bias, LANE_F)
    dskip_lane = jnp.repeat(d_skip, 64, axis=1)
    fng = final_norm_g[None]

    u1 = _rmsnorm_fwd(xs, norm_mix_g, "norm_mix_fwd")
    p_ssd, (g_out,) = _matmul(u1, W_ssd, "nn", F32, 1024, 1024, 2048, "proj_ssd", side=_gather_side([shards["w_out"]]))
    p_ml, (g_gate,) = _matmul(u1, W_ml, "nn", F32, 1024, 1024, 2048, "proj_ml", side=_gather_side([shards["w_gate"]]))
    p_small = _matmul(u1, W_small, "nn", F32, 1024, 128, 2048, "proj_small")
    y_ssd, ssd_st = _ssd_forward(p_ssd, p_small, cwg, cbg, dtb_row, al_row, dskip_lane, ssd_norm_g)
    y_ml, ml_c, ml_nm = _mlstm_forward(p_ml, p_small, ib_row, fb_row, mlstm_norm_g)
    y_mix = jnp.concatenate([y_ssd, y_ml], axis=1)
    W_out = gathered("w_out", g_out).reshape(2 * SSD_WIDTH, D_MODEL)
    h1, (g_up,) = _matmul(y_mix, W_out, "nn", F32, 1024, 1024, 2048, "out_proj", addend=xs,
                          side=_gather_side([shards["w_up"]]))
    u2 = _rmsnorm_fwd(h1, norm_ffn_g, "norm_ffn_fwd")
    W_gate = by_cols(gathered("w_gate", g_gate), D_FF)
    gate, (g_down,) = _matmul(u2, W_gate, "nn", F32, 1024, FF_TILE, 2048, "ffn_gate", side=_gather_side([shards["w_down"]]))
    W_up = by_cols(gathered("w_up", g_up), D_FF)
    W_down = gathered("w_down", g_down).reshape(D_FF, D_MODEL)
    up = _matmul(u2, W_up, "nn", F32, 1024, FF_TILE, 2048, "ffn_up")
    act = _swiglu_fwd(gate, up)
    h2 = _matmul(act, W_down, "nn", F32, 1024, 1024, FF_TILE, "ffn_down", addend=h1)
    dh2, dh2_b, loss_row, d_fng = _loss_head(h2, tgt, fng)

    dact = _matmul(dh2_b, W_down, "nt", F32, 1024, FF_TILE, 2048, "d_act")
    dW_down = _matmul(act, dh2_b, "tn", BF16, FF_TILE, 1024, 1024, "dw_down", layout="rows4")
    d_gate, d_up = _swiglu_bwd(dact, gate, up)
    du2 = _matmul(d_gate, W_gate, "nt", F32, 1024, 1024, FF_TILE, "du2_gate")
    du2 = _matmul(d_up, W_up, "nt", F32, 1024, 1024, FF_TILE, "du2_up", addend=du2)
    dW_gate = _matmul(u2, d_gate, "tn", BF16, 1024, FF_TILE, 1024, "dw_gate", layout="cols4")
    dW_up = _matmul(u2, d_up, "tn", BF16, 1024, FF_TILE, 1024, "dw_up", layout="cols4")
    dh1, dh1_b, d_ffn_g = _rmsnorm_bwd(du2, h1, norm_ffn_g, dh2, "norm_ffn_bwd")
    dy_mix = _matmul(dh1_b, W_out, "nt", F32, 1024, 1024, 2048, "d_mix")
    dW_out = _matmul(y_mix, dh1_b, "tn", BF16, 1024, 1024, 1024, "dw_out", layout="rows4")
    early = ["w_out", "w_gate", "w_up", "w_down"]
    early_g = [dW_out, dW_gate, dW_up, dW_down]
    early_got = _pair_exchange(early_g, "grad_pair_exchange_early")
    parts = {n: _pair_add(g, r, ci, "grad_pair_add_" + n) for g, r, n in zip(early_g, early_got, early)}
    (d_ssd, d_small, d_cw, d_cb, d_dtb, d_alog, d_dskip, d_sng), early_slots = _ssd_backward(
        p_ssd, p_small, cwg, cbg, dtb_row, al_row, dskip_lane, ssd_norm_g, ssd_st, dy_mix,
        side=_scatter_side([parts[n] for n in early]))
    slots = dict(zip(early, early_slots))
    d_ml, d_small, d_ib, d_fb, d_mng = _mlstm_backward(p_ml, p_small, ib_row, fb_row, mlstm_norm_g, ml_c, ml_nm, dy_mix, d_small)
    du1 = _matmul(d_ssd, W_ssd, "nt", F32, 1024, 1024, 1280, "du1_ssd")
    du1 = _matmul(d_ml, W_ml, "nt", F32, 1024, 1024, 1024, "du1_ml", addend=du1)
    du1 = _matmul(d_small, W_small, "nt", F32, 1024, 1024, 128, "du1_small", addend=du1)
    dW_ssd = _matmul(u1, d_ssd, "tn", BF16, 1024, 1280, 1024, "dw_ssd")
    dW_ml = _matmul(u1, d_ml, "tn", BF16, 1024, 1024, 1024, "dw_ml")
    dW_small = _matmul(u1, d_small, "tn", BF16, 1024, 128, 1024, "dw_small")
    grad_x, _, d_mix_g = _rmsnorm_bwd(du1, xs, norm_mix_g, dh1, "norm_mix_bwd")
    dW_in = _merge_w_in(dW_ssd, dW_ml, dW_small)

    g_w_in = _to_halves_cols(dW_in, 2828)
    (got_in,) = _pair_exchange([g_w_in], "grad_pair_exchange_w_in")
    parts["w_in"] = _pair_add(g_w_in, got_in, ci, "grad_pair_add_w_in")
    (slots["w_in"],) = _run_side(_scatter_side([parts["w_in"]]), "grad_chip_exchange_w_in")
    fulls = _pair_share([_chip_sum(parts[n], slots[n], chip, ci, "grad_chip_sum_" + n) for n in _BIG])
    grads = {n: f.reshape(weights[n].shape[1:]) for n, f in zip(_BIG, fulls)}

    small = _pack(dict(norm_mix_g=d_mix_g, conv_w=_conv_from_groups(d_cw), conv_b=_conv_from_groups(d_cb),
                       dt_bias=d_dtb[:, 0:32], a_log=d_alog[:, 0:32], d_skip=d_dskip.reshape(SSD_HEADS, 64).sum(axis=1),
                       ssd_norm_g=d_sng, i_bias=d_ib[:, LANE_I:LANE_I + 8], f_bias=d_fb[:, LANE_F:LANE_F + 8],
                       mlstm_norm_g=d_mng, norm_ffn_g=d_ffn_g, final_norm_g=d_fng, loss=loss_row[:, 0:1]))
    total = _sum_slots(_allgather_small(small, "allgather_small_grads"), "small_grad_sum")
    small_w = {n: weights[n] for n in _WEIGHTS if n not in _BIG and n != "conv_w"}
    sd, sm, sv = _adamw(_pack(small_w), total, _pack({n: mom1[n] for n in small_w}), _pack({n: mom2[n] for n in small_w}),
                        PACK_ROWS, "adamw_small")
    tot, sd, sm, sv = _unpack(total), _unpack(sd), _unpack(sm), _unpack(sv)
    for n in small_w:
        grads[n] = tot[n].reshape(weights[n].shape)
    grads["conv_w"] = lax.dynamic_slice_in_dim(tot["conv_w"], chip * 768, 768, axis=1)
    loss = tot["loss"][0, 0]

    delta, new_m, new_v = {}, {}, {}
    for n in _BIG + ["conv_w"]:
        w2 = weights[n][0]
        d, nm, nv = _adamw(w2, grads[n], mom1[n][0], mom2[n][0], _row_tile(w2.shape[0]) if n != "conv_w" else SSD_CONV,
                           "adamw_" + n)
        delta[n], new_m[n], new_v[n] = d[None], nm[None], nv[None]
        grads[n] = grads[n][None]
    for n in small_w:
        delta[n], new_m[n], new_v[n] = (t[n].reshape(weights[n].shape) for t in (sd, sm, sv))
    return (loss, grad_x[None], *[grads[n] for n in _WEIGHTS], *[delta[n] for n in _WEIGHTS],
            *[new_m[n] for n in _WEIGHTS], *[new_v[n] for n in _WEIGHTS])
```

```python
import functools

import jax
import jax.numpy as jnp
import numpy as np
from jax import lax
from jax.experimental import pallas as pl
from jax.experimental.pallas import tpu as pltpu

F32 = jnp.float32
BF16 = jnp.bfloat16

D_MODEL = 2048
SSD_HEADS = 32
SSD_GROUPS = 4
SSD_STATE = 128
SSD_WIDTH = 2048
SSD_CONV = 4
SSD_GROUP_COLS = 1280
SSD_XBC_COLS = 768
ML_HEADS = 8
ML_HEAD_COLS = 768
ML_DK = 128
ML_DV = 256
CHUNK = 128
SMALL_COLS = 128
LANE_I = 32
LANE_F = 40
D_FF = 5632
GATE_SOFTCAP = 15.0
EPS = 1e-6
ADAM_LR, ADAM_B1, ADAM_B2, ADAM_EPS, ADAM_WD, ADAM_STEP = 0.001, 0.9, 0.999, 1e-8, 0.01, 10
MESH = pl.DeviceIdType.MESH
VMEM_LIMIT = 56 * 1024 * 1024


def _dg(a, b, ca, cb):
    return lax.dot_general(a.astype(BF16), b.astype(BF16), (((ca,), (cb,)), ((), ())), preferred_element_type=F32)


@jax.custom_vjp
def _mm(a, b):
    return _dg(a, b, 1, 0)


_mm.defvjp(lambda a, b: (_dg(a, b, 1, 0), (a, b)),
           lambda r, g: (_dg(g, r[1], 1, 1), _dg(r[0], g, 0, 0)))


@jax.custom_vjp
def _mm_nt(a, b):
    return _dg(a, b, 1, 1)


_mm_nt.defvjp(lambda a, b: (_dg(a, b, 1, 1), (a, b)),
              lambda r, g: (_dg(g, r[1], 1, 0), _dg(g, r[0], 0, 0)))


@jax.custom_vjp
def _mm_tn(a, b):
    return _dg(a, b, 0, 0)


_mm_tn.defvjp(lambda a, b: (_dg(a, b, 0, 0), (a, b)),
              lambda r, g: (_dg(r[1], g, 1, 1), _dg(r[0], g, 1, 0)))


def _tri(lower):
    r = lax.broadcasted_iota(jnp.int32, (CHUNK, CHUNK), 0)
    c = lax.broadcasted_iota(jnp.int32, (CHUNK, CHUNK), 1)
    return ((r >= c) if lower else (r <= c)).astype(F32)


def _dg32(t, x):
    return lax.dot_general(t, x, (((1,), (0,)), ((), ())), precision=lax.Precision.HIGHEST, preferred_element_type=F32)


@jax.custom_vjp
def _cumsum(x):
    return _dg32(_tri(True), x)


_cumsum.defvjp(lambda x: (_dg32(_tri(True), x), None), lambda r, g: (_dg32(_tri(False), g),))


_sigmoid = jax.nn.sigmoid


def _silu(x):
    return x * _sigmoid(x)


def _softplus(x):
    return jnp.maximum(x, 0.0) + jnp.log(1.0 + jnp.exp(-jnp.abs(x)))


def _lane_col(m, lane, h):
    return jnp.sum(jnp.where(lane == h, m, 0.0), axis=1, keepdims=True)


SSD_GPS = 4


def _ssd_math(convs, zs, smallb, S_in, dtb_row, alog_row, dskip, ng, g0):
    dt_all = _softplus(smallb + dtb_row)
    a_all = dt_all * (-jnp.exp(alog_row))
    prep = (dt_all, _cumsum(a_all), jnp.sum(a_all, axis=0, keepdims=True))
    ys, S_out = [], []
    for k in range(SSD_GPS):
        y, S = _ssd_group(convs[k][:, 0:512], convs[k][:, 512:640], convs[k][:, 640:768], zs[k], prep, S_in[k],
                          dskip[:, 512 * k:512 * (k + 1)], ng[:, 512 * k:512 * (k + 1)], g0 + k)
        ys.append(y)
        S_out.append(S)
    return tuple(ys), tuple(S_out)


def _ssd_group(cx, cB, cC, z, prep, S_in, dskip, ng, g):
    dt_all, acum_all, alast_all = prep
    lane = lax.broadcasted_iota(jnp.int32, (1, CHUNK), 1)
    row_i = lax.broadcasted_iota(jnp.int32, (CHUNK, CHUNK), 0)
    col_i = lax.broadcasted_iota(jnp.int32, (CHUNK, CHUNK), 1)
    causal = row_i >= col_i
    half = lane < 64
    rhalf = lax.broadcasted_iota(jnp.int32, (CHUNK, 1), 0) < 64
    xs, Bm, Cm = _silu(cx), _silu(cB), _silu(cC)
    cb = _mm_nt(Cm, Bm)

    def lmat(ac):
        acb = jnp.broadcast_to(ac, (CHUNK, CHUNK))
        return jnp.exp(jnp.where(causal, acb - acb.T, -jnp.inf))

    ys, S_out = [], []
    for j in range(4):
        h0 = 8 * g + 2 * j
        ac0, ac1 = _lane_col(acum_all, lane, h0), _lane_col(acum_all, lane, h0 + 1)
        dt0, dt1 = _lane_col(dt_all, lane, h0), _lane_col(dt_all, lane, h0 + 1)
        al0, al1 = _lane_col(alast_all, lane, h0), _lane_col(alast_all, lane, h0 + 1)
        Xp = xs[:, 128 * j:128 * (j + 1)]
        Xd = Xp * jnp.where(half, dt0, dt1)
        ac_sel = jnp.where(half, ac0, ac1)
        al_sel = jnp.where(half, al0, al1)
        Yd = jnp.where(half, _mm(cb * lmat(ac0), Xd), _mm(cb * lmat(ac1), Xd))
        Yoff = _mm_nt(Cm, S_in[j]) * jnp.exp(ac_sel)
        ys.append(Yd + Yoff + dskip[:, 128 * j:128 * (j + 1)] * Xp)
        S_new = _mm_tn(Xd * jnp.exp(al_sel - ac_sel), Bm)
        S_out.append(S_in[j] * jnp.exp(jnp.where(rhalf, al0, al1)) + S_new)
    y = jnp.concatenate(ys, axis=1)
    y = y * _silu(z)
    y = y * lax.rsqrt(jnp.mean(y * y, axis=1, keepdims=True) + EPS) * ng
    return y, tuple(S_out)


def _ssd_conv(blk_ref, halo_ref, cw_ref, cb_ref, ext, first, k):
    cols = slice(SSD_GROUP_COLS * k + 512, SSD_GROUP_COLS * (k + 1))
    halo = halo_ref[:, cols]
    ext[k, 0:8, :] = jnp.where(first, jnp.zeros_like(halo), halo)
    ext[k, 8:8 + CHUNK, :] = blk_ref[:, cols]
    conv = jnp.broadcast_to(cb_ref[k], (CHUNK, SSD_XBC_COLS))
    for tap in range(SSD_CONV):
        conv = conv + cw_ref[k, tap:tap + 1, :] * ext[k, pl.ds(5 + tap, CHUNK), :]
    return conv


def _ssd_specs(nc, rev):
    cc = (lambda c: nc - 1 - c) if rev else (lambda c: c)
    width = SSD_GPS * SSD_GROUP_COLS
    return [
        pl.BlockSpec((CHUNK, width), lambda c, g: (cc(c), g)),
        pl.BlockSpec((8, width), lambda c, g: (jnp.maximum(cc(c) * (CHUNK // 8) - 1, 0), g)),
        pl.BlockSpec((CHUNK, SMALL_COLS), lambda c, g: (cc(c), 0)),
        pl.BlockSpec((SSD_GPS, SSD_CONV, SSD_XBC_COLS), lambda c, g: (g, 0, 0)),
        pl.BlockSpec((SSD_GPS, 1, SSD_XBC_COLS), lambda c, g: (g, 0, 0)),
        pl.BlockSpec((1, SMALL_COLS), lambda c, g: (0, 0)),
        pl.BlockSpec((1, SMALL_COLS), lambda c, g: (0, 0)),
        pl.BlockSpec((1, SSD_GPS * 512), lambda c, g: (0, g)),
        pl.BlockSpec((1, SSD_GPS * 512), lambda c, g: (0, g)),
    ]


def _ssd_forward(proj, small, cw, cb, dtb, alog, dskip, ng):
    L = proj.shape[0]
    nc = L // CHUNK

    def body(blk_ref, halo_ref, small_ref, cw_ref, cb_ref, dtb_ref, alog_ref, dskip_ref, ng_ref, y_ref, st_ref, carry, ext):
        c, g0 = pl.program_id(0), pl.program_id(1) * SSD_GPS

        @pl.when(c == 0)
        def _():
            for k in range(SSD_GPS):
                carry[g0 + k] = jnp.zeros((4, CHUNK, CHUNK), F32)

        convs = [_ssd_conv(blk_ref, halo_ref, cw_ref, cb_ref, ext, c == 0, k) for k in range(SSD_GPS)]
        zs = [blk_ref[:, SSD_GROUP_COLS * k:SSD_GROUP_COLS * k + 512] for k in range(SSD_GPS)]
        S_in = tuple(tuple(carry[g0 + k, j] for j in range(4)) for k in range(SSD_GPS))
        for k in range(SSD_GPS):
            st_ref[0, k] = carry[g0 + k]
        ys, S_out = _ssd_math(convs, zs, small_ref[...], S_in, dtb_ref[...], alog_ref[...], dskip_ref[...], ng_ref[...], g0)
        for k in range(SSD_GPS):
            y_ref[:, 512 * k:512 * (k + 1)] = ys[k].astype(BF16)
            for j in range(4):
                carry[g0 + k, j] = S_out[k][j]

    return pl.pallas_call(
        body, name="ssd_fwd", grid=(nc, SSD_GROUPS // SSD_GPS),
        in_specs=_ssd_specs(nc, False),
        out_specs=[pl.BlockSpec((CHUNK, SSD_GPS * 512), lambda c, g: (c, g)),
                   pl.BlockSpec((1, SSD_GPS, 4, CHUNK, CHUNK), lambda c, g: (c, g, 0, 0, 0))],
        out_shape=[jax.ShapeDtypeStruct((L, SSD_WIDTH), BF16), jax.ShapeDtypeStruct((nc, SSD_GROUPS, 4, CHUNK, CHUNK), F32)],
        scratch_shapes=[pltpu.VMEM((SSD_GROUPS, 4, CHUNK, CHUNK), F32), pltpu.VMEM((SSD_GPS, 8 + CHUNK, SSD_XBC_COLS), F32)],
        compiler_params=pltpu.CompilerParams(dimension_semantics=("arbitrary", "arbitrary"), vmem_limit_bytes=VMEM_LIMIT),
    )(proj, proj, small, cw, cb, dtb, alog, dskip, ng)


def _ssd_backward(proj, small, cw, cb, dtb, alog, dskip, ng, states, dy, side=None):
    L = proj.shape[0]
    nc = L // CHUNK
    s_in, s_out = (len(side.operands), len(side.out_shapes)) if side is not None else (0, 0)
    grid = (nc, SSD_GROUPS // SSD_GPS)

    def body(*refs):
        blk_ref, halo_ref, small_ref, cw_ref, cb_ref, dtb_ref, alog_ref, dskip_ref, ng_ref, st_ref, dy_ref = refs[:11]
        dproj_ref, dsmall_ref, dcw_ref, dcb_ref, ddtb_ref, dalog_ref, ddskip_ref, dng_ref = refs[11 + s_in:19 + s_in]
        dcarry, nxt, ext, dext = refs[19 + s_in + s_out:23 + s_in + s_out]
        side_refs = (refs[11:11 + s_in], refs[19 + s_in:19 + s_in + s_out], refs[23 + s_in + s_out:])
        s, gb = pl.program_id(0), pl.program_id(1)
        g0 = gb * SSD_GPS
        c = nc - 1 - s
        if side is not None:
            pl.when(_grid_edge(grid, first=True))(lambda: side.start(*side_refs))

        @pl.when(s == 0)
        def _():
            for k in range(SSD_GPS):
                dcarry[g0 + k] = jnp.zeros((4, CHUNK, CHUNK), F32)
                nxt[g0 + k] = jnp.zeros((8, SSD_XBC_COLS), F32)
                dcw_ref[g0 + k] = jnp.zeros((SSD_CONV, SSD_XBC_COLS), F32)
                dcb_ref[g0 + k] = jnp.zeros((1, SSD_XBC_COLS), F32)
                ddskip_ref[g0 + k] = jnp.zeros((1, 512), F32)
                dng_ref[g0 + k] = jnp.zeros((1, 512), F32)

        @pl.when((s == 0) & (gb == 0))
        def _():
            ddtb_ref[...] = jnp.zeros((1, SMALL_COLS), F32)
            dalog_ref[...] = jnp.zeros((1, SMALL_COLS), F32)

        convs = [_ssd_conv(blk_ref, halo_ref, cw_ref, cb_ref, ext, c == 0, k) for k in range(SSD_GPS)]
        zs = [blk_ref[:, SSD_GROUP_COLS * k:SSD_GROUP_COLS * k + 512] for k in range(SSD_GPS)]
        S_in = tuple(tuple(st_ref[0, k, j] for j in range(4)) for k in range(SSD_GPS))
        dS_out = tuple(tuple(dcarry[g0 + k, j] for j in range(4)) for k in range(SSD_GPS))
        dys = tuple(dy_ref[:, 512 * k:512 * (k + 1)].astype(F32) for k in range(SSD_GPS))
        _, vjp = jax.vjp(functools.partial(_ssd_math, g0=g0), convs, zs, small_ref[...], S_in, dtb_ref[...], alog_ref[...],
                         dskip_ref[...], ng_ref[...])
        d_convs, d_zs, d_small, dS_in, d_dtb, d_alog, d_dskip, d_ng = vjp((dys, dS_out))
        for k in range(SSD_GPS):
            for j in range(4):
                dcarry[g0 + k, j] = dS_in[k][j]
            dext[k, 0:8, :] = jnp.zeros((8, SSD_XBC_COLS), F32)
            dext[k, 8:8 + CHUNK, :] = d_convs[k]
            dext[k, 8 + CHUNK:16 + CHUNK, :] = nxt[g0 + k]
            nxt[g0 + k] = dext[k, 8:16, :]
            dconv = d_convs[k]
            d_xbc = jnp.zeros((CHUNK, SSD_XBC_COLS), F32)
            for tap in range(SSD_CONV):
                d_xbc = d_xbc + cw_ref[k, tap:tap + 1, :] * dext[k, pl.ds(8 + 3 - tap, CHUNK), :]
                dcw_ref[g0 + k, tap:tap + 1, :] += jnp.sum(dconv * ext[k, pl.ds(5 + tap, CHUNK), :], axis=0, keepdims=True)
            dcb_ref[g0 + k] += jnp.sum(dconv, axis=0, keepdims=True)
            dproj_ref[:, SSD_GROUP_COLS * k:SSD_GROUP_COLS * k + 512] = d_zs[k].astype(BF16)
            dproj_ref[:, SSD_GROUP_COLS * k + 512:SSD_GROUP_COLS * (k + 1)] = d_xbc.astype(BF16)
            ddskip_ref[g0 + k] += d_dskip[:, 512 * k:512 * (k + 1)]
            dng_ref[g0 + k] += d_ng[:, 512 * k:512 * (k + 1)]

        @pl.when(gb == 0)
        def _():
            dsmall_ref[...] = d_small

        @pl.when(gb != 0)
        def _():
            dsmall_ref[...] += d_small

        ddtb_ref[...] += d_dtb
        dalog_ref[...] += d_alog
        if side is not None:
            pl.when(_grid_edge(grid, first=False))(lambda: side.finish(*side_refs))

    whole = lambda shape: pl.BlockSpec(shape, lambda s, g: (0,) * len(shape))
    side_ops = tuple(side.operands) if side is not None else ()
    res = pl.pallas_call(
        body, name="ssd_bwd", grid=grid,
        in_specs=_ssd_specs(nc, True) + [
            pl.BlockSpec((1, SSD_GPS, 4, CHUNK, CHUNK), lambda s, g: (nc - 1 - s, g, 0, 0, 0)),
            pl.BlockSpec((CHUNK, SSD_GPS * 512), lambda s, g: (nc - 1 - s, g))] + [ANY] * s_in,
        out_specs=[pl.BlockSpec((CHUNK, SSD_GPS * SSD_GROUP_COLS), lambda s, g: (nc - 1 - s, g)),
                   pl.BlockSpec((CHUNK, SMALL_COLS), lambda s, g: (nc - 1 - s, 0)),
                   whole((SSD_GROUPS, SSD_CONV, SSD_XBC_COLS)), whole((SSD_GROUPS, 1, SSD_XBC_COLS)),
                   whole((1, SMALL_COLS)), whole((1, SMALL_COLS)),
                   whole((SSD_GROUPS, 1, 512)), whole((SSD_GROUPS, 1, 512))] + [ANY] * s_out,
        out_shape=[jax.ShapeDtypeStruct((L, SSD_GROUPS * SSD_GROUP_COLS), BF16), jax.ShapeDtypeStruct((L, SMALL_COLS), F32),
                   jax.ShapeDtypeStruct((SSD_GROUPS, SSD_CONV, SSD_XBC_COLS), F32),
                   jax.ShapeDtypeStruct((SSD_GROUPS, 1, SSD_XBC_COLS), F32),
                   jax.ShapeDtypeStruct((1, SMALL_COLS), F32), jax.ShapeDtypeStruct((1, SMALL_COLS), F32),
                   jax.ShapeDtypeStruct((SSD_GROUPS, 1, 512), F32), jax.ShapeDtypeStruct((SSD_GROUPS, 1, 512), F32)]
        + (list(side.out_shapes) if side is not None else []),
        scratch_shapes=[pltpu.VMEM((SSD_GROUPS, 4, CHUNK, CHUNK), F32), pltpu.VMEM((SSD_GROUPS, 8, SSD_XBC_COLS), F32),
                        pltpu.VMEM((SSD_GPS, 8 + CHUNK, SSD_XBC_COLS), F32),
                        pltpu.VMEM((SSD_GPS, 16 + CHUNK, SSD_XBC_COLS), F32)]
        + (list(side.sems) if side is not None else []),
        compiler_params=pltpu.CompilerParams(dimension_semantics=("arbitrary", "arbitrary"), vmem_limit_bytes=VMEM_LIMIT),
    )(proj, proj, small, cw, cb, dtb, alog, dskip, ng, states, dy, *side_ops)
    return res if side is None else (res[:8], res[8:])


ML_HPS = 8


def _mlstm_math(blks, smallb, C_in, n_in, m_in, ib_row, fb_row, ng, h0):
    li_all = GATE_SOFTCAP * jnp.tanh((smallb + ib_row) / GATE_SOFTCAP)
    lf_all = -_softplus(-(GATE_SOFTCAP * jnp.tanh((smallb + fb_row) / GATE_SOFTCAP)))
    prep = (li_all, lf_all, _cumsum(lf_all))
    out = [_mlstm_head(blks[k][:, 0:128], blks[k][:, 128:256], blks[k][:, 256:512], blks[k][:, 512:768], prep,
                       C_in[k], n_in[k], m_in[k], ng[:, ML_DV * k:ML_DV * (k + 1)], h0 + k) for k in range(ML_HPS)]
    return tuple(zip(*out))


def _mlstm_head(q, k, v, o_raw, prep, C_in, n_in, m_in_row, ng, h):
    li_all, lf_all, bcum_all = prep
    lane = lax.broadcasted_iota(jnp.int32, (1, CHUNK), 1)
    row_i = lax.broadcasted_iota(jnp.int32, (CHUNK, CHUNK), 0)
    col_i = lax.broadcasted_iota(jnp.int32, (CHUNK, CHUNK), 1)
    causal = row_i >= col_i
    kk = k * (ML_DK ** -0.5)
    li = _lane_col(li_all, lane, LANE_I + h)
    lf = _lane_col(lf_all, lane, LANE_F + h)
    bc = _lane_col(bcum_all, lane, LANE_F + h)
    b_last = jnp.sum(lf, axis=0, keepdims=True)
    m_in = _lane_col(m_in_row, lane, 0)
    a = b_last - bc + li
    m_loc = jnp.max(a, axis=0, keepdims=True)
    w = jnp.exp(a - m_loc)
    C_loc = _mm_tn(w * v, kk)
    n_loc = jnp.sum(w * kk, axis=0, keepdims=True)
    m_new = jnp.maximum(b_last + m_in, m_loc)
    s_old = jnp.exp(b_last + m_in - m_new)
    s_new = jnp.exp(m_loc - m_new)
    C_out = s_old * C_in + s_new * C_loc
    n_out = s_old * n_in + s_new * n_loc
    bc_b = jnp.broadcast_to(bc, (CHUNK, CHUNK))
    li_b = jnp.broadcast_to(li, (CHUNK, CHUNK))
    D = jnp.where(causal, bc_b - bc_b.T + li_b.T, -jnp.inf)
    m_intra = jnp.max(D, axis=1, keepdims=True)
    inter_log = bc + m_in
    m_t = jnp.maximum(inter_log, m_intra)
    S = _mm_nt(q, kk) * jnp.exp(D - m_t)
    w_inter = jnp.exp(inter_log - m_t)
    num = _mm(S, v) + w_inter * _mm_nt(q, C_in)
    nq = jnp.sum(S, axis=1, keepdims=True) + w_inter * jnp.sum(q * n_in, axis=1, keepdims=True)
    den = jnp.maximum(jnp.abs(nq), jnp.exp(-m_t))
    hh = num / den
    hh = hh * lax.rsqrt(jnp.mean(hh * hh, axis=1, keepdims=True) + EPS)
    hh = hh * ng * _sigmoid(o_raw)
    return hh, C_out, n_out, jnp.broadcast_to(m_new, (1, CHUNK))


def _ml_specs(nc, rev):
    cc = (lambda c: nc - 1 - c) if rev else (lambda c: c)
    return [
        pl.BlockSpec((CHUNK, ML_HPS * ML_HEAD_COLS), lambda c, h: (cc(c), h)),
        pl.BlockSpec((CHUNK, SMALL_COLS), lambda c, h: (cc(c), 0)),
        pl.BlockSpec((1, SMALL_COLS), lambda c, h: (0, 0)),
        pl.BlockSpec((1, SMALL_COLS), lambda c, h: (0, 0)),
        pl.BlockSpec((1, ML_HPS * ML_DV), lambda c, h: (0, h)),
    ]


def _mlstm_forward(proj, small, ib, fb, ng):
    L = proj.shape[0]
    nc = L // CHUNK

    def body(blk_ref, small_ref, ib_ref, fb_ref, ng_ref, y_ref, cst_ref, nm_ref, c_carry, nm_carry):
        c, h0 = pl.program_id(0), pl.program_id(1) * ML_HPS

        @pl.when(c == 0)
        def _():
            for k in range(ML_HPS):
                c_carry[h0 + k] = jnp.zeros((ML_DV, ML_DK), F32)
                nm_carry[h0 + k] = jnp.zeros((2, ML_DK), F32)

        for k in range(ML_HPS):
            cst_ref[0, k] = c_carry[h0 + k]
            nm_ref[0, k] = nm_carry[h0 + k]
        hh, C_out, n_out, m_out = _mlstm_math(
            [blk_ref[:, ML_HEAD_COLS * k:ML_HEAD_COLS * (k + 1)] for k in range(ML_HPS)], small_ref[...],
            [c_carry[h0 + k] for k in range(ML_HPS)], [nm_carry[h0 + k, 0:1, :] for k in range(ML_HPS)],
            [nm_carry[h0 + k, 1:2, :] for k in range(ML_HPS)], ib_ref[...], fb_ref[...], ng_ref[...], h0)
        for k in range(ML_HPS):
            y_ref[:, ML_DV * k:ML_DV * (k + 1)] = hh[k].astype(BF16)
            c_carry[h0 + k] = C_out[k]
            nm_carry[h0 + k, 0:1, :] = n_out[k]
            nm_carry[h0 + k, 1:2, :] = m_out[k]

    return pl.pallas_call(
        body, name="mlstm_fwd", grid=(nc, ML_HEADS // ML_HPS),
        in_specs=_ml_specs(nc, False),
        out_specs=[pl.BlockSpec((CHUNK, ML_HPS * ML_DV), lambda c, h: (c, h)),
                   pl.BlockSpec((1, ML_HPS, ML_DV, ML_DK), lambda c, h: (c, h, 0, 0)),
                   pl.BlockSpec((1, ML_HPS, 2, ML_DK), lambda c, h: (c, h, 0, 0))],
        out_shape=[jax.ShapeDtypeStruct((L, ML_HEADS * ML_DV), BF16),
                   jax.ShapeDtypeStruct((nc, ML_HEADS, ML_DV, ML_DK), F32),
                   jax.ShapeDtypeStruct((nc, ML_HEADS, 2, ML_DK), F32)],
        scratch_shapes=[pltpu.VMEM((ML_HEADS, ML_DV, ML_DK), F32), pltpu.VMEM((ML_HEADS, 2, ML_DK), F32)],
        compiler_params=pltpu.CompilerParams(dimension_semantics=("arbitrary", "arbitrary"), vmem_limit_bytes=VMEM_LIMIT),
    )(proj, small, ib, fb, ng)


def _mlstm_backward(proj, small, ib, fb, ng, cst, nmst, dy, dsmall_in):
    L = proj.shape[0]
    nc = L // CHUNK

    def body(blk_ref, small_ref, ib_ref, fb_ref, ng_ref, cst_ref, nm_ref, dy_ref, dsin_ref,
             dproj_ref, dsmall_ref, dib_ref, dfb_ref, dng_ref, dc_carry, dnm_carry):
        s, hb = pl.program_id(0), pl.program_id(1)
        h0 = hb * ML_HPS

        @pl.when(s == 0)
        def _():
            for k in range(ML_HPS):
                dc_carry[h0 + k] = jnp.zeros((ML_DV, ML_DK), F32)
                dnm_carry[h0 + k] = jnp.zeros((2, ML_DK), F32)
                dng_ref[h0 + k] = jnp.zeros((1, ML_DV), F32)

        @pl.when((s == 0) & (hb == 0))
        def _():
            dib_ref[...] = jnp.zeros((1, SMALL_COLS), F32)
            dfb_ref[...] = jnp.zeros((1, SMALL_COLS), F32)

        heads = range(ML_HPS)
        _, vjp = jax.vjp(functools.partial(_mlstm_math, h0=h0),
                         [blk_ref[:, ML_HEAD_COLS * k:ML_HEAD_COLS * (k + 1)] for k in heads], small_ref[...],
                         [cst_ref[0, k] for k in heads], [nm_ref[0, k, 0:1, :] for k in heads],
                         [nm_ref[0, k, 1:2, :] for k in heads], ib_ref[...], fb_ref[...], ng_ref[...])
        d_blk, d_small, dC, dn, dm, d_ib, d_fb, d_ng = vjp(
            (tuple(dy_ref[:, ML_DV * k:ML_DV * (k + 1)].astype(F32) for k in heads),
             tuple(dc_carry[h0 + k] for k in heads), tuple(dnm_carry[h0 + k, 0:1, :] for k in heads),
             tuple(dnm_carry[h0 + k, 1:2, :] for k in heads)))
        for k in heads:
            dc_carry[h0 + k] = dC[k]
            dnm_carry[h0 + k, 0:1, :] = dn[k]
            dnm_carry[h0 + k, 1:2, :] = dm[k]
            dproj_ref[:, ML_HEAD_COLS * k:ML_HEAD_COLS * (k + 1)] = d_blk[k].astype(BF16)
            dng_ref[h0 + k] += d_ng[:, ML_DV * k:ML_DV * (k + 1)]

        @pl.when(hb == 0)
        def _():
            dsmall_ref[...] = dsin_ref[...] + d_small

        @pl.when(hb != 0)
        def _():
            dsmall_ref[...] += d_small

        dib_ref[...] += d_ib
        dfb_ref[...] += d_fb

    whole = lambda shape: pl.BlockSpec(shape, lambda s, h: (0,) * len(shape))
    return pl.pallas_call(
        body, name="mlstm_bwd", grid=(nc, ML_HEADS // ML_HPS),
        in_specs=_ml_specs(nc, True) + [
            pl.BlockSpec((1, ML_HPS, ML_DV, ML_DK), lambda s, h: (nc - 1 - s, h, 0, 0)),
            pl.BlockSpec((1, ML_HPS, 2, ML_DK), lambda s, h: (nc - 1 - s, h, 0, 0)),
            pl.BlockSpec((CHUNK, ML_HPS * ML_DV), lambda s, h: (nc - 1 - s, SSD_WIDTH // (ML_HPS * ML_DV) + h)),
            pl.BlockSpec((CHUNK, SMALL_COLS), lambda s, h: (nc - 1 - s, 0))],
        out_specs=[pl.BlockSpec((CHUNK, ML_HPS * ML_HEAD_COLS), lambda s, h: (nc - 1 - s, h)),
                   pl.BlockSpec((CHUNK, SMALL_COLS), lambda s, h: (nc - 1 - s, 0)),
                   whole((1, SMALL_COLS)), whole((1, SMALL_COLS)), whole((ML_HEADS, 1, ML_DV))],
        out_shape=[jax.ShapeDtypeStruct((L, ML_HEADS * ML_HEAD_COLS), BF16), jax.ShapeDtypeStruct((L, SMALL_COLS), F32),
                   jax.ShapeDtypeStruct((1, SMALL_COLS), F32), jax.ShapeDtypeStruct((1, SMALL_COLS), F32),
                   jax.ShapeDtypeStruct((ML_HEADS, 1, ML_DV), F32)],
        scratch_shapes=[pltpu.VMEM((ML_HEADS, ML_DV, ML_DK), F32), pltpu.VMEM((ML_HEADS, 2, ML_DK), F32)],
        compiler_params=pltpu.CompilerParams(dimension_semantics=("arbitrary", "arbitrary"), vmem_limit_bytes=VMEM_LIMIT),
    )(proj, small, ib, fb, ng, cst, nmst, dy, dsmall_in)


_OFF_Z, _OFF_X, _OFF_B, _OFF_C, _OFF_DT = 0, 2048, 4096, 4608, 5120
_OFF_Q, _OFF_K, _OFF_V, _OFF_O, _OFF_I, _OFF_F, IN_WIDTH = 5152, 6176, 7200, 9248, 11296, 11304, 11312


def _ssd_segments():
    segs = []
    for g in range(SSD_GROUPS):
        segs += [(_OFF_Z + 512 * g, 512), (_OFF_X + 512 * g, 512), (_OFF_B + 128 * g, 128), (_OFF_C + 128 * g, 128)]
    return segs


def _ml_segments():
    segs = []
    for h in range(ML_HEADS):
        segs += [(_OFF_Q + 128 * h, 128), (_OFF_K + 128 * h, 128), (_OFF_V + 256 * h, 256), (_OFF_O + 256 * h, 256)]
    return segs


_SMALL_SEGMENTS = [(_OFF_DT, 32), (_OFF_I, 8), (_OFF_F, 8)]


def _take_cols(w, segs):
    return jnp.concatenate([w[:, s:s + n] for s, n in segs], axis=1)


def _split_w_in(w):
    small = _take_cols(w, _SMALL_SEGMENTS)
    small = jnp.concatenate([small, jnp.zeros((w.shape[0], SMALL_COLS - small.shape[1]), w.dtype)], axis=1)
    return _take_cols(w, _ssd_segments()), _take_cols(w, _ml_segments()), small


def _merge_w_in(d_ssd, d_ml, d_small):
    pieces = []
    for arr, segs in ((d_ssd, _ssd_segments()), (d_ml, _ml_segments()), (d_small, _SMALL_SEGMENTS)):
        pos = 0
        for s, n in segs:
            pieces.append((s, arr[:, pos:pos + n]))
            pos += n
    pieces.sort(key=lambda t: t[0])
    return jnp.concatenate([p for _, p in pieces], axis=1)


def _conv_to_groups(cw):
    return jnp.stack([jnp.concatenate([cw[:, 512 * g:512 * (g + 1)], cw[:, 2048 + 128 * g:2048 + 128 * (g + 1)],
                                       cw[:, 2560 + 128 * g:2560 + 128 * (g + 1)]], axis=1) for g in range(SSD_GROUPS)])


def _conv_from_groups(d):
    return jnp.concatenate([d[g, :, 0:512] for g in range(SSD_GROUPS)] + [d[g, :, 512:640] for g in range(SSD_GROUPS)]
                           + [d[g, :, 640:768] for g in range(SSD_GROUPS)], axis=1)


def _small_row(vec, lane0):
    n = vec.shape[1]
    return jnp.concatenate([jnp.zeros((1, lane0), F32), vec, jnp.zeros((1, SMALL_COLS - lane0 - n), F32)], axis=1)


class _Side:
    def __init__(self, operands, out_shapes, sems, start, finish):
        self.operands, self.out_shapes, self.sems, self.start, self.finish = operands, out_shapes, sems, start, finish


def _grid_edge(grid, first):
    hit = None
    for axis, n in enumerate(grid):
        here = pl.program_id(axis) == (0 if first else n - 1)
        hit = here if hit is None else hit & here
    return hit


def _matmul(a, b, mode, out_dtype, tm, tn, tk, name, addend=None, layout=None, side=None):
    M, Kd = (a.shape[1], a.shape[0]) if mode == "tn" else a.shape
    N = b.shape[0] if mode == "nt" else b.shape[1]
    if layout == "cols4":
        tm, tn = M // 2, N // 4
    elif layout == "rows4":
        tm = M // 4
    tm, tn, tk = min(tm, M), min(tn, N), min(tk, Kd)
    if mode == "nn":
        a_spec = pl.BlockSpec((tm, tk), lambda i, j, k: (i, k))
        b_spec = pl.BlockSpec((tk, tn), lambda i, j, k: (k, j))
        dims = (((1,), (0,)), ((), ()))
    elif mode == "nt":
        a_spec = pl.BlockSpec((tm, tk), lambda i, j, k: (i, k))
        b_spec = pl.BlockSpec((tn, tk), lambda i, j, k: (j, k))
        dims = (((1,), (1,)), ((), ()))
    else:
        a_spec = pl.BlockSpec((tk, tm), lambda i, j, k: (k, i))
        b_spec = pl.BlockSpec((tk, tn), lambda i, j, k: (k, j))
        dims = (((0,), (0,)), ((), ()))
    assert M % tm == 0 and N % tn == 0 and Kd % tk == 0, (name, M, N, Kd, tm, tn, tk)
    nk = Kd // tk
    has_add = addend is not None
    n_in = 2 + has_add
    s_in, s_out = (len(side.operands), len(side.out_shapes)) if side is not None else (0, 0)
    grid = (M // tm, N // tn, nk)

    def body(*refs):
        a_ref, b_ref = refs[0], refs[1]
        add_ref = refs[2] if has_add else None
        o_ref, acc_ref = refs[n_in + s_in], refs[n_in + s_in + 1 + s_out]
        k = pl.program_id(2)
        side_refs = (refs[n_in:n_in + s_in], refs[n_in + s_in + 1:n_in + s_in + 1 + s_out], refs[n_in + s_in + 2 + s_out:])
        if side is not None:
            pl.when(_grid_edge(grid, first=True))(lambda: side.start(*side_refs))
        part = lax.dot_general(a_ref[...].astype(BF16), b_ref[...].astype(BF16), dims, preferred_element_type=F32)

        @pl.when(k == 0)
        def _():
            acc_ref[...] = part

        @pl.when(k != 0)
        def _():
            acc_ref[...] += part

        @pl.when(k == nk - 1)
        def _():
            r = acc_ref[...]
            if has_add:
                r = r + add_ref[...].astype(F32)
            if layout == "cols4":
                o_ref[0, 0] = r.astype(out_dtype)
            elif layout == "rows4":
                o_ref[0, 0] = r[:tm // 2].astype(out_dtype)
                o_ref[1, 0] = r[tm // 2:].astype(out_dtype)
            else:
                o_ref[...] = r.astype(out_dtype)

        if side is not None:
            pl.when(_grid_edge(grid, first=False))(lambda: side.finish(*side_refs))

    o_spec = pl.BlockSpec((tm, tn), lambda i, j, k: (i, j))
    out_spec, out_dims = o_spec, (M, N)
    if layout == "cols4":
        out_spec, out_dims = pl.BlockSpec((1, 1, tm, tn), lambda i, j, k: (i, j, 0, 0)), (2, 4, tm, tn)
    elif layout == "rows4":
        out_spec, out_dims = pl.BlockSpec((2, 1, tm // 2, tn), lambda i, j, k: (0, i, 0, j)), (2, 4, tm // 2, N)
    out_shape = jax.ShapeDtypeStruct(out_dims, out_dtype)
    semantics = ("parallel", "parallel", "arbitrary")
    if side is None:
        in_specs, operands, out_specs, scratch = [], (), out_spec, []
    else:
        in_specs, operands, scratch = [ANY] * s_in, tuple(side.operands), list(side.sems)
        out_specs, out_shape = [out_spec] + [ANY] * s_out, [out_shape] + list(side.out_shapes)
        semantics = ("arbitrary",) * 3
    res = pl.pallas_call(
        body, name=name, grid=grid,
        in_specs=[a_spec, b_spec] + ([o_spec] if has_add else []) + in_specs, out_specs=out_specs,
        out_shape=out_shape, scratch_shapes=[pltpu.VMEM((tm, tn), F32)] + scratch,
        compiler_params=pltpu.CompilerParams(dimension_semantics=semantics, vmem_limit_bytes=VMEM_LIMIT),
    )(*((a, b) + ((addend,) if has_add else ()) + operands))
    return res if side is None else (res[0], res[1:])


ROW_TILE = 256


def _rmsnorm_fwd(x, g, name):
    L, D = x.shape

    def body(x_ref, g_ref, u_ref):
        xv = x_ref[...]
        r = lax.rsqrt(jnp.mean(xv * xv, axis=1, keepdims=True) + EPS)
        u_ref[...] = (xv * r * g_ref[...]).astype(BF16)

    return pl.pallas_call(
        body, name=name, grid=(L // ROW_TILE,),
        in_specs=[pl.BlockSpec((ROW_TILE, D), lambda i: (i, 0)), pl.BlockSpec((1, D), lambda i: (0, 0))],
        out_specs=pl.BlockSpec((ROW_TILE, D), lambda i: (i, 0)),
        out_shape=jax.ShapeDtypeStruct((L, D), BF16),
        compiler_params=pltpu.CompilerParams(dimension_semantics=("parallel",), vmem_limit_bytes=VMEM_LIMIT),
    )(x, g)


def _rmsnorm_bwd(du, x, g, dres, name):
    L, D = x.shape

    def body(du_ref, x_ref, g_ref, dres_ref, dx_ref, dxb_ref, dg_ref):
        i = pl.program_id(0)
        xv, duv = x_ref[...], du_ref[...]
        r = lax.rsqrt(jnp.mean(xv * xv, axis=1, keepdims=True) + EPS)
        t = duv * g_ref[...]
        dx = dres_ref[...] + r * t - xv * (r * r * r) * jnp.mean(t * xv, axis=1, keepdims=True)
        dx_ref[...] = dx
        dxb_ref[...] = dx.astype(BF16)
        dg = jnp.sum(duv * xv * r, axis=0, keepdims=True)

        @pl.when(i == 0)
        def _():
            dg_ref[...] = dg

        @pl.when(i != 0)
        def _():
            dg_ref[...] += dg

    row = pl.BlockSpec((ROW_TILE, D), lambda i: (i, 0))
    vec = pl.BlockSpec((1, D), lambda i: (0, 0))
    return pl.pallas_call(
        body, name=name, grid=(L // ROW_TILE,),
        in_specs=[row, row, vec, row], out_specs=[row, row, vec],
        out_shape=[jax.ShapeDtypeStruct((L, D), F32), jax.ShapeDtypeStruct((L, D), BF16), jax.ShapeDtypeStruct((1, D), F32)],
        compiler_params=pltpu.CompilerParams(dimension_semantics=("arbitrary",), vmem_limit_bytes=VMEM_LIMIT),
    )(du, x, g, dres)


def _loss_head(h, target, g):
    L, D = h.shape

    def body(h_ref, t_ref, g_ref, dh_ref, dhb_ref, loss_ref, dg_ref):
        i = pl.program_id(0)
        hv = h_ref[...]
        r = lax.rsqrt(jnp.mean(hv * hv, axis=1, keepdims=True) + EPS)
        diff = hv * r * g_ref[...] - t_ref[...]
        part = 0.5 * jnp.sum(jnp.mean(diff * diff, axis=1, keepdims=True), axis=0, keepdims=True)
        dy = diff * (1.0 / D)
        t = dy * g_ref[...]
        dh = r * t - hv * (r * r * r) * jnp.mean(t * hv, axis=1, keepdims=True)
        dh_ref[...] = dh
        dhb_ref[...] = dh.astype(BF16)
        dg = jnp.sum(dy * hv * r, axis=0, keepdims=True)

        @pl.when(i == 0)
        def _():
            dg_ref[...] = dg
            loss_ref[...] = jnp.broadcast_to(part, (1, 128))

        @pl.when(i != 0)
        def _():
            dg_ref[...] += dg
            loss_ref[...] += jnp.broadcast_to(part, (1, 128))

    row = pl.BlockSpec((ROW_TILE, D), lambda i: (i, 0))
    vec = pl.BlockSpec((1, D), lambda i: (0, 0))
    return pl.pallas_call(
        body, name="loss_head", grid=(L // ROW_TILE,),
        in_specs=[row, row, vec], out_specs=[row, row, pl.BlockSpec((1, 128), lambda i: (0, 0)), vec],
        out_shape=[jax.ShapeDtypeStruct((L, D), F32), jax.ShapeDtypeStruct((L, D), BF16), jax.ShapeDtypeStruct((1, 128), F32),
                   jax.ShapeDtypeStruct((1, D), F32)],
        compiler_params=pltpu.CompilerParams(dimension_semantics=("arbitrary",), vmem_limit_bytes=VMEM_LIMIT),
    )(h, target, g)


FF_TILE = 1408


def _swiglu_fwd(gate, up):
    L, N = gate.shape

    def body(g_ref, u_ref, o_ref):
        o_ref[...] = (_silu(g_ref[...]) * u_ref[...]).astype(BF16)

    blk = pl.BlockSpec((ROW_TILE, FF_TILE), lambda i, j: (i, j))
    return pl.pallas_call(
        body, name="swiglu_fwd", grid=(L // ROW_TILE, N // FF_TILE), in_specs=[blk, blk], out_specs=blk,
        out_shape=jax.ShapeDtypeStruct((L, N), BF16),
        compiler_params=pltpu.CompilerParams(dimension_semantics=("parallel", "parallel"), vmem_limit_bytes=VMEM_LIMIT),
    )(gate, up)


def _swiglu_bwd(dact, gate, up):
    L, N = gate.shape

    def body(d_ref, g_ref, u_ref, dg_ref, du_ref):
        gv, d = g_ref[...], d_ref[...]
        s = _sigmoid(gv)
        dg_ref[...] = (d * u_ref[...] * (s * (1.0 + gv * (1.0 - s)))).astype(BF16)
        du_ref[...] = (d * gv * s).astype(BF16)

    blk = pl.BlockSpec((ROW_TILE, FF_TILE), lambda i, j: (i, j))
    return pl.pallas_call(
        body, name="swiglu_bwd", grid=(L // ROW_TILE, N // FF_TILE), in_specs=[blk, blk, blk], out_specs=[blk, blk],
        out_shape=[jax.ShapeDtypeStruct((L, N), BF16), jax.ShapeDtypeStruct((L, N), BF16)],
        compiler_params=pltpu.CompilerParams(dimension_semantics=("parallel", "parallel"), vmem_limit_bytes=VMEM_LIMIT),
    )(dact, gate, up)


def _adamw(w, g, m, v, rows, name):
    R, C = w.shape
    assert R % rows == 0, (name, R, rows)

    def body(w_ref, g_ref, m_ref, v_ref, d_ref, nm_ref, nv_ref):
        gv = g_ref[...]
        mn = ADAM_B1 * m_ref[...] + (1.0 - ADAM_B1) * gv
        vn = ADAM_B2 * v_ref[...] + (1.0 - ADAM_B2) * (gv * gv)
        m_hat = mn / (1.0 - ADAM_B1 ** ADAM_STEP)
        v_hat = vn / (1.0 - ADAM_B2 ** ADAM_STEP)
        d_ref[...] = -ADAM_LR * (m_hat / (jnp.sqrt(v_hat) + ADAM_EPS) + ADAM_WD * w_ref[...])
        nm_ref[...] = mn
        nv_ref[...] = vn

    blk = pl.BlockSpec((rows, C), lambda i: (i, 0))
    out = jax.ShapeDtypeStruct((R, C), F32)
    return pl.pallas_call(
        body, name=name, grid=(R // rows,), in_specs=[blk] * 4, out_specs=[blk] * 3, out_shape=[out] * 3,
        compiler_params=pltpu.CompilerParams(dimension_semantics=("parallel",), vmem_limit_bytes=VMEM_LIMIT),
    )(w, g, m, v)


ANY = pl.BlockSpec(memory_space=pl.ANY)
N_DEV = 8


def _allgather_small(p, name):
    R, C = p.shape

    def body(p_ref, out_ref, send_sems, recv_sems, local_sem):
        x, y, c = lax.axis_index("x"), lax.axis_index("y"), lax.axis_index("c")
        me = 4 * x + 2 * y + c
        mine = pltpu.make_async_copy(p_ref, out_ref.at[me], local_sem)
        mine.start()

        def peer(d):
            return (x ^ ((d >> 2) & 1), y ^ ((d >> 1) & 1), c ^ (d & 1))

        def copy(d, block):
            return pltpu.make_async_remote_copy(src_ref=p_ref, dst_ref=out_ref.at[block], send_sem=send_sems.at[d - 1],
                                                recv_sem=recv_sems.at[d - 1], device_id=peer(d), device_id_type=MESH)

        sends = [copy(d, me) for d in range(1, N_DEV)]
        for cp in sends:
            cp.start()
        for d in range(1, N_DEV):
            px, py, pc = peer(d)
            copy(d, 4 * px + 2 * py + pc).wait_recv()
        for cp in sends:
            cp.wait_send()
        mine.wait()

    return pl.pallas_call(
        body, name=name, out_shape=jax.ShapeDtypeStruct((N_DEV, R, C), p.dtype),
        in_specs=[pl.BlockSpec(memory_space=pltpu.VMEM)], out_specs=pl.BlockSpec(memory_space=pltpu.VMEM),
        scratch_shapes=[pltpu.SemaphoreType.DMA((N_DEV - 1,)), pltpu.SemaphoreType.DMA((N_DEV - 1,)), pltpu.SemaphoreType.DMA],
    )(p)


def _other_chips(x, y):
    return [(1 - x, y), (x, 1 - y), (1 - x, 1 - y)]


def _run_side(side, name):
    s_in, s_out = len(side.operands), len(side.out_shapes)

    def body(*refs):
        parts = (refs[:s_in], refs[s_in:s_in + s_out], refs[s_in + s_out:])
        side.start(*parts)
        side.finish(*parts)

    return pl.pallas_call(body, name=name, out_shape=list(side.out_shapes), in_specs=[ANY] * s_in, out_specs=[ANY] * s_out,
                          scratch_shapes=list(side.sems))(*side.operands)


def _gather_side(shards):
    T = len(shards)
    halves = [s.shape[0] // 2 for s in shards]

    def tools(ins, outs, sems):
        send_sems, recv_sems = sems
        x, y, c = lax.axis_index("x"), lax.axis_index("y"), lax.axis_index("c")

        def rows(t, px, py, pc):
            return outs[t].at[2 * px + py, pl.ds(pc * halves[t], halves[t]), :]

        def copy(t, k, block, to, own=False):
            src = ins[t].at[pl.ds(c * halves[t], halves[t]), :] if own else rows(t, *block)
            return pltpu.make_async_remote_copy(src_ref=src, dst_ref=rows(t, *block), send_sem=send_sems.at[t, k],
                                                recv_sem=recv_sems.at[t, k], device_id=to, device_id_type=MESH)

        return (x, y, c), _other_chips(x, y), copy

    def start(ins, outs, sems):
        (x, y, c), chips, copy = tools(ins, outs, sems)
        for t in range(T):
            for j, chip in enumerate(chips):
                copy(t, j, (x, y, c), (*chip, c), own=True).start()

    def finish(ins, outs, sems):
        (x, y, c), chips, copy = tools(ins, outs, sems)
        for t in range(T):
            for j, chip in enumerate(chips):
                copy(t, j, (*chip, c), (x, y, c)).wait_recv()
                copy(t, 3 + j, (*chip, c), (x, y, 1 - c)).start()
        for t in range(T):
            for j, chip in enumerate(chips):
                copy(t, 3 + j, (*chip, 1 - c), (x, y, c)).wait_recv()
        for t in range(T):
            for j, chip in enumerate(chips):
                copy(t, j, (x, y, c), (*chip, c), own=True).wait_send()
                copy(t, 3 + j, (*chip, c), (x, y, 1 - c)).wait_send()

    return _Side(shards, [jax.ShapeDtypeStruct((4,) + s.shape, s.dtype) for s in shards],
                 [pltpu.SemaphoreType.DMA((T, 6)), pltpu.SemaphoreType.DMA((T, 6))], start, finish)


def _with_own_shard(gathered, shard, chip):
    return lax.dynamic_update_slice(gathered, shard[None], (chip,) + (0,) * shard.ndim)


def _pair_exchange(grads, name):
    T = len(grads)

    def body(*refs):
        ins, outs = refs[:T], refs[T:2 * T]
        send_sems, recv_sems = refs[2 * T:]
        x, y, c = lax.axis_index("x"), lax.axis_index("y"), lax.axis_index("c")
        cps = [pltpu.make_async_remote_copy(src_ref=ins[t].at[1 - c], dst_ref=outs[t], send_sem=send_sems.at[t],
                                            recv_sem=recv_sems.at[t], device_id=(x, y, 1 - c), device_id_type=MESH)
               for t in range(T)]
        for cp in cps:
            cp.start()
        for cp in cps:
            cp.wait()

    return pl.pallas_call(
        body, name=name,
        out_shape=[jax.ShapeDtypeStruct(g.shape[1:], g.dtype) for g in grads],
        in_specs=[ANY] * T, out_specs=[ANY] * T,
        scratch_shapes=[pltpu.SemaphoreType.DMA((T,)), pltpu.SemaphoreType.DMA((T,))],
    )(*grads)


def _scatter_side(parts):
    T = len(parts)

    def tools(ins, outs, sems):
        send_sems, recv_sems = sems
        x, y, c = lax.axis_index("x"), lax.axis_index("y"), lax.axis_index("c")

        def copy(t, j, src_slot, dst_slot, chip):
            return pltpu.make_async_remote_copy(src_ref=ins[t].at[src_slot], dst_ref=outs[t].at[dst_slot],
                                                send_sem=send_sems.at[t, j], recv_sem=recv_sems.at[t, j],
                                                device_id=(*chip, c), device_id_type=MESH)

        return 2 * x + y, _other_chips(x, y), copy

    def start(ins, outs, sems):
        my_chip, chips, copy = tools(ins, outs, sems)
        for t in range(T):
            for j, (px, py) in enumerate(chips):
                copy(t, j, 2 * px + py, my_chip, (px, py)).start()

    def finish(ins, outs, sems):
        my_chip, chips, copy = tools(ins, outs, sems)
        for t in range(T):
            for j, (px, py) in enumerate(chips):
                copy(t, j, my_chip, 2 * px + py, (px, py)).wait_recv()
        for t in range(T):
            for j, (px, py) in enumerate(chips):
                copy(t, j, 2 * px + py, my_chip, (px, py)).wait_send()

    return _Side(parts, [jax.ShapeDtypeStruct(p.shape, p.dtype) for p in parts],
                 [pltpu.SemaphoreType.DMA((T, 3)), pltpu.SemaphoreType.DMA((T, 3))], start, finish)


def _pair_share(fulls):
    T = len(fulls)

    def body(*refs):
        outs = refs[T:2 * T]
        send_sems, recv_sems = refs[2 * T:]
        x, y, c = lax.axis_index("x"), lax.axis_index("y"), lax.axis_index("c")

        def copy(t, half):
            return pltpu.make_async_remote_copy(src_ref=outs[t].at[half], dst_ref=outs[t].at[half], send_sem=send_sems.at[t],
                                                recv_sem=recv_sems.at[t], device_id=(x, y, 1 - c), device_id_type=MESH)

        for t in range(T):
            copy(t, c).start()
        for t in range(T):
            copy(t, 1 - c).wait_recv()
        for t in range(T):
            copy(t, c).wait_send()

    return pl.pallas_call(
        body, name="grad_pair_share", out_shape=[jax.ShapeDtypeStruct(f.shape, f.dtype) for f in fulls],
        in_specs=[ANY] * T, out_specs=[ANY] * T, input_output_aliases={t: t for t in range(T)},
        scratch_shapes=[pltpu.SemaphoreType.DMA((T,)), pltpu.SemaphoreType.DMA((T,))],
    )(*fulls)


def _row_tile(hr):
    return next(r for r in (128, 64, 32, 16, 8) if hr % r == 0)


def _pair_add(g, got, c, name):
    _, _, hr, C = g.shape
    tr = _row_tile(hr)

    def body(c_ref, g_ref, r_ref, o_ref):
        o_ref[...] = (g_ref[0].astype(F32) + r_ref[...].astype(F32)).astype(o_ref.dtype)

    return pl.pallas_call(
        body, name=name,
        grid_spec=pltpu.PrefetchScalarGridSpec(
            num_scalar_prefetch=1, grid=(4, hr // tr),
            in_specs=[pl.BlockSpec((1, 1, tr, C), lambda k, i, c_ref: (c_ref[0], k, i, 0)),
                      pl.BlockSpec((1, tr, C), lambda k, i, c_ref: (k, i, 0))],
            out_specs=pl.BlockSpec((1, tr, C), lambda k, i, c_ref: (k, i, 0))),
        out_shape=jax.ShapeDtypeStruct(got.shape, g.dtype),
        compiler_params=pltpu.CompilerParams(dimension_semantics=("parallel", "parallel"), vmem_limit_bytes=VMEM_LIMIT),
    )(jnp.reshape(c, (1,)).astype(jnp.int32), g, got)


def _chip_sum(parts, slots, chip, c, name):
    _, hr, C = parts.shape
    tr = _row_tile(hr)

    def body(chip_ref, half_ref, own_ref, a_ref, b_ref, c_ref, o_ref):
        acc = own_ref[...].astype(F32)
        for r in (a_ref, b_ref, c_ref):
            acc = acc + r[...].astype(F32)
        o_ref[...] = acc

    other = lambda j: pl.BlockSpec((1, tr, C), lambda i, chip_ref, half_ref: ((chip_ref[0] + j) % 4, i, 0))
    return pl.pallas_call(
        body, name=name,
        grid_spec=pltpu.PrefetchScalarGridSpec(
            num_scalar_prefetch=2, grid=(hr // tr,),
            in_specs=[pl.BlockSpec((1, tr, C), lambda i, chip_ref, half_ref: (chip_ref[0], i, 0)),
                      other(1), other(2), other(3)],
            out_specs=pl.BlockSpec((1, tr, C), lambda i, chip_ref, half_ref: (half_ref[0], i, 0))),
        out_shape=jax.ShapeDtypeStruct((2, hr, C), F32),
        compiler_params=pltpu.CompilerParams(dimension_semantics=("parallel",), vmem_limit_bytes=VMEM_LIMIT),
    )(jnp.reshape(chip, (1,)).astype(jnp.int32), jnp.reshape(c, (1,)).astype(jnp.int32), parts, slots, slots, slots)


def _sum_slots(parts, name):
    n, hr, C = parts.shape
    tr = _row_tile(hr)

    def body(p_ref, o_ref):
        acc = p_ref[0].astype(F32)
        for k in range(1, n):
            acc = acc + p_ref[k].astype(F32)
        o_ref[...] = acc

    return pl.pallas_call(
        body, name=name, grid=(hr // tr,),
        in_specs=[pl.BlockSpec((n, tr, C), lambda i: (0, i, 0))], out_specs=pl.BlockSpec((tr, C), lambda i: (i, 0)),
        out_shape=jax.ShapeDtypeStruct((hr, C), F32),
        compiler_params=pltpu.CompilerParams(dimension_semantics=("parallel",), vmem_limit_bytes=VMEM_LIMIT),
    )(parts)


PACK_ROWS, PACK_COLS = 16, 3072
_PACK = [("norm_mix_g", 0, 1, 2048), ("conv_w", 1, 4, 3072), ("conv_b", 5, 1, 3072), ("dt_bias", 6, 1, 32),
         ("a_log", 7, 1, 32), ("d_skip", 8, 1, 32), ("ssd_norm_g", 9, 1, 2048), ("i_bias", 10, 1, 8), ("f_bias", 11, 1, 8),
         ("mlstm_norm_g", 12, 1, 2048), ("norm_ffn_g", 13, 1, 2048), ("final_norm_g", 14, 1, 2048), ("loss", 15, 1, 1)]
_WEIGHTS = ["norm_mix_g", "w_in", "conv_w", "conv_b", "dt_bias", "a_log", "d_skip", "ssd_norm_g", "i_bias", "f_bias",
            "mlstm_norm_g", "w_out", "norm_ffn_g", "w_gate", "w_up", "w_down", "final_norm_g"]
_BIG = ["w_in", "w_out", "w_gate", "w_up", "w_down"]


def _pack(vals):
    rows = []
    for name, _, r, w in _PACK:
        v = vals.get(name)
        v = jnp.zeros((r, w), F32) if v is None else v.reshape(r, w).astype(F32)
        rows.append(jnp.concatenate([v, jnp.zeros((r, PACK_COLS - w), F32)], axis=1) if w < PACK_COLS else v)
    return jnp.concatenate(rows, axis=0)


def _unpack(p):
    return {name: p[r0:r0 + r, :w] for name, r0, r, w in _PACK}


def _to_halves_cols(dw, cols):
    R = dw.shape[0]
    return dw.reshape(2, R // 2, 4, cols).transpose(0, 2, 1, 3)


def kernel(x, norm_mix_g, w_in, conv_w, conv_b, dt_bias, a_log, d_skip, ssd_norm_g, i_bias, f_bias, mlstm_norm_g, w_out, norm_ffn_g, w_gate, w_up, w_down, final_norm_g, loss_target, m_norm_mix_g, m_w_in, m_conv_w, m_conv_b, m_dt_bias, m_a_log, m_d_skip, m_ssd_norm_g, m_i_bias, m_f_bias, m_mlstm_norm_g, m_w_out, m_norm_ffn_g, m_w_gate, m_w_up, m_w_down, m_final_norm_g, v_norm_mix_g, v_w_in, v_conv_w, v_conv_b, v_dt_bias, v_a_log, v_d_skip, v_ssd_norm_g, v_i_bias, v_f_bias, v_mlstm_norm_g, v_w_out, v_norm_ffn_g, v_w_gate, v_w_up, v_w_down, v_final_norm_g):
    weights = dict(norm_mix_g=norm_mix_g, w_in=w_in, conv_w=conv_w, conv_b=conv_b, dt_bias=dt_bias, a_log=a_log, d_skip=d_skip,
                   ssd_norm_g=ssd_norm_g, i_bias=i_bias, f_bias=f_bias, mlstm_norm_g=mlstm_norm_g, w_out=w_out,
                   norm_ffn_g=norm_ffn_g, w_gate=w_gate, w_up=w_up, w_down=w_down, final_norm_g=final_norm_g)
    mom1 = dict(norm_mix_g=m_norm_mix_g, w_in=m_w_in, conv_w=m_conv_w, conv_b=m_conv_b, dt_bias=m_dt_bias, a_log=m_a_log,
                d_skip=m_d_skip, ssd_norm_g=m_ssd_norm_g, i_bias=m_i_bias, f_bias=m_f_bias, mlstm_norm_g=m_mlstm_norm_g,
                w_out=m_w_out, norm_ffn_g=m_norm_ffn_g, w_gate=m_w_gate, w_up=m_w_up, w_down=m_w_down,
                final_norm_g=m_final_norm_g)
    mom2 = dict(norm_mix_g=v_norm_mix_g, w_in=v_w_in, conv_w=v_conv_w, conv_b=v_conv_b, dt_bias=v_dt_bias, a_log=v_a_log,
                d_skip=v_d_skip, ssd_norm_g=v_ssd_norm_g, i_bias=v_i_bias, f_bias=v_f_bias, mlstm_norm_g=v_mlstm_norm_g,
                w_out=v_w_out, norm_ffn_g=v_norm_ffn_g, w_gate=v_w_gate, w_up=v_w_up, w_down=v_w_down,
                final_norm_g=v_final_norm_g)
    xi, yi, ci = lax.axis_index("x"), lax.axis_index("y"), lax.axis_index("c")
    chip = 2 * xi + yi
    xs, tgt = x[0], loss_target[0]

    shards = {n: weights[n][0].astype(BF16) for n in _BIG}
    gathered = lambda n, g: _with_own_shard(g, shards[n], chip)
    by_cols = lambda g, width: jnp.transpose(g, (1, 0, 2)).reshape(D_MODEL, width)
    (g_in,) = _run_side(_gather_side([shards["w_in"]]), "allgather_w_in")
    W_ssd, W_ml, W_small = _split_w_in(by_cols(gathered("w_in", g_in), IN_WIDTH))
    cw_all = _allgather_small(jnp.concatenate([conv_w[0], jnp.zeros((4, 768), F32)], axis=0), "allgather_conv_w")
    conv_w_full = jnp.concatenate([cw_all[2 * k, :SSD_CONV] for k in range(4)], axis=1)
    cwg, cbg = _conv_to_groups(conv_w_full), _conv_to_groups(conv_b)
    dtb_row, al_row = _small_row(dt_bias, 0), _small_row(a_log, 0)
    ib_row, fb_row = _small_row(i_bias, LANE_I), _small_row(f_bias, LANE_F)
    dskip_lane = jnp.repeat(d_skip, 64, axis=1)
    fng = final_norm_g[None]

    u1 = _rmsnorm_fwd(xs, norm_mix_g, "norm_mix_fwd")
    p_ssd, (g_out,) = _matmul(u1, W_ssd, "nn", F32, 1024, 1024, 2048, "proj_ssd", side=_gather_side([shards["w_out"]]))
    p_ml, (g_gate,) = _matmul(u1, W_ml, "nn", F32, 1024, 1024, 2048, "proj_ml", side=_gather_side([shards["w_gate"]]))
    p_small = _matmul(u1, W_small, "nn", F32, 1024, 128, 2048, "proj_small")
    y_ssd, ssd_st = _ssd_forward(p_ssd, p_small, cwg, cbg, dtb_row, al_row, dskip_lane, ssd_norm_g)
    y_ml, ml_c, ml_nm = _mlstm_forward(p_ml, p_small, ib_row, fb_row, mlstm_norm_g)
    y_mix = jnp.concatenate([y_ssd, y_ml], axis=1)
    W_out = gathered("w_out", g_out).reshape(2 * SSD_WIDTH, D_MODEL)
    h1, (g_up,) = _matmul(y_mix, W_out, "nn", F32, 1024, 1024, 2048, "out_proj", addend=xs,
                          side=_gather_side([shards["w_up"]]))
    u2 = _rmsnorm_fwd(h1, norm_ffn_g, "norm_ffn_fwd")
    W_gate = by_cols(gathered("w_gate", g_gate), D_FF)
    gate, (g_down,) = _matmul(u2, W_gate, "nn", F32, 1024, FF_TILE, 2048, "ffn_gate", side=_gather_side([shards["w_down"]]))
    W_up = by_cols(gathered("w_up", g_up), D_FF)
    W_down = gathered("w_down", g_down).reshape(D_FF, D_MODEL)
    up = _matmul(u2, W_up, "nn", F32, 1024, FF_TILE, 2048, "ffn_up")
    act = _swiglu_fwd(gate, up)
    h2 = _matmul(act, W_down, "nn", F32, 1024, 1024, FF_TILE, "ffn_down", addend=h1)
    dh2, dh2_b, loss_row, d_fng = _loss_head(h2, tgt, fng)

    dact = _matmul(dh2_b, W_down, "nt", F32, 1024, FF_TILE, 2048, "d_act")
    dW_down = _matmul(act, dh2_b, "tn", BF16, FF_TILE, 1024, 1024, "dw_down", layout="rows4")
    d_gate, d_up = _swiglu_bwd(dact, gate, up)
    du2 = _matmul(d_gate, W_gate, "nt", F32, 1024, 1024, FF_TILE, "du2_gate")
    du2 = _matmul(d_up, W_up, "nt", F32, 1024, 1024, FF_TILE, "du2_up", addend=du2)
    dW_gate = _matmul(u2, d_gate, "tn", BF16, 1024, FF_TILE, 1024, "dw_gate", layout="cols4")
    dW_up = _matmul(u2, d_up, "tn", BF16, 1024, FF_TILE, 1024, "dw_up", layout="cols4")
    dh1, dh1_b, d_ffn_g = _rmsnorm_bwd(du2, h1, norm_ffn_g, dh2, "norm_ffn_bwd")
    dy_mix = _matmul(dh1_b, W_out, "nt", F32, 1024, 1024, 2048, "d_mix")
    dW_out = _matmul(y_mix, dh1_b, "tn", BF16, 1024, 1024, 1024, "dw_out", layout="rows4")
    early = ["w_out", "w_gate", "w_up", "w_down"]
    early_g = [dW_out, dW_gate, dW_up, dW_down]
    early_got = _pair_exchange(early_g, "grad_pair_exchange_early")
    parts = {n: _pair_add(g, r, ci, "grad_pair_add_" + n) for g, r, n in zip(early_g, early_got, early)}
    (d_ssd, d_small, d_cw, d_cb, d_dtb, d_alog, d_dskip, d_sng), early_slots = _ssd_backward(
        p_ssd, p_small, cwg, cbg, dtb_row, al_row, dskip_lane, ssd_norm_g, ssd_st, dy_mix,
        side=_scatter_side([parts[n] for n in early]))
    slots = dict(zip(early, early_slots))
    d_ml, d_small, d_ib, d_fb, d_mng = _mlstm_backward(p_ml, p_small, ib_row, fb_row, mlstm_norm_g, ml_c, ml_nm, dy_mix, d_small)
    du1 = _matmul(d_ssd, W_ssd, "nt", F32, 1024, 1024, 1280, "du1_ssd")
    du1 = _matmul(d_ml, W_ml, "nt", F32, 1024, 1024, 1024, "du1_ml", addend=du1)
    du1 = _matmul(d_small, W_small, "nt", F32, 1024, 1024, 128, "du1_small", addend=du1)
    dW_ssd = _matmul(u1, d_ssd, "tn", BF16, 1024, 1280, 1024, "dw_ssd")
    dW_ml = _matmul(u1, d_ml, "tn", BF16, 1024, 1024, 1024, "dw_ml")
    dW_small = _matmul(u1, d_small, "tn", BF16, 1024, 128, 1024, "dw_small")
    grad_x, _, d_mix_g = _rmsnorm_bwd(du1, xs, norm_mix_g, dh1, "norm_mix_bwd")
    dW_in = _merge_w_in(dW_ssd, dW_ml, dW_small)

    g_w_in = _to_halves_cols(dW_in, 2828)
    (got_in,) = _pair_exchange([g_w_in], "grad_pair_exchange_w_in")
    parts["w_in"] = _pair_add(g_w_in, got_in, ci, "grad_pair_add_w_in")
    (slots["w_in"],) = _run_side(_scatter_side([parts["w_in"]]), "grad_chip_exchange_w_in")
    fulls = _pair_share([_chip_sum(parts[n], slots[n], chip, ci, "grad_chip_sum_" + n) for n in _BIG])
    grads = {n: f.reshape(weights[n].shape[1:]) for n, f in zip(_BIG, fulls)}

    small = _pack(dict(norm_mix_g=d_mix_g, conv_w=_conv_from_groups(d_cw), conv_b=_conv_from_groups(d_cb),
                       dt_bias=d_dtb[:, 0:32], a_log=d_alog[:, 0:32], d_skip=d_dskip.reshape(SSD_HEADS, 64).sum(axis=1),
                       ssd_norm_g=d_sng, i_bias=d_ib[:, LANE_I:LANE_I + 8], f_bias=d_fb[:, LANE_F:LANE_F + 8],
                       mlstm_norm_g=d_mng, norm_ffn_g=d_ffn_g, final_norm_g=d_fng, loss=loss_row[:, 0:1]))
    total = _sum_slots(_allgather_small(small, "allgather_small_grads"), "small_grad_sum")
    small_w = {n: weights[n] for n in _WEIGHTS if n not in _BIG and n != "conv_w"}
    sd, sm, sv = _adamw(_pack(small_w), total, _pack({n: mom1[n] for n in small_w}), _pack({n: mom2[n] for n in small_w}),
                        PACK_ROWS, "adamw_small")
    tot, sd, sm, sv = _unpack(total), _unpack(sd), _unpack(sm), _unpack(sv)
    for n in small_w:
        grads[n] = tot[n].reshape(weights[n].shape)
    grads["conv_w"] = lax.dynamic_slice_in_dim(tot["conv_w"], chip * 768, 768, axis=1)
    loss = tot["loss"][0, 0]

    delta, new_m, new_v = {}, {}, {}
    for n in _BIG + ["conv_w"]:
        w2 = weights[n][0]
        d, nm, nv = _adamw(w2, grads[n], mom1[n][0], mom2[n][0], _row_tile(w2.shape[0]) if n != "conv_w" else SSD_CONV,
                           "adamw_" + n)
        delta[n], new_m[n], new_v[n] = d[None], nm[None], nv[None]
        grads[n] = grads[n][None]
    for n in small_w:
        delta[n], new_m[n], new_v[n] = (t[n].reshape(weights[n].shape) for t in (sd, sm, sv))
    return (loss, grad_x[None], *[grads[n] for n in _WEIGHTS], *[delta[n] for n in _WEIGHTS],
            *[new_m[n] for n in _WEIGHTS], *[new_v[n] for n in _WEIGHTS])
```

```python
import functools

import jax
import jax.numpy as jnp
import numpy as np
from jax import lax
from jax.experimental import pallas as pl
from jax.experimental.pallas import tpu as pltpu

F32 = jnp.float32
BF16 = jnp.bfloat16

D_MODEL = 2048
SSD_HEADS = 32
SSD_GROUPS = 4
SSD_STATE = 128
SSD_WIDTH = 2048
SSD_CONV = 4
SSD_GROUP_COLS = 1280
SSD_XBC_COLS = 768
ML_HEADS = 8
ML_HEAD_COLS = 768
ML_DK = 128
ML_DV = 256
CHUNK = 128
SMALL_COLS = 128
LANE_I = 32
LANE_F = 40
D_FF = 5632
GATE_SOFTCAP = 15.0
EPS = 1e-6
ADAM_LR, ADAM_B1, ADAM_B2, ADAM_EPS, ADAM_WD, ADAM_STEP = 0.001, 0.9, 0.999, 1e-8, 0.01, 10
MESH = pl.DeviceIdType.MESH
VMEM_LIMIT = 56 * 1024 * 1024


def _dg(a, b, ca, cb):
    return lax.dot_general(a.astype(BF16), b.astype(BF16), (((ca,), (cb,)), ((), ())), preferred_element_type=F32)


@jax.custom_vjp
def _mm(a, b):
    return _dg(a, b, 1, 0)


_mm.defvjp(lambda a, b: (_dg(a, b, 1, 0), (a, b)),
           lambda r, g: (_dg(g, r[1], 1, 1), _dg(r[0], g, 0, 0)))


@jax.custom_vjp
def _mm_nt(a, b):
    return _dg(a, b, 1, 1)


_mm_nt.defvjp(lambda a, b: (_dg(a, b, 1, 1), (a, b)),
              lambda r, g: (_dg(g, r[1], 1, 0), _dg(g, r[0], 0, 0)))


@jax.custom_vjp
def _mm_tn(a, b):
    return _dg(a, b, 0, 0)


_mm_tn.defvjp(lambda a, b: (_dg(a, b, 0, 0), (a, b)),
              lambda r, g: (_dg(r[1], g, 1, 1), _dg(r[0], g, 1, 0)))


def _tri(lower):
    r = lax.broadcasted_iota(jnp.int32, (CHUNK, CHUNK), 0)
    c = lax.broadcasted_iota(jnp.int32, (CHUNK, CHUNK), 1)
    return ((r >= c) if lower else (r <= c)).astype(F32)


def _dg32(t, x):
    return lax.dot_general(t, x, (((1,), (0,)), ((), ())), precision=lax.Precision.HIGHEST, preferred_element_type=F32)


@jax.custom_vjp
def _cumsum(x):
    return _dg32(_tri(True), x)


_cumsum.defvjp(lambda x: (_dg32(_tri(True), x), None), lambda r, g: (_dg32(_tri(False), g),))


_sigmoid = jax.nn.sigmoid


def _silu(x):
    return x * _sigmoid(x)


def _softplus(x):
    return jnp.maximum(x, 0.0) + jnp.log(1.0 + jnp.exp(-jnp.abs(x)))


def _lane_col(m, lane, h):
    return jnp.sum(jnp.where(lane == h, m, 0.0), axis=1, keepdims=True)


SSD_GPS = 4


def _ssd_math(convs, zs, smallb, S_in, dtb_row, alog_row, dskip, ng, g0):
    dt_all = _softplus(smallb + dtb_row)
    a_all = dt_all * (-jnp.exp(alog_row))
    prep = (dt_all, _cumsum(a_all), jnp.sum(a_all, axis=0, keepdims=True))
    ys, S_out = [], []
    for k in range(SSD_GPS):
        y, S = _ssd_group(convs[k][:, 0:512], convs[k][:, 512:640], convs[k][:, 640:768], zs[k], prep, S_in[k],
                          dskip[:, 512 * k:512 * (k + 1)], ng[:, 512 * k:512 * (k + 1)], g0 + k)
        ys.append(y)
        S_out.append(S)
    return tuple(ys), tuple(S_out)


def _ssd_group(cx, cB, cC, z, prep, S_in, dskip, ng, g):
    dt_all, acum_all, alast_all = prep
    lane = lax.broadcasted_iota(jnp.int32, (1, CHUNK), 1)
    row_i = lax.broadcasted_iota(jnp.int32, (CHUNK, CHUNK), 0)
    col_i = lax.broadcasted_iota(jnp.int32, (CHUNK, CHUNK), 1)
    causal = row_i >= col_i
    half = lane < 64
    rhalf = lax.broadcasted_iota(jnp.int32, (CHUNK, 1), 0) < 64
    xs, Bm, Cm = _silu(cx), _silu(cB), _silu(cC)
    cb = _mm_nt(Cm, Bm)

    def lmat(ac):
        acb = jnp.broadcast_to(ac, (CHUNK, CHUNK))
        return jnp.exp(jnp.where(causal, acb - acb.T, -jnp.inf))

    ys, S_out = [], []
    for j in range(4):
        h0 = 8 * g + 2 * j
        ac0, ac1 = _lane_col(acum_all, lane, h0), _lane_col(acum_all, lane, h0 + 1)
        dt0, dt1 = _lane_col(dt_all, lane, h0), _lane_col(dt_all, lane, h0 + 1)
        al0, al1 = _lane_col(alast_all, lane, h0), _lane_col(alast_all, lane, h0 + 1)
        Xp = xs[:, 128 * j:128 * (j + 1)]
        Xd = Xp * jnp.where(half, dt0, dt1)
        ac_sel = jnp.where(half, ac0, ac1)
        al_sel = jnp.where(half, al0, al1)
        Yd = jnp.where(half, _mm(cb * lmat(ac0), Xd), _mm(cb * lmat(ac1), Xd))
        Yoff = _mm_nt(Cm, S_in[j]) * jnp.exp(ac_sel)
        ys.append(Yd + Yoff + dskip[:, 128 * j:128 * (j + 1)] * Xp)
        S_new = _mm_tn(Xd * jnp.exp(al_sel - ac_sel), Bm)
        S_out.append(S_in[j] * jnp.exp(jnp.where(rhalf, al0, al1)) + S_new)
    y = jnp.concatenate(ys, axis=1)
    y = y * _silu(z)
    y = y * lax.rsqrt(jnp.mean(y * y, axis=1, keepdims=True) + EPS) * ng
    return y, tuple(S_out)


def _ssd_conv(blk_ref, halo_ref, cw_ref, cb_ref, ext, first, k):
    cols = slice(SSD_GROUP_COLS * k + 512, SSD_GROUP_COLS * (k + 1))
    halo = halo_ref[:, cols]
    ext[k, 0:8, :] = jnp.where(first, jnp.zeros_like(halo), halo)
    ext[k, 8:8 + CHUNK, :] = blk_ref[:, cols]
    conv = jnp.broadcast_to(cb_ref[k], (CHUNK, SSD_XBC_COLS))
    for tap in range(SSD_CONV):
        conv = conv + cw_ref[k, tap:tap + 1, :] * ext[k, pl.ds(5 + tap, CHUNK), :]
    return conv


def _ssd_specs(nc, rev):
    cc = (lambda c: nc - 1 - c) if rev else (lambda c: c)
    width = SSD_GPS * SSD_GROUP_COLS
    return [
        pl.BlockSpec((CHUNK, width), lambda c, g: (cc(c), g)),
        pl.BlockSpec((8, width), lambda c, g: (jnp.maximum(cc(c) * (CHUNK // 8) - 1, 0), g)),
        pl.BlockSpec((CHUNK, SMALL_COLS), lambda c, g: (cc(c), 0)),
        pl.BlockSpec((SSD_GPS, SSD_CONV, SSD_XBC_COLS), lambda c, g: (g, 0, 0)),
        pl.BlockSpec((SSD_GPS, 1, SSD_XBC_COLS), lambda c, g: (g, 0, 0)),
        pl.BlockSpec((1, SMALL_COLS), lambda c, g: (0, 0)),
        pl.BlockSpec((1, SMALL_COLS), lambda c, g: (0, 0)),
        pl.BlockSpec((1, SSD_GPS * 512), lambda c, g: (0, g)),
        pl.BlockSpec((1, SSD_GPS * 512), lambda c, g: (0, g)),
    ]


def _ssd_forward(proj, small, cw, cb, dtb, alog, dskip, ng):
    L = proj.shape[0]
    nc = L // CHUNK

    def body(blk_ref, halo_ref, small_ref, cw_ref, cb_ref, dtb_ref, alog_ref, dskip_ref, ng_ref, y_ref, st_ref, carry, ext):
        c, g0 = pl.program_id(0), pl.program_id(1) * SSD_GPS

        @pl.when(c == 0)
        def _():
            for k in range(SSD_GPS):
                carry[g0 + k] = jnp.zeros((4, CHUNK, CHUNK), F32)

        convs = [_ssd_conv(blk_ref, halo_ref, cw_ref, cb_ref, ext, c == 0, k) for k in range(SSD_GPS)]
        zs = [blk_ref[:, SSD_GROUP_COLS * k:SSD_GROUP_COLS * k + 512] for k in range(SSD_GPS)]
        S_in = tuple(tuple(carry[g0 + k, j] for j in range(4)) for k in range(SSD_GPS))
        for k in range(SSD_GPS):
            st_ref[0, k] = carry[g0 + k]
        ys, S_out = _ssd_math(convs, zs, small_ref[...], S_in, dtb_ref[...], alog_ref[...], dskip_ref[...], ng_ref[...], g0)
        for k in range(SSD_GPS):
            y_ref[:, 512 * k:512 * (k + 1)] = ys[k].astype(BF16)
            for j in range(4):
                carry[g0 + k, j] = S_out[k][j]

    return pl.pallas_call(
        body, name="ssd_fwd", grid=(nc, SSD_GROUPS // SSD_GPS),
        in_specs=_ssd_specs(nc, False),
        out_specs=[pl.BlockSpec((CHUNK, SSD_GPS * 512), lambda c, g: (c, g)),
                   pl.BlockSpec((1, SSD_GPS, 4, CHUNK, CHUNK), lambda c, g: (c, g, 0, 0, 0))],
        out_shape=[jax.ShapeDtypeStruct((L, SSD_WIDTH), BF16), jax.ShapeDtypeStruct((nc, SSD_GROUPS, 4, CHUNK, CHUNK), F32)],
        scratch_shapes=[pltpu.VMEM((SSD_GROUPS, 4, CHUNK, CHUNK), F32), pltpu.VMEM((SSD_GPS, 8 + CHUNK, SSD_XBC_COLS), F32)],
        compiler_params=pltpu.CompilerParams(dimension_semantics=("arbitrary", "arbitrary"), vmem_limit_bytes=VMEM_LIMIT),
    )(proj, proj, small, cw, cb, dtb, alog, dskip, ng)


def _ssd_backward(proj, small, cw, cb, dtb, alog, dskip, ng, states, dy, side=None):
    L = proj.shape[0]
    nc = L // CHUNK
    s_in, s_out = (len(side.operands), len(side.out_shapes)) if side is not None else (0, 0)
    grid = (nc, SSD_GROUPS // SSD_GPS)

    def body(*refs):
        blk_ref, halo_ref, small_ref, cw_ref, cb_ref, dtb_ref, alog_ref, dskip_ref, ng_ref, st_ref, dy_ref = refs[:11]
        dproj_ref, dsmall_ref, dcw_ref, dcb_ref, ddtb_ref, dalog_ref, ddskip_ref, dng_ref = refs[11 + s_in:19 + s_in]
        dcarry, nxt, ext, dext = refs[19 + s_in + s_out:23 + s_in + s_out]
        side_refs = (refs[11:11 + s_in], refs[19 + s_in:19 + s_in + s_out], refs[23 + s_in + s_out:])
        s, gb = pl.program_id(0), pl.program_id(1)
        g0 = gb * SSD_GPS
        c = nc - 1 - s
        if side is not None:
            pl.when(_grid_edge(grid, first=True))(lambda: side.start(*side_refs))

        @pl.when(s == 0)
        def _():
            for k in range(SSD_GPS):
                dcarry[g0 + k] = jnp.zeros((4, CHUNK, CHUNK), F32)
                nxt[g0 + k] = jnp.zeros((8, SSD_XBC_COLS), F32)
                dcw_ref[g0 + k] = jnp.zeros((SSD_CONV, SSD_XBC_COLS), F32)
                dcb_ref[g0 + k] = jnp.zeros((1, SSD_XBC_COLS), F32)
                ddskip_ref[g0 + k] = jnp.zeros((1, 512), F32)
                dng_ref[g0 + k] = jnp.zeros((1, 512), F32)

        @pl.when((s == 0) & (gb == 0))
        def _():
            ddtb_ref[...] = jnp.zeros((1, SMALL_COLS), F32)
            dalog_ref[...] = jnp.zeros((1, SMALL_COLS), F32)

        convs = [_ssd_conv(blk_ref, halo_ref, cw_ref, cb_ref, ext, c == 0, k) for k in range(SSD_GPS)]
        zs = [blk_ref[:, SSD_GROUP_COLS * k:SSD_GROUP_COLS * k + 512] for k in range(SSD_GPS)]
        S_in = tuple(tuple(st_ref[0, k, j] for j in range(4)) for k in range(SSD_GPS))
        dS_out = tuple(tuple(dcarry[g0 + k, j] for j in range(4)) for k in range(SSD_GPS))
        dys = tuple(dy_ref[:, 512 * k:512 * (k + 1)].astype(F32) for k in range(SSD_GPS))
        _, vjp = jax.vjp(functools.partial(_ssd_math, g0=g0), convs, zs, small_ref[...], S_in, dtb_ref[...], alog_ref[...],
                         dskip_ref[...], ng_ref[...])
        d_convs, d_zs, d_small, dS_in, d_dtb, d_alog, d_dskip, d_ng = vjp((dys, dS_out))
        for k in range(SSD_GPS):
            for j in range(4):
                dcarry[g0 + k, j] = dS_in[k][j]
            dext[k, 0:8, :] = jnp.zeros((8, SSD_XBC_COLS), F32)
            dext[k, 8:8 + CHUNK, :] = d_convs[k]
            dext[k, 8 + CHUNK:16 + CHUNK, :] = nxt[g0 + k]
            nxt[g0 + k] = dext[k, 8:16, :]
            dconv = d_convs[k]
            d_xbc = jnp.zeros((CHUNK, SSD_XBC_COLS), F32)
            for tap in range(SSD_CONV):
                d_xbc = d_xbc + cw_ref[k, tap:tap + 1, :] * dext[k, pl.ds(8 + 3 - tap, CHUNK), :]
                dcw_ref[g0 + k, tap:tap + 1, :] += jnp.sum(dconv * ext[k, pl.ds(5 + tap, CHUNK), :], axis=0, keepdims=True)
            dcb_ref[g0 + k] += jnp.sum(dconv, axis=0, keepdims=True)
            dproj_ref[:, SSD_GROUP_COLS * k:SSD_GROUP_COLS * k + 512] = d_zs[k].astype(BF16)
            dproj_ref[:, SSD_GROUP_COLS * k + 512:SSD_GROUP_COLS * (k + 1)] = d_xbc.astype(BF16)
            ddskip_ref[g0 + k] += d_dskip[:, 512 * k:512 * (k + 1)]
            dng_ref[g0 + k] += d_ng[:, 512 * k:512 * (k + 1)]

        @pl.when(gb == 0)
        def _():
            dsmall_ref[...] = d_small

        @pl.when(gb != 0)
        def _():
            dsmall_ref[...] += d_small

        ddtb_ref[...] += d_dtb
        dalog_ref[...] += d_alog
        if side is not None:
            pl.when(_grid_edge(grid, first=False))(lambda: side.finish(*side_refs))

    whole = lambda shape: pl.BlockSpec(shape, lambda s, g: (0,) * len(shape))
    side_ops = tuple(side.operands) if side is not None else ()
    res = pl.pallas_call(
        body, name="ssd_bwd", grid=grid,
        in_specs=_ssd_specs(nc, True) + [
            pl.BlockSpec((1, SSD_GPS, 4, CHUNK, CHUNK), lambda s, g: (nc - 1 - s, g, 0, 0, 0)),
            pl.BlockSpec((CHUNK, SSD_GPS * 512), lambda s, g: (nc - 1 - s, g))] + [ANY] * s_in,
        out_specs=[pl.BlockSpec((CHUNK, SSD_GPS * SSD_GROUP_COLS), lambda s, g: (nc - 1 - s, g)),
                   pl.BlockSpec((CHUNK, SMALL_COLS), lambda s, g: (nc - 1 - s, 0)),
                   whole((SSD_GROUPS, SSD_CONV, SSD_XBC_COLS)), whole((SSD_GROUPS, 1, SSD_XBC_COLS)),
                   whole((1, SMALL_COLS)), whole((1, SMALL_COLS)),
                   whole((SSD_GROUPS, 1, 512)), whole((SSD_GROUPS, 1, 512))] + [ANY] * s_out,
        out_shape=[jax.ShapeDtypeStruct((L, SSD_GROUPS * SSD_GROUP_COLS), BF16), jax.ShapeDtypeStruct((L, SMALL_COLS), F32),
                   jax.ShapeDtypeStruct((SSD_GROUPS, SSD_CONV, SSD_XBC_COLS), F32),
                   jax.ShapeDtypeStruct((SSD_GROUPS, 1, SSD_XBC_COLS), F32),
                   jax.ShapeDtypeStruct((1, SMALL_COLS), F32), jax.ShapeDtypeStruct((1, SMALL_COLS), F32),
                   jax.ShapeDtypeStruct((SSD_GROUPS, 1, 512), F32), jax.ShapeDtypeStruct((SSD_GROUPS, 1, 512), F32)]
        + (list(side.out_shapes) if side is not None else []),
        scratch_shapes=[pltpu.VMEM((SSD_GROUPS, 4, CHUNK, CHUNK), F32), pltpu.VMEM((SSD_GROUPS, 8, SSD_XBC_COLS), F32),
                        pltpu.VMEM((SSD_GPS, 8 + CHUNK, SSD_XBC_COLS), F32),
                        pltpu.VMEM((SSD_GPS, 16 + CHUNK, SSD_XBC_COLS), F32)]
        + (list(side.sems) if side is not None else []),
        compiler_params=pltpu.CompilerParams(dimension_semantics=("arbitrary", "arbitrary"), vmem_limit_bytes=VMEM_LIMIT),
    )(proj, proj, small, cw, cb, dtb, alog, dskip, ng, states, dy, *side_ops)
    return res if side is None else (res[:8], res[8:])


ML_HPS = 8


def _mlstm_math(blks, smallb, C_in, n_in, m_in, ib_row, fb_row, ng, h0):
    li_all = GATE_SOFTCAP * jnp.tanh((smallb + ib_row) / GATE_SOFTCAP)
    lf_all = -_softplus(-(GATE_SOFTCAP * jnp.tanh((smallb + fb_row) / GATE_SOFTCAP)))
    prep = (li_all, lf_all, _cumsum(lf_all))
    out = [_mlstm_head(blks[k][:, 0:128], blks[k][:, 128:256], blks[k][:, 256:512], blks[k][:, 512:768], prep,
                       C_in[k], n_in[k], m_in[k], ng[:, ML_DV * k:ML_DV * (k + 1)], h0 + k) for k in range(ML_HPS)]
    return tuple(zip(*out))


def _mlstm_head(q, k, v, o_raw, prep, C_in, n_in, m_in_row, ng, h):
    li_all, lf_all, bcum_all = prep
    lane = lax.broadcasted_iota(jnp.int32, (1, CHUNK), 1)
    row_i = lax.broadcasted_iota(jnp.int32, (CHUNK, CHUNK), 0)
    col_i = lax.broadcasted_iota(jnp.int32, (CHUNK, CHUNK), 1)
    causal = row_i >= col_i
    kk = k * (ML_DK ** -0.5)
    li = _lane_col(li_all, lane, LANE_I + h)
    lf = _lane_col(lf_all, lane, LANE_F + h)
    bc = _lane_col(bcum_all, lane, LANE_F + h)
    b_last = jnp.sum(lf, axis=0, keepdims=True)
    m_in = _lane_col(m_in_row, lane, 0)
    a = b_last - bc + li
    m_loc = jnp.max(a, axis=0, keepdims=True)
    w = jnp.exp(a - m_loc)
    C_loc = _mm_tn(w * v, kk)
    n_loc = jnp.sum(w * kk, axis=0, keepdims=True)
    m_new = jnp.maximum(b_last + m_in, m_loc)
    s_old = jnp.exp(b_last + m_in - m_new)
    s_new = jnp.exp(m_loc - m_new)
    C_out = s_old * C_in + s_new * C_loc
    n_out = s_old * n_in + s_new * n_loc
    bc_b = jnp.broadcast_to(bc, (CHUNK, CHUNK))
    li_b = jnp.broadcast_to(li, (CHUNK, CHUNK))
    D = jnp.where(causal, bc_b - bc_b.T + li_b.T, -jnp.inf)
    m_intra = jnp.max(D, axis=1, keepdims=True)
    inter_log = bc + m_in
    m_t = jnp.maximum(inter_log, m_intra)
    S = _mm_nt(q, kk) * jnp.exp(D - m_t)
    w_inter = jnp.exp(inter_log - m_t)
    num = _mm(S, v) + w_inter * _mm_nt(q, C_in)
    nq = jnp.sum(S, axis=1, keepdims=True) + w_inter * jnp.sum(q * n_in, axis=1, keepdims=True)
    den = jnp.maximum(jnp.abs(nq), jnp.exp(-m_t))
    hh = num / den
    hh = hh * lax.rsqrt(jnp.mean(hh * hh, axis=1, keepdims=True) + EPS)
    hh = hh * ng * _sigmoid(o_raw)
    return hh, C_out, n_out, jnp.broadcast_to(m_new, (1, CHUNK))


def _ml_specs(nc, rev):
    cc = (lambda c: nc - 1 - c) if rev else (lambda c: c)
    return [
        pl.BlockSpec((CHUNK, ML_HPS * ML_HEAD_COLS), lambda c, h: (cc(c), h)),
        pl.BlockSpec((CHUNK, SMALL_COLS), lambda c, h: (cc(c), 0)),
        pl.BlockSpec((1, SMALL_COLS), lambda c, h: (0, 0)),
        pl.BlockSpec((1, SMALL_COLS), lambda c, h: (0, 0)),
        pl.BlockSpec((1, ML_HPS * ML_DV), lambda c, h: (0, h)),
    ]


def _mlstm_forward(proj, small, ib, fb, ng):
    L = proj.shape[0]
    nc = L // CHUNK

    def body(blk_ref, small_ref, ib_ref, fb_ref, ng_ref, y_ref, cst_ref, nm_ref, c_carry, nm_carry):
        c, h0 = pl.program_id(0), pl.program_id(1) * ML_HPS

        @pl.when(c == 0)
        def _():
            for k in range(ML_HPS):
                c_carry[h0 + k] = jnp.zeros((ML_DV, ML_DK), F32)
                nm_carry[h0 + k] = jnp.zeros((2, ML_DK), F32)

        for k in range(ML_HPS):
            cst_ref[0, k] = c_carry[h0 + k]
            nm_ref[0, k] = nm_carry[h0 + k]
        hh, C_out, n_out, m_out = _mlstm_math(
            [blk_ref[:, ML_HEAD_COLS * k:ML_HEAD_COLS * (k + 1)] for k in range(ML_HPS)], small_ref[...],
            [c_carry[h0 + k] for k in range(ML_HPS)], [nm_carry[h0 + k, 0:1, :] for k in range(ML_HPS)],
            [nm_carry[h0 + k, 1:2, :] for k in range(ML_HPS)], ib_ref[...], fb_ref[...], ng_ref[...], h0)
        for k in range(ML_HPS):
            y_ref[:, ML_DV * k:ML_DV * (k + 1)] = hh[k].astype(BF16)
            c_carry[h0 + k] = C_out[k]
            nm_carry[h0 + k, 0:1, :] = n_out[k]
            nm_carry[h0 + k, 1:2, :] = m_out[k]

    return pl.pallas_call(
        body, name="mlstm_fwd", grid=(nc, ML_HEADS // ML_HPS),
        in_specs=_ml_specs(nc, False),
        out_specs=[pl.BlockSpec((CHUNK, ML_HPS * ML_DV), lambda c, h: (c, h)),
                   pl.BlockSpec((1, ML_HPS, ML_DV, ML_DK), lambda c, h: (c, h, 0, 0)),
                   pl.BlockSpec((1, ML_HPS, 2, ML_DK), lambda c, h: (c, h, 0, 0))],
        out_shape=[jax.ShapeDtypeStruct((L, ML_HEADS * ML_DV), BF16),
                   jax.ShapeDtypeStruct((nc, ML_HEADS, ML_DV, ML_DK), F32),
                   jax.ShapeDtypeStruct((nc, ML_HEADS, 2, ML_DK), F32)],
        scratch_shapes=[pltpu.VMEM((ML_HEADS, ML_DV, ML_DK), F32), pltpu.VMEM((ML_HEADS, 2, ML_DK), F32)],
        compiler_params=pltpu.CompilerParams(dimension_semantics=("arbitrary", "arbitrary"), vmem_limit_bytes=VMEM_LIMIT),
    )(proj, small, ib, fb, ng)


def _mlstm_backward(proj, small, ib, fb, ng, cst, nmst, dy, dsmall_in):
    L = proj.shape[0]
    nc = L // CHUNK

    def body(blk_ref, small_ref, ib_ref, fb_ref, ng_ref, cst_ref, nm_ref, dy_ref, dsin_ref,
             dproj_ref, dsmall_ref, dib_ref, dfb_ref, dng_ref, dc_carry, dnm_carry):
        s, hb = pl.program_id(0), pl.program_id(1)
        h0 = hb * ML_HPS

        @pl.when(s == 0)
        def _():
            for k in range(ML_HPS):
                dc_carry[h0 + k] = jnp.zeros((ML_DV, ML_DK), F32)
                dnm_carry[h0 + k] = jnp.zeros((2, ML_DK), F32)
                dng_ref[h0 + k] = jnp.zeros((1, ML_DV), F32)

        @pl.when((s == 0) & (hb == 0))
        def _():
            dib_ref[...] = jnp.zeros((1, SMALL_COLS), F32)
            dfb_ref[...] = jnp.zeros((1, SMALL_COLS), F32)

        heads = range(ML_HPS)
        _, vjp = jax.vjp(functools.partial(_mlstm_math, h0=h0),
                         [blk_ref[:, ML_HEAD_COLS * k:ML_HEAD_COLS * (k + 1)] for k in heads], small_ref[...],
                         [cst_ref[0, k] for k in heads], [nm_ref[0, k, 0:1, :] for k in heads],
                         [nm_ref[0, k, 1:2, :] for k in heads], ib_ref[...], fb_ref[...], ng_ref[...])
        d_blk, d_small, dC, dn, dm, d_ib, d_fb, d_ng = vjp(
            (tuple(dy_ref[:, ML_DV * k:ML_DV * (k + 1)].astype(F32) for k in heads),
             tuple(dc_carry[h0 + k] for k in heads), tuple(dnm_carry[h0 + k, 0:1, :] for k in heads),
             tuple(dnm_carry[h0 + k, 1:2, :] for k in heads)))
        for k in heads:
            dc_carry[h0 + k] = dC[k]
            dnm_carry[h0 + k, 0:1, :] = dn[k]
            dnm_carry[h0 + k, 1:2, :] = dm[k]
            dproj_ref[:, ML_HEAD_COLS * k:ML_HEAD_COLS * (k + 1)] = d_blk[k].astype(BF16)
            dng_ref[h0 + k] += d_ng[:, ML_DV * k:ML_DV * (k + 1)]

        @pl.when(hb == 0)
        def _():
            dsmall_ref[...] = dsin_ref[...] + d_small

        @pl.when(hb != 0)
        def _():
            dsmall_ref[...] += d_small

        dib_ref[...] += d_ib
        dfb_ref[...] += d_fb

    whole = lambda shape: pl.BlockSpec(shape, lambda s, h: (0,) * len(shape))
    return pl.pallas_call(
        body, name="mlstm_bwd", grid=(nc, ML_HEADS // ML_HPS),
        in_specs=_ml_specs(nc, True) + [
            pl.BlockSpec((1, ML_HPS, ML_DV, ML_DK), lambda s, h: (nc - 1 - s, h, 0, 0)),
            pl.BlockSpec((1, ML_HPS, 2, ML_DK), lambda s, h: (nc - 1 - s, h, 0, 0)),
            pl.BlockSpec((CHUNK, ML_HPS * ML_DV), lambda s, h: (nc - 1 - s, SSD_WIDTH // (ML_HPS * ML_DV) + h)),
            pl.BlockSpec((CHUNK, SMALL_COLS), lambda s, h: (nc - 1 - s, 0))],
        out_specs=[pl.BlockSpec((CHUNK, ML_HPS * ML_HEAD_COLS), lambda s, h: (nc - 1 - s, h)),
                   pl.BlockSpec((CHUNK, SMALL_COLS), lambda s, h: (nc - 1 - s, 0)),
                   whole((1, SMALL_COLS)), whole((1, SMALL_COLS)), whole((ML_HEADS, 1, ML_DV))],
        out_shape=[jax.ShapeDtypeStruct((L, ML_HEADS * ML_HEAD_COLS), BF16), jax.ShapeDtypeStruct((L, SMALL_COLS), F32),
                   jax.ShapeDtypeStruct((1, SMALL_COLS), F32), jax.ShapeDtypeStruct((1, SMALL_COLS), F32),
                   jax.ShapeDtypeStruct((ML_HEADS, 1, ML_DV), F32)],
        scratch_shapes=[pltpu.VMEM((ML_HEADS, ML_DV, ML_DK), F32), pltpu.VMEM((ML_HEADS, 2, ML_DK), F32)],
        compiler_params=pltpu.CompilerParams(dimension_semantics=("arbitrary", "arbitrary"), vmem_limit_bytes=VMEM_LIMIT),
    )(proj, small, ib, fb, ng, cst, nmst, dy, dsmall_in)


_OFF_Z, _OFF_X, _OFF_B, _OFF_C, _OFF_DT = 0, 2048, 4096, 4608, 5120
_OFF_Q, _OFF_K, _OFF_V, _OFF_O, _OFF_I, _OFF_F, IN_WIDTH = 5152, 6176, 7200, 9248, 11296, 11304, 11312


def _ssd_segments():
    segs = []
    for g in range(SSD_GROUPS):
        segs += [(_OFF_Z + 512 * g, 512), (_OFF_X + 512 * g, 512), (_OFF_B + 128 * g, 128), (_OFF_C + 128 * g, 128)]
    return segs


def _ml_segments():
    segs = []
    for h in range(ML_HEADS):
        segs += [(_OFF_Q + 128 * h, 128), (_OFF_K + 128 * h, 128), (_OFF_V + 256 * h, 256), (_OFF_O + 256 * h, 256)]
    return segs


_SMALL_SEGMENTS = [(_OFF_DT, 32), (_OFF_I, 8), (_OFF_F, 8)]


def _take_cols(w, segs):
    return jnp.concatenate([w[:, s:s + n] for s, n in segs], axis=1)


def _split_w_in(w):
    small = _take_cols(w, _SMALL_SEGMENTS)
    small = jnp.concatenate([small, jnp.zeros((w.shape[0], SMALL_COLS - small.shape[1]), w.dtype)], axis=1)
    return _take_cols(w, _ssd_segments()), _take_cols(w, _ml_segments()), small


def _merge_w_in(d_ssd, d_ml, d_small):
    pieces = []
    for arr, segs in ((d_ssd, _ssd_segments()), (d_ml, _ml_segments()), (d_small, _SMALL_SEGMENTS)):
        pos = 0
        for s, n in segs:
            pieces.append((s, arr[:, pos:pos + n]))
            pos += n
    pieces.sort(key=lambda t: t[0])
    return jnp.concatenate([p for _, p in pieces], axis=1)


def _conv_to_groups(cw):
    return jnp.stack([jnp.concatenate([cw[:, 512 * g:512 * (g + 1)], cw[:, 2048 + 128 * g:2048 + 128 * (g + 1)],
                                       cw[:, 2560 + 128 * g:2560 + 128 * (g + 1)]], axis=1) for g in range(SSD_GROUPS)])


def _conv_from_groups(d):
    return jnp.concatenate([d[g, :, 0:512] for g in range(SSD_GROUPS)] + [d[g, :, 512:640] for g in range(SSD_GROUPS)]
                           + [d[g, :, 640:768] for g in range(SSD_GROUPS)], axis=1)


def _small_row(vec, lane0):
    n = vec.shape[1]
    return jnp.concatenate([jnp.zeros((1, lane0), F32), vec, jnp.zeros((1, SMALL_COLS - lane0 - n), F32)], axis=1)


class _Side:
    def __init__(self, operands, out_shapes, sems, start, finish):
        self.operands, self.out_shapes, self.sems, self.start, self.finish = operands, out_shapes, sems, start, finish


def _grid_edge(grid, first):
    hit = None
    for axis, n in enumerate(grid):
        here = pl.program_id(axis) == (0 if first else n - 1)
        hit = here if hit is None else hit & here
    return hit


def _matmul(a, b, mode, out_dtype, tm, tn, tk, name, addend=None, layout=None, side=None):
    M, Kd = (a.shape[1], a.shape[0]) if mode == "tn" else a.shape
    N = b.shape[0] if mode == "nt" else b.shape[1]
    if layout == "cols4":
        tm, tn = M // 2, N // 4
    elif layout == "rows4":
        tm = M // 4
    tm, tn, tk = min(tm, M), min(tn, N), min(tk, Kd)
    if mode == "nn":
        a_spec = pl.BlockSpec((tm, tk), lambda i, j, k: (i, k))
        b_spec = pl.BlockSpec((tk, tn), lambda i, j, k: (k, j))
        dims = (((1,), (0,)), ((), ()))
    elif mode == "nt":
        a_spec = pl.BlockSpec((tm, tk), lambda i, j, k: (i, k))
        b_spec = pl.BlockSpec((tn, tk), lambda i, j, k: (j, k))
        dims = (((1,), (1,)), ((), ()))
    else:
        a_spec = pl.BlockSpec((tk, tm), lambda i, j, k: (k, i))
        b_spec = pl.BlockSpec((tk, tn), lambda i, j, k: (k, j))
        dims = (((0,), (0,)), ((), ()))
    assert M % tm == 0 and N % tn == 0 and Kd % tk == 0, (name, M, N, Kd, tm, tn, tk)
    nk = Kd // tk
    has_add = addend is not None
    n_in = 2 + has_add
    s_in, s_out = (len(side.operands), len(side.out_shapes)) if side is not None else (0, 0)
    grid = (M // tm, N // tn, nk)

    def body(*refs):
        a_ref, b_ref = refs[0], refs[1]
        add_ref = refs[2] if has_add else None
        o_ref, acc_ref = refs[n_in + s_in], refs[n_in + s_in + 1 + s_out]
        k = pl.program_id(2)
        side_refs = (refs[n_in:n_in + s_in], refs[n_in + s_in + 1:n_in + s_in + 1 + s_out], refs[n_in + s_in + 2 + s_out:])
        if side is not None:
            pl.when(_grid_edge(grid, first=True))(lambda: side.start(*side_refs))
        part = lax.dot_general(a_ref[...].astype(BF16), b_ref[...].astype(BF16), dims, preferred_element_type=F32)

        @pl.when(k == 0)
        def _():
            acc_ref[...] = part

        @pl.when(k != 0)
        def _():
            acc_ref[...] += part

        @pl.when(k == nk - 1)
        def _():
            r = acc_ref[...]
            if has_add:
                r = r + add_ref[...].astype(F32)
            if layout == "cols4":
                o_ref[0, 0] = r.astype(out_dtype)
            elif layout == "rows4":
                o_ref[0, 0] = r[:tm // 2].astype(out_dtype)
                o_ref[1, 0] = r[tm // 2:].astype(out_dtype)
            else:
                o_ref[...] = r.astype(out_dtype)

        if side is not None:
            pl.when(_grid_edge(grid, first=False))(lambda: side.finish(*side_refs))

    o_spec = pl.BlockSpec((tm, tn), lambda i, j, k: (i, j))
    out_spec, out_dims = o_spec, (M, N)
    if layout == "cols4":
        out_spec, out_dims = pl.BlockSpec((1, 1, tm, tn), lambda i, j, k: (i, j, 0, 0)), (2, 4, tm, tn)
    elif layout == "rows4":
        out_spec, out_dims = pl.BlockSpec((2, 1, tm // 2, tn), lambda i, j, k: (0, i, 0, j)), (2, 4, tm // 2, N)
    out_shape = jax.ShapeDtypeStruct(out_dims, out_dtype)
    semantics = ("parallel", "parallel", "arbitrary")
    if side is None:
        in_specs, operands, out_specs, scratch = [], (), out_spec, []
    else:
        in_specs, operands, scratch = [ANY] * s_in, tuple(side.operands), list(side.sems)
        out_specs, out_shape = [out_spec] + [ANY] * s_out, [out_shape] + list(side.out_shapes)
        semantics = ("arbitrary",) * 3
    res = pl.pallas_call(
        body, name=name, grid=grid,
        in_specs=[a_spec, b_spec] + ([o_spec] if has_add else []) + in_specs, out_specs=out_specs,
        out_shape=out_shape, scratch_shapes=[pltpu.VMEM((tm, tn), F32)] + scratch,
        compiler_params=pltpu.CompilerParams(dimension_semantics=semantics, vmem_limit_bytes=VMEM_LIMIT),
    )(*((a, b) + ((addend,) if has_add else ()) + operands))
    return res if side is None else (res[0], res[1:])


ROW_TILE = 256


def _rmsnorm_fwd(x, g, name):
    L, D = x.shape

    def body(x_ref, g_ref, u_ref):
        xv = x_ref[...]
        r = lax.rsqrt(jnp.mean(xv * xv, axis=1, keepdims=True) + EPS)
        u_ref[...] = (xv * r * g_ref[...]).astype(BF16)

    return pl.pallas_call(
        body, name=name, grid=(L // ROW_TILE,),
        in_specs=[pl.BlockSpec((ROW_TILE, D), lambda i: (i, 0)), pl.BlockSpec((1, D), lambda i: (0, 0))],
        out_specs=pl.BlockSpec((ROW_TILE, D), lambda i: (i, 0)),
        out_shape=jax.ShapeDtypeStruct((L, D), BF16),
        compiler_params=pltpu.CompilerParams(dimension_semantics=("parallel",), vmem_limit_bytes=VMEM_LIMIT),
    )(x, g)


def _rmsnorm_bwd(du, x, g, dres, name):
    L, D = x.shape

    def body(du_ref, x_ref, g_ref, dres_ref, dx_ref, dxb_ref, dg_ref):
        i = pl.program_id(0)
        xv, duv = x_ref[...], du_ref[...]
        r = lax.rsqrt(jnp.mean(xv * xv, axis=1, keepdims=True) + EPS)
        t = duv * g_ref[...]
        dx = dres_ref[...] + r * t - xv * (r * r * r) * jnp.mean(t * xv, axis=1, keepdims=True)
        dx_ref[...] = dx
        dxb_ref[...] = dx.astype(BF16)
        dg = jnp.sum(duv * xv * r, axis=0, keepdims=True)

        @pl.when(i == 0)
        def _():
            dg_ref[...] = dg

        @pl.when(i != 0)
        def _():
            dg_ref[...] += dg

    row = pl.BlockSpec((ROW_TILE, D), lambda i: (i, 0))
    vec = pl.BlockSpec((1, D), lambda i: (0, 0))
    return pl.pallas_call(
        body, name=name, grid=(L // ROW_TILE,),
        in_specs=[row, row, vec, row], out_specs=[row, row, vec],
        out_shape=[jax.ShapeDtypeStruct((L, D), F32), jax.ShapeDtypeStruct((L, D), BF16), jax.ShapeDtypeStruct((1, D), F32)],
        compiler_params=pltpu.CompilerParams(dimension_semantics=("arbitrary",), vmem_limit_bytes=VMEM_LIMIT),
    )(du, x, g, dres)


def _loss_head(h, target, g):
    L, D = h.shape

    def body(h_ref, t_ref, g_ref, dh_ref, dhb_ref, loss_ref, dg_ref):
        i = pl.program_id(0)
        hv = h_ref[...]
        r = lax.rsqrt(jnp.mean(hv * hv, axis=1, keepdims=True) + EPS)
        diff = hv * r * g_ref[...] - t_ref[...]
        part = 0.5 * jnp.sum(jnp.mean(diff * diff, axis=1, keepdims=True), axis=0, keepdims=True)
        dy = diff * (1.0 / D)
        t = dy * g_ref[...]
        dh = r * t - hv * (r * r * r) * jnp.mean(t * hv, axis=1, keepdims=True)
        dh_ref[...] = dh
        dhb_ref[...] = dh.astype(BF16)
        dg = jnp.sum(dy * hv * r, axis=0, keepdims=True)

        @pl.when(i == 0)
        def _():
            dg_ref[...] = dg
            loss_ref[...] = jnp.broadcast_to(part, (1, 128))

        @pl.when(i != 0)
        def _():
            dg_ref[...] += dg
            loss_ref[...] += jnp.broadcast_to(part, (1, 128))

    row = pl.BlockSpec((ROW_TILE, D), lambda i: (i, 0))
    vec = pl.BlockSpec((1, D), lambda i: (0, 0))
    return pl.pallas_call(
        body, name="loss_head", grid=(L // ROW_TILE,),
        in_specs=[row, row, vec], out_specs=[row, row, pl.BlockSpec((1, 128), lambda i: (0, 0)), vec],
        out_shape=[jax.ShapeDtypeStruct((L, D), F32), jax.ShapeDtypeStruct((L, D), BF16), jax.ShapeDtypeStruct((1, 128), F32),
                   jax.ShapeDtypeStruct((1, D), F32)],
        compiler_params=pltpu.CompilerParams(dimension_semantics=("arbitrary",), vmem_limit_bytes=VMEM_LIMIT),
    )(h, target, g)


FF_TILE = 1408


FF_COLS = 512


def _ffn_in(u, wg, wu, tm, side):
    L, D = u.shape
    F = wg.shape[1]
    tm = min(tm, L)
    s_in, s_out = len(side.operands), len(side.out_shapes)
    grid = (L // tm, F // FF_COLS)

    def body(*refs):
        u_ref, wg_ref, wu_ref = refs[:3]
        g_ref, up_ref, act_ref = refs[3 + s_in:6 + s_in]
        side_refs = (refs[3:3 + s_in], refs[6 + s_in:6 + s_in + s_out], refs[6 + s_in + s_out:])
        pl.when(_grid_edge(grid, first=True))(lambda: side.start(*side_refs))
        g = jnp.dot(u_ref[...], wg_ref[...], preferred_element_type=F32)
        v = jnp.dot(u_ref[...], wu_ref[...], preferred_element_type=F32)
        g_ref[...] = g.astype(BF16)
        up_ref[...] = v.astype(BF16)
        act_ref[...] = (_silu(g) * v).astype(BF16)
        pl.when(_grid_edge(grid, first=False))(lambda: side.finish(*side_refs))

    w_spec = pl.BlockSpec((D, FF_COLS), lambda i, j: (0, j))
    o_spec = pl.BlockSpec((tm, FF_COLS), lambda i, j: (i, j))
    out = jax.ShapeDtypeStruct((L, F), BF16)
    res = pl.pallas_call(
        body, name="ffn_gate_up", grid=grid,
        in_specs=[pl.BlockSpec((tm, D), lambda i, j: (i, 0)), w_spec, w_spec] + [ANY] * s_in,
        out_specs=[o_spec] * 3 + [ANY] * s_out, out_shape=[out] * 3 + list(side.out_shapes),
        scratch_shapes=list(side.sems),
        compiler_params=pltpu.CompilerParams(dimension_semantics=("arbitrary", "arbitrary"), vmem_limit_bytes=VMEM_LIMIT),
    )(u, wg, wu, *side.operands)
    return res[0], res[1], res[2], res[3:]


def _ffn_back(dh, w_down, gate, up, tm):
    L, D = dh.shape
    F = w_down.shape[0]
    tm = min(tm, L)

    def body(dh_ref, w_ref, g_ref, u_ref, dg_ref, du_ref):
        d = lax.dot_general(dh_ref[...], w_ref[...], (((1,), (1,)), ((), ())), preferred_element_type=F32)
        gv = g_ref[...].astype(F32)
        s = _sigmoid(gv)
        dg_ref[...] = (d * u_ref[...].astype(F32) * (s * (1.0 + gv * (1.0 - s)))).astype(BF16)
        du_ref[...] = (d * gv * s).astype(BF16)

    blk = pl.BlockSpec((tm, FF_COLS), lambda i, j: (i, j))
    out = jax.ShapeDtypeStruct((L, F), BF16)
    return pl.pallas_call(
        body, name="ffn_back", grid=(L // tm, F // FF_COLS),
        in_specs=[pl.BlockSpec((tm, D), lambda i, j: (i, 0)), pl.BlockSpec((FF_COLS, D), lambda i, j: (j, 0)), blk, blk],
        out_specs=[blk, blk], out_shape=[out, out],
        compiler_params=pltpu.CompilerParams(dimension_semantics=("parallel", "parallel"), vmem_limit_bytes=VMEM_LIMIT),
    )(dh, w_down, gate, up)


def _adamw(w, g, m, v, rows, name):
    R, C = w.shape
    assert R % rows == 0, (name, R, rows)

    def body(w_ref, g_ref, m_ref, v_ref, d_ref, nm_ref, nv_ref):
        gv = g_ref[...]
        mn = ADAM_B1 * m_ref[...] + (1.0 - ADAM_B1) * gv
        vn = ADAM_B2 * v_ref[...] + (1.0 - ADAM_B2) * (gv * gv)
        m_hat = mn / (1.0 - ADAM_B1 ** ADAM_STEP)
        v_hat = vn / (1.0 - ADAM_B2 ** ADAM_STEP)
        d_ref[...] = -ADAM_LR * (m_hat / (jnp.sqrt(v_hat) + ADAM_EPS) + ADAM_WD * w_ref[...])
        nm_ref[...] = mn
        nv_ref[...] = vn

    blk = pl.BlockSpec((rows, C), lambda i: (i, 0))
    out = jax.ShapeDtypeStruct((R, C), F32)
    return pl.pallas_call(
        body, name=name, grid=(R // rows,), in_specs=[blk] * 4, out_specs=[blk] * 3, out_shape=[out] * 3,
        compiler_params=pltpu.CompilerParams(dimension_semantics=("parallel",), vmem_limit_bytes=VMEM_LIMIT),
    )(w, g, m, v)


ANY = pl.BlockSpec(memory_space=pl.ANY)
N_DEV = 8


def _allgather_small(p, name):
    R, C = p.shape

    def body(p_ref, out_ref, send_sems, recv_sems, local_sem):
        x, y, c = lax.axis_index("x"), lax.axis_index("y"), lax.axis_index("c")
        me = 4 * x + 2 * y + c
        mine = pltpu.make_async_copy(p_ref, out_ref.at[me], local_sem)
        mine.start()

        def peer(d):
            return (x ^ ((d >> 2) & 1), y ^ ((d >> 1) & 1), c ^ (d & 1))

        def copy(d, block):
            return pltpu.make_async_remote_copy(src_ref=p_ref, dst_ref=out_ref.at[block], send_sem=send_sems.at[d - 1],
                                                recv_sem=recv_sems.at[d - 1], device_id=peer(d), device_id_type=MESH)

        sends = [copy(d, me) for d in range(1, N_DEV)]
        for cp in sends:
            cp.start()
        for d in range(1, N_DEV):
            px, py, pc = peer(d)
            copy(d, 4 * px + 2 * py + pc).wait_recv()
        for cp in sends:
            cp.wait_send()
        mine.wait()

    return pl.pallas_call(
        body, name=name, out_shape=jax.ShapeDtypeStruct((N_DEV, R, C), p.dtype),
        in_specs=[pl.BlockSpec(memory_space=pltpu.VMEM)], out_specs=pl.BlockSpec(memory_space=pltpu.VMEM),
        scratch_shapes=[pltpu.SemaphoreType.DMA((N_DEV - 1,)), pltpu.SemaphoreType.DMA((N_DEV - 1,)), pltpu.SemaphoreType.DMA],
    )(p)


def _other_chips(x, y):
    return [(1 - x, y), (x, 1 - y), (1 - x, 1 - y)]


def _run_side(side, name):
    s_in, s_out = len(side.operands), len(side.out_shapes)

    def body(*refs):
        parts = (refs[:s_in], refs[s_in:s_in + s_out], refs[s_in + s_out:])
        side.start(*parts)
        side.finish(*parts)

    return pl.pallas_call(body, name=name, out_shape=list(side.out_shapes), in_specs=[ANY] * s_in, out_specs=[ANY] * s_out,
                          scratch_shapes=list(side.sems))(*side.operands)


def _gather_side(shards):
    T = len(shards)
    halves = [s.shape[0] // 2 for s in shards]

    def tools(ins, outs, sems):
        send_sems, recv_sems = sems
        x, y, c = lax.axis_index("x"), lax.axis_index("y"), lax.axis_index("c")

        def rows(t, px, py, pc):
            return outs[t].at[2 * px + py, pl.ds(pc * halves[t], halves[t]), :]

        def copy(t, k, block, to, own=False):
            src = ins[t].at[pl.ds(c * halves[t], halves[t]), :] if own else rows(t, *block)
            return pltpu.make_async_remote_copy(src_ref=src, dst_ref=rows(t, *block), send_sem=send_sems.at[t, k],
                                                recv_sem=recv_sems.at[t, k], device_id=to, device_id_type=MESH)

        return (x, y, c), _other_chips(x, y), copy

    def start(ins, outs, sems):
        (x, y, c), chips, copy = tools(ins, outs, sems)
        for t in range(T):
            for j, chip in enumerate(chips):
                copy(t, j, (x, y, c), (*chip, c), own=True).start()

    def finish(ins, outs, sems):
        (x, y, c), chips, copy = tools(ins, outs, sems)
        for t in range(T):
            for j, chip in enumerate(chips):
                copy(t, j, (*chip, c), (x, y, c)).wait_recv()
                copy(t, 3 + j, (*chip, c), (x, y, 1 - c)).start()
        for t in range(T):
            for j, chip in enumerate(chips):
                copy(t, 3 + j, (*chip, 1 - c), (x, y, c)).wait_recv()
        for t in range(T):
            for j, chip in enumerate(chips):
                copy(t, j, (x, y, c), (*chip, c), own=True).wait_send()
                copy(t, 3 + j, (*chip, c), (x, y, 1 - c)).wait_send()

    return _Side(shards, [jax.ShapeDtypeStruct((4,) + s.shape, s.dtype) for s in shards],
                 [pltpu.SemaphoreType.DMA((T, 6)), pltpu.SemaphoreType.DMA((T, 6))], start, finish)


def _with_own_shard(gathered, shard, chip):
    return lax.dynamic_update_slice(gathered, shard[None], (chip,) + (0,) * shard.ndim)


def _pair_exchange(grads, name):
    T = len(grads)

    def body(*refs):
        ins, outs = refs[:T], refs[T:2 * T]
        send_sems, recv_sems = refs[2 * T:]
        x, y, c = lax.axis_index("x"), lax.axis_index("y"), lax.axis_index("c")
        cps = [pltpu.make_async_remote_copy(src_ref=ins[t].at[1 - c], dst_ref=outs[t], send_sem=send_sems.at[t],
                                            recv_sem=recv_sems.at[t], device_id=(x, y, 1 - c), device_id_type=MESH)
               for t in range(T)]
        for cp in cps:
            cp.start()
        for cp in cps:
            cp.wait()

    return pl.pallas_call(
        body, name=name,
        out_shape=[jax.ShapeDtypeStruct(g.shape[1:], g.dtype) for g in grads],
        in_specs=[ANY] * T, out_specs=[ANY] * T,
        scratch_shapes=[pltpu.SemaphoreType.DMA((T,)), pltpu.SemaphoreType.DMA((T,))],
    )(*grads)


def _scatter_side(parts):
    T = len(parts)

    def tools(ins, outs, sems):
        send_sems, recv_sems = sems
        x, y, c = lax.axis_index("x"), lax.axis_index("y"), lax.axis_index("c")

        def copy(t, j, src_slot, dst_slot, chip):
            return pltpu.make_async_remote_copy(src_ref=ins[t].at[src_slot], dst_ref=outs[t].at[dst_slot],
                                                send_sem=send_sems.at[t, j], recv_sem=recv_sems.at[t, j],
                                                device_id=(*chip, c), device_id_type=MESH)

        return 2 * x + y, _other_chips(x, y), copy

    def start(ins, outs, sems):
        my_chip, chips, copy = tools(ins, outs, sems)
        for t in range(T):
            for j, (px, py) in enumerate(chips):
                copy(t, j, 2 * px + py, my_chip, (px, py)).start()

    def finish(ins, outs, sems):
        my_chip, chips, copy = tools(ins, outs, sems)
        for t in range(T):
            for j, (px, py) in enumerate(chips):
                copy(t, j, my_chip, 2 * px + py, (px, py)).wait_recv()
        for t in range(T):
            for j, (px, py) in enumerate(chips):
                copy(t, j, 2 * px + py, my_chip, (px, py)).wait_send()

    return _Side(parts, [jax.ShapeDtypeStruct(p.shape, p.dtype) for p in parts],
                 [pltpu.SemaphoreType.DMA((T, 3)), pltpu.SemaphoreType.DMA((T, 3))], start, finish)


def _pair_share(fulls):
    T = len(fulls)

    def body(*refs):
        outs = refs[T:2 * T]
        send_sems, recv_sems = refs[2 * T:]
        x, y, c = lax.axis_index("x"), lax.axis_index("y"), lax.axis_index("c")

        def copy(t, half):
            return pltpu.make_async_remote_copy(src_ref=outs[t].at[half], dst_ref=outs[t].at[half], send_sem=send_sems.at[t],
                                                recv_sem=recv_sems.at[t], device_id=(x, y, 1 - c), device_id_type=MESH)

        for t in range(T):
            copy(t, c).start()
        for t in range(T):
            copy(t, 1 - c).wait_recv()
        for t in range(T):
            copy(t, c).wait_send()

    return pl.pallas_call(
        body, name="grad_pair_share", out_shape=[jax.ShapeDtypeStruct(f.shape, f.dtype) for f in fulls],
        in_specs=[ANY] * T, out_specs=[ANY] * T, input_output_aliases={t: t for t in range(T)},
        scratch_shapes=[pltpu.SemaphoreType.DMA((T,)), pltpu.SemaphoreType.DMA((T,))],
    )(*fulls)


def _row_tile(hr):
    return next(r for r in (128, 64, 32, 16, 8) if hr % r == 0)


def _pair_add(g, got, c, name):
    _, _, hr, C = g.shape
    tr = _row_tile(hr)

    def body(c_ref, g_ref, r_ref, o_ref):
        o_ref[...] = (g_ref[0].astype(F32) + r_ref[...].astype(F32)).astype(o_ref.dtype)

    return pl.pallas_call(
        body, name=name,
        grid_spec=pltpu.PrefetchScalarGridSpec(
            num_scalar_prefetch=1, grid=(4, hr // tr),
            in_specs=[pl.BlockSpec((1, 1, tr, C), lambda k, i, c_ref: (c_ref[0], k, i, 0)),
                      pl.BlockSpec((1, tr, C), lambda k, i, c_ref: (k, i, 0))],
            out_specs=pl.BlockSpec((1, tr, C), lambda k, i, c_ref: (k, i, 0))),
        out_shape=jax.ShapeDtypeStruct(got.shape, g.dtype),
        compiler_params=pltpu.CompilerParams(dimension_semantics=("parallel", "parallel"), vmem_limit_bytes=VMEM_LIMIT),
    )(jnp.reshape(c, (1,)).astype(jnp.int32), g, got)


def _chip_sum(parts, slots, chip, c, name):
    _, hr, C = parts.shape
    tr = _row_tile(hr)

    def body(chip_ref, half_ref, own_ref, a_ref, b_ref, c_ref, o_ref):
        acc = own_ref[...].astype(F32)
        for r in (a_ref, b_ref, c_ref):
            acc = acc + r[...].astype(F32)
        o_ref[...] = acc

    other = lambda j: pl.BlockSpec((1, tr, C), lambda i, chip_ref, half_ref: ((chip_ref[0] + j) % 4, i, 0))
    return pl.pallas_call(
        body, name=name,
        grid_spec=pltpu.PrefetchScalarGridSpec(
            num_scalar_prefetch=2, grid=(hr // tr,),
            in_specs=[pl.BlockSpec((1, tr, C), lambda i, chip_ref, half_ref: (chip_ref[0], i, 0)),
                      other(1), other(2), other(3)],
            out_specs=pl.BlockSpec((1, tr, C), lambda i, chip_ref, half_ref: (half_ref[0], i, 0))),
        out_shape=jax.ShapeDtypeStruct((2, hr, C), F32),
        compiler_params=pltpu.CompilerParams(dimension_semantics=("parallel",), vmem_limit_bytes=VMEM_LIMIT),
    )(jnp.reshape(chip, (1,)).astype(jnp.int32), jnp.reshape(c, (1,)).astype(jnp.int32), parts, slots, slots, slots)


def _sum_slots(parts, name):
    n, hr, C = parts.shape
    tr = _row_tile(hr)

    def body(p_ref, o_ref):
        acc = p_ref[0].astype(F32)
        for k in range(1, n):
            acc = acc + p_ref[k].astype(F32)
        o_ref[...] = acc

    return pl.pallas_call(
        body, name=name, grid=(hr // tr,),
        in_specs=[pl.BlockSpec((n, tr, C), lambda i: (0, i, 0))], out_specs=pl.BlockSpec((tr, C), lambda i: (i, 0)),
        out_shape=jax.ShapeDtypeStruct((hr, C), F32),
        compiler_params=pltpu.CompilerParams(dimension_semantics=("parallel",), vmem_limit_bytes=VMEM_LIMIT),
    )(parts)


PACK_ROWS, PACK_COLS = 16, 3072
_PACK = [("norm_mix_g", 0, 1, 2048), ("conv_w", 1, 4, 3072), ("conv_b", 5, 1, 3072), ("dt_bias", 6, 1, 32),
         ("a_log", 7, 1, 32), ("d_skip", 8, 1, 32), ("ssd_norm_g", 9, 1, 2048), ("i_bias", 10, 1, 8), ("f_bias", 11, 1, 8),
         ("mlstm_norm_g", 12, 1, 2048), ("norm_ffn_g", 13, 1, 2048), ("final_norm_g", 14, 1, 2048), ("loss", 15, 1, 1)]
_WEIGHTS = ["norm_mix_g", "w_in", "conv_w", "conv_b", "dt_bias", "a_log", "d_skip", "ssd_norm_g", "i_bias", "f_bias",
            "mlstm_norm_g", "w_out", "norm_ffn_g", "w_gate", "w_up", "w_down", "final_norm_g"]
_BIG = ["w_in", "w_out", "w_gate", "w_up", "w_down"]


def _pack(vals):
    rows = []
    for name, _, r, w in _PACK:
        v = vals.get(name)
        v = jnp.zeros((r, w), F32) if v is None else v.reshape(r, w).astype(F32)
        rows.append(jnp.concatenate([v, jnp.zeros((r, PACK_COLS - w), F32)], axis=1) if w < PACK_COLS else v)
    return jnp.concatenate(rows, axis=0)


def _unpack(p):
    return {name: p[r0:r0 + r, :w] for name, r0, r, w in _PACK}


def _to_halves_cols(dw, cols):
    R = dw.shape[0]
    return dw.reshape(2, R // 2, 4, cols).transpose(0, 2, 1, 3)


def kernel(x, norm_mix_g, w_in, conv_w, conv_b, dt_bias, a_log, d_skip, ssd_norm_g, i_bias, f_bias, mlstm_norm_g, w_out, norm_ffn_g, w_gate, w_up, w_down, final_norm_g, loss_target, m_norm_mix_g, m_w_in, m_conv_w, m_conv_b, m_dt_bias, m_a_log, m_d_skip, m_ssd_norm_g, m_i_bias, m_f_bias, m_mlstm_norm_g, m_w_out, m_norm_ffn_g, m_w_gate, m_w_up, m_w_down, m_final_norm_g, v_norm_mix_g, v_w_in, v_conv_w, v_conv_b, v_dt_bias, v_a_log, v_d_skip, v_ssd_norm_g, v_i_bias, v_f_bias, v_mlstm_norm_g, v_w_out, v_norm_ffn_g, v_w_gate, v_w_up, v_w_down, v_final_norm_g):
    weights = dict(norm_mix_g=norm_mix_g, w_in=w_in, conv_w=conv_w, conv_b=conv_b, dt_bias=dt_bias, a_log=a_log, d_skip=d_skip,
                   ssd_norm_g=ssd_norm_g, i_bias=i_bias, f_bias=f_bias, mlstm_norm_g=mlstm_norm_g, w_out=w_out,
                   norm_ffn_g=norm_ffn_g, w_gate=w_gate, w_up=w_up, w_down=w_down, final_norm_g=final_norm_g)
    mom1 = dict(norm_mix_g=m_norm_mix_g, w_in=m_w_in, conv_w=m_conv_w, conv_b=m_conv_b, dt_bias=m_dt_bias, a_log=m_a_log,
                d_skip=m_d_skip, ssd_norm_g=m_ssd_norm_g, i_bias=m_i_bias, f_bias=m_f_bias, mlstm_norm_g=m_mlstm_norm_g,
                w_out=m_w_out, norm_ffn_g=m_norm_ffn_g, w_gate=m_w_gate, w_up=m_w_up, w_down=m_w_down,
                final_norm_g=m_final_norm_g)
    mom2 = dict(norm_mix_g=v_norm_mix_g, w_in=v_w_in, conv_w=v_conv_w, conv_b=v_conv_b, dt_bias=v_dt_bias, a_log=v_a_log,
                d_skip=v_d_skip, ssd_norm_g=v_ssd_norm_g, i_bias=v_i_bias, f_bias=v_f_bias, mlstm_norm_g=v_mlstm_norm_g,
                w_out=v_w_out, norm_ffn_g=v_norm_ffn_g, w_gate=v_w_gate, w_up=v_w_up, w_down=v_w_down,
                final_norm_g=v_final_norm_g)
    xi, yi, ci = lax.axis_index("x"), lax.axis_index("y"), lax.axis_index("c")
    chip = 2 * xi + yi
    xs, tgt = x[0], loss_target[0]

    shards = {n: weights[n][0].astype(BF16) for n in _BIG}
    gathered = lambda n, g: _with_own_shard(g, shards[n], chip)
    by_cols = lambda g, width: jnp.transpose(g, (1, 0, 2)).reshape(D_MODEL, width)
    (g_in,) = _run_side(_gather_side([shards["w_in"]]), "allgather_w_in")
    W_ssd, W_ml, W_small = _split_w_in(by_cols(gathered("w_in", g_in), IN_WIDTH))
    cw_all = _allgather_small(jnp.concatenate([conv_w[0], jnp.zeros((4, 768), F32)], axis=0), "allgather_conv_w")
    conv_w_full = jnp.concatenate([cw_all[2 * k, :SSD_CONV] for k in range(4)], axis=1)
    cwg, cbg = _conv_to_groups(conv_w_full), _conv_to_groups(conv_b)
    dtb_row, al_row = _small_row(dt_bias, 0), _small_row(a_log, 0)
    ib_row, fb_row = _small_row(i_bias, LANE_I), _small_row(f_bias, LANE_F)
    dskip_lane = jnp.repeat(d_skip, 64, axis=1)
    fng = final_norm_g[None]

    u1 = _rmsnorm_fwd(xs, norm_mix_g, "norm_mix_fwd")
    p_ssd, (g_out,) = _matmul(u1, W_ssd, "nn", F32, 1024, 1024, 2048, "proj_ssd", side=_gather_side([shards["w_out"]]))
    p_ml, (g_gate,) = _matmul(u1, W_ml, "nn", F32, 1024, 1024, 2048, "proj_ml", side=_gather_side([shards["w_gate"]]))
    p_small = _matmul(u1, W_small, "nn", F32, 1024, 128, 2048, "proj_small")
    y_ssd, ssd_st = _ssd_forward(p_ssd, p_small, cwg, cbg, dtb_row, al_row, dskip_lane, ssd_norm_g)
    y_ml, ml_c, ml_nm = _mlstm_forward(p_ml, p_small, ib_row, fb_row, mlstm_norm_g)
    y_mix = jnp.concatenate([y_ssd, y_ml], axis=1)
    W_out = gathered("w_out", g_out).reshape(2 * SSD_WIDTH, D_MODEL)
    h1, (g_up,) = _matmul(y_mix, W_out, "nn", F32, 1024, 1024, 2048, "out_proj", addend=xs,
                          side=_gather_side([shards["w_up"]]))
    u2 = _rmsnorm_fwd(h1, norm_ffn_g, "norm_ffn_fwd")
    W_gate = by_cols(gathered("w_gate", g_gate), D_FF)
    W_up = by_cols(gathered("w_up", g_up), D_FF)
    gate, up, act, (g_down,) = _ffn_in(u2, W_gate, W_up, 1024, _gather_side([shards["w_down"]]))
    W_down = gathered("w_down", g_down).reshape(D_FF, D_MODEL)
    h2 = _matmul(act, W_down, "nn", F32, 1024, 1024, 2 * FF_TILE, "ffn_down", addend=h1)
    dh2, dh2_b, loss_row, d_fng = _loss_head(h2, tgt, fng)

    dW_down = _matmul(act, dh2_b, "tn", BF16, FF_TILE, 1024, 2048, "dw_down", layout="rows4")
    d_gate, d_up = _ffn_back(dh2_b, W_down, gate, up, 1024)
    du2 = _matmul(d_gate, W_gate, "nt", F32, 1024, 1024, 2 * FF_TILE, "du2_gate")
    du2 = _matmul(d_up, W_up, "nt", F32, 1024, 1024, 2 * FF_TILE, "du2_up", addend=du2)
    dW_gate = _matmul(u2, d_gate, "tn", BF16, 1024, FF_TILE, 2048, "dw_gate", layout="cols4")
    dW_up = _matmul(u2, d_up, "tn", BF16, 1024, FF_TILE, 2048, "dw_up", layout="cols4")
    dh1, dh1_b, d_ffn_g = _rmsnorm_bwd(du2, h1, norm_ffn_g, dh2, "norm_ffn_bwd")
    dy_mix = _matmul(dh1_b, W_out, "nt", F32, 1024, 1024, 2048, "d_mix")
    dW_out = _matmul(y_mix, dh1_b, "tn", BF16, 1024, 1024, 2048, "dw_out", layout="rows4")
    early = ["w_out", "w_gate", "w_up", "w_down"]
    early_g = [dW_out, dW_gate, dW_up, dW_down]
    early_got = _pair_exchange(early_g, "grad_pair_exchange_early")
    parts = {n: _pair_add(g, r, ci, "grad_pair_add_" + n) for g, r, n in zip(early_g, early_got, early)}
    (d_ssd, d_small, d_cw, d_cb, d_dtb, d_alog, d_dskip, d_sng), early_slots = _ssd_backward(
        p_ssd, p_small, cwg, cbg, dtb_row, al_row, dskip_lane, ssd_norm_g, ssd_st, dy_mix,
        side=_scatter_side([parts[n] for n in early]))
    slots = dict(zip(early, early_slots))
    d_ml, d_small, d_ib, d_fb, d_mng = _mlstm_backward(p_ml, p_small, ib_row, fb_row, mlstm_norm_g, ml_c, ml_nm, dy_mix, d_small)
    dW_ssd = _matmul(u1, d_ssd, "tn", BF16, 1024, 1280, 2048, "dw_ssd")
    dW_ml = _matmul(u1, d_ml, "tn", BF16, 1024, 1024, 2048, "dw_ml")
    dW_small = _matmul(u1, d_small, "tn", BF16, 1024, 128, 2048, "dw_small")
    g_w_in = _to_halves_cols(_merge_w_in(dW_ssd, dW_ml, dW_small), 2828)
    (got_in,) = _pair_exchange([g_w_in], "grad_pair_exchange_w_in")
    parts["w_in"] = _pair_add(g_w_in, got_in, ci, "grad_pair_add_w_in")
    du1, (slots["w_in"],) = _matmul(d_ssd, W_ssd, "nt", F32, 1024, 1024, 2560, "du1_ssd",
                                    side=_scatter_side([parts["w_in"]]))
    du1 = _matmul(d_ml, W_ml, "nt", F32, 1024, 1024, 3072, "du1_ml", addend=du1)
    du1 = _matmul(d_small, W_small, "nt", F32, 1024, 1024, 128, "du1_small", addend=du1)
    grad_x, _, d_mix_g = _rmsnorm_bwd(du1, xs, norm_mix_g, dh1, "norm_mix_bwd")

    fulls = _pair_share([_chip_sum(parts[n], slots[n], chip, ci, "grad_chip_sum_" + n) for n in _BIG])
    grads = {n: f.reshape(weights[n].shape[1:]) for n, f in zip(_BIG, fulls)}

    small = _pack(dict(norm_mix_g=d_mix_g, conv_w=_conv_from_groups(d_cw), conv_b=_conv_from_groups(d_cb),
                       dt_bias=d_dtb[:, 0:32], a_log=d_alog[:, 0:32], d_skip=d_dskip.reshape(SSD_HEADS, 64).sum(axis=1),
                       ssd_norm_g=d_sng, i_bias=d_ib[:, LANE_I:LANE_I + 8], f_bias=d_fb[:, LANE_F:LANE_F + 8],
                       mlstm_norm_g=d_mng, norm_ffn_g=d_ffn_g, final_norm_g=d_fng, loss=loss_row[:, 0:1]))
    total = _sum_slots(_allgather_small(small, "allgather_small_grads"), "small_grad_sum")
    small_w = {n: weights[n] for n in _WEIGHTS if n not in _BIG and n != "conv_w"}
    sd, sm, sv = _adamw(_pack(small_w), total, _pack({n: mom1[n] for n in small_w}), _pack({n: mom2[n] for n in small_w}),
                        PACK_ROWS, "adamw_small")
    tot, sd, sm, sv = _unpack(total), _unpack(sd), _unpack(sm), _unpack(sv)
    for n in small_w:
        grads[n] = tot[n].reshape(weights[n].shape)
    grads["conv_w"] = lax.dynamic_slice_in_dim(tot["conv_w"], chip * 768, 768, axis=1)
    loss = tot["loss"][0, 0]

    delta, new_m, new_v = {}, {}, {}
    for n in _BIG + ["conv_w"]:
        w2 = weights[n][0]
        d, nm, nv = _adamw(w2, grads[n], mom1[n][0], mom2[n][0], _row_tile(w2.shape[0]) if n != "conv_w" else SSD_CONV,
                           "adamw_" + n)
        delta[n], new_m[n], new_v[n] = d[None], nm[None], nv[None]
        grads[n] = grads[n][None]
    for n in small_w:
        delta[n], new_m[n], new_v[n] = (t[n].reshape(weights[n].shape) for t in (sd, sm, sv))
    return (loss, grad_x[None], *[grads[n] for n in _WEIGHTS], *[delta[n] for n in _WEIGHTS],
            *[new_m[n] for n in _WEIGHTS], *[new_v[n] for n in _WEIGHTS])
```

```python
import functools

import jax
import jax.numpy as jnp
import numpy as np
from jax import lax
from jax.experimental import pallas as pl
from jax.experimental.pallas import tpu as pltpu

F32 = jnp.float32
BF16 = jnp.bfloat16

D_MODEL = 2048
SSD_HEADS = 32
SSD_GROUPS = 4
SSD_STATE = 128
SSD_WIDTH = 2048
SSD_CONV = 4
SSD_GROUP_COLS = 1280
SSD_XBC_COLS = 768
ML_HEADS = 8
ML_HEAD_COLS = 768
ML_DK = 128
ML_DV = 256
CHUNK = 128
SMALL_COLS = 128
LANE_I = 32
LANE_F = 40
D_FF = 5632
GATE_SOFTCAP = 15.0
EPS = 1e-6
ADAM_LR, ADAM_B1, ADAM_B2, ADAM_EPS, ADAM_WD, ADAM_STEP = 0.001, 0.9, 0.999, 1e-8, 0.01, 10
MESH = pl.DeviceIdType.MESH
VMEM_LIMIT = 56 * 1024 * 1024


def _dg(a, b, ca, cb):
    return lax.dot_general(a.astype(BF16), b.astype(BF16), (((ca,), (cb,)), ((), ())), preferred_element_type=F32)


@jax.custom_vjp
def _mm(a, b):
    return _dg(a, b, 1, 0)


_mm.defvjp(lambda a, b: (_dg(a, b, 1, 0), (a, b)),
           lambda r, g: (_dg(g, r[1], 1, 1), _dg(r[0], g, 0, 0)))


@jax.custom_vjp
def _mm_nt(a, b):
    return _dg(a, b, 1, 1)


_mm_nt.defvjp(lambda a, b: (_dg(a, b, 1, 1), (a, b)),
              lambda r, g: (_dg(g, r[1], 1, 0), _dg(g, r[0], 0, 0)))


@jax.custom_vjp
def _mm_tn(a, b):
    return _dg(a, b, 0, 0)


_mm_tn.defvjp(lambda a, b: (_dg(a, b, 0, 0), (a, b)),
              lambda r, g: (_dg(r[1], g, 1, 1), _dg(r[0], g, 1, 0)))


def _tri(lower):
    r = lax.broadcasted_iota(jnp.int32, (CHUNK, CHUNK), 0)
    c = lax.broadcasted_iota(jnp.int32, (CHUNK, CHUNK), 1)
    return ((r >= c) if lower else (r <= c)).astype(F32)


def _dg32(t, x):
    return lax.dot_general(t, x, (((1,), (0,)), ((), ())), precision=lax.Precision.HIGHEST, preferred_element_type=F32)


@jax.custom_vjp
def _cumsum(x):
    return _dg32(_tri(True), x)


_cumsum.defvjp(lambda x: (_dg32(_tri(True), x), None), lambda r, g: (_dg32(_tri(False), g),))


_sigmoid = jax.nn.sigmoid


def _silu(x):
    return x * _sigmoid(x)


def _softplus(x):
    return jnp.maximum(x, 0.0) + jnp.log(1.0 + jnp.exp(-jnp.abs(x)))


def _lane_col(m, lane, h):
    return jnp.sum(jnp.where(lane == h, m, 0.0), axis=1, keepdims=True)


SSD_GPS = 4


def _ssd_math(convs, zs, smallb, S_in, dtb_row, alog_row, dskip, ng, g0):
    dt_all = _softplus(smallb + dtb_row)
    a_all = dt_all * (-jnp.exp(alog_row))
    prep = (dt_all, _cumsum(a_all), jnp.sum(a_all, axis=0, keepdims=True))
    ys, S_out = [], []
    for k in range(SSD_GPS):
        y, S = _ssd_group(convs[k][:, 0:512], convs[k][:, 512:640], convs[k][:, 640:768], zs[k], prep, S_in[k],
                          dskip[:, 512 * k:512 * (k + 1)], ng[:, 512 * k:512 * (k + 1)], g0 + k)
        ys.append(y)
        S_out.append(S)
    return tuple(ys), tuple(S_out)


def _ssd_group(cx, cB, cC, z, prep, S_in, dskip, ng, g):
    dt_all, acum_all, alast_all = prep
    lane = lax.broadcasted_iota(jnp.int32, (1, CHUNK), 1)
    row_i = lax.broadcasted_iota(jnp.int32, (CHUNK, CHUNK), 0)
    col_i = lax.broadcasted_iota(jnp.int32, (CHUNK, CHUNK), 1)
    causal = row_i >= col_i
    half = lane < 64
    rhalf = lax.broadcasted_iota(jnp.int32, (CHUNK, 1), 0) < 64
    xs, Bm, Cm = _silu(cx), _silu(cB), _silu(cC)
    cb = _mm_nt(Cm, Bm)

    def lmat(ac):
        acb = jnp.broadcast_to(ac, (CHUNK, CHUNK))
        return jnp.exp(jnp.where(causal, acb - acb.T, -jnp.inf))

    ys, S_out = [], []
    for j in range(4):
        h0 = 8 * g + 2 * j
        ac0, ac1 = _lane_col(acum_all, lane, h0), _lane_col(acum_all, lane, h0 + 1)
        dt0, dt1 = _lane_col(dt_all, lane, h0), _lane_col(dt_all, lane, h0 + 1)
        al0, al1 = _lane_col(alast_all, lane, h0), _lane_col(alast_all, lane, h0 + 1)
        Xp = xs[:, 128 * j:128 * (j + 1)]
        Xd = Xp * jnp.where(half, dt0, dt1)
        ac_sel = jnp.where(half, ac0, ac1)
        al_sel = jnp.where(half, al0, al1)
        Yd = jnp.where(half, _mm(cb * lmat(ac0), Xd), _mm(cb * lmat(ac1), Xd))
        Yoff = _mm_nt(Cm, S_in[j]) * jnp.exp(ac_sel)
        ys.append(Yd + Yoff + dskip[:, 128 * j:128 * (j + 1)] * Xp)
        S_new = _mm_tn(Xd * jnp.exp(al_sel - ac_sel), Bm)
        S_out.append(S_in[j] * jnp.exp(jnp.where(rhalf, al0, al1)) + S_new)
    y = jnp.concatenate(ys, axis=1)
    y = y * _silu(z)
    y = y * lax.rsqrt(jnp.mean(y * y, axis=1, keepdims=True) + EPS) * ng
    return y, tuple(S_out)


def _ssd_conv(blk_ref, halo_ref, cw_ref, cb_ref, ext, first, k):
    cols = slice(SSD_GROUP_COLS * k + 512, SSD_GROUP_COLS * (k + 1))
    halo = halo_ref[:, cols]
    ext[k, 0:8, :] = jnp.where(first, jnp.zeros_like(halo), halo)
    ext[k, 8:8 + CHUNK, :] = blk_ref[:, cols]
    conv = jnp.broadcast_to(cb_ref[k], (CHUNK, SSD_XBC_COLS))
    for tap in range(SSD_CONV):
        conv = conv + cw_ref[k, tap:tap + 1, :] * ext[k, pl.ds(5 + tap, CHUNK), :]
    return conv


def _ssd_specs(nc, rev):
    cc = (lambda c: nc - 1 - c) if rev else (lambda c: c)
    width = SSD_GPS * SSD_GROUP_COLS
    return [
        pl.BlockSpec((CHUNK, width), lambda c, g: (cc(c), g)),
        pl.BlockSpec((8, width), lambda c, g: (jnp.maximum(cc(c) * (CHUNK // 8) - 1, 0), g)),
        pl.BlockSpec((CHUNK, SMALL_COLS), lambda c, g: (cc(c), 0)),
        pl.BlockSpec((SSD_GPS, SSD_CONV, SSD_XBC_COLS), lambda c, g: (g, 0, 0)),
        pl.BlockSpec((SSD_GPS, 1, SSD_XBC_COLS), lambda c, g: (g, 0, 0)),
        pl.BlockSpec((1, SMALL_COLS), lambda c, g: (0, 0)),
        pl.BlockSpec((1, SMALL_COLS), lambda c, g: (0, 0)),
        pl.BlockSpec((1, SSD_GPS * 512), lambda c, g: (0, g)),
        pl.BlockSpec((1, SSD_GPS * 512), lambda c, g: (0, g)),
    ]


def _ssd_forward(proj, small, cw, cb, dtb, alog, dskip, ng):
    L = proj.shape[0]
    nc = L // CHUNK

    def body(blk_ref, halo_ref, small_ref, cw_ref, cb_ref, dtb_ref, alog_ref, dskip_ref, ng_ref, y_ref, st_ref, carry, ext):
        c, g0 = pl.program_id(0), pl.program_id(1) * SSD_GPS

        @pl.when(c == 0)
        def _():
            for k in range(SSD_GPS):
                carry[g0 + k] = jnp.zeros((4, CHUNK, CHUNK), F32)

        convs = [_ssd_conv(blk_ref, halo_ref, cw_ref, cb_ref, ext, c == 0, k) for k in range(SSD_GPS)]
        zs = [blk_ref[:, SSD_GROUP_COLS * k:SSD_GROUP_COLS * k + 512] for k in range(SSD_GPS)]
        S_in = tuple(tuple(carry[g0 + k, j] for j in range(4)) for k in range(SSD_GPS))
        for k in range(SSD_GPS):
            st_ref[0, k] = carry[g0 + k]
        ys, S_out = _ssd_math(convs, zs, small_ref[...], S_in, dtb_ref[...], alog_ref[...], dskip_ref[...], ng_ref[...], g0)
        for k in range(SSD_GPS):
            y_ref[:, 512 * k:512 * (k + 1)] = ys[k].astype(BF16)
            for j in range(4):
                carry[g0 + k, j] = S_out[k][j]

    return pl.pallas_call(
        body, name="ssd_fwd", grid=(nc, SSD_GROUPS // SSD_GPS),
        in_specs=_ssd_specs(nc, False),
        out_specs=[pl.BlockSpec((CHUNK, SSD_GPS * 512), lambda c, g: (c, g)),
                   pl.BlockSpec((1, SSD_GPS, 4, CHUNK, CHUNK), lambda c, g: (c, g, 0, 0, 0))],
        out_shape=[jax.ShapeDtypeStruct((L, SSD_WIDTH), BF16), jax.ShapeDtypeStruct((nc, SSD_GROUPS, 4, CHUNK, CHUNK), F32)],
        scratch_shapes=[pltpu.VMEM((SSD_GROUPS, 4, CHUNK, CHUNK), F32), pltpu.VMEM((SSD_GPS, 8 + CHUNK, SSD_XBC_COLS), F32)],
        compiler_params=pltpu.CompilerParams(dimension_semantics=("arbitrary", "arbitrary"), vmem_limit_bytes=VMEM_LIMIT),
    )(proj, proj, small, cw, cb, dtb, alog, dskip, ng)


def _ssd_backward(proj, small, cw, cb, dtb, alog, dskip, ng, states, dy, side=None):
    L = proj.shape[0]
    nc = L // CHUNK
    s_in, s_out = (len(side.operands), len(side.out_shapes)) if side is not None else (0, 0)
    grid = (nc, SSD_GROUPS // SSD_GPS)

    def body(*refs):
        blk_ref, halo_ref, small_ref, cw_ref, cb_ref, dtb_ref, alog_ref, dskip_ref, ng_ref, st_ref, dy_ref = refs[:11]
        dproj_ref, dsmall_ref, dcw_ref, dcb_ref, ddtb_ref, dalog_ref, ddskip_ref, dng_ref = refs[11 + s_in:19 + s_in]
        dcarry, nxt, ext, dext = refs[19 + s_in + s_out:23 + s_in + s_out]
        side_refs = (refs[11:11 + s_in], refs[19 + s_in:19 + s_in + s_out], refs[23 + s_in + s_out:])
        s, gb = pl.program_id(0), pl.program_id(1)
        g0 = gb * SSD_GPS
        c = nc - 1 - s
        if side is not None:
            pl.when(_grid_edge(grid, first=True))(lambda: side.start(*side_refs))

        @pl.when(s == 0)
        def _():
            for k in range(SSD_GPS):
                dcarry[g0 + k] = jnp.zeros((4, CHUNK, CHUNK), F32)
                nxt[g0 + k] = jnp.zeros((8, SSD_XBC_COLS), F32)
                dcw_ref[g0 + k] = jnp.zeros((SSD_CONV, SSD_XBC_COLS), F32)
                dcb_ref[g0 + k] = jnp.zeros((1, SSD_XBC_COLS), F32)
                ddskip_ref[g0 + k] = jnp.zeros((1, 512), F32)
                dng_ref[g0 + k] = jnp.zeros((1, 512), F32)

        @pl.when((s == 0) & (gb == 0))
        def _():
            ddtb_ref[...] = jnp.zeros((1, SMALL_COLS), F32)
            dalog_ref[...] = jnp.zeros((1, SMALL_COLS), F32)

        convs = [_ssd_conv(blk_ref, halo_ref, cw_ref, cb_ref, ext, c == 0, k) for k in range(SSD_GPS)]
        zs = [blk_ref[:, SSD_GROUP_COLS * k:SSD_GROUP_COLS * k + 512] for k in range(SSD_GPS)]
        S_in = tuple(tuple(st_ref[0, k, j] for j in range(4)) for k in range(SSD_GPS))
        dS_out = tuple(tuple(dcarry[g0 + k, j] for j in range(4)) for k in range(SSD_GPS))
        dys = tuple(dy_ref[:, 512 * k:512 * (k + 1)].astype(F32) for k in range(SSD_GPS))
        _, vjp = jax.vjp(functools.partial(_ssd_math, g0=g0), convs, zs, small_ref[...], S_in, dtb_ref[...], alog_ref[...],
                         dskip_ref[...], ng_ref[...])
        d_convs, d_zs, d_small, dS_in, d_dtb, d_alog, d_dskip, d_ng = vjp((dys, dS_out))
        for k in range(SSD_GPS):
            for j in range(4):
                dcarry[g0 + k, j] = dS_in[k][j]
            dext[k, 0:8, :] = jnp.zeros((8, SSD_XBC_COLS), F32)
            dext[k, 8:8 + CHUNK, :] = d_convs[k]
            dext[k, 8 + CHUNK:16 + CHUNK, :] = nxt[g0 + k]
            nxt[g0 + k] = dext[k, 8:16, :]
            dconv = d_convs[k]
            d_xbc = jnp.zeros((CHUNK, SSD_XBC_COLS), F32)
            for tap in range(SSD_CONV):
                d_xbc = d_xbc + cw_ref[k, tap:tap + 1, :] * dext[k, pl.ds(8 + 3 - tap, CHUNK), :]
                dcw_ref[g0 + k, tap:tap + 1, :] += jnp.sum(dconv * ext[k, pl.ds(5 + tap, CHUNK), :], axis=0, keepdims=True)
            dcb_ref[g0 + k] += jnp.sum(dconv, axis=0, keepdims=True)
            dproj_ref[:, SSD_GROUP_COLS * k:SSD_GROUP_COLS * k + 512] = d_zs[k].astype(BF16)
            dproj_ref[:, SSD_GROUP_COLS * k + 512:SSD_GROUP_COLS * (k + 1)] = d_xbc.astype(BF16)
            ddskip_ref[g0 + k] += d_dskip[:, 512 * k:512 * (k + 1)]
            dng_ref[g0 + k] += d_ng[:, 512 * k:512 * (k + 1)]

        @pl.when(gb == 0)
        def _():
            dsmall_ref[...] = d_small

        @pl.when(gb != 0)
        def _():
            dsmall_ref[...] += d_small

        ddtb_ref[...] += d_dtb
        dalog_ref[...] += d_alog
        if side is not None:
            pl.when(_grid_edge(grid, first=False))(lambda: side.finish(*side_refs))

    whole = lambda shape: pl.BlockSpec(shape, lambda s, g: (0,) * len(shape))
    side_ops = tuple(side.operands) if side is not None else ()
    res = pl.pallas_call(
        body, name="ssd_bwd", grid=grid,
        in_specs=_ssd_specs(nc, True) + [
            pl.BlockSpec((1, SSD_GPS, 4, CHUNK, CHUNK), lambda s, g: (nc - 1 - s, g, 0, 0, 0)),
            pl.BlockSpec((CHUNK, SSD_GPS * 512), lambda s, g: (nc - 1 - s, g))] + [ANY] * s_in,
        out_specs=[pl.BlockSpec((CHUNK, SSD_GPS * SSD_GROUP_COLS), lambda s, g: (nc - 1 - s, g)),
                   pl.BlockSpec((CHUNK, SMALL_COLS), lambda s, g: (nc - 1 - s, 0)),
                   whole((SSD_GROUPS, SSD_CONV, SSD_XBC_COLS)), whole((SSD_GROUPS, 1, SSD_XBC_COLS)),
                   whole((1, SMALL_COLS)), whole((1, SMALL_COLS)),
                   whole((SSD_GROUPS, 1, 512)), whole((SSD_GROUPS, 1, 512))] + [ANY] * s_out,
        out_shape=[jax.ShapeDtypeStruct((L, SSD_GROUPS * SSD_GROUP_COLS), BF16), jax.ShapeDtypeStruct((L, SMALL_COLS), F32),
                   jax.ShapeDtypeStruct((SSD_GROUPS, SSD_CONV, SSD_XBC_COLS), F32),
                   jax.ShapeDtypeStruct((SSD_GROUPS, 1, SSD_XBC_COLS), F32),
                   jax.ShapeDtypeStruct((1, SMALL_COLS), F32), jax.ShapeDtypeStruct((1, SMALL_COLS), F32),
                   jax.ShapeDtypeStruct((SSD_GROUPS, 1, 512), F32), jax.ShapeDtypeStruct((SSD_GROUPS, 1, 512), F32)]
        + (list(side.out_shapes) if side is not None else []),
        scratch_shapes=[pltpu.VMEM((SSD_GROUPS, 4, CHUNK, CHUNK), F32), pltpu.VMEM((SSD_GROUPS, 8, SSD_XBC_COLS), F32),
                        pltpu.VMEM((SSD_GPS, 8 + CHUNK, SSD_XBC_COLS), F32),
                        pltpu.VMEM((SSD_GPS, 16 + CHUNK, SSD_XBC_COLS), F32)]
        + (list(side.sems) if side is not None else []),
        compiler_params=pltpu.CompilerParams(dimension_semantics=("arbitrary", "arbitrary"), vmem_limit_bytes=VMEM_LIMIT),
    )(proj, proj, small, cw, cb, dtb, alog, dskip, ng, states, dy, *side_ops)
    return res if side is None else (res[:8], res[8:])


ML_HPS = 8


def _mlstm_math(blks, smallb, C_in, n_in, m_in, ib_row, fb_row, ng, h0):
    li_all = GATE_SOFTCAP * jnp.tanh((smallb + ib_row) / GATE_SOFTCAP)
    lf_all = -_softplus(-(GATE_SOFTCAP * jnp.tanh((smallb + fb_row) / GATE_SOFTCAP)))
    prep = (li_all, lf_all, _cumsum(lf_all))
    out = [_mlstm_head(blks[k][:, 0:128], blks[k][:, 128:256], blks[k][:, 256:512], blks[k][:, 512:768], prep,
                       C_in[k], n_in[k], m_in[k], ng[:, ML_DV * k:ML_DV * (k + 1)], h0 + k) for k in range(ML_HPS)]
    return tuple(zip(*out))


def _mlstm_head(q, k, v, o_raw, prep, C_in, n_in, m_in_row, ng, h):
    li_all, lf_all, bcum_all = prep
    lane = lax.broadcasted_iota(jnp.int32, (1, CHUNK), 1)
    row_i = lax.broadcasted_iota(jnp.int32, (CHUNK, CHUNK), 0)
    col_i = lax.broadcasted_iota(jnp.int32, (CHUNK, CHUNK), 1)
    causal = row_i >= col_i
    kk = k * (ML_DK ** -0.5)
    li = _lane_col(li_all, lane, LANE_I + h)
    lf = _lane_col(lf_all, lane, LANE_F + h)
    bc = _lane_col(bcum_all, lane, LANE_F + h)
    b_last = jnp.sum(lf, axis=0, keepdims=True)
    m_in = _lane_col(m_in_row, lane, 0)
    a = b_last - bc + li
    m_loc = jnp.max(a, axis=0, keepdims=True)
    w = jnp.exp(a - m_loc)
    C_loc = _mm_tn(w * v, kk)
    n_loc = jnp.sum(w * kk, axis=0, keepdims=True)
    m_new = jnp.maximum(b_last + m_in, m_loc)
    s_old = jnp.exp(b_last + m_in - m_new)
    s_new = jnp.exp(m_loc - m_new)
    C_out = s_old * C_in + s_new * C_loc
    n_out = s_old * n_in + s_new * n_loc
    bc_b = jnp.broadcast_to(bc, (CHUNK, CHUNK))
    li_b = jnp.broadcast_to(li, (CHUNK, CHUNK))
    D = jnp.where(causal, bc_b - bc_b.T + li_b.T, -jnp.inf)
    m_intra = jnp.max(D, axis=1, keepdims=True)
    inter_log = bc + m_in
    m_t = jnp.maximum(inter_log, m_intra)
    S = _mm_nt(q, kk) * jnp.exp(D - m_t)
    w_inter = jnp.exp(inter_log - m_t)
    num = _mm(S, v) + w_inter * _mm_nt(q, C_in)
    nq = jnp.sum(S, axis=1, keepdims=True) + w_inter * jnp.sum(q * n_in, axis=1, keepdims=True)
    den = jnp.maximum(jnp.abs(nq), jnp.exp(-m_t))
    hh = num / den
    hh = hh * lax.rsqrt(jnp.mean(hh * hh, axis=1, keepdims=True) + EPS)
    hh = hh * ng * _sigmoid(o_raw)
    return hh, C_out, n_out, jnp.broadcast_to(m_new, (1, CHUNK))


def _ml_specs(nc, rev):
    cc = (lambda c: nc - 1 - c) if rev else (lambda c: c)
    return [
        pl.BlockSpec((CHUNK, ML_HPS * ML_HEAD_COLS), lambda c, h: (cc(c), h)),
        pl.BlockSpec((CHUNK, SMALL_COLS), lambda c, h: (cc(c), 0)),
        pl.BlockSpec((1, SMALL_COLS), lambda c, h: (0, 0)),
        pl.BlockSpec((1, SMALL_COLS), lambda c, h: (0, 0)),
        pl.BlockSpec((1, ML_HPS * ML_DV), lambda c, h: (0, h)),
    ]


def _mlstm_forward(proj, small, ib, fb, ng):
    L = proj.shape[0]
    nc = L // CHUNK

    def body(blk_ref, small_ref, ib_ref, fb_ref, ng_ref, y_ref, cst_ref, nm_ref, c_carry, nm_carry):
        c, h0 = pl.program_id(0), pl.program_id(1) * ML_HPS

        @pl.when(c == 0)
        def _():
            for k in range(ML_HPS):
                c_carry[h0 + k] = jnp.zeros((ML_DV, ML_DK), F32)
                nm_carry[h0 + k] = jnp.zeros((2, ML_DK), F32)

        for k in range(ML_HPS):
            cst_ref[0, k] = c_carry[h0 + k]
            nm_ref[0, k] = nm_carry[h0 + k]
        hh, C_out, n_out, m_out = _mlstm_math(
            [blk_ref[:, ML_HEAD_COLS * k:ML_HEAD_COLS * (k + 1)] for k in range(ML_HPS)], small_ref[...],
            [c_carry[h0 + k] for k in range(ML_HPS)], [nm_carry[h0 + k, 0:1, :] for k in range(ML_HPS)],
            [nm_carry[h0 + k, 1:2, :] for k in range(ML_HPS)], ib_ref[...], fb_ref[...], ng_ref[...], h0)
        for k in range(ML_HPS):
            y_ref[:, ML_DV * k:ML_DV * (k + 1)] = hh[k].astype(BF16)
            c_carry[h0 + k] = C_out[k]
            nm_carry[h0 + k, 0:1, :] = n_out[k]
            nm_carry[h0 + k, 1:2, :] = m_out[k]

    return pl.pallas_call(
        body, name="mlstm_fwd", grid=(nc, ML_HEADS // ML_HPS),
        in_specs=_ml_specs(nc, False),
        out_specs=[pl.BlockSpec((CHUNK, ML_HPS * ML_DV), lambda c, h: (c, h)),
                   pl.BlockSpec((1, ML_HPS, ML_DV, ML_DK), lambda c, h: (c, h, 0, 0)),
                   pl.BlockSpec((1, ML_HPS, 2, ML_DK), lambda c, h: (c, h, 0, 0))],
        out_shape=[jax.ShapeDtypeStruct((L, ML_HEADS * ML_DV), BF16),
                   jax.ShapeDtypeStruct((nc, ML_HEADS, ML_DV, ML_DK), F32),
                   jax.ShapeDtypeStruct((nc, ML_HEADS, 2, ML_DK), F32)],
        scratch_shapes=[pltpu.VMEM((ML_HEADS, ML_DV, ML_DK), F32), pltpu.VMEM((ML_HEADS, 2, ML_DK), F32)],
        compiler_params=pltpu.CompilerParams(dimension_semantics=("arbitrary", "arbitrary"), vmem_limit_bytes=VMEM_LIMIT),
    )(proj, small, ib, fb, ng)


def _mlstm_backward(proj, small, ib, fb, ng, cst, nmst, dy, dsmall_in):
    L = proj.shape[0]
    nc = L // CHUNK

    def body(blk_ref, small_ref, ib_ref, fb_ref, ng_ref, cst_ref, nm_ref, dy_ref, dsin_ref,
             dproj_ref, dsmall_ref, dib_ref, dfb_ref, dng_ref, dc_carry, dnm_carry):
        s, hb = pl.program_id(0), pl.program_id(1)
        h0 = hb * ML_HPS

        @pl.when(s == 0)
        def _():
            for k in range(ML_HPS):
                dc_carry[h0 + k] = jnp.zeros((ML_DV, ML_DK), F32)
                dnm_carry[h0 + k] = jnp.zeros((2, ML_DK), F32)
                dng_ref[h0 + k] = jnp.zeros((1, ML_DV), F32)

        @pl.when((s == 0) & (hb == 0))
        def _():
            dib_ref[...] = jnp.zeros((1, SMALL_COLS), F32)
            dfb_ref[...] = jnp.zeros((1, SMALL_COLS), F32)

        heads = range(ML_HPS)
        _, vjp = jax.vjp(functools.partial(_mlstm_math, h0=h0),
                         [blk_ref[:, ML_HEAD_COLS * k:ML_HEAD_COLS * (k + 1)] for k in heads], small_ref[...],
                         [cst_ref[0, k] for k in heads], [nm_ref[0, k, 0:1, :] for k in heads],
                         [nm_ref[0, k, 1:2, :] for k in heads], ib_ref[...], fb_ref[...], ng_ref[...])
        d_blk, d_small, dC, dn, dm, d_ib, d_fb, d_ng = vjp(
            (tuple(dy_ref[:, ML_DV * k:ML_DV * (k + 1)].astype(F32) for k in heads),
             tuple(dc_carry[h0 + k] for k in heads), tuple(dnm_carry[h0 + k, 0:1, :] for k in heads),
             tuple(dnm_carry[h0 + k, 1:2, :] for k in heads)))
        for k in heads:
            dc_carry[h0 + k] = dC[k]
            dnm_carry[h0 + k, 0:1, :] = dn[k]
            dnm_carry[h0 + k, 1:2, :] = dm[k]
            dproj_ref[:, ML_HEAD_COLS * k:ML_HEAD_COLS * (k + 1)] = d_blk[k].astype(BF16)
            dng_ref[h0 + k] += d_ng[:, ML_DV * k:ML_DV * (k + 1)]

        @pl.when(hb == 0)
        def _():
            dsmall_ref[...] = dsin_ref[...] + d_small

        @pl.when(hb != 0)
        def _():
            dsmall_ref[...] += d_small

        dib_ref[...] += d_ib
        dfb_ref[...] += d_fb

    whole = lambda shape: pl.BlockSpec(shape, lambda s, h: (0,) * len(shape))
    return pl.pallas_call(
        body, name="mlstm_bwd", grid=(nc, ML_HEADS // ML_HPS),
        in_specs=_ml_specs(nc, True) + [
            pl.BlockSpec((1, ML_HPS, ML_DV, ML_DK), lambda s, h: (nc - 1 - s, h, 0, 0)),
            pl.BlockSpec((1, ML_HPS, 2, ML_DK), lambda s, h: (nc - 1 - s, h, 0, 0)),
            pl.BlockSpec((CHUNK, ML_HPS * ML_DV), lambda s, h: (nc - 1 - s, SSD_WIDTH // (ML_HPS * ML_DV) + h)),
            pl.BlockSpec((CHUNK, SMALL_COLS), lambda s, h: (nc - 1 - s, 0))],
        out_specs=[pl.BlockSpec((CHUNK, ML_HPS * ML_HEAD_COLS), lambda s, h: (nc - 1 - s, h)),
                   pl.BlockSpec((CHUNK, SMALL_COLS), lambda s, h: (nc - 1 - s, 0)),
                   whole((1, SMALL_COLS)), whole((1, SMALL_COLS)), whole((ML_HEADS, 1, ML_DV))],
        out_shape=[jax.ShapeDtypeStruct((L, ML_HEADS * ML_HEAD_COLS), BF16), jax.ShapeDtypeStruct((L, SMALL_COLS), F32),
                   jax.ShapeDtypeStruct((1, SMALL_COLS), F32), jax.ShapeDtypeStruct((1, SMALL_COLS), F32),
                   jax.ShapeDtypeStruct((ML_HEADS, 1, ML_DV), F32)],
        scratch_shapes=[pltpu.VMEM((ML_HEADS, ML_DV, ML_DK), F32), pltpu.VMEM((ML_HEADS, 2, ML_DK), F32)],
        compiler_params=pltpu.CompilerParams(dimension_semantics=("arbitrary", "arbitrary"), vmem_limit_bytes=VMEM_LIMIT),
    )(proj, small, ib, fb, ng, cst, nmst, dy, dsmall_in)


_OFF_Z, _OFF_X, _OFF_B, _OFF_C, _OFF_DT = 0, 2048, 4096, 4608, 5120
_OFF_Q, _OFF_K, _OFF_V, _OFF_O, _OFF_I, _OFF_F, IN_WIDTH = 5152, 6176, 7200, 9248, 11296, 11304, 11312


def _ssd_segments():
    segs = []
    for g in range(SSD_GROUPS):
        segs += [(_OFF_Z + 512 * g, 512), (_OFF_X + 512 * g, 512), (_OFF_B + 128 * g, 128), (_OFF_C + 128 * g, 128)]
    return segs


def _ml_segments():
    segs = []
    for h in range(ML_HEADS):
        segs += [(_OFF_Q + 128 * h, 128), (_OFF_K + 128 * h, 128), (_OFF_V + 256 * h, 256), (_OFF_O + 256 * h, 256)]
    return segs


_SMALL_SEGMENTS = [(_OFF_DT, 32), (_OFF_I, 8), (_OFF_F, 8)]


SHARD_IN = IN_WIDTH // 4


def _take_cols(slabs, segs):
    parts = []
    for s, n in segs:
        while n > 0:
            k, lo = divmod(s, SHARD_IN)
            m = min(n, SHARD_IN - lo)
            parts.append(slabs[k][:, lo:lo + m])
            s, n = s + m, n - m
    return jnp.concatenate(parts, axis=1)


def _split_w_in(slabs):
    small = _take_cols(slabs, _SMALL_SEGMENTS)
    small = jnp.concatenate([small, jnp.zeros((small.shape[0], SMALL_COLS - small.shape[1]), small.dtype)], axis=1)
    return _take_cols(slabs, _ssd_segments()), _take_cols(slabs, _ml_segments()), small


def _merge_w_in(d_ssd, d_ml, d_small):
    pieces = []
    for arr, segs in ((d_ssd, _ssd_segments()), (d_ml, _ml_segments()), (d_small, _SMALL_SEGMENTS)):
        pos = 0
        for s, n in segs:
            while n > 0:
                lo = s % SHARD_IN
                m = min(n, SHARD_IN - lo)
                pieces.append((s, arr[:, pos:pos + m]))
                s, n, pos = s + m, n - m, pos + m
    pieces.sort(key=lambda t: t[0])
    return [jnp.concatenate([p for s, p in pieces if s // SHARD_IN == k], axis=1) for k in range(4)]


def _conv_to_groups(cw):
    return jnp.stack([jnp.concatenate([cw[:, 512 * g:512 * (g + 1)], cw[:, 2048 + 128 * g:2048 + 128 * (g + 1)],
                                       cw[:, 2560 + 128 * g:2560 + 128 * (g + 1)]], axis=1) for g in range(SSD_GROUPS)])


def _conv_from_groups(d):
    return jnp.concatenate([d[g, :, 0:512] for g in range(SSD_GROUPS)] + [d[g, :, 512:640] for g in range(SSD_GROUPS)]
                           + [d[g, :, 640:768] for g in range(SSD_GROUPS)], axis=1)


def _small_row(vec, lane0):
    n = vec.shape[1]
    return jnp.concatenate([jnp.zeros((1, lane0), F32), vec, jnp.zeros((1, SMALL_COLS - lane0 - n), F32)], axis=1)


class _Side:
    def __init__(self, operands, out_shapes, sems, start, finish):
        self.operands, self.out_shapes, self.sems, self.start, self.finish = operands, out_shapes, sems, start, finish


def _grid_edge(grid, first):
    hit = None
    for axis, n in enumerate(grid):
        here = pl.program_id(axis) == (0 if first else n - 1)
        hit = here if hit is None else hit & here
    return hit


def _matmul(a, b, mode, out_dtype, tm, tn, tk, name, addend=None, layout=None, side=None):
    M, Kd = (a.shape[1], a.shape[0]) if mode == "tn" else a.shape
    N = b.shape[0] if mode == "nt" else b.shape[1]
    if layout == "cols4":
        tm, tn = M // 2, N // 4
    elif layout == "rows4":
        tm = M // 4
    tm, tn, tk = min(tm, M), min(tn, N), min(tk, Kd)
    if mode == "nn":
        a_spec = pl.BlockSpec((tm, tk), lambda i, j, k: (i, k))
        b_spec = pl.BlockSpec((tk, tn), lambda i, j, k: (k, j))
        dims = (((1,), (0,)), ((), ()))
    elif mode == "nt":
        a_spec = pl.BlockSpec((tm, tk), lambda i, j, k: (i, k))
        b_spec = pl.BlockSpec((tn, tk), lambda i, j, k: (j, k))
        dims = (((1,), (1,)), ((), ()))
    else:
        a_spec = pl.BlockSpec((tk, tm), lambda i, j, k: (k, i))
        b_spec = pl.BlockSpec((tk, tn), lambda i, j, k: (k, j))
        dims = (((0,), (0,)), ((), ()))
    assert M % tm == 0 and N % tn == 0 and Kd % tk == 0, (name, M, N, Kd, tm, tn, tk)
    nk = Kd // tk
    has_add = addend is not None
    n_in = 2 + has_add
    s_in, s_out = (len(side.operands), len(side.out_shapes)) if side is not None else (0, 0)
    grid = (M // tm, N // tn, nk)

    def body(*refs):
        a_ref, b_ref = refs[0], refs[1]
        add_ref = refs[2] if has_add else None
        o_ref, acc_ref = refs[n_in + s_in], refs[n_in + s_in + 1 + s_out]
        k = pl.program_id(2)
        side_refs = (refs[n_in:n_in + s_in], refs[n_in + s_in + 1:n_in + s_in + 1 + s_out], refs[n_in + s_in + 2 + s_out:])
        if side is not None:
            pl.when(_grid_edge(grid, first=True))(lambda: side.start(*side_refs))
        part = lax.dot_general(a_ref[...].astype(BF16), b_ref[...].astype(BF16), dims, preferred_element_type=F32)

        @pl.when(k == 0)
        def _():
            acc_ref[...] = part

        @pl.when(k != 0)
        def _():
            acc_ref[...] += part

        @pl.when(k == nk - 1)
        def _():
            r = acc_ref[...]
            if has_add:
                r = r + add_ref[...].astype(F32)
            if layout == "cols4":
                o_ref[0, 0] = r.astype(out_dtype)
            elif layout == "rows4":
                o_ref[0, 0] = r[:tm // 2].astype(out_dtype)
                o_ref[1, 0] = r[tm // 2:].astype(out_dtype)
            else:
                o_ref[...] = r.astype(out_dtype)

        if side is not None:
            pl.when(_grid_edge(grid, first=False))(lambda: side.finish(*side_refs))

    o_spec = pl.BlockSpec((tm, tn), lambda i, j, k: (i, j))
    out_spec, out_dims = o_spec, (M, N)
    if layout == "cols4":
        out_spec, out_dims = pl.BlockSpec((1, 1, tm, tn), lambda i, j, k: (i, j, 0, 0)), (2, 4, tm, tn)
    elif layout == "rows4":
        out_spec, out_dims = pl.BlockSpec((2, 1, tm // 2, tn), lambda i, j, k: (0, i, 0, j)), (2, 4, tm // 2, N)
    out_shape = jax.ShapeDtypeStruct(out_dims, out_dtype)
    semantics = ("parallel", "parallel", "arbitrary")
    if side is None:
        in_specs, operands, out_specs, scratch = [], (), out_spec, []
    else:
        in_specs, operands, scratch = [ANY] * s_in, tuple(side.operands), list(side.sems)
        out_specs, out_shape = [out_spec] + [ANY] * s_out, [out_shape] + list(side.out_shapes)
        semantics = ("arbitrary",) * 3
    res = pl.pallas_call(
        body, name=name, grid=grid,
        in_specs=[a_spec, b_spec] + ([o_spec] if has_add else []) + in_specs, out_specs=out_specs,
        out_shape=out_shape, scratch_shapes=[pltpu.VMEM((tm, tn), F32)] + scratch,
        compiler_params=pltpu.CompilerParams(dimension_semantics=semantics, vmem_limit_bytes=VMEM_LIMIT),
    )(*((a, b) + ((addend,) if has_add else ()) + operands))
    return res if side is None else (res[0], res[1:])


ROW_TILE = 256


def _rmsnorm_fwd(x, g, name):
    L, D = x.shape

    def body(x_ref, g_ref, u_ref):
        xv = x_ref[...]
        r = lax.rsqrt(jnp.mean(xv * xv, axis=1, keepdims=True) + EPS)
        u_ref[...] = (xv * r * g_ref[...]).astype(BF16)

    return pl.pallas_call(
        body, name=name, grid=(L // ROW_TILE,),
        in_specs=[pl.BlockSpec((ROW_TILE, D), lambda i: (i, 0)), pl.BlockSpec((1, D), lambda i: (0, 0))],
        out_specs=pl.BlockSpec((ROW_TILE, D), lambda i: (i, 0)),
        out_shape=jax.ShapeDtypeStruct((L, D), BF16),
        compiler_params=pltpu.CompilerParams(dimension_semantics=("parallel",), vmem_limit_bytes=VMEM_LIMIT),
    )(x, g)


def _rmsnorm_bwd(du, x, g, dres, name):
    L, D = x.shape

    def body(du_ref, x_ref, g_ref, dres_ref, dx_ref, dxb_ref, dg_ref):
        i = pl.program_id(0)
        xv, duv = x_ref[...], du_ref[...]
        r = lax.rsqrt(jnp.mean(xv * xv, axis=1, keepdims=True) + EPS)
        t = duv * g_ref[...]
        dx = dres_ref[...] + r * t - xv * (r * r * r) * jnp.mean(t * xv, axis=1, keepdims=True)
        dx_ref[...] = dx
        dxb_ref[...] = dx.astype(BF16)
        dg = jnp.sum(duv * xv * r, axis=0, keepdims=True)

        @pl.when(i == 0)
        def _():
            dg_ref[...] = dg

        @pl.when(i != 0)
        def _():
            dg_ref[...] += dg

    row = pl.BlockSpec((ROW_TILE, D), lambda i: (i, 0))
    vec = pl.BlockSpec((1, D), lambda i: (0, 0))
    return pl.pallas_call(
        body, name=name, grid=(L // ROW_TILE,),
        in_specs=[row, row, vec, row], out_specs=[row, row, vec],
        out_shape=[jax.ShapeDtypeStruct((L, D), F32), jax.ShapeDtypeStruct((L, D), BF16), jax.ShapeDtypeStruct((1, D), F32)],
        compiler_params=pltpu.CompilerParams(dimension_semantics=("arbitrary",), vmem_limit_bytes=VMEM_LIMIT),
    )(du, x, g, dres)


def _loss_head(h, target, g):
    L, D = h.shape

    def body(h_ref, t_ref, g_ref, dh_ref, dhb_ref, loss_ref, dg_ref):
        i = pl.program_id(0)
        hv = h_ref[...]
        r = lax.rsqrt(jnp.mean(hv * hv, axis=1, keepdims=True) + EPS)
        diff = hv * r * g_ref[...] - t_ref[...]
        part = 0.5 * jnp.sum(jnp.mean(diff * diff, axis=1, keepdims=True), axis=0, keepdims=True)
        dy = diff * (1.0 / D)
        t = dy * g_ref[...]
        dh = r * t - hv * (r * r * r) * jnp.mean(t * hv, axis=1, keepdims=True)
        dh_ref[...] = dh
        dhb_ref[...] = dh.astype(BF16)
        dg = jnp.sum(dy * hv * r, axis=0, keepdims=True)

        @pl.when(i == 0)
        def _():
            dg_ref[...] = dg
            loss_ref[...] = jnp.broadcast_to(part, (1, 128))

        @pl.when(i != 0)
        def _():
            dg_ref[...] += dg
            loss_ref[...] += jnp.broadcast_to(part, (1, 128))

    row = pl.BlockSpec((ROW_TILE, D), lambda i: (i, 0))
    vec = pl.BlockSpec((1, D), lambda i: (0, 0))
    return pl.pallas_call(
        body, name="loss_head", grid=(L // ROW_TILE,),
        in_specs=[row, row, vec], out_specs=[row, row, pl.BlockSpec((1, 128), lambda i: (0, 0)), vec],
        out_shape=[jax.ShapeDtypeStruct((L, D), F32), jax.ShapeDtypeStruct((L, D), BF16), jax.ShapeDtypeStruct((1, 128), F32),
                   jax.ShapeDtypeStruct((1, D), F32)],
        compiler_params=pltpu.CompilerParams(dimension_semantics=("arbitrary",), vmem_limit_bytes=VMEM_LIMIT),
    )(h, target, g)


FF_TILE = 1408


FF_COLS = 512


def _ffn_in(u, wg, wu, tm, side):
    L, D = u.shape
    F = wg.shape[1]
    tm = min(tm, L)
    s_in, s_out = len(side.operands), len(side.out_shapes)
    grid = (L // tm, F // FF_COLS)

    def body(*refs):
        u_ref, wg_ref, wu_ref = refs[:3]
        g_ref, up_ref, act_ref = refs[3 + s_in:6 + s_in]
        side_refs = (refs[3:3 + s_in], refs[6 + s_in:6 + s_in + s_out], refs[6 + s_in + s_out:])
        pl.when(_grid_edge(grid, first=True))(lambda: side.start(*side_refs))
        g = jnp.dot(u_ref[...], wg_ref[...], preferred_element_type=F32)
        v = jnp.dot(u_ref[...], wu_ref[...], preferred_element_type=F32)
        g_ref[...] = g.astype(BF16)
        up_ref[...] = v.astype(BF16)
        act_ref[...] = (_silu(g) * v).astype(BF16)
        pl.when(_grid_edge(grid, first=False))(lambda: side.finish(*side_refs))

    w_spec = pl.BlockSpec((D, FF_COLS), lambda i, j: (0, j))
    o_spec = pl.BlockSpec((tm, FF_COLS), lambda i, j: (i, j))
    out = jax.ShapeDtypeStruct((L, F), BF16)
    res = pl.pallas_call(
        body, name="ffn_gate_up", grid=grid,
        in_specs=[pl.BlockSpec((tm, D), lambda i, j: (i, 0)), w_spec, w_spec] + [ANY] * s_in,
        out_specs=[o_spec] * 3 + [ANY] * s_out, out_shape=[out] * 3 + list(side.out_shapes),
        scratch_shapes=list(side.sems),
        compiler_params=pltpu.CompilerParams(dimension_semantics=("arbitrary", "arbitrary"), vmem_limit_bytes=VMEM_LIMIT),
    )(u, wg, wu, *side.operands)
    return res[0], res[1], res[2], res[3:]


def _ffn_back(dh, w_down, gate, up, tm):
    L, D = dh.shape
    F = w_down.shape[0]
    tm = min(tm, L)

    def body(dh_ref, w_ref, g_ref, u_ref, dg_ref, du_ref):
        d = lax.dot_general(dh_ref[...], w_ref[...], (((1,), (1,)), ((), ())), preferred_element_type=F32)
        gv = g_ref[...].astype(F32)
        s = _sigmoid(gv)
        dg_ref[...] = (d * u_ref[...].astype(F32) * (s * (1.0 + gv * (1.0 - s)))).astype(BF16)
        du_ref[...] = (d * gv * s).astype(BF16)

    blk = pl.BlockSpec((tm, FF_COLS), lambda i, j: (i, j))
    out = jax.ShapeDtypeStruct((L, F), BF16)
    return pl.pallas_call(
        body, name="ffn_back", grid=(L // tm, F // FF_COLS),
        in_specs=[pl.BlockSpec((tm, D), lambda i, j: (i, 0)), pl.BlockSpec((FF_COLS, D), lambda i, j: (j, 0)), blk, blk],
        out_specs=[blk, blk], out_shape=[out, out],
        compiler_params=pltpu.CompilerParams(dimension_semantics=("parallel", "parallel"), vmem_limit_bytes=VMEM_LIMIT),
    )(dh, w_down, gate, up)


def _adamw(w, g, m, v, rows, name, cols=None):
    R, C = w.shape
    assert R % rows == 0 and (cols is None or (rows == R and C % cols == 0)), (name, R, C, rows, cols)

    def body(w_ref, g_ref, m_ref, v_ref, d_ref, nm_ref, nv_ref):
        gv = g_ref[...]
        mn = ADAM_B1 * m_ref[...] + (1.0 - ADAM_B1) * gv
        vn = ADAM_B2 * v_ref[...] + (1.0 - ADAM_B2) * (gv * gv)
        m_hat = mn / (1.0 - ADAM_B1 ** ADAM_STEP)
        v_hat = vn / (1.0 - ADAM_B2 ** ADAM_STEP)
        d_ref[...] = -ADAM_LR * (m_hat / (jnp.sqrt(v_hat) + ADAM_EPS) + ADAM_WD * w_ref[...])
        nm_ref[...] = mn
        nv_ref[...] = vn

    blk = pl.BlockSpec((rows, C), lambda i: (i, 0)) if cols is None else pl.BlockSpec((R, cols), lambda i: (0, i))
    out = jax.ShapeDtypeStruct((R, C), F32)
    return pl.pallas_call(
        body, name=name, grid=(R // rows if cols is None else C // cols,), in_specs=[blk] * 4, out_specs=[blk] * 3,
        out_shape=[out] * 3,
        compiler_params=pltpu.CompilerParams(dimension_semantics=("parallel",), vmem_limit_bytes=VMEM_LIMIT),
    )(w, g, m, v)


ANY = pl.BlockSpec(memory_space=pl.ANY)
N_DEV = 8


def _allgather_small(p, name):
    R, C = p.shape

    def body(p_ref, out_ref, send_sems, recv_sems, local_sem):
        x, y, c = lax.axis_index("x"), lax.axis_index("y"), lax.axis_index("c")
        me = 4 * x + 2 * y + c
        mine = pltpu.make_async_copy(p_ref, out_ref.at[me], local_sem)
        mine.start()

        def peer(d):
            return (x ^ ((d >> 2) & 1), y ^ ((d >> 1) & 1), c ^ (d & 1))

        def copy(d, block):
            return pltpu.make_async_remote_copy(src_ref=p_ref, dst_ref=out_ref.at[block], send_sem=send_sems.at[d - 1],
                                                recv_sem=recv_sems.at[d - 1], device_id=peer(d), device_id_type=MESH)

        sends = [copy(d, me) for d in range(1, N_DEV)]
        for cp in sends:
            cp.start()
        for d in range(1, N_DEV):
            px, py, pc = peer(d)
            copy(d, 4 * px + 2 * py + pc).wait_recv()
        for cp in sends:
            cp.wait_send()
        mine.wait()

    return pl.pallas_call(
        body, name=name, out_shape=jax.ShapeDtypeStruct((N_DEV, R, C), p.dtype),
        in_specs=[pl.BlockSpec(memory_space=pltpu.VMEM)], out_specs=pl.BlockSpec(memory_space=pltpu.VMEM),
        scratch_shapes=[pltpu.SemaphoreType.DMA((N_DEV - 1,)), pltpu.SemaphoreType.DMA((N_DEV - 1,)), pltpu.SemaphoreType.DMA],
    )(p)


def _other_chips(x, y):
    return [(1 - x, y), (x, 1 - y), (1 - x, 1 - y)]


def _run_side(side, name):
    s_in, s_out = len(side.operands), len(side.out_shapes)

    def body(*refs):
        parts = (refs[:s_in], refs[s_in:s_in + s_out], refs[s_in + s_out:])
        side.start(*parts)
        side.finish(*parts)

    return pl.pallas_call(body, name=name, out_shape=list(side.out_shapes), in_specs=[ANY] * s_in, out_specs=[ANY] * s_out,
                          scratch_shapes=list(side.sems))(*side.operands)


def _gather_side(shards, sender_x=None):
    T = len(shards)
    halves = [s.shape[0] // 2 for s in shards]

    def tools(ins, outs, sems):
        send_sems, recv_sems = sems
        x, y, c = lax.axis_index("x"), lax.axis_index("y"), lax.axis_index("c")

        def rows(t, px, py, pc):
            return outs[t].at[2 * px + py if sender_x is None else py, pl.ds(pc * halves[t], halves[t]), :]

        def copy(t, k, block, to, own=False):
            src = ins[t].at[pl.ds(c * halves[t], halves[t]), :] if own else rows(t, *block)
            return pltpu.make_async_remote_copy(src_ref=src, dst_ref=rows(t, *block), send_sem=send_sems.at[t, k],
                                                recv_sem=recv_sems.at[t, k], device_id=to, device_id_type=MESH)

        sends = [None] * 4 if sender_x is None else [x == sender_x, x != sender_x, x == sender_x, x != sender_x]
        return (x, y, c), _other_chips(x, y), copy, sends

    def when(pred, fn):
        fn() if pred is None else pl.when(pred)(fn)

    def start(ins, outs, sems):
        (x, y, c), chips, copy, sends = tools(ins, outs, sems)

        def go():
            for t in range(T):
                for j, chip in enumerate(chips):
                    copy(t, j, (x, y, c), (*chip, c), own=True).start()

        when(sends[0], go)

    def finish(ins, outs, sems):
        (x, y, c), chips, copy, sends = tools(ins, outs, sems)
        for j, chip in enumerate(chips):
            def relay(j=j, chip=chip):
                for t in range(T):
                    copy(t, j, (*chip, c), (x, y, c)).wait_recv()
                    copy(t, 3 + j, (*chip, c), (x, y, 1 - c)).start()

            when(sends[1 + j], relay)
        for j, chip in enumerate(chips):
            def land(j=j, chip=chip):
                for t in range(T):
                    copy(t, 3 + j, (*chip, 1 - c), (x, y, c)).wait_recv()
                    copy(t, 3 + j, (*chip, c), (x, y, 1 - c)).wait_send()

            when(sends[1 + j], land)

        def sent():
            for t in range(T):
                for j, chip in enumerate(chips):
                    copy(t, j, (x, y, c), (*chip, c), own=True).wait_send()

        when(sends[0], sent)

    slots = 4 if sender_x is None else 3
    return _Side(shards, [jax.ShapeDtypeStruct((slots,) + s.shape, s.dtype) for s in shards],
                 [pltpu.SemaphoreType.DMA((T, 6)), pltpu.SemaphoreType.DMA((T, 6))], start, finish)


def _with_own_shard(gathered, shard, slot):
    return lax.dynamic_update_slice(gathered, shard[None], (slot,) + (0,) * shard.ndim)


def _pair_exchange(grads, name, chip_major=False):
    T = len(grads)

    def body(*refs):
        ins, outs = refs[:T], refs[T:2 * T]
        send_sems, recv_sems = refs[2 * T:]
        x, y, c = lax.axis_index("x"), lax.axis_index("y"), lax.axis_index("c")
        theirs = (lambda r: r.at[:, 1 - c]) if chip_major else (lambda r: r.at[1 - c])
        cps = [pltpu.make_async_remote_copy(src_ref=theirs(ins[t]), dst_ref=outs[t], send_sem=send_sems.at[t],
                                            recv_sem=recv_sems.at[t], device_id=(x, y, 1 - c), device_id_type=MESH)
               for t in range(T)]
        for cp in cps:
            cp.start()
        for cp in cps:
            cp.wait()

    return pl.pallas_call(
        body, name=name,
        out_shape=[jax.ShapeDtypeStruct((4,) + g.shape[2:], g.dtype) for g in grads],
        in_specs=[ANY] * T, out_specs=[ANY] * T,
        scratch_shapes=[pltpu.SemaphoreType.DMA((T,)), pltpu.SemaphoreType.DMA((T,))],
    )(*grads)


def _scatter_side(parts):
    T = len(parts)

    def tools(ins, outs, sems):
        send_sems, recv_sems = sems
        x, y, c = lax.axis_index("x"), lax.axis_index("y"), lax.axis_index("c")

        def copy(t, j, src_slot, dst_slot, chip):
            return pltpu.make_async_remote_copy(src_ref=ins[t].at[src_slot], dst_ref=outs[t].at[dst_slot],
                                                send_sem=send_sems.at[t, j], recv_sem=recv_sems.at[t, j],
                                                device_id=(*chip, c), device_id_type=MESH)

        return 2 * x + y, _other_chips(x, y), copy

    def start(ins, outs, sems):
        my_chip, chips, copy = tools(ins, outs, sems)
        for t in range(T):
            for j, (px, py) in enumerate(chips):
                copy(t, j, 2 * px + py, my_chip, (px, py)).start()

    def finish(ins, outs, sems):
        my_chip, chips, copy = tools(ins, outs, sems)
        for t in range(T):
            for j, (px, py) in enumerate(chips):
                copy(t, j, my_chip, 2 * px + py, (px, py)).wait_recv()
        for t in range(T):
            for j, (px, py) in enumerate(chips):
                copy(t, j, 2 * px + py, my_chip, (px, py)).wait_send()

    return _Side(parts, [jax.ShapeDtypeStruct(p.shape, p.dtype) for p in parts],
                 [pltpu.SemaphoreType.DMA((T, 3)), pltpu.SemaphoreType.DMA((T, 3))], start, finish)


def _pair_share(fulls):
    T = len(fulls)

    def body(*refs):
        outs = refs[T:2 * T]
        send_sems, recv_sems = refs[2 * T:]
        x, y, c = lax.axis_index("x"), lax.axis_index("y"), lax.axis_index("c")

        def copy(t, half):
            return pltpu.make_async_remote_copy(src_ref=outs[t].at[half], dst_ref=outs[t].at[half], send_sem=send_sems.at[t],
                                                recv_sem=recv_sems.at[t], device_id=(x, y, 1 - c), device_id_type=MESH)

        for t in range(T):
            copy(t, c).start()
        for t in range(T):
            copy(t, 1 - c).wait_recv()
        for t in range(T):
            copy(t, c).wait_send()

    return pl.pallas_call(
        body, name="grad_pair_share", out_shape=[jax.ShapeDtypeStruct(f.shape, f.dtype) for f in fulls],
        in_specs=[ANY] * T, out_specs=[ANY] * T, input_output_aliases={t: t for t in range(T)},
        scratch_shapes=[pltpu.SemaphoreType.DMA((T,)), pltpu.SemaphoreType.DMA((T,))],
    )(*fulls)


def _row_tile(hr):
    return next(r for r in (128, 64, 32, 16, 8) if hr % r == 0)


def _pair_add(g, got, c, name, chip_major=False):
    _, _, hr, C = g.shape
    tr = _row_tile(hr)
    own = (lambda k, i, c_ref: (k, c_ref[0], i, 0)) if chip_major else (lambda k, i, c_ref: (c_ref[0], k, i, 0))

    def body(c_ref, g_ref, r_ref, o_ref):
        o_ref[...] = (g_ref[0].astype(F32) + r_ref[...].astype(F32)).astype(o_ref.dtype)

    return pl.pallas_call(
        body, name=name,
        grid_spec=pltpu.PrefetchScalarGridSpec(
            num_scalar_prefetch=1, grid=(4, hr // tr),
            in_specs=[pl.BlockSpec((1, 1, tr, C), own),
                      pl.BlockSpec((1, tr, C), lambda k, i, c_ref: (k, i, 0))],
            out_specs=pl.BlockSpec((1, tr, C), lambda k, i, c_ref: (k, i, 0))),
        out_shape=jax.ShapeDtypeStruct(got.shape, g.dtype),
        compiler_params=pltpu.CompilerParams(dimension_semantics=("parallel", "parallel"), vmem_limit_bytes=VMEM_LIMIT),
    )(jnp.reshape(c, (1,)).astype(jnp.int32), g, got)


def _chip_sum(parts, slots, chip, c, name):
    _, hr, C = parts.shape
    tr = _row_tile(hr)

    def body(chip_ref, half_ref, own_ref, a_ref, b_ref, c_ref, o_ref):
        acc = own_ref[...].astype(F32)
        for r in (a_ref, b_ref, c_ref):
            acc = acc + r[...].astype(F32)
        o_ref[...] = acc

    other = lambda j: pl.BlockSpec((1, tr, C), lambda i, chip_ref, half_ref: ((chip_ref[0] + j) % 4, i, 0))
    return pl.pallas_call(
        body, name=name,
        grid_spec=pltpu.PrefetchScalarGridSpec(
            num_scalar_prefetch=2, grid=(hr // tr,),
            in_specs=[pl.BlockSpec((1, tr, C), lambda i, chip_ref, half_ref: (chip_ref[0], i, 0)),
                      other(1), other(2), other(3)],
            out_specs=pl.BlockSpec((1, tr, C), lambda i, chip_ref, half_ref: (half_ref[0], i, 0))),
        out_shape=jax.ShapeDtypeStruct((2, hr, C), F32),
        compiler_params=pltpu.CompilerParams(dimension_semantics=("parallel",), vmem_limit_bytes=VMEM_LIMIT),
    )(jnp.reshape(chip, (1,)).astype(jnp.int32), jnp.reshape(c, (1,)).astype(jnp.int32), parts, slots, slots, slots)


def _sum_slots(parts, name):
    n, hr, C = parts.shape
    tr = _row_tile(hr)

    def body(p_ref, o_ref):
        acc = p_ref[0].astype(F32)
        for k in range(1, n):
            acc = acc + p_ref[k].astype(F32)
        o_ref[...] = acc

    return pl.pallas_call(
        body, name=name, grid=(hr // tr,),
        in_specs=[pl.BlockSpec((n, tr, C), lambda i: (0, i, 0))], out_specs=pl.BlockSpec((tr, C), lambda i: (i, 0)),
        out_shape=jax.ShapeDtypeStruct((hr, C), F32),
        compiler_params=pltpu.CompilerParams(dimension_semantics=("parallel",), vmem_limit_bytes=VMEM_LIMIT),
    )(parts)


PACK_ROWS, PACK_COLS = 16, 3072
_PACK = [("norm_mix_g", 0, 1, 2048), ("conv_w", 1, 4, 3072), ("conv_b", 5, 1, 3072), ("dt_bias", 6, 1, 32),
         ("a_log", 7, 1, 32), ("d_skip", 8, 1, 32), ("ssd_norm_g", 9, 1, 2048), ("i_bias", 10, 1, 8), ("f_bias", 11, 1, 8),
         ("mlstm_norm_g", 12, 1, 2048), ("norm_ffn_g", 13, 1, 2048), ("final_norm_g", 14, 1, 2048), ("loss", 15, 1, 1)]
_WEIGHTS = ["norm_mix_g", "w_in", "conv_w", "conv_b", "dt_bias", "a_log", "d_skip", "ssd_norm_g", "i_bias", "f_bias",
            "mlstm_norm_g", "w_out", "norm_ffn_g", "w_gate", "w_up", "w_down", "final_norm_g"]
_BIG = ["w_in", "w_out", "w_gate", "w_up", "w_down"]


def _pack(vals):
    rows = []
    for name, _, r, w in _PACK:
        v = vals.get(name)
        v = jnp.zeros((r, w), F32) if v is None else v.reshape(r, w).astype(F32)
        rows.append(jnp.concatenate([v, jnp.zeros((r, PACK_COLS - w), F32)], axis=1) if w < PACK_COLS else v)
    return jnp.concatenate(rows, axis=0)


def _unpack(p):
    return {name: p[r0:r0 + r, :w] for name, r0, r, w in _PACK}


def kernel(x, norm_mix_g, w_in, conv_w, conv_b, dt_bias, a_log, d_skip, ssd_norm_g, i_bias, f_bias, mlstm_norm_g, w_out, norm_ffn_g, w_gate, w_up, w_down, final_norm_g, loss_target, m_norm_mix_g, m_w_in, m_conv_w, m_conv_b, m_dt_bias, m_a_log, m_d_skip, m_ssd_norm_g, m_i_bias, m_f_bias, m_mlstm_norm_g, m_w_out, m_norm_ffn_g, m_w_gate, m_w_up, m_w_down, m_final_norm_g, v_norm_mix_g, v_w_in, v_conv_w, v_conv_b, v_dt_bias, v_a_log, v_d_skip, v_ssd_norm_g, v_i_bias, v_f_bias, v_mlstm_norm_g, v_w_out, v_norm_ffn_g, v_w_gate, v_w_up, v_w_down, v_final_norm_g):
    weights = dict(norm_mix_g=norm_mix_g, w_in=w_in, conv_w=conv_w, conv_b=conv_b, dt_bias=dt_bias, a_log=a_log, d_skip=d_skip,
                   ssd_norm_g=ssd_norm_g, i_bias=i_bias, f_bias=f_bias, mlstm_norm_g=mlstm_norm_g, w_out=w_out,
                   norm_ffn_g=norm_ffn_g, w_gate=w_gate, w_up=w_up, w_down=w_down, final_norm_g=final_norm_g)
    mom1 = dict(norm_mix_g=m_norm_mix_g, w_in=m_w_in, conv_w=m_conv_w, conv_b=m_conv_b, dt_bias=m_dt_bias, a_log=m_a_log,
                d_skip=m_d_skip, ssd_norm_g=m_ssd_norm_g, i_bias=m_i_bias, f_bias=m_f_bias, mlstm_norm_g=m_mlstm_norm_g,
                w_out=m_w_out, norm_ffn_g=m_norm_ffn_g, w_gate=m_w_gate, w_up=m_w_up, w_down=m_w_down,
                final_norm_g=m_final_norm_g)
    mom2 = dict(norm_mix_g=v_norm_mix_g, w_in=v_w_in, conv_w=v_conv_w, conv_b=v_conv_b, dt_bias=v_dt_bias, a_log=v_a_log,
                d_skip=v_d_skip, ssd_norm_g=v_ssd_norm_g, i_bias=v_i_bias, f_bias=v_f_bias, mlstm_norm_g=v_mlstm_norm_g,
                w_out=v_w_out, norm_ffn_g=v_norm_ffn_g, w_gate=v_w_gate, w_up=v_w_up, w_down=v_w_down,
                final_norm_g=v_final_norm_g)
    xi, yi, ci = lax.axis_index("x"), lax.axis_index("y"), lax.axis_index("c")
    chip = 2 * xi + yi
    xs, tgt = x[0], loss_target[0]

    shards = {n: weights[n][0].astype(BF16) for n in _BIG}
    gathered = lambda n, g: _with_own_shard(g, shards[n], chip)
    by_cols = lambda g, width: jnp.transpose(g, (1, 0, 2)).reshape(D_MODEL, width)
    half_in = lambda g, sx: _with_own_shard(g, shards["w_in"], jnp.where(xi == sx, yi, 2))
    (g_in0,) = _run_side(_gather_side([shards["w_in"]], sender_x=0), "allgather_w_in_first")
    g_in0 = half_in(g_in0, 0)
    W_ssd = _take_cols([g_in0[0], g_in0[1], None, None], _ssd_segments())
    cw_all = _allgather_small(jnp.concatenate([conv_w[0], jnp.zeros((4, 768), F32)], axis=0), "allgather_conv_w")
    conv_w_full = jnp.concatenate([cw_all[2 * k, :SSD_CONV] for k in range(4)], axis=1)
    cwg, cbg = _conv_to_groups(conv_w_full), _conv_to_groups(conv_b)
    dtb_row, al_row = _small_row(dt_bias, 0), _small_row(a_log, 0)
    ib_row, fb_row = _small_row(i_bias, LANE_I), _small_row(f_bias, LANE_F)
    dskip_lane = jnp.repeat(d_skip, 64, axis=1)
    fng = final_norm_g[None]

    u1 = _rmsnorm_fwd(xs, norm_mix_g, "norm_mix_fwd")
    p_ssd, (g_in1,) = _matmul(u1, W_ssd, "nn", F32, 1024, 1024, 2048, "proj_ssd",
                              side=_gather_side([shards["w_in"]], sender_x=1))
    g_in1 = half_in(g_in1, 1)
    _, W_ml, W_small = _split_w_in([g_in0[0], g_in0[1], g_in1[0], g_in1[1]])
    p_ml, (g_out, g_gate) = _matmul(u1, W_ml, "nn", F32, 1024, 1024, 2048, "proj_ml",
                                    side=_gather_side([shards["w_out"], shards["w_gate"]]))
    p_small = _matmul(u1, W_small, "nn", F32, 1024, 128, 2048, "proj_small")
    y_ssd, ssd_st = _ssd_forward(p_ssd, p_small, cwg, cbg, dtb_row, al_row, dskip_lane, ssd_norm_g)
    y_ml, ml_c, ml_nm = _mlstm_forward(p_ml, p_small, ib_row, fb_row, mlstm_norm_g)
    y_mix = jnp.concatenate([y_ssd, y_ml], axis=1)
    W_out = gathered("w_out", g_out).reshape(2 * SSD_WIDTH, D_MODEL)
    h1, (g_up,) = _matmul(y_mix, W_out, "nn", F32, 1024, 1024, 2048, "out_proj", addend=xs,
                          side=_gather_side([shards["w_up"]]))
    u2 = _rmsnorm_fwd(h1, norm_ffn_g, "norm_ffn_fwd")
    W_gate = by_cols(gathered("w_gate", g_gate), D_FF)
    W_up = by_cols(gathered("w_up", g_up), D_FF)
    gate, up, act, (g_down,) = _ffn_in(u2, W_gate, W_up, 1024, _gather_side([shards["w_down"]]))
    W_down = gathered("w_down", g_down).reshape(D_FF, D_MODEL)
    h2 = _matmul(act, W_down, "nn", F32, 1024, 1024, 2 * FF_TILE, "ffn_down", addend=h1)
    dh2, dh2_b, loss_row, d_fng = _loss_head(h2, tgt, fng)

    dW_down = _matmul(act, dh2_b, "tn", BF16, FF_TILE, 1024, 2048, "dw_down", layout="rows4")
    d_gate, d_up = _ffn_back(dh2_b, W_down, gate, up, 1024)
    du2 = _matmul(d_gate, W_gate, "nt", F32, 1024, 1024, 2 * FF_TILE, "du2_gate")
    du2 = _matmul(d_up, W_up, "nt", F32, 1024, 1024, 2 * FF_TILE, "du2_up", addend=du2)
    dW_gate = _matmul(u2, d_gate, "tn", BF16, 1024, FF_TILE, 2048, "dw_gate", layout="cols4")
    dW_up = _matmul(u2, d_up, "tn", BF16, 1024, FF_TILE, 2048, "dw_up", layout="cols4")
    dh1, dh1_b, d_ffn_g = _rmsnorm_bwd(du2, h1, norm_ffn_g, dh2, "norm_ffn_bwd")
    dy_mix = _matmul(dh1_b, W_out, "nt", F32, 1024, 1024, 2048, "d_mix")
    dW_out = _matmul(y_mix, dh1_b, "tn", BF16, 1024, 1024, 2048, "dw_out", layout="rows4")
    early = ["w_out", "w_gate", "w_up", "w_down"]
    early_g = [dW_out, dW_gate, dW_up, dW_down]
    early_got = _pair_exchange(early_g, "grad_pair_exchange_early")
    parts = {n: _pair_add(g, r, ci, "grad_pair_add_" + n) for g, r, n in zip(early_g, early_got, early)}
    (d_ssd, d_small, d_cw, d_cb, d_dtb, d_alog, d_dskip, d_sng), early_slots = _ssd_backward(
        p_ssd, p_small, cwg, cbg, dtb_row, al_row, dskip_lane, ssd_norm_g, ssd_st, dy_mix,
        side=_scatter_side([parts[n] for n in early]))
    slots = dict(zip(early, early_slots))
    d_ml, d_small, d_ib, d_fb, d_mng = _mlstm_backward(p_ml, p_small, ib_row, fb_row, mlstm_norm_g, ml_c, ml_nm, dy_mix, d_small)
    dW_ssd = _matmul(u1, d_ssd, "tn", BF16, 1024, 1280, 2048, "dw_ssd")
    dW_ml = _matmul(u1, d_ml, "tn", BF16, 1024, 1024, 2048, "dw_ml")
    dW_small = _matmul(u1, d_small, "tn", BF16, 1024, 128, 2048, "dw_small")
    g_w_in = jnp.stack(_merge_w_in(dW_ssd, dW_ml, dW_small)).reshape(4, 2, D_MODEL // 2, SHARD_IN)
    (got_in,) = _pair_exchange([g_w_in], "grad_pair_exchange_w_in", chip_major=True)
    parts["w_in"] = _pair_add(g_w_in, got_in, ci, "grad_pair_add_w_in", chip_major=True)
    du1, (slots["w_in"],) = _matmul(d_ssd, W_ssd, "nt", F32, 1024, 1024, 2560, "du1_ssd",
                                    side=_scatter_side([parts["w_in"]]))
    du1 = _matmul(d_ml, W_ml, "nt", F32, 1024, 1024, 3072, "du1_ml", addend=du1)
    du1 = _matmul(d_small, W_small, "nt", F32, 1024, 1024, 128, "du1_small", addend=du1)
    grad_x, _, d_mix_g = _rmsnorm_bwd(du1, xs, norm_mix_g, dh1, "norm_mix_bwd")

    fulls = _pair_share([_chip_sum(parts[n], slots[n], chip, ci, "grad_chip_sum_" + n) for n in _BIG])
    grads = {n: f.reshape(weights[n].shape[1:]) for n, f in zip(_BIG, fulls)}

    small = _pack(dict(norm_mix_g=d_mix_g, conv_w=_conv_from_groups(d_cw), conv_b=_conv_from_groups(d_cb),
                       dt_bias=d_dtb[:, 0:32], a_log=d_alog[:, 0:32], d_skip=d_dskip.reshape(SSD_HEADS, 64).sum(axis=1),
                       ssd_norm_g=d_sng, i_bias=d_ib[:, LANE_I:LANE_I + 8], f_bias=d_fb[:, LANE_F:LANE_F + 8],
                       mlstm_norm_g=d_mng, norm_ffn_g=d_ffn_g, final_norm_g=d_fng, loss=loss_row[:, 0:1]))
    total = _sum_slots(_allgather_small(small, "allgather_small_grads"), "small_grad_sum")
    small_w = {n: weights[n] for n in _WEIGHTS if n not in _BIG and n != "conv_w"}
    sd, sm, sv = _adamw(_pack(small_w), total, _pack({n: mom1[n] for n in small_w}), _pack({n: mom2[n] for n in small_w}),
                        PACK_ROWS, "adamw_small")
    tot, sd, sm, sv = _unpack(total), _unpack(sd), _unpack(sm), _unpack(sv)
    for n in small_w:
        grads[n] = tot[n].reshape(weights[n].shape)
    grads["conv_w"] = lax.dynamic_slice_in_dim(tot["conv_w"], chip * 768, 768, axis=1)
    loss = tot["loss"][0, 0]

    delta, new_m, new_v = {}, {}, {}
    flat = lambda a: jnp.swapaxes(a, 0, 1).reshape(-1, 128)
    unflat = lambda a: jnp.swapaxes(a.reshape(w_in.shape[2], w_in.shape[1]), 0, 1)[None]
    d, nm, nv = _adamw(flat(w_in[0]), flat(grads["w_in"]), flat(m_w_in[0]), flat(v_w_in[0]), 5656, "adamw_w_in")
    delta["w_in"], new_m["w_in"], new_v["w_in"] = unflat(d), unflat(nm), unflat(nv)
    grads["w_in"] = grads["w_in"][None]
    for n in _BIG[1:] + ["conv_w"]:
        w2 = weights[n][0]
        d, nm, nv = _adamw(w2, grads[n], mom1[n][0], mom2[n][0], _row_tile(w2.shape[0]) if n != "conv_w" else SSD_CONV,
                           "adamw_" + n)
        delta[n], new_m[n], new_v[n] = d[None], nm[None], nv[None]
        grads[n] = grads[n][None]
    for n in small_w:
        delta[n], new_m[n], new_v[n] = (t[n].reshape(weights[n].shape) for t in (sd, sm, sv))
    return (loss, grad_x[None], *[grads[n] for n in _WEIGHTS], *[delta[n] for n in _WEIGHTS],
            *[new_m[n] for n in _WEIGHTS], *[new_v[n] for n in _WEIGHTS])
```

```python
import functools

import jax
import jax.numpy as jnp
import numpy as np
from jax import lax
from jax.experimental import pallas as pl
from jax.experimental.pallas import tpu as pltpu

F32 = jnp.float32
BF16 = jnp.bfloat16

D_MODEL = 2048
SSD_HEADS = 32
SSD_GROUPS = 4
SSD_STATE = 128
SSD_WIDTH = 2048
SSD_CONV = 4
SSD_GROUP_COLS = 1280
SSD_XBC_COLS = 768
ML_HEADS = 8
ML_HEAD_COLS = 768
ML_DK = 128
ML_DV = 256
CHUNK = 128
SMALL_COLS = 128
LANE_I = 32
LANE_F = 40
D_FF = 5632
GATE_SOFTCAP = 15.0
EPS = 1e-6
ADAM_LR, ADAM_B1, ADAM_B2, ADAM_EPS, ADAM_WD, ADAM_STEP = 0.001, 0.9, 0.999, 1e-8, 0.01, 10
MESH = pl.DeviceIdType.MESH
VMEM_LIMIT = 56 * 1024 * 1024


def _dg(a, b, ca, cb):
    return lax.dot_general(a.astype(BF16), b.astype(BF16), (((ca,), (cb,)), ((), ())), preferred_element_type=F32)


@jax.custom_vjp
def _mm(a, b):
    return _dg(a, b, 1, 0)


_mm.defvjp(lambda a, b: (_dg(a, b, 1, 0), (a, b)),
           lambda r, g: (_dg(g, r[1], 1, 1), _dg(r[0], g, 0, 0)))


@jax.custom_vjp
def _mm_nt(a, b):
    return _dg(a, b, 1, 1)


_mm_nt.defvjp(lambda a, b: (_dg(a, b, 1, 1), (a, b)),
              lambda r, g: (_dg(g, r[1], 1, 0), _dg(g, r[0], 0, 0)))


@jax.custom_vjp
def _mm_tn(a, b):
    return _dg(a, b, 0, 0)


_mm_tn.defvjp(lambda a, b: (_dg(a, b, 0, 0), (a, b)),
              lambda r, g: (_dg(r[1], g, 1, 1), _dg(r[0], g, 1, 0)))


def _tri(lower):
    r = lax.broadcasted_iota(jnp.int32, (CHUNK, CHUNK), 0)
    c = lax.broadcasted_iota(jnp.int32, (CHUNK, CHUNK), 1)
    return ((r >= c) if lower else (r <= c)).astype(F32)


def _dg32(t, x):
    return lax.dot_general(t, x, (((1,), (0,)), ((), ())), precision=lax.Precision.HIGHEST, preferred_element_type=F32)


@jax.custom_vjp
def _cumsum(x):
    return _dg32(_tri(True), x)


_cumsum.defvjp(lambda x: (_dg32(_tri(True), x), None), lambda r, g: (_dg32(_tri(False), g),))


_sigmoid = jax.nn.sigmoid


def _silu(x):
    return x * _sigmoid(x)


def _softplus(x):
    return jnp.maximum(x, 0.0) + jnp.log(1.0 + jnp.exp(-jnp.abs(x)))


def _lane_col(m, lane, h):
    return jnp.sum(jnp.where(lane == h, m, 0.0), axis=1, keepdims=True)


SSD_GPS = 4


def _ssd_math(convs, zs, smallb, S_in, dtb_row, alog_row, dskip, ng, g0):
    dt_all = _softplus(smallb + dtb_row)
    a_all = dt_all * (-jnp.exp(alog_row))
    prep = (dt_all, _cumsum(a_all), jnp.sum(a_all, axis=0, keepdims=True))
    ys, S_out = [], []
    for k in range(SSD_GPS):
        y, S = _ssd_group(convs[k][:, 0:512], convs[k][:, 512:640], convs[k][:, 640:768], zs[k], prep, S_in[k],
                          dskip[:, 512 * k:512 * (k + 1)], ng[:, 512 * k:512 * (k + 1)], g0 + k)
        ys.append(y)
        S_out.append(S)
    return tuple(ys), tuple(S_out)


def _ssd_group(cx, cB, cC, z, prep, S_in, dskip, ng, g):
    dt_all, acum_all, alast_all = prep
    lane = lax.broadcasted_iota(jnp.int32, (1, CHUNK), 1)
    row_i = lax.broadcasted_iota(jnp.int32, (CHUNK, CHUNK), 0)
    col_i = lax.broadcasted_iota(jnp.int32, (CHUNK, CHUNK), 1)
    causal = row_i >= col_i
    half = lane < 64
    rhalf = lax.broadcasted_iota(jnp.int32, (CHUNK, 1), 0) < 64
    xs, Bm, Cm = _silu(cx), _silu(cB), _silu(cC)
    cb = _mm_nt(Cm, Bm)

    def lmat(ac):
        acb = jnp.broadcast_to(ac, (CHUNK, CHUNK))
        return jnp.exp(jnp.where(causal, acb - acb.T, -jnp.inf))

    ys, S_out = [], []
    for j in range(4):
        h0 = 8 * g + 2 * j
        ac0, ac1 = _lane_col(acum_all, lane, h0), _lane_col(acum_all, lane, h0 + 1)
        dt0, dt1 = _lane_col(dt_all, lane, h0), _lane_col(dt_all, lane, h0 + 1)
        al0, al1 = _lane_col(alast_all, lane, h0), _lane_col(alast_all, lane, h0 + 1)
        Xp = xs[:, 128 * j:128 * (j + 1)]
        Xd = Xp * jnp.where(half, dt0, dt1)
        ac_sel = jnp.where(half, ac0, ac1)
        al_sel = jnp.where(half, al0, al1)
        Yd = jnp.where(half, _mm(cb * lmat(ac0), Xd), _mm(cb * lmat(ac1), Xd))
        Yoff = _mm_nt(Cm, S_in[j]) * jnp.exp(ac_sel)
        ys.append(Yd + Yoff + dskip[:, 128 * j:128 * (j + 1)] * Xp)
        S_new = _mm_tn(Xd * jnp.exp(al_sel - ac_sel), Bm)
        S_out.append(S_in[j] * jnp.exp(jnp.where(rhalf, al0, al1)) + S_new)
    y = jnp.concatenate(ys, axis=1)
    y = y * _silu(z)
    y = y * lax.rsqrt(jnp.mean(y * y, axis=1, keepdims=True) + EPS) * ng
    return y, tuple(S_out)


def _ssd_conv(blk_ref, halo_ref, cw_ref, cb_ref, ext, first, k):
    cols = slice(SSD_GROUP_COLS * k + 512, SSD_GROUP_COLS * (k + 1))
    halo = halo_ref[:, cols]
    ext[k, 0:8, :] = jnp.where(first, jnp.zeros_like(halo), halo)
    ext[k, 8:8 + CHUNK, :] = blk_ref[:, cols]
    conv = jnp.broadcast_to(cb_ref[k], (CHUNK, SSD_XBC_COLS))
    for tap in range(SSD_CONV):
        conv = conv + cw_ref[k, tap:tap + 1, :] * ext[k, pl.ds(5 + tap, CHUNK), :]
    return conv


def _ssd_specs(nc, rev):
    cc = (lambda c: nc - 1 - c) if rev else (lambda c: c)
    width = SSD_GPS * SSD_GROUP_COLS
    return [
        pl.BlockSpec((CHUNK, width), lambda c, g: (cc(c), g)),
        pl.BlockSpec((8, width), lambda c, g: (jnp.maximum(cc(c) * (CHUNK // 8) - 1, 0), g)),
        pl.BlockSpec((CHUNK, SMALL_COLS), lambda c, g: (cc(c), 0)),
        pl.BlockSpec((SSD_GPS, SSD_CONV, SSD_XBC_COLS), lambda c, g: (g, 0, 0)),
        pl.BlockSpec((SSD_GPS, 1, SSD_XBC_COLS), lambda c, g: (g, 0, 0)),
        pl.BlockSpec((1, SMALL_COLS), lambda c, g: (0, 0)),
        pl.BlockSpec((1, SMALL_COLS), lambda c, g: (0, 0)),
        pl.BlockSpec((1, SSD_GPS * 512), lambda c, g: (0, g)),
        pl.BlockSpec((1, SSD_GPS * 512), lambda c, g: (0, g)),
    ]


def _ssd_forward(proj, small, cw, cb, dtb, alog, dskip, ng):
    L = proj.shape[0]
    nc = L // CHUNK

    def body(blk_ref, halo_ref, small_ref, cw_ref, cb_ref, dtb_ref, alog_ref, dskip_ref, ng_ref, y_ref, st_ref, carry, ext):
        c, g0 = pl.program_id(0), pl.program_id(1) * SSD_GPS

        @pl.when(c == 0)
        def _():
            for k in range(SSD_GPS):
                carry[g0 + k] = jnp.zeros((4, CHUNK, CHUNK), F32)

        convs = [_ssd_conv(blk_ref, halo_ref, cw_ref, cb_ref, ext, c == 0, k) for k in range(SSD_GPS)]
        zs = [blk_ref[:, SSD_GROUP_COLS * k:SSD_GROUP_COLS * k + 512] for k in range(SSD_GPS)]
        S_in = tuple(tuple(carry[g0 + k, j] for j in range(4)) for k in range(SSD_GPS))
        for k in range(SSD_GPS):
            st_ref[0, k] = carry[g0 + k]
        ys, S_out = _ssd_math(convs, zs, small_ref[...], S_in, dtb_ref[...], alog_ref[...], dskip_ref[...], ng_ref[...], g0)
        for k in range(SSD_GPS):
            y_ref[:, 512 * k:512 * (k + 1)] = ys[k].astype(BF16)
            for j in range(4):
                carry[g0 + k, j] = S_out[k][j]

    return pl.pallas_call(
        body, name="ssd_fwd", grid=(nc, SSD_GROUPS // SSD_GPS),
        in_specs=_ssd_specs(nc, False),
        out_specs=[pl.BlockSpec((CHUNK, SSD_GPS * 512), lambda c, g: (c, g)),
                   pl.BlockSpec((1, SSD_GPS, 4, CHUNK, CHUNK), lambda c, g: (c, g, 0, 0, 0))],
        out_shape=[jax.ShapeDtypeStruct((L, SSD_WIDTH), BF16), jax.ShapeDtypeStruct((nc, SSD_GROUPS, 4, CHUNK, CHUNK), F32)],
        scratch_shapes=[pltpu.VMEM((SSD_GROUPS, 4, CHUNK, CHUNK), F32), pltpu.VMEM((SSD_GPS, 8 + CHUNK, SSD_XBC_COLS), F32)],
        compiler_params=pltpu.CompilerParams(dimension_semantics=("arbitrary", "arbitrary"), vmem_limit_bytes=VMEM_LIMIT),
    )(proj, proj, small, cw, cb, dtb, alog, dskip, ng)


def _ssd_backward(proj, small, cw, cb, dtb, alog, dskip, ng, states, dy, side=None):
    L = proj.shape[0]
    nc = L // CHUNK
    s_in, s_out = (len(side.operands), len(side.out_shapes)) if side is not None else (0, 0)
    grid = (nc, SSD_GROUPS // SSD_GPS)

    def body(*refs):
        blk_ref, halo_ref, small_ref, cw_ref, cb_ref, dtb_ref, alog_ref, dskip_ref, ng_ref, st_ref, dy_ref = refs[:11]
        dproj_ref, dsmall_ref, dcw_ref, dcb_ref, ddtb_ref, dalog_ref, ddskip_ref, dng_ref = refs[11 + s_in:19 + s_in]
        dcarry, nxt, ext, dext = refs[19 + s_in + s_out:23 + s_in + s_out]
        side_refs = (refs[11:11 + s_in], refs[19 + s_in:19 + s_in + s_out], refs[23 + s_in + s_out:])
        s, gb = pl.program_id(0), pl.program_id(1)
        g0 = gb * SSD_GPS
        c = nc - 1 - s
        if side is not None:
            pl.when(_grid_edge(grid, first=True))(lambda: side.start(*side_refs))

        @pl.when(s == 0)
        def _():
            for k in range(SSD_GPS):
                dcarry[g0 + k] = jnp.zeros((4, CHUNK, CHUNK), F32)
                nxt[g0 + k] = jnp.zeros((8, SSD_XBC_COLS), F32)
                dcw_ref[g0 + k] = jnp.zeros((SSD_CONV, SSD_XBC_COLS), F32)
                dcb_ref[g0 + k] = jnp.zeros((1, SSD_XBC_COLS), F32)
                ddskip_ref[g0 + k] = jnp.zeros((1, 512), F32)
                dng_ref[g0 + k] = jnp.zeros((1, 512), F32)

        @pl.when((s == 0) & (gb == 0))
        def _():
            ddtb_ref[...] = jnp.zeros((1, SMALL_COLS), F32)
            dalog_ref[...] = jnp.zeros((1, SMALL_COLS), F32)

        convs = [_ssd_conv(blk_ref, halo_ref, cw_ref, cb_ref, ext, c == 0, k) for k in range(SSD_GPS)]
        zs = [blk_ref[:, SSD_GROUP_COLS * k:SSD_GROUP_COLS * k + 512] for k in range(SSD_GPS)]
        S_in = tuple(tuple(st_ref[0, k, j] for j in range(4)) for k in range(SSD_GPS))
        dS_out = tuple(tuple(dcarry[g0 + k, j] for j in range(4)) for k in range(SSD_GPS))
        dys = tuple(dy_ref[:, 512 * k:512 * (k + 1)].astype(F32) for k in range(SSD_GPS))
        _, vjp = jax.vjp(functools.partial(_ssd_math, g0=g0), convs, zs, small_ref[...], S_in, dtb_ref[...], alog_ref[...],
                         dskip_ref[...], ng_ref[...])
        d_convs, d_zs, d_small, dS_in, d_dtb, d_alog, d_dskip, d_ng = vjp((dys, dS_out))
        for k in range(SSD_GPS):
            for j in range(4):
                dcarry[g0 + k, j] = dS_in[k][j]
            dext[k, 0:8, :] = jnp.zeros((8, SSD_XBC_COLS), F32)
            dext[k, 8:8 + CHUNK, :] = d_convs[k]
            dext[k, 8 + CHUNK:16 + CHUNK, :] = nxt[g0 + k]
            nxt[g0 + k] = dext[k, 8:16, :]
            dconv = d_convs[k]
            d_xbc = jnp.zeros((CHUNK, SSD_XBC_COLS), F32)
            for tap in range(SSD_CONV):
                d_xbc = d_xbc + cw_ref[k, tap:tap + 1, :] * dext[k, pl.ds(8 + 3 - tap, CHUNK), :]
                dcw_ref[g0 + k, tap:tap + 1, :] += jnp.sum(dconv * ext[k, pl.ds(5 + tap, CHUNK), :], axis=0, keepdims=True)
            dcb_ref[g0 + k] += jnp.sum(dconv, axis=0, keepdims=True)
            dproj_ref[:, SSD_GROUP_COLS * k:SSD_GROUP_COLS * k + 512] = d_zs[k].astype(BF16)
            dproj_ref[:, SSD_GROUP_COLS * k + 512:SSD_GROUP_COLS * (k + 1)] = d_xbc.astype(BF16)
            ddskip_ref[g0 + k] += d_dskip[:, 512 * k:512 * (k + 1)]
            dng_ref[g0 + k] += d_ng[:, 512 * k:512 * (k + 1)]

        @pl.when(gb == 0)
        def _():
            dsmall_ref[...] = d_small

        @pl.when(gb != 0)
        def _():
            dsmall_ref[...] += d_small

        ddtb_ref[...] += d_dtb
        dalog_ref[...] += d_alog
        if side is not None:
            pl.when(_grid_edge(grid, first=False))(lambda: side.finish(*side_refs))

    whole = lambda shape: pl.BlockSpec(shape, lambda s, g: (0,) * len(shape))
    side_ops = tuple(side.operands) if side is not None else ()
    res = pl.pallas_call(
        body, name="ssd_bwd", grid=grid,
        in_specs=_ssd_specs(nc, True) + [
            pl.BlockSpec((1, SSD_GPS, 4, CHUNK, CHUNK), lambda s, g: (nc - 1 - s, g, 0, 0, 0)),
            pl.BlockSpec((CHUNK, SSD_GPS * 512), lambda s, g: (nc - 1 - s, g))] + [ANY] * s_in,
        out_specs=[pl.BlockSpec((CHUNK, SSD_GPS * SSD_GROUP_COLS), lambda s, g: (nc - 1 - s, g)),
                   pl.BlockSpec((CHUNK, SMALL_COLS), lambda s, g: (nc - 1 - s, 0)),
                   whole((SSD_GROUPS, SSD_CONV, SSD_XBC_COLS)), whole((SSD_GROUPS, 1, SSD_XBC_COLS)),
                   whole((1, SMALL_COLS)), whole((1, SMALL_COLS)),
                   whole((SSD_GROUPS, 1, 512)), whole((SSD_GROUPS, 1, 512))] + [ANY] * s_out,
        out_shape=[jax.ShapeDtypeStruct((L, SSD_GROUPS * SSD_GROUP_COLS), BF16), jax.ShapeDtypeStruct((L, SMALL_COLS), F32),
                   jax.ShapeDtypeStruct((SSD_GROUPS, SSD_CONV, SSD_XBC_COLS), F32),
                   jax.ShapeDtypeStruct((SSD_GROUPS, 1, SSD_XBC_COLS), F32),
                   jax.ShapeDtypeStruct((1, SMALL_COLS), F32), jax.ShapeDtypeStruct((1, SMALL_COLS), F32),
                   jax.ShapeDtypeStruct((SSD_GROUPS, 1, 512), F32), jax.ShapeDtypeStruct((SSD_GROUPS, 1, 512), F32)]
        + (list(side.out_shapes) if side is not None else []),
        scratch_shapes=[pltpu.VMEM((SSD_GROUPS, 4, CHUNK, CHUNK), F32), pltpu.VMEM((SSD_GROUPS, 8, SSD_XBC_COLS), F32),
                        pltpu.VMEM((SSD_GPS, 8 + CHUNK, SSD_XBC_COLS), F32),
                        pltpu.VMEM((SSD_GPS, 16 + CHUNK, SSD_XBC_COLS), F32)]
        + (list(side.sems) if side is not None else []),
        compiler_params=pltpu.CompilerParams(dimension_semantics=("arbitrary", "arbitrary"), vmem_limit_bytes=VMEM_LIMIT),
    )(proj, proj, small, cw, cb, dtb, alog, dskip, ng, states, dy, *side_ops)
    return res if side is None else (res[:8], res[8:])


ML_HPS = 8


def _mlstm_math(blks, smallb, C_in, n_in, m_in, ib_row, fb_row, ng, h0):
    li_all = GATE_SOFTCAP * jnp.tanh((smallb + ib_row) / GATE_SOFTCAP)
    lf_all = -_softplus(-(GATE_SOFTCAP * jnp.tanh((smallb + fb_row) / GATE_SOFTCAP)))
    prep = (li_all, lf_all, _cumsum(lf_all))
    out = [_mlstm_head(blks[k][:, 0:128], blks[k][:, 128:256], blks[k][:, 256:512], blks[k][:, 512:768], prep,
                       C_in[k], n_in[k], m_in[k], ng[:, ML_DV * k:ML_DV * (k + 1)], h0 + k) for k in range(ML_HPS)]
    return tuple(zip(*out))


def _mlstm_head(q, k, v, o_raw, prep, C_in, n_in, m_in_row, ng, h):
    li_all, lf_all, bcum_all = prep
    lane = lax.broadcasted_iota(jnp.int32, (1, CHUNK), 1)
    row_i = lax.broadcasted_iota(jnp.int32, (CHUNK, CHUNK), 0)
    col_i = lax.broadcasted_iota(jnp.int32, (CHUNK, CHUNK), 1)
    causal = row_i >= col_i
    kk = k * (ML_DK ** -0.5)
    li = _lane_col(li_all, lane, LANE_I + h)
    lf = _lane_col(lf_all, lane, LANE_F + h)
    bc = _lane_col(bcum_all, lane, LANE_F + h)
    b_last = jnp.sum(lf, axis=0, keepdims=True)
    m_in = _lane_col(m_in_row, lane, 0)
    a = b_last - bc + li
    m_loc = jnp.max(a, axis=0, keepdims=True)
    w = jnp.exp(a - m_loc)
    C_loc = _mm_tn(w * v, kk)
    n_loc = jnp.sum(w * kk, axis=0, keepdims=True)
    m_new = jnp.maximum(b_last + m_in, m_loc)
    s_old = jnp.exp(b_last + m_in - m_new)
    s_new = jnp.exp(m_loc - m_new)
    C_out = s_old * C_in + s_new * C_loc
    n_out = s_old * n_in + s_new * n_loc
    bc_b = jnp.broadcast_to(bc, (CHUNK, CHUNK))
    li_b = jnp.broadcast_to(li, (CHUNK, CHUNK))
    D = jnp.where(causal, bc_b - bc_b.T + li_b.T, -jnp.inf)
    m_intra = jnp.max(D, axis=1, keepdims=True)
    inter_log = bc + m_in
    m_t = jnp.maximum(inter_log, m_intra)
    S = _mm_nt(q, kk) * jnp.exp(D - m_t)
    w_inter = jnp.exp(inter_log - m_t)
    num = _mm(S, v) + w_inter * _mm_nt(q, C_in)
    nq = jnp.sum(S, axis=1, keepdims=True) + w_inter * jnp.sum(q * n_in, axis=1, keepdims=True)
    den = jnp.maximum(jnp.abs(nq), jnp.exp(-m_t))
    hh = num / den
    hh = hh * lax.rsqrt(jnp.mean(hh * hh, axis=1, keepdims=True) + EPS)
    hh = hh * ng * _sigmoid(o_raw)
    return hh, C_out, n_out, jnp.broadcast_to(m_new, (1, CHUNK))


def _ml_specs(nc, rev):
    cc = (lambda c: nc - 1 - c) if rev else (lambda c: c)
    return [
        pl.BlockSpec((CHUNK, ML_HPS * ML_HEAD_COLS), lambda c, h: (cc(c), h)),
        pl.BlockSpec((CHUNK, SMALL_COLS), lambda c, h: (cc(c), 0)),
        pl.BlockSpec((1, SMALL_COLS), lambda c, h: (0, 0)),
        pl.BlockSpec((1, SMALL_COLS), lambda c, h: (0, 0)),
        pl.BlockSpec((1, ML_HPS * ML_DV), lambda c, h: (0, h)),
    ]


def _mlstm_forward(proj, small, ib, fb, ng):
    L = proj.shape[0]
    nc = L // CHUNK

    def body(blk_ref, small_ref, ib_ref, fb_ref, ng_ref, y_ref, cst_ref, nm_ref, c_carry, nm_carry):
        c, h0 = pl.program_id(0), pl.program_id(1) * ML_HPS

        @pl.when(c == 0)
        def _():
            for k in range(ML_HPS):
                c_carry[h0 + k] = jnp.zeros((ML_DV, ML_DK), F32)
                nm_carry[h0 + k] = jnp.zeros((2, ML_DK), F32)

        for k in range(ML_HPS):
            cst_ref[0, k] = c_carry[h0 + k]
            nm_ref[0, k] = nm_carry[h0 + k]
        hh, C_out, n_out, m_out = _mlstm_math(
            [blk_ref[:, ML_HEAD_COLS * k:ML_HEAD_COLS * (k + 1)] for k in range(ML_HPS)], small_ref[...],
            [c_carry[h0 + k] for k in range(ML_HPS)], [nm_carry[h0 + k, 0:1, :] for k in range(ML_HPS)],
            [nm_carry[h0 + k, 1:2, :] for k in range(ML_HPS)], ib_ref[...], fb_ref[...], ng_ref[...], h0)
        for k in range(ML_HPS):
            y_ref[:, ML_DV * k:ML_DV * (k + 1)] = hh[k].astype(BF16)
            c_carry[h0 + k] = C_out[k]
            nm_carry[h0 + k, 0:1, :] = n_out[k]
            nm_carry[h0 + k, 1:2, :] = m_out[k]

    return pl.pallas_call(
        body, name="mlstm_fwd", grid=(nc, ML_HEADS // ML_HPS),
        in_specs=_ml_specs(nc, False),
        out_specs=[pl.BlockSpec((CHUNK, ML_HPS * ML_DV), lambda c, h: (c, h)),
                   pl.BlockSpec((1, ML_HPS, ML_DV, ML_DK), lambda c, h: (c, h, 0, 0)),
                   pl.BlockSpec((1, ML_HPS, 2, ML_DK), lambda c, h: (c, h, 0, 0))],
        out_shape=[jax.ShapeDtypeStruct((L, ML_HEADS * ML_DV), BF16),
                   jax.ShapeDtypeStruct((nc, ML_HEADS, ML_DV, ML_DK), F32),
                   jax.ShapeDtypeStruct((nc, ML_HEADS, 2, ML_DK), F32)],
        scratch_shapes=[pltpu.VMEM((ML_HEADS, ML_DV, ML_DK), F32), pltpu.VMEM((ML_HEADS, 2, ML_DK), F32)],
        compiler_params=pltpu.CompilerParams(dimension_semantics=("arbitrary", "arbitrary"), vmem_limit_bytes=VMEM_LIMIT),
    )(proj, small, ib, fb, ng)


def _mlstm_backward(proj, small, ib, fb, ng, cst, nmst, dy, dsmall_in):
    L = proj.shape[0]
    nc = L // CHUNK

    def body(blk_ref, small_ref, ib_ref, fb_ref, ng_ref, cst_ref, nm_ref, dy_ref, dsin_ref,
             dproj_ref, dsmall_ref, dib_ref, dfb_ref, dng_ref, dc_carry, dnm_carry):
        s, hb = pl.program_id(0), pl.program_id(1)
        h0 = hb * ML_HPS

        @pl.when(s == 0)
        def _():
            for k in range(ML_HPS):
                dc_carry[h0 + k] = jnp.zeros((ML_DV, ML_DK), F32)
                dnm_carry[h0 + k] = jnp.zeros((2, ML_DK), F32)
                dng_ref[h0 + k] = jnp.zeros((1, ML_DV), F32)

        @pl.when((s == 0) & (hb == 0))
        def _():
            dib_ref[...] = jnp.zeros((1, SMALL_COLS), F32)
            dfb_ref[...] = jnp.zeros((1, SMALL_COLS), F32)

        heads = range(ML_HPS)
        _, vjp = jax.vjp(functools.partial(_mlstm_math, h0=h0),
                         [blk_ref[:, ML_HEAD_COLS * k:ML_HEAD_COLS * (k + 1)] for k in heads], small_ref[...],
                         [cst_ref[0, k] for k in heads], [nm_ref[0, k, 0:1, :] for k in heads],
                         [nm_ref[0, k, 1:2, :] for k in heads], ib_ref[...], fb_ref[...], ng_ref[...])
        d_blk, d_small, dC, dn, dm, d_ib, d_fb, d_ng = vjp(
            (tuple(dy_ref[:, ML_DV * k:ML_DV * (k + 1)].astype(F32) for k in heads),
             tuple(dc_carry[h0 + k] for k in heads), tuple(dnm_carry[h0 + k, 0:1, :] for k in heads),
             tuple(dnm_carry[h0 + k, 1:2, :] for k in heads)))
        for k in heads:
            dc_carry[h0 + k] = dC[k]
            dnm_carry[h0 + k, 0:1, :] = dn[k]
            dnm_carry[h0 + k, 1:2, :] = dm[k]
            dproj_ref[:, ML_HEAD_COLS * k:ML_HEAD_COLS * (k + 1)] = d_blk[k].astype(BF16)
            dng_ref[h0 + k] += d_ng[:, ML_DV * k:ML_DV * (k + 1)]

        @pl.when(hb == 0)
        def _():
            dsmall_ref[...] = dsin_ref[...] + d_small

        @pl.when(hb != 0)
        def _():
            dsmall_ref[...] += d_small

        dib_ref[...] += d_ib
        dfb_ref[...] += d_fb

    whole = lambda shape: pl.BlockSpec(shape, lambda s, h: (0,) * len(shape))
    return pl.pallas_call(
        body, name="mlstm_bwd", grid=(nc, ML_HEADS // ML_HPS),
        in_specs=_ml_specs(nc, True) + [
            pl.BlockSpec((1, ML_HPS, ML_DV, ML_DK), lambda s, h: (nc - 1 - s, h, 0, 0)),
            pl.BlockSpec((1, ML_HPS, 2, ML_DK), lambda s, h: (nc - 1 - s, h, 0, 0)),
            pl.BlockSpec((CHUNK, ML_HPS * ML_DV), lambda s, h: (nc - 1 - s, SSD_WIDTH // (ML_HPS * ML_DV) + h)),
            pl.BlockSpec((CHUNK, SMALL_COLS), lambda s, h: (nc - 1 - s, 0))],
        out_specs=[pl.BlockSpec((CHUNK, ML_HPS * ML_HEAD_COLS), lambda s, h: (nc - 1 - s, h)),
                   pl.BlockSpec((CHUNK, SMALL_COLS), lambda s, h: (nc - 1 - s, 0)),
                   whole((1, SMALL_COLS)), whole((1, SMALL_COLS)), whole((ML_HEADS, 1, ML_DV))],
        out_shape=[jax.ShapeDtypeStruct((L, ML_HEADS * ML_HEAD_COLS), BF16), jax.ShapeDtypeStruct((L, SMALL_COLS), F32),
                   jax.ShapeDtypeStruct((1, SMALL_COLS), F32), jax.ShapeDtypeStruct((1, SMALL_COLS), F32),
                   jax.ShapeDtypeStruct((ML_HEADS, 1, ML_DV), F32)],
        scratch_shapes=[pltpu.VMEM((ML_HEADS, ML_DV, ML_DK), F32), pltpu.VMEM((ML_HEADS, 2, ML_DK), F32)],
        compiler_params=pltpu.CompilerParams(dimension_semantics=("arbitrary", "arbitrary"), vmem_limit_bytes=VMEM_LIMIT),
    )(proj, small, ib, fb, ng, cst, nmst, dy, dsmall_in)


_OFF_Z, _OFF_X, _OFF_B, _OFF_C, _OFF_DT = 0, 2048, 4096, 4608, 5120
_OFF_Q, _OFF_K, _OFF_V, _OFF_O, _OFF_I, _OFF_F, IN_WIDTH = 5152, 6176, 7200, 9248, 11296, 11304, 11312


def _ssd_segments():
    segs = []
    for g in range(SSD_GROUPS):
        segs += [(_OFF_Z + 512 * g, 512), (_OFF_X + 512 * g, 512), (_OFF_B + 128 * g, 128), (_OFF_C + 128 * g, 128)]
    return segs


def _ml_segments():
    segs = []
    for h in range(ML_HEADS):
        segs += [(_OFF_Q + 128 * h, 128), (_OFF_K + 128 * h, 128), (_OFF_V + 256 * h, 256), (_OFF_O + 256 * h, 256)]
    return segs


_SMALL_SEGMENTS = [(_OFF_DT, 32), (_OFF_I, 8), (_OFF_F, 8)]


SHARD_IN = IN_WIDTH // 4


def _take_cols(slabs, segs):
    parts = []
    for s, n in segs:
        while n > 0:
            k, lo = divmod(s, SHARD_IN)
            m = min(n, SHARD_IN - lo)
            parts.append(slabs[k][:, lo:lo + m])
            s, n = s + m, n - m
    return jnp.concatenate(parts, axis=1)


def _split_w_in(slabs):
    small = _take_cols(slabs, _SMALL_SEGMENTS)
    small = jnp.concatenate([small, jnp.zeros((small.shape[0], SMALL_COLS - small.shape[1]), small.dtype)], axis=1)
    return _take_cols(slabs, _ssd_segments()), _take_cols(slabs, _ml_segments()), small


def _merge_w_in(d_ssd, d_ml, d_small):
    pieces = []
    for arr, segs in ((d_ssd, _ssd_segments()), (d_ml, _ml_segments()), (d_small, _SMALL_SEGMENTS)):
        pos = 0
        for s, n in segs:
            while n > 0:
                lo = s % SHARD_IN
                m = min(n, SHARD_IN - lo)
                pieces.append((s, arr[:, pos:pos + m]))
                s, n, pos = s + m, n - m, pos + m
    pieces.sort(key=lambda t: t[0])
    return [jnp.concatenate([p for s, p in pieces if s // SHARD_IN == k], axis=1) for k in range(4)]


def _conv_to_groups(cw):
    return jnp.stack([jnp.concatenate([cw[:, 512 * g:512 * (g + 1)], cw[:, 2048 + 128 * g:2048 + 128 * (g + 1)],
                                       cw[:, 2560 + 128 * g:2560 + 128 * (g + 1)]], axis=1) for g in range(SSD_GROUPS)])


def _conv_from_groups(d):
    return jnp.concatenate([d[g, :, 0:512] for g in range(SSD_GROUPS)] + [d[g, :, 512:640] for g in range(SSD_GROUPS)]
                           + [d[g, :, 640:768] for g in range(SSD_GROUPS)], axis=1)


def _small_row(vec, lane0):
    n = vec.shape[1]
    return jnp.concatenate([jnp.zeros((1, lane0), F32), vec, jnp.zeros((1, SMALL_COLS - lane0 - n), F32)], axis=1)


class _Side:
    def __init__(self, operands, out_shapes, sems, start, finish):
        self.operands, self.out_shapes, self.sems, self.start, self.finish = operands, out_shapes, sems, start, finish


def _grid_edge(grid, first):
    hit = None
    for axis, n in enumerate(grid):
        here = pl.program_id(axis) == (0 if first else n - 1)
        hit = here if hit is None else hit & here
    return hit


def _matmul(a, b, mode, out_dtype, tm, tn, tk, name, addend=None, layout=None, side=None):
    M, Kd = (a.shape[1], a.shape[0]) if mode == "tn" else a.shape
    N = b.shape[0] if mode == "nt" else b.shape[1]
    if layout == "cols4":
        tm, tn = M // 2, N // 4
    elif layout == "rows4":
        tm = M // 4
    tm, tn, tk = min(tm, M), min(tn, N), min(tk, Kd)
    if mode == "nn":
        a_spec = pl.BlockSpec((tm, tk), lambda i, j, k: (i, k))
        b_spec = pl.BlockSpec((tk, tn), lambda i, j, k: (k, j))
        dims = (((1,), (0,)), ((), ()))
    elif mode == "nt":
        a_spec = pl.BlockSpec((tm, tk), lambda i, j, k: (i, k))
        b_spec = pl.BlockSpec((tn, tk), lambda i, j, k: (j, k))
        dims = (((1,), (1,)), ((), ()))
    else:
        a_spec = pl.BlockSpec((tk, tm), lambda i, j, k: (k, i))
        b_spec = pl.BlockSpec((tk, tn), lambda i, j, k: (k, j))
        dims = (((0,), (0,)), ((), ()))
    assert M % tm == 0 and N % tn == 0 and Kd % tk == 0, (name, M, N, Kd, tm, tn, tk)
    nk = Kd // tk
    has_add = addend is not None
    n_in = 2 + has_add
    s_in, s_out = (len(side.operands), len(side.out_shapes)) if side is not None else (0, 0)
    grid = (M // tm, N // tn, nk)

    def body(*refs):
        a_ref, b_ref = refs[0], refs[1]
        add_ref = refs[2] if has_add else None
        o_ref, acc_ref = refs[n_in + s_in], refs[n_in + s_in + 1 + s_out]
        k = pl.program_id(2)
        side_refs = (refs[n_in:n_in + s_in], refs[n_in + s_in + 1:n_in + s_in + 1 + s_out], refs[n_in + s_in + 2 + s_out:])
        if side is not None:
            pl.when(_grid_edge(grid, first=True))(lambda: side.start(*side_refs))
        part = lax.dot_general(a_ref[...].astype(BF16), b_ref[...].astype(BF16), dims, preferred_element_type=F32)

        @pl.when(k == 0)
        def _():
            acc_ref[...] = part

        @pl.when(k != 0)
        def _():
            acc_ref[...] += part

        @pl.when(k == nk - 1)
        def _():
            r = acc_ref[...]
            if has_add:
                r = r + add_ref[...].astype(F32)
            if layout == "cols4":
                o_ref[0, 0] = r.astype(out_dtype)
            elif layout == "rows4":
                o_ref[0, 0] = r[:tm // 2].astype(out_dtype)
                o_ref[1, 0] = r[tm // 2:].astype(out_dtype)
            else:
                o_ref[...] = r.astype(out_dtype)

        if side is not None:
            pl.when(_grid_edge(grid, first=False))(lambda: side.finish(*side_refs))

    o_spec = pl.BlockSpec((tm, tn), lambda i, j, k: (i, j))
    out_spec, out_dims = o_spec, (M, N)
    if layout == "cols4":
        out_spec, out_dims = pl.BlockSpec((1, 1, tm, tn), lambda i, j, k: (i, j, 0, 0)), (2, 4, tm, tn)
    elif layout == "rows4":
        out_spec, out_dims = pl.BlockSpec((2, 1, tm // 2, tn), lambda i, j, k: (0, i, 0, j)), (2, 4, tm // 2, N)
    out_shape = jax.ShapeDtypeStruct(out_dims, out_dtype)
    semantics = ("parallel", "parallel", "arbitrary")
    if side is None:
        in_specs, operands, out_specs, scratch = [], (), out_spec, []
    else:
        in_specs, operands, scratch = [ANY] * s_in, tuple(side.operands), list(side.sems)
        out_specs, out_shape = [out_spec] + [ANY] * s_out, [out_shape] + list(side.out_shapes)
        semantics = ("arbitrary",) * 3
    res = pl.pallas_call(
        body, name=name, grid=grid,
        in_specs=[a_spec, b_spec] + ([o_spec] if has_add else []) + in_specs, out_specs=out_specs,
        out_shape=out_shape, scratch_shapes=[pltpu.VMEM((tm, tn), F32)] + scratch,
        compiler_params=pltpu.CompilerParams(dimension_semantics=semantics, vmem_limit_bytes=VMEM_LIMIT),
    )(*((a, b) + ((addend,) if has_add else ()) + operands))
    return res if side is None else (res[0], res[1:])


ROW_TILE = 256
NORM_TILE = 512


def _rmsnorm_fwd(x, g, name, side=None):
    L, D = x.shape
    tr = min(NORM_TILE, L)
    s_in, s_out = (len(side.operands), len(side.out_shapes)) if side is not None else (0, 0)
    grid = (L // tr,)

    def body(*refs):
        x_ref, g_ref, u_ref = refs[0], refs[1], refs[2 + s_in]
        side_refs = (refs[2:2 + s_in], refs[3 + s_in:3 + s_in + s_out], refs[3 + s_in + s_out:])
        if side is not None:
            pl.when(_grid_edge(grid, first=True))(lambda: side.start(*side_refs))
        xv = x_ref[...]
        r = lax.rsqrt(jnp.mean(xv * xv, axis=1, keepdims=True) + EPS)
        u_ref[...] = (xv * r * g_ref[...]).astype(BF16)
        if side is not None:
            pl.when(_grid_edge(grid, first=False))(lambda: side.finish(*side_refs))

    row = pl.BlockSpec((tr, D), lambda i: (i, 0))
    res = pl.pallas_call(
        body, name=name, grid=grid,
        in_specs=[row, pl.BlockSpec((1, D), lambda i: (0, 0))] + [ANY] * s_in, out_specs=[row] + [ANY] * s_out,
        out_shape=[jax.ShapeDtypeStruct((L, D), BF16)] + (list(side.out_shapes) if side is not None else []),
        scratch_shapes=list(side.sems) if side is not None else [],
        compiler_params=pltpu.CompilerParams(dimension_semantics=("arbitrary",), vmem_limit_bytes=VMEM_LIMIT),
    )(x, g, *(side.operands if side is not None else ()))
    return res[0] if side is None else (res[0], res[1:])


def _rmsnorm_bwd(du, x, g, dres, name):
    L, D = x.shape

    def body(du_ref, x_ref, g_ref, dres_ref, dx_ref, dxb_ref, dg_ref):
        i = pl.program_id(0)
        xv, duv = x_ref[...], du_ref[...]
        r = lax.rsqrt(jnp.mean(xv * xv, axis=1, keepdims=True) + EPS)
        t = duv * g_ref[...]
        dx = dres_ref[...] + r * t - xv * (r * r * r) * jnp.mean(t * xv, axis=1, keepdims=True)
        dx_ref[...] = dx
        dxb_ref[...] = dx.astype(BF16)
        dg = jnp.sum(duv * xv * r, axis=0, keepdims=True)

        @pl.when(i == 0)
        def _():
            dg_ref[...] = dg

        @pl.when(i != 0)
        def _():
            dg_ref[...] += dg

    row = pl.BlockSpec((ROW_TILE, D), lambda i: (i, 0))
    vec = pl.BlockSpec((1, D), lambda i: (0, 0))
    return pl.pallas_call(
        body, name=name, grid=(L // ROW_TILE,),
        in_specs=[row, row, vec, row], out_specs=[row, row, vec],
        out_shape=[jax.ShapeDtypeStruct((L, D), F32), jax.ShapeDtypeStruct((L, D), BF16), jax.ShapeDtypeStruct((1, D), F32)],
        compiler_params=pltpu.CompilerParams(dimension_semantics=("arbitrary",), vmem_limit_bytes=VMEM_LIMIT),
    )(du, x, g, dres)


def _loss_head(h, target, g):
    L, D = h.shape

    def body(h_ref, t_ref, g_ref, dh_ref, dhb_ref, loss_ref, dg_ref):
        i = pl.program_id(0)
        hv = h_ref[...]
        r = lax.rsqrt(jnp.mean(hv * hv, axis=1, keepdims=True) + EPS)
        diff = hv * r * g_ref[...] - t_ref[...]
        part = 0.5 * jnp.sum(jnp.mean(diff * diff, axis=1, keepdims=True), axis=0, keepdims=True)
        dy = diff * (1.0 / D)
        t = dy * g_ref[...]
        dh = r * t - hv * (r * r * r) * jnp.mean(t * hv, axis=1, keepdims=True)
        dh_ref[...] = dh
        dhb_ref[...] = dh.astype(BF16)
        dg = jnp.sum(dy * hv * r, axis=0, keepdims=True)

        @pl.when(i == 0)
        def _():
            dg_ref[...] = dg
            loss_ref[...] = jnp.broadcast_to(part, (1, 128))

        @pl.when(i != 0)
        def _():
            dg_ref[...] += dg
            loss_ref[...] += jnp.broadcast_to(part, (1, 128))

    row = pl.BlockSpec((ROW_TILE, D), lambda i: (i, 0))
    vec = pl.BlockSpec((1, D), lambda i: (0, 0))
    return pl.pallas_call(
        body, name="loss_head", grid=(L // ROW_TILE,),
        in_specs=[row, row, vec], out_specs=[row, row, pl.BlockSpec((1, 128), lambda i: (0, 0)), vec],
        out_shape=[jax.ShapeDtypeStruct((L, D), F32), jax.ShapeDtypeStruct((L, D), BF16), jax.ShapeDtypeStruct((1, 128), F32),
                   jax.ShapeDtypeStruct((1, D), F32)],
        compiler_params=pltpu.CompilerParams(dimension_semantics=("arbitrary",), vmem_limit_bytes=VMEM_LIMIT),
    )(h, target, g)


FF_TILE = 1408


FF_COLS = 512


def _ffn_in(u, wg, wu, tm, side):
    L, D = u.shape
    F = wg.shape[1]
    tm = min(tm, L)
    s_in, s_out = len(side.operands), len(side.out_shapes)
    grid = (L // tm, F // FF_COLS)

    def body(*refs):
        u_ref, wg_ref, wu_ref = refs[:3]
        g_ref, up_ref, act_ref = refs[3 + s_in:6 + s_in]
        side_refs = (refs[3:3 + s_in], refs[6 + s_in:6 + s_in + s_out], refs[6 + s_in + s_out:])
        pl.when(_grid_edge(grid, first=True))(lambda: side.start(*side_refs))
        g = jnp.dot(u_ref[...], wg_ref[...], preferred_element_type=F32)
        v = jnp.dot(u_ref[...], wu_ref[...], preferred_element_type=F32)
        s = _sigmoid(g)
        g_ref[...] = (v * (s * (1.0 + g * (1.0 - s)))).astype(BF16)
        up_ref[...] = (g * s).astype(BF16)
        act_ref[...] = (g * s * v).astype(BF16)
        pl.when(_grid_edge(grid, first=False))(lambda: side.finish(*side_refs))

    w_spec = pl.BlockSpec((D, FF_COLS), lambda i, j: (0, j))
    o_spec = pl.BlockSpec((tm, FF_COLS), lambda i, j: (i, j))
    out = jax.ShapeDtypeStruct((L, F), BF16)
    res = pl.pallas_call(
        body, name="ffn_gate_up", grid=grid,
        in_specs=[pl.BlockSpec((tm, D), lambda i, j: (i, 0)), w_spec, w_spec] + [ANY] * s_in,
        out_specs=[o_spec] * 3 + [ANY] * s_out, out_shape=[out] * 3 + list(side.out_shapes),
        scratch_shapes=list(side.sems),
        compiler_params=pltpu.CompilerParams(dimension_semantics=("arbitrary", "arbitrary"), vmem_limit_bytes=VMEM_LIMIT),
    )(u, wg, wu, *side.operands)
    return res[0], res[1], res[2], res[3:]


def _ffn_back(dh, w_down, gate_factor, up_factor, tm):
    L, D = dh.shape
    F = w_down.shape[0]
    tm = min(tm, L)

    def body(dh_ref, w_ref, g_ref, u_ref, dg_ref, du_ref):
        d = lax.dot_general(dh_ref[...], w_ref[...], (((1,), (1,)), ((), ())), preferred_element_type=F32)
        dg_ref[...] = (d * g_ref[...].astype(F32)).astype(BF16)
        du_ref[...] = (d * u_ref[...].astype(F32)).astype(BF16)

    blk = pl.BlockSpec((tm, FF_COLS), lambda i, j: (i, j))
    out = jax.ShapeDtypeStruct((L, F), BF16)
    return pl.pallas_call(
        body, name="ffn_back", grid=(L // tm, F // FF_COLS),
        in_specs=[pl.BlockSpec((tm, D), lambda i, j: (i, 0)), pl.BlockSpec((FF_COLS, D), lambda i, j: (j, 0)), blk, blk],
        out_specs=[blk, blk], out_shape=[out, out],
        compiler_params=pltpu.CompilerParams(dimension_semantics=("parallel", "parallel"), vmem_limit_bytes=VMEM_LIMIT),
    )(dh, w_down, gate_factor, up_factor)


def _adamw(w, g, m, v, rows, name):
    R, C = w.shape
    assert R % rows == 0, (name, R, rows)

    def body(w_ref, g_ref, m_ref, v_ref, d_ref, nm_ref, nv_ref):
        gv = g_ref[...]
        mn = ADAM_B1 * m_ref[...] + (1.0 - ADAM_B1) * gv
        vn = ADAM_B2 * v_ref[...] + (1.0 - ADAM_B2) * (gv * gv)
        m_hat = mn / (1.0 - ADAM_B1 ** ADAM_STEP)
        v_hat = vn / (1.0 - ADAM_B2 ** ADAM_STEP)
        d_ref[...] = -ADAM_LR * (m_hat / (jnp.sqrt(v_hat) + ADAM_EPS) + ADAM_WD * w_ref[...])
        nm_ref[...] = mn
        nv_ref[...] = vn

    blk = pl.BlockSpec((rows, C), lambda i: (i, 0))
    out = jax.ShapeDtypeStruct((R, C), F32)
    return pl.pallas_call(
        body, name=name, grid=(R // rows,), in_specs=[blk] * 4, out_specs=[blk] * 3, out_shape=[out] * 3,
        compiler_params=pltpu.CompilerParams(dimension_semantics=("parallel",), vmem_limit_bytes=VMEM_LIMIT),
    )(w, g, m, v)


ANY = pl.BlockSpec(memory_space=pl.ANY)
N_DEV = 8


def _allgather_small(p, name):
    R, C = p.shape

    def body(p_ref, out_ref, send_sems, recv_sems, local_sem):
        x, y, c = lax.axis_index("x"), lax.axis_index("y"), lax.axis_index("c")
        me = 4 * x + 2 * y + c
        mine = pltpu.make_async_copy(p_ref, out_ref.at[me], local_sem)
        mine.start()

        def peer(d):
            return (x ^ ((d >> 2) & 1), y ^ ((d >> 1) & 1), c ^ (d & 1))

        def copy(d, block):
            return pltpu.make_async_remote_copy(src_ref=p_ref, dst_ref=out_ref.at[block], send_sem=send_sems.at[d - 1],
                                                recv_sem=recv_sems.at[d - 1], device_id=peer(d), device_id_type=MESH)

        sends = [copy(d, me) for d in range(1, N_DEV)]
        for cp in sends:
            cp.start()
        for d in range(1, N_DEV):
            px, py, pc = peer(d)
            copy(d, 4 * px + 2 * py + pc).wait_recv()
        for cp in sends:
            cp.wait_send()
        mine.wait()

    return pl.pallas_call(
        body, name=name, out_shape=jax.ShapeDtypeStruct((N_DEV, R, C), p.dtype),
        in_specs=[pl.BlockSpec(memory_space=pltpu.VMEM)], out_specs=pl.BlockSpec(memory_space=pltpu.VMEM),
        scratch_shapes=[pltpu.SemaphoreType.DMA((N_DEV - 1,)), pltpu.SemaphoreType.DMA((N_DEV - 1,)), pltpu.SemaphoreType.DMA],
    )(p)


def _other_chips(x, y):
    return [(1 - x, y), (x, 1 - y), (1 - x, 1 - y)]


def _run_side(side, name):
    s_in, s_out = len(side.operands), len(side.out_shapes)

    def body(*refs):
        parts = (refs[:s_in], refs[s_in:s_in + s_out], refs[s_in + s_out:])
        side.start(*parts)
        side.finish(*parts)

    return pl.pallas_call(body, name=name, out_shape=list(side.out_shapes), in_specs=[ANY] * s_in, out_specs=[ANY] * s_out,
                          scratch_shapes=list(side.sems))(*side.operands)


def _gather_side(shards, sender_x=None):
    T = len(shards)
    halves = [s.shape[0] // 2 for s in shards]

    def tools(ins, outs, sems):
        send_sems, recv_sems = sems
        x, y, c = lax.axis_index("x"), lax.axis_index("y"), lax.axis_index("c")

        def rows(t, px, py, pc):
            return outs[t].at[2 * px + py if sender_x is None else py, pl.ds(pc * halves[t], halves[t]), :]

        def copy(t, k, block, to, own=False):
            src = ins[t].at[pl.ds(c * halves[t], halves[t]), :] if own else rows(t, *block)
            return pltpu.make_async_remote_copy(src_ref=src, dst_ref=rows(t, *block), send_sem=send_sems.at[t, k],
                                                recv_sem=recv_sems.at[t, k], device_id=to, device_id_type=MESH)

        sends = [None] * 4 if sender_x is None else [x == sender_x, x != sender_x, x == sender_x, x != sender_x]
        return (x, y, c), _other_chips(x, y), copy, sends

    def when(pred, fn):
        fn() if pred is None else pl.when(pred)(fn)

    def start(ins, outs, sems):
        (x, y, c), chips, copy, sends = tools(ins, outs, sems)

        def go():
            for t in range(T):
                for j, chip in enumerate(chips):
                    copy(t, j, (x, y, c), (*chip, c), own=True).start()

        when(sends[0], go)

    def finish(ins, outs, sems):
        (x, y, c), chips, copy, sends = tools(ins, outs, sems)
        for j, chip in enumerate(chips):
            def relay(j=j, chip=chip):
                for t in range(T):
                    copy(t, j, (*chip, c), (x, y, c)).wait_recv()
                    copy(t, 3 + j, (*chip, c), (x, y, 1 - c)).start()

            when(sends[1 + j], relay)
        for j, chip in enumerate(chips):
            def land(j=j, chip=chip):
                for t in range(T):
                    copy(t, 3 + j, (*chip, 1 - c), (x, y, c)).wait_recv()
                    copy(t, 3 + j, (*chip, c), (x, y, 1 - c)).wait_send()

            when(sends[1 + j], land)

        def sent():
            for t in range(T):
                for j, chip in enumerate(chips):
                    copy(t, j, (x, y, c), (*chip, c), own=True).wait_send()

        when(sends[0], sent)

    slots = 4 if sender_x is None else 3
    return _Side(shards, [jax.ShapeDtypeStruct((slots,) + s.shape, s.dtype) for s in shards],
                 [pltpu.SemaphoreType.DMA((T, 6)), pltpu.SemaphoreType.DMA((T, 6))], start, finish)


def _with_own_shard(gathered, shard, slot):
    return lax.dynamic_update_slice(gathered, shard[None], (slot,) + (0,) * shard.ndim)


def _pair_exchange(grads, name, chip_major=False):
    T = len(grads)

    def body(*refs):
        ins, outs = refs[:T], refs[T:2 * T]
        send_sems, recv_sems = refs[2 * T:]
        x, y, c = lax.axis_index("x"), lax.axis_index("y"), lax.axis_index("c")
        theirs = (lambda r: r.at[:, 1 - c]) if chip_major else (lambda r: r.at[1 - c])
        cps = [pltpu.make_async_remote_copy(src_ref=theirs(ins[t]), dst_ref=outs[t], send_sem=send_sems.at[t],
                                            recv_sem=recv_sems.at[t], device_id=(x, y, 1 - c), device_id_type=MESH)
               for t in range(T)]
        for cp in cps:
            cp.start()
        for cp in cps:
            cp.wait()

    return pl.pallas_call(
        body, name=name,
        out_shape=[jax.ShapeDtypeStruct((4,) + g.shape[2:], g.dtype) for g in grads],
        in_specs=[ANY] * T, out_specs=[ANY] * T,
        scratch_shapes=[pltpu.SemaphoreType.DMA((T,)), pltpu.SemaphoreType.DMA((T,))],
    )(*grads)


def _scatter_side(parts):
    T = len(parts)

    def tools(ins, outs, sems):
        send_sems, recv_sems = sems
        x, y, c = lax.axis_index("x"), lax.axis_index("y"), lax.axis_index("c")

        def copy(t, j, src_slot, dst_slot, chip):
            return pltpu.make_async_remote_copy(src_ref=ins[t].at[src_slot], dst_ref=outs[t].at[dst_slot],
                                                send_sem=send_sems.at[t, j], recv_sem=recv_sems.at[t, j],
                                                device_id=(*chip, c), device_id_type=MESH)

        return 2 * x + y, _other_chips(x, y), copy

    def start(ins, outs, sems):
        my_chip, chips, copy = tools(ins, outs, sems)
        for t in range(T):
            for j, (px, py) in enumerate(chips):
                copy(t, j, 2 * px + py, my_chip, (px, py)).start()

    def finish(ins, outs, sems):
        my_chip, chips, copy = tools(ins, outs, sems)
        for t in range(T):
            for j, (px, py) in enumerate(chips):
                copy(t, j, my_chip, 2 * px + py, (px, py)).wait_recv()
        for t in range(T):
            for j, (px, py) in enumerate(chips):
                copy(t, j, 2 * px + py, my_chip, (px, py)).wait_send()

    return _Side(parts, [jax.ShapeDtypeStruct(p.shape, p.dtype) for p in parts],
                 [pltpu.SemaphoreType.DMA((T, 3)), pltpu.SemaphoreType.DMA((T, 3))], start, finish)


def _pair_share(fulls):
    T = len(fulls)

    def body(*refs):
        outs = refs[T:2 * T]
        send_sems, recv_sems = refs[2 * T:]
        x, y, c = lax.axis_index("x"), lax.axis_index("y"), lax.axis_index("c")

        def copy(t, half):
            return pltpu.make_async_remote_copy(src_ref=outs[t].at[half], dst_ref=outs[t].at[half], send_sem=send_sems.at[t],
                                                recv_sem=recv_sems.at[t], device_id=(x, y, 1 - c), device_id_type=MESH)

        for t in range(T):
            copy(t, c).start()
        for t in range(T):
            copy(t, 1 - c).wait_recv()
        for t in range(T):
            copy(t, c).wait_send()

    return pl.pallas_call(
        body, name="grad_pair_share", out_shape=[jax.ShapeDtypeStruct(f.shape, f.dtype) for f in fulls],
        in_specs=[ANY] * T, out_specs=[ANY] * T, input_output_aliases={t: t for t in range(T)},
        scratch_shapes=[pltpu.SemaphoreType.DMA((T,)), pltpu.SemaphoreType.DMA((T,))],
    )(*fulls)


def _row_tile(hr):
    return next(r for r in (128, 64, 32, 16, 8) if hr % r == 0)


def _pair_add(g, got, c, name, chip_major=False):
    _, _, hr, C = g.shape
    tr = _row_tile(hr)
    own = (lambda k, i, c_ref: (k, c_ref[0], i, 0)) if chip_major else (lambda k, i, c_ref: (c_ref[0], k, i, 0))

    def body(c_ref, g_ref, r_ref, o_ref):
        o_ref[...] = (g_ref[0].astype(F32) + r_ref[...].astype(F32)).astype(o_ref.dtype)

    return pl.pallas_call(
        body, name=name,
        grid_spec=pltpu.PrefetchScalarGridSpec(
            num_scalar_prefetch=1, grid=(4, hr // tr),
            in_specs=[pl.BlockSpec((1, 1, tr, C), own),
                      pl.BlockSpec((1, tr, C), lambda k, i, c_ref: (k, i, 0))],
            out_specs=pl.BlockSpec((1, tr, C), lambda k, i, c_ref: (k, i, 0))),
        out_shape=jax.ShapeDtypeStruct(got.shape, g.dtype),
        compiler_params=pltpu.CompilerParams(dimension_semantics=("parallel", "parallel"), vmem_limit_bytes=VMEM_LIMIT),
    )(jnp.reshape(c, (1,)).astype(jnp.int32), g, got)


def _chip_sum(parts, slots, chip, c, name):
    _, hr, C = parts.shape
    tr = _row_tile(hr)

    def body(chip_ref, half_ref, own_ref, a_ref, b_ref, c_ref, o_ref):
        acc = own_ref[...].astype(F32)
        for r in (a_ref, b_ref, c_ref):
            acc = acc + r[...].astype(F32)
        o_ref[...] = acc

    other = lambda j: pl.BlockSpec((1, tr, C), lambda i, chip_ref, half_ref: ((chip_ref[0] + j) % 4, i, 0))
    return pl.pallas_call(
        body, name=name,
        grid_spec=pltpu.PrefetchScalarGridSpec(
            num_scalar_prefetch=2, grid=(hr // tr,),
            in_specs=[pl.BlockSpec((1, tr, C), lambda i, chip_ref, half_ref: (chip_ref[0], i, 0)),
                      other(1), other(2), other(3)],
            out_specs=pl.BlockSpec((1, tr, C), lambda i, chip_ref, half_ref: (half_ref[0], i, 0))),
        out_shape=jax.ShapeDtypeStruct((2, hr, C), F32),
        compiler_params=pltpu.CompilerParams(dimension_semantics=("parallel",), vmem_limit_bytes=VMEM_LIMIT),
    )(jnp.reshape(chip, (1,)).astype(jnp.int32), jnp.reshape(c, (1,)).astype(jnp.int32), parts, slots, slots, slots)


def _sum_slots(parts, name):
    n, hr, C = parts.shape
    tr = _row_tile(hr)

    def body(p_ref, o_ref):
        acc = p_ref[0].astype(F32)
        for k in range(1, n):
            acc = acc + p_ref[k].astype(F32)
        o_ref[...] = acc

    return pl.pallas_call(
        body, name=name, grid=(hr // tr,),
        in_specs=[pl.BlockSpec((n, tr, C), lambda i: (0, i, 0))], out_specs=pl.BlockSpec((tr, C), lambda i: (i, 0)),
        out_shape=jax.ShapeDtypeStruct((hr, C), F32),
        compiler_params=pltpu.CompilerParams(dimension_semantics=("parallel",), vmem_limit_bytes=VMEM_LIMIT),
    )(parts)


PACK_ROWS, PACK_COLS = 16, 3072
_PACK = [("norm_mix_g", 0, 1, 2048), ("conv_w", 1, 4, 3072), ("conv_b", 5, 1, 3072), ("dt_bias", 6, 1, 32),
         ("a_log", 7, 1, 32), ("d_skip", 8, 1, 32), ("ssd_norm_g", 9, 1, 2048), ("i_bias", 10, 1, 8), ("f_bias", 11, 1, 8),
         ("mlstm_norm_g", 12, 1, 2048), ("norm_ffn_g", 13, 1, 2048), ("final_norm_g", 14, 1, 2048), ("loss", 15, 1, 1)]
_WEIGHTS = ["norm_mix_g", "w_in", "conv_w", "conv_b", "dt_bias", "a_log", "d_skip", "ssd_norm_g", "i_bias", "f_bias",
            "mlstm_norm_g", "w_out", "norm_ffn_g", "w_gate", "w_up", "w_down", "final_norm_g"]
_BIG = ["w_in", "w_out", "w_gate", "w_up", "w_down"]


def _pack(vals):
    rows = []
    for name, _, r, w in _PACK:
        v = vals.get(name)
        v = jnp.zeros((r, w), F32) if v is None else v.reshape(r, w).astype(F32)
        rows.append(jnp.concatenate([v, jnp.zeros((r, PACK_COLS - w), F32)], axis=1) if w < PACK_COLS else v)
    return jnp.concatenate(rows, axis=0)


def _unpack(p):
    return {name: p[r0:r0 + r, :w] for name, r0, r, w in _PACK}


def kernel(x, norm_mix_g, w_in, conv_w, conv_b, dt_bias, a_log, d_skip, ssd_norm_g, i_bias, f_bias, mlstm_norm_g, w_out, norm_ffn_g, w_gate, w_up, w_down, final_norm_g, loss_target, m_norm_mix_g, m_w_in, m_conv_w, m_conv_b, m_dt_bias, m_a_log, m_d_skip, m_ssd_norm_g, m_i_bias, m_f_bias, m_mlstm_norm_g, m_w_out, m_norm_ffn_g, m_w_gate, m_w_up, m_w_down, m_final_norm_g, v_norm_mix_g, v_w_in, v_conv_w, v_conv_b, v_dt_bias, v_a_log, v_d_skip, v_ssd_norm_g, v_i_bias, v_f_bias, v_mlstm_norm_g, v_w_out, v_norm_ffn_g, v_w_gate, v_w_up, v_w_down, v_final_norm_g):
    weights = dict(norm_mix_g=norm_mix_g, w_in=w_in, conv_w=conv_w, conv_b=conv_b, dt_bias=dt_bias, a_log=a_log, d_skip=d_skip,
                   ssd_norm_g=ssd_norm_g, i_bias=i_bias, f_bias=f_bias, mlstm_norm_g=mlstm_norm_g, w_out=w_out,
                   norm_ffn_g=norm_ffn_g, w_gate=w_gate, w_up=w_up, w_down=w_down, final_norm_g=final_norm_g)
    mom1 = dict(norm_mix_g=m_norm_mix_g, w_in=m_w_in, conv_w=m_conv_w, conv_b=m_conv_b, dt_bias=m_dt_bias, a_log=m_a_log,
                d_skip=m_d_skip, ssd_norm_g=m_ssd_norm_g, i_bias=m_i_bias, f_bias=m_f_bias, mlstm_norm_g=m_mlstm_norm_g,
                w_out=m_w_out, norm_ffn_g=m_norm_ffn_g, w_gate=m_w_gate, w_up=m_w_up, w_down=m_w_down,
                final_norm_g=m_final_norm_g)
    mom2 = dict(norm_mix_g=v_norm_mix_g, w_in=v_w_in, conv_w=v_conv_w, conv_b=v_conv_b, dt_bias=v_dt_bias, a_log=v_a_log,
                d_skip=v_d_skip, ssd_norm_g=v_ssd_norm_g, i_bias=v_i_bias, f_bias=v_f_bias, mlstm_norm_g=v_mlstm_norm_g,
                w_out=v_w_out, norm_ffn_g=v_norm_ffn_g, w_gate=v_w_gate, w_up=v_w_up, w_down=v_w_down,
                final_norm_g=v_final_norm_g)
    xi, yi, ci = lax.axis_index("x"), lax.axis_index("y"), lax.axis_index("c")
    chip = 2 * xi + yi
    xs, tgt = x[0], loss_target[0]

    shards = {n: weights[n][0].astype(BF16) for n in _BIG}
    gathered = lambda n, g: _with_own_shard(g, shards[n], chip)
    by_cols = lambda g, width: jnp.transpose(g, (1, 0, 2)).reshape(D_MODEL, width)
    half_in = lambda g, sx: _with_own_shard(g, shards["w_in"], jnp.where(xi == sx, yi, 2))
    u1, (g_in0,) = _rmsnorm_fwd(xs, norm_mix_g, "norm_mix_fwd", side=_gather_side([shards["w_in"]], sender_x=0))
    g_in0 = half_in(g_in0, 0)
    W_ssd = _take_cols([g_in0[0], g_in0[1], None, None], _ssd_segments())
    cw_all = _allgather_small(jnp.concatenate([conv_w[0], jnp.zeros((4, 768), F32)], axis=0), "allgather_conv_w")
    conv_w_full = jnp.concatenate([cw_all[2 * k, :SSD_CONV] for k in range(4)], axis=1)
    cwg, cbg = _conv_to_groups(conv_w_full), _conv_to_groups(conv_b)
    dtb_row, al_row = _small_row(dt_bias, 0), _small_row(a_log, 0)
    ib_row, fb_row = _small_row(i_bias, LANE_I), _small_row(f_bias, LANE_F)
    dskip_lane = jnp.repeat(d_skip, 64, axis=1)
    fng = final_norm_g[None]

    p_ssd, (g_in1,) = _matmul(u1, W_ssd, "nn", F32, 1024, 1024, 2048, "proj_ssd",
                              side=_gather_side([shards["w_in"]], sender_x=1))
    g_in1 = half_in(g_in1, 1)
    _, W_ml, W_small = _split_w_in([g_in0[0], g_in0[1], g_in1[0], g_in1[1]])
    p_ml, (g_out, g_gate) = _matmul(u1, W_ml, "nn", F32, 1024, 1024, 2048, "proj_ml",
                                    side=_gather_side([shards["w_out"], shards["w_gate"]]))
    p_small = _matmul(u1, W_small, "nn", F32, 1024, 128, 2048, "proj_small")
    y_ssd, ssd_st = _ssd_forward(p_ssd, p_small, cwg, cbg, dtb_row, al_row, dskip_lane, ssd_norm_g)
    y_ml, ml_c, ml_nm = _mlstm_forward(p_ml, p_small, ib_row, fb_row, mlstm_norm_g)
    y_mix = jnp.concatenate([y_ssd, y_ml], axis=1)
    W_out = gathered("w_out", g_out).reshape(2 * SSD_WIDTH, D_MODEL)
    h1, (g_up,) = _matmul(y_mix, W_out, "nn", F32, 1024, 1024, 2048, "out_proj", addend=xs,
                          side=_gather_side([shards["w_up"]]))
    u2 = _rmsnorm_fwd(h1, norm_ffn_g, "norm_ffn_fwd")
    W_gate = by_cols(gathered("w_gate", g_gate), D_FF)
    W_up = by_cols(gathered("w_up", g_up), D_FF)
    gate, up, act, (g_down,) = _ffn_in(u2, W_gate, W_up, 1024, _gather_side([shards["w_down"]]))
    W_down = gathered("w_down", g_down).reshape(D_FF, D_MODEL)
    h2 = _matmul(act, W_down, "nn", F32, 1024, 1024, 2 * FF_TILE, "ffn_down", addend=h1)
    dh2, dh2_b, loss_row, d_fng = _loss_head(h2, tgt, fng)

    dW_down = _matmul(act, dh2_b, "tn", BF16, FF_TILE, 1024, 2048, "dw_down", layout="rows4")
    d_gate, d_up = _ffn_back(dh2_b, W_down, gate, up, 1024)
    du2 = _matmul(d_gate, W_gate, "nt", F32, 1024, 1024, 2 * FF_TILE, "du2_gate")
    du2 = _matmul(d_up, W_up, "nt", F32, 1024, 1024, 2 * FF_TILE, "du2_up", addend=du2)
    dW_gate = _matmul(u2, d_gate, "tn", BF16, 1024, FF_TILE, 2048, "dw_gate", layout="cols4")
    dW_up = _matmul(u2, d_up, "tn", BF16, 1024, FF_TILE, 2048, "dw_up", layout="cols4")
    dh1, dh1_b, d_ffn_g = _rmsnorm_bwd(du2, h1, norm_ffn_g, dh2, "norm_ffn_bwd")
    dy_mix = _matmul(dh1_b, W_out, "nt", F32, 1024, 1024, 2048, "d_mix")
    dW_out = _matmul(y_mix, dh1_b, "tn", BF16, 1024, 1024, 2048, "dw_out", layout="rows4")
    early = ["w_out", "w_gate", "w_up", "w_down"]
    early_g = [dW_out, dW_gate, dW_up, dW_down]
    early_got = _pair_exchange(early_g, "grad_pair_exchange_early")
    parts = {n: _pair_add(g, r, ci, "grad_pair_add_" + n) for g, r, n in zip(early_g, early_got, early)}
    (d_ssd, d_small, d_cw, d_cb, d_dtb, d_alog, d_dskip, d_sng), early_slots = _ssd_backward(
        p_ssd, p_small, cwg, cbg, dtb_row, al_row, dskip_lane, ssd_norm_g, ssd_st, dy_mix,
        side=_scatter_side([parts[n] for n in early]))
    slots = dict(zip(early, early_slots))
    d_ml, d_small, d_ib, d_fb, d_mng = _mlstm_backward(p_ml, p_small, ib_row, fb_row, mlstm_norm_g, ml_c, ml_nm, dy_mix, d_small)
    dW_ssd = _matmul(u1, d_ssd, "tn", BF16, 1024, 1280, 2048, "dw_ssd")
    dW_ml = _matmul(u1, d_ml, "tn", BF16, 1024, 1024, 2048, "dw_ml")
    dW_small = _matmul(u1, d_small, "tn", BF16, 1024, 128, 2048, "dw_small")
    g_w_in = jnp.stack(_merge_w_in(dW_ssd, dW_ml, dW_small)).reshape(4, 2, D_MODEL // 2, SHARD_IN)
    (got_in,) = _pair_exchange([g_w_in], "grad_pair_exchange_w_in", chip_major=True)
    parts["w_in"] = _pair_add(g_w_in, got_in, ci, "grad_pair_add_w_in", chip_major=True)
    du1, (slots["w_in"],) = _matmul(d_ssd, W_ssd, "nt", F32, 1024, 1024, 2560, "du1_ssd",
                                    side=_scatter_side([parts["w_in"]]))
    du1 = _matmul(d_ml, W_ml, "nt", F32, 1024, 1024, 3072, "du1_ml", addend=du1)
    du1 = _matmul(d_small, W_small, "nt", F32, 1024, 1024, 128, "du1_small", addend=du1)
    grad_x, _, d_mix_g = _rmsnorm_bwd(du1, xs, norm_mix_g, dh1, "norm_mix_bwd")

    fulls = _pair_share([_chip_sum(parts[n], slots[n], chip, ci, "grad_chip_sum_" + n) for n in _BIG])
    grads = {n: f.reshape(weights[n].shape[1:]) for n, f in zip(_BIG, fulls)}

    small = _pack(dict(norm_mix_g=d_mix_g, conv_w=_conv_from_groups(d_cw), conv_b=_conv_from_groups(d_cb),
                       dt_bias=d_dtb[:, 0:32], a_log=d_alog[:, 0:32], d_skip=d_dskip.reshape(SSD_HEADS, 64).sum(axis=1),
                       ssd_norm_g=d_sng, i_bias=d_ib[:, LANE_I:LANE_I + 8], f_bias=d_fb[:, LANE_F:LANE_F + 8],
                       mlstm_norm_g=d_mng, norm_ffn_g=d_ffn_g, final_norm_g=d_fng, loss=loss_row[:, 0:1]))
    total = _sum_slots(_allgather_small(small, "allgather_small_grads"), "small_grad_sum")
    small_w = {n: weights[n] for n in _WEIGHTS if n not in _BIG and n != "conv_w"}
    sd, sm, sv = _adamw(_pack(small_w), total, _pack({n: mom1[n] for n in small_w}), _pack({n: mom2[n] for n in small_w}),
                        PACK_ROWS, "adamw_small")
    tot, sd, sm, sv = _unpack(total), _unpack(sd), _unpack(sm), _unpack(sv)
    for n in small_w:
        grads[n] = tot[n].reshape(weights[n].shape)
    grads["conv_w"] = lax.dynamic_slice_in_dim(tot["conv_w"], chip * 768, 768, axis=1)
    loss = tot["loss"][0, 0]

    delta, new_m, new_v = {}, {}, {}
    for n in _BIG + ["conv_w"]:
        w2 = weights[n][0]
        d, nm, nv = _adamw(w2, grads[n], mom1[n][0], mom2[n][0], _row_tile(w2.shape[0]) if n != "conv_w" else SSD_CONV,
                           "adamw_" + n)
        delta[n], new_m[n], new_v[n] = d[None], nm[None], nv[None]
        grads[n] = grads[n][None]
    for n in small_w:
        delta[n], new_m[n], new_v[n] = (t[n].reshape(weights[n].shape) for t in (sd, sm, sv))
    return (loss, grad_x[None], *[grads[n] for n in _WEIGHTS], *[delta[n] for n in _WEIGHTS],
            *[new_m[n] for n in _WEIGHTS], *[new_v[n] for n in _WEIGHTS])
```

```python
import functools

import jax
import jax.numpy as jnp
import numpy as np
from jax import lax
from jax.experimental import pallas as pl
from jax.experimental.pallas import tpu as pltpu

F32 = jnp.float32
BF16 = jnp.bfloat16

D_MODEL = 2048
SSD_HEADS = 32
SSD_GROUPS = 4
SSD_STATE = 128
SSD_WIDTH = 2048
SSD_CONV = 4
SSD_GROUP_COLS = 1280
SSD_XBC_COLS = 768
ML_HEADS = 8
ML_HEAD_COLS = 768
ML_DK = 128
ML_DV = 256
CHUNK = 128
SMALL_COLS = 128
LANE_I = 32
LANE_F = 40
D_FF = 5632
GATE_SOFTCAP = 15.0
EPS = 1e-6
ADAM_LR, ADAM_B1, ADAM_B2, ADAM_EPS, ADAM_WD, ADAM_STEP = 0.001, 0.9, 0.999, 1e-8, 0.01, 10
MESH = pl.DeviceIdType.MESH
VMEM_LIMIT = 56 * 1024 * 1024


def _dg(a, b, ca, cb):
    return lax.dot_general(a.astype(BF16), b.astype(BF16), (((ca,), (cb,)), ((), ())), preferred_element_type=F32)


@jax.custom_vjp
def _mm(a, b):
    return _dg(a, b, 1, 0)


_mm.defvjp(lambda a, b: (_dg(a, b, 1, 0), (a, b)),
           lambda r, g: (_dg(g, r[1], 1, 1), _dg(r[0], g, 0, 0)))


@jax.custom_vjp
def _mm_nt(a, b):
    return _dg(a, b, 1, 1)


_mm_nt.defvjp(lambda a, b: (_dg(a, b, 1, 1), (a, b)),
              lambda r, g: (_dg(g, r[1], 1, 0), _dg(g, r[0], 0, 0)))


@jax.custom_vjp
def _mm_tn(a, b):
    return _dg(a, b, 0, 0)


_mm_tn.defvjp(lambda a, b: (_dg(a, b, 0, 0), (a, b)),
              lambda r, g: (_dg(r[1], g, 1, 1), _dg(r[0], g, 1, 0)))


def _tri(lower):
    r = lax.broadcasted_iota(jnp.int32, (CHUNK, CHUNK), 0)
    c = lax.broadcasted_iota(jnp.int32, (CHUNK, CHUNK), 1)
    return ((r >= c) if lower else (r <= c)).astype(F32)


def _dg32(t, x):
    return lax.dot_general(t, x, (((1,), (0,)), ((), ())), precision=lax.Precision.HIGHEST, preferred_element_type=F32)


@jax.custom_vjp
def _cumsum(x):
    return _dg32(_tri(True), x)


_cumsum.defvjp(lambda x: (_dg32(_tri(True), x), None), lambda r, g: (_dg32(_tri(False), g),))


_sigmoid = jax.nn.sigmoid


def _silu(x):
    return x * _sigmoid(x)


def _softplus(x):
    return jnp.maximum(x, 0.0) + jnp.log(1.0 + jnp.exp(-jnp.abs(x)))


def _lane_col(m, lane, h):
    return jnp.sum(jnp.where(lane == h, m, 0.0), axis=1, keepdims=True)


SSD_GPS = 4


def _ssd_math(convs, zs, smallb, S_in, dtb_row, alog_row, dskip, ng, g0):
    dt_all = _softplus(smallb + dtb_row)
    a_all = dt_all * (-jnp.exp(alog_row))
    prep = (dt_all, _cumsum(a_all), jnp.sum(a_all, axis=0, keepdims=True))
    ys, S_out = [], []
    for k in range(SSD_GPS):
        y, S = _ssd_group(convs[k][:, 0:512], convs[k][:, 512:640], convs[k][:, 640:768], zs[k], prep, S_in[k],
                          dskip[:, 512 * k:512 * (k + 1)], ng[:, 512 * k:512 * (k + 1)], g0 + k)
        ys.append(y)
        S_out.append(S)
    return tuple(ys), tuple(S_out)


def _ssd_group(cx, cB, cC, z, prep, S_in, dskip, ng, g):
    dt_all, acum_all, alast_all = prep
    lane = lax.broadcasted_iota(jnp.int32, (1, CHUNK), 1)
    row_i = lax.broadcasted_iota(jnp.int32, (CHUNK, CHUNK), 0)
    col_i = lax.broadcasted_iota(jnp.int32, (CHUNK, CHUNK), 1)
    causal = row_i >= col_i
    half = lane < 64
    rhalf = lax.broadcasted_iota(jnp.int32, (CHUNK, 1), 0) < 64
    xs, Bm, Cm = _silu(cx), _silu(cB), _silu(cC)
    cb = _mm_nt(Cm, Bm)

    def lmat(ac):
        acb = jnp.broadcast_to(ac, (CHUNK, CHUNK))
        return jnp.exp(jnp.where(causal, acb - acb.T, -jnp.inf))

    ys, S_out = [], []
    for j in range(4):
        h0 = 8 * g + 2 * j
        ac0, ac1 = _lane_col(acum_all, lane, h0), _lane_col(acum_all, lane, h0 + 1)
        dt0, dt1 = _lane_col(dt_all, lane, h0), _lane_col(dt_all, lane, h0 + 1)
        al0, al1 = _lane_col(alast_all, lane, h0), _lane_col(alast_all, lane, h0 + 1)
        Xp = xs[:, 128 * j:128 * (j + 1)]
        Xd = Xp * jnp.where(half, dt0, dt1)
        ac_sel = jnp.where(half, ac0, ac1)
        al_sel = jnp.where(half, al0, al1)
        Yd = jnp.where(half, _mm(cb * lmat(ac0), Xd), _mm(cb * lmat(ac1), Xd))
        Yoff = _mm_nt(Cm, S_in[j]) * jnp.exp(ac_sel)
        ys.append(Yd + Yoff + dskip[:, 128 * j:128 * (j + 1)] * Xp)
        S_new = _mm_tn(Xd * jnp.exp(al_sel - ac_sel), Bm)
        S_out.append(S_in[j] * jnp.exp(jnp.where(rhalf, al0, al1)) + S_new)
    y = jnp.concatenate(ys, axis=1)
    y = y * _silu(z)
    y = y * lax.rsqrt(jnp.mean(y * y, axis=1, keepdims=True) + EPS) * ng
    return y, tuple(S_out)


def _ssd_conv(blk_ref, halo_ref, cw_ref, cb_ref, ext, first, k):
    cols = slice(SSD_GROUP_COLS * k + 512, SSD_GROUP_COLS * (k + 1))
    halo = halo_ref[:, cols]
    ext[k, 0:8, :] = jnp.where(first, jnp.zeros_like(halo), halo)
    ext[k, 8:8 + CHUNK, :] = blk_ref[:, cols]
    conv = jnp.broadcast_to(cb_ref[k], (CHUNK, SSD_XBC_COLS))
    for tap in range(SSD_CONV):
        conv = conv + cw_ref[k, tap:tap + 1, :] * ext[k, pl.ds(5 + tap, CHUNK), :]
    return conv


def _ssd_specs(nc, rev):
    cc = (lambda c: nc - 1 - c) if rev else (lambda c: c)
    width = SSD_GPS * SSD_GROUP_COLS
    return [
        pl.BlockSpec((CHUNK, width), lambda c, g: (cc(c), g)),
        pl.BlockSpec((8, width), lambda c, g: (jnp.maximum(cc(c) * (CHUNK // 8) - 1, 0), g)),
        pl.BlockSpec((CHUNK, SMALL_COLS), lambda c, g: (cc(c), 0)),
        pl.BlockSpec((SSD_GPS, SSD_CONV, SSD_XBC_COLS), lambda c, g: (g, 0, 0)),
        pl.BlockSpec((SSD_GPS, 1, SSD_XBC_COLS), lambda c, g: (g, 0, 0)),
        pl.BlockSpec((1, SMALL_COLS), lambda c, g: (0, 0)),
        pl.BlockSpec((1, SMALL_COLS), lambda c, g: (0, 0)),
        pl.BlockSpec((1, SSD_GPS * 512), lambda c, g: (0, g)),
        pl.BlockSpec((1, SSD_GPS * 512), lambda c, g: (0, g)),
    ]


def _ssd_forward(proj, small, cw, cb, dtb, alog, dskip, ng):
    L = proj.shape[0]
    nc = L // CHUNK

    def body(blk_ref, halo_ref, small_ref, cw_ref, cb_ref, dtb_ref, alog_ref, dskip_ref, ng_ref, y_ref, st_ref, carry, ext):
        c, g0 = pl.program_id(0), pl.program_id(1) * SSD_GPS

        @pl.when(c == 0)
        def _():
            for k in range(SSD_GPS):
                carry[g0 + k] = jnp.zeros((4, CHUNK, CHUNK), F32)

        convs = [_ssd_conv(blk_ref, halo_ref, cw_ref, cb_ref, ext, c == 0, k) for k in range(SSD_GPS)]
        zs = [blk_ref[:, SSD_GROUP_COLS * k:SSD_GROUP_COLS * k + 512] for k in range(SSD_GPS)]
        S_in = tuple(tuple(carry[g0 + k, j] for j in range(4)) for k in range(SSD_GPS))
        for k in range(SSD_GPS):
            st_ref[0, k] = carry[g0 + k]
        ys, S_out = _ssd_math(convs, zs, small_ref[...], S_in, dtb_ref[...], alog_ref[...], dskip_ref[...], ng_ref[...], g0)
        for k in range(SSD_GPS):
            y_ref[:, 512 * k:512 * (k + 1)] = ys[k].astype(BF16)
            for j in range(4):
                carry[g0 + k, j] = S_out[k][j]

    return pl.pallas_call(
        body, name="ssd_fwd", grid=(nc, SSD_GROUPS // SSD_GPS),
        in_specs=_ssd_specs(nc, False),
        out_specs=[pl.BlockSpec((CHUNK, SSD_GPS * 512), lambda c, g: (c, g)),
                   pl.BlockSpec((1, SSD_GPS, 4, CHUNK, CHUNK), lambda c, g: (c, g, 0, 0, 0))],
        out_shape=[jax.ShapeDtypeStruct((L, 2 * SSD_WIDTH), BF16),
                   jax.ShapeDtypeStruct((nc, SSD_GROUPS, 4, CHUNK, CHUNK), F32)],
        scratch_shapes=[pltpu.VMEM((SSD_GROUPS, 4, CHUNK, CHUNK), F32), pltpu.VMEM((SSD_GPS, 8 + CHUNK, SSD_XBC_COLS), F32)],
        compiler_params=pltpu.CompilerParams(dimension_semantics=("arbitrary", "arbitrary"), vmem_limit_bytes=VMEM_LIMIT),
    )(proj, proj, small, cw, cb, dtb, alog, dskip, ng)


def _ssd_backward(proj, small, cw, cb, dtb, alog, dskip, ng, states, dy, side=None):
    L = proj.shape[0]
    nc = L // CHUNK
    s_in, s_out = (len(side.operands), len(side.out_shapes)) if side is not None else (0, 0)
    grid = (nc, SSD_GROUPS // SSD_GPS)

    def body(*refs):
        blk_ref, halo_ref, small_ref, cw_ref, cb_ref, dtb_ref, alog_ref, dskip_ref, ng_ref, st_ref, dy_ref = refs[:11]
        dproj_ref, dsmall_ref, dcw_ref, dcb_ref, ddtb_ref, dalog_ref, ddskip_ref, dng_ref = refs[11 + s_in:19 + s_in]
        dcarry, nxt, ext, dext = refs[19 + s_in + s_out:23 + s_in + s_out]
        side_refs = (refs[11:11 + s_in], refs[19 + s_in:19 + s_in + s_out], refs[23 + s_in + s_out:])
        s, gb = pl.program_id(0), pl.program_id(1)
        g0 = gb * SSD_GPS
        c = nc - 1 - s
        if side is not None:
            pl.when(_grid_edge(grid, first=True))(lambda: side.start(*side_refs))

        @pl.when(s == 0)
        def _():
            for k in range(SSD_GPS):
                dcarry[g0 + k] = jnp.zeros((4, CHUNK, CHUNK), F32)
                nxt[g0 + k] = jnp.zeros((8, SSD_XBC_COLS), F32)
                dcw_ref[g0 + k] = jnp.zeros((SSD_CONV, SSD_XBC_COLS), F32)
                dcb_ref[g0 + k] = jnp.zeros((1, SSD_XBC_COLS), F32)
                ddskip_ref[g0 + k] = jnp.zeros((1, 512), F32)
                dng_ref[g0 + k] = jnp.zeros((1, 512), F32)

        @pl.when((s == 0) & (gb == 0))
        def _():
            ddtb_ref[...] = jnp.zeros((1, SMALL_COLS), F32)
            dalog_ref[...] = jnp.zeros((1, SMALL_COLS), F32)

        convs = [_ssd_conv(blk_ref, halo_ref, cw_ref, cb_ref, ext, c == 0, k) for k in range(SSD_GPS)]
        zs = [blk_ref[:, SSD_GROUP_COLS * k:SSD_GROUP_COLS * k + 512] for k in range(SSD_GPS)]
        S_in = tuple(tuple(st_ref[0, k, j] for j in range(4)) for k in range(SSD_GPS))
        dS_out = tuple(tuple(dcarry[g0 + k, j] for j in range(4)) for k in range(SSD_GPS))
        dys = tuple(dy_ref[:, 512 * k:512 * (k + 1)].astype(F32) for k in range(SSD_GPS))
        _, vjp = jax.vjp(functools.partial(_ssd_math, g0=g0), convs, zs, small_ref[...], S_in, dtb_ref[...], alog_ref[...],
                         dskip_ref[...], ng_ref[...])
        d_convs, d_zs, d_small, dS_in, d_dtb, d_alog, d_dskip, d_ng = vjp((dys, dS_out))
        for k in range(SSD_GPS):
            for j in range(4):
                dcarry[g0 + k, j] = dS_in[k][j]
            dext[k, 0:8, :] = jnp.zeros((8, SSD_XBC_COLS), F32)
            dext[k, 8:8 + CHUNK, :] = d_convs[k]
            dext[k, 8 + CHUNK:16 + CHUNK, :] = nxt[g0 + k]
            nxt[g0 + k] = dext[k, 8:16, :]
            dconv = d_convs[k]
            d_xbc = jnp.zeros((CHUNK, SSD_XBC_COLS), F32)
            for tap in range(SSD_CONV):
                d_xbc = d_xbc + cw_ref[k, tap:tap + 1, :] * dext[k, pl.ds(8 + 3 - tap, CHUNK), :]
                dcw_ref[g0 + k, tap:tap + 1, :] += jnp.sum(dconv * ext[k, pl.ds(5 + tap, CHUNK), :], axis=0, keepdims=True)
            dcb_ref[g0 + k] += jnp.sum(dconv, axis=0, keepdims=True)
            dproj_ref[:, SSD_GROUP_COLS * k:SSD_GROUP_COLS * k + 512] = d_zs[k].astype(BF16)
            dproj_ref[:, SSD_GROUP_COLS * k + 512:SSD_GROUP_COLS * (k + 1)] = d_xbc.astype(BF16)
            ddskip_ref[g0 + k] += d_dskip[:, 512 * k:512 * (k + 1)]
            dng_ref[g0 + k] += d_ng[:, 512 * k:512 * (k + 1)]

        @pl.when(gb == 0)
        def _():
            dsmall_ref[...] = d_small

        @pl.when(gb != 0)
        def _():
            dsmall_ref[...] += d_small

        ddtb_ref[...] += d_dtb
        dalog_ref[...] += d_alog
        if side is not None:
            pl.when(_grid_edge(grid, first=False))(lambda: side.finish(*side_refs))

    whole = lambda shape: pl.BlockSpec(shape, lambda s, g: (0,) * len(shape))
    side_ops = tuple(side.operands) if side is not None else ()
    res = pl.pallas_call(
        body, name="ssd_bwd", grid=grid,
        in_specs=_ssd_specs(nc, True) + [
            pl.BlockSpec((1, SSD_GPS, 4, CHUNK, CHUNK), lambda s, g: (nc - 1 - s, g, 0, 0, 0)),
            pl.BlockSpec((CHUNK, SSD_GPS * 512), lambda s, g: (nc - 1 - s, g))] + [ANY] * s_in,
        out_specs=[pl.BlockSpec((CHUNK, SSD_GPS * SSD_GROUP_COLS), lambda s, g: (nc - 1 - s, g)),
                   pl.BlockSpec((CHUNK, SMALL_COLS), lambda s, g: (nc - 1 - s, 0)),
                   whole((SSD_GROUPS, SSD_CONV, SSD_XBC_COLS)), whole((SSD_GROUPS, 1, SSD_XBC_COLS)),
                   whole((1, SMALL_COLS)), whole((1, SMALL_COLS)),
                   whole((SSD_GROUPS, 1, 512)), whole((SSD_GROUPS, 1, 512))] + [ANY] * s_out,
        out_shape=[jax.ShapeDtypeStruct((L, SSD_GROUPS * SSD_GROUP_COLS), BF16), jax.ShapeDtypeStruct((L, SMALL_COLS), F32),
                   jax.ShapeDtypeStruct((SSD_GROUPS, SSD_CONV, SSD_XBC_COLS), F32),
                   jax.ShapeDtypeStruct((SSD_GROUPS, 1, SSD_XBC_COLS), F32),
                   jax.ShapeDtypeStruct((1, SMALL_COLS), F32), jax.ShapeDtypeStruct((1, SMALL_COLS), F32),
                   jax.ShapeDtypeStruct((SSD_GROUPS, 1, 512), F32), jax.ShapeDtypeStruct((SSD_GROUPS, 1, 512), F32)]
        + (list(side.out_shapes) if side is not None else []),
        scratch_shapes=[pltpu.VMEM((SSD_GROUPS, 4, CHUNK, CHUNK), F32), pltpu.VMEM((SSD_GROUPS, 8, SSD_XBC_COLS), F32),
                        pltpu.VMEM((SSD_GPS, 8 + CHUNK, SSD_XBC_COLS), F32),
                        pltpu.VMEM((SSD_GPS, 16 + CHUNK, SSD_XBC_COLS), F32)]
        + (list(side.sems) if side is not None else []),
        compiler_params=pltpu.CompilerParams(dimension_semantics=("arbitrary", "arbitrary"), vmem_limit_bytes=VMEM_LIMIT),
    )(proj, proj, small, cw, cb, dtb, alog, dskip, ng, states, dy, *side_ops)
    return res if side is None else (res[:8], res[8:])


ML_HPS = 8


def _mlstm_math(blks, smallb, C_in, n_in, m_in, ib_row, fb_row, ng, h0):
    li_all = GATE_SOFTCAP * jnp.tanh((smallb + ib_row) / GATE_SOFTCAP)
    lf_all = -_softplus(-(GATE_SOFTCAP * jnp.tanh((smallb + fb_row) / GATE_SOFTCAP)))
    prep = (li_all, lf_all, _cumsum(lf_all))
    out = [_mlstm_head(blks[k][:, 0:128], blks[k][:, 128:256], blks[k][:, 256:512], blks[k][:, 512:768], prep,
                       C_in[k], n_in[k], m_in[k], ng[:, ML_DV * k:ML_DV * (k + 1)], h0 + k) for k in range(ML_HPS)]
    return tuple(zip(*out))


def _mlstm_head(q, k, v, o_raw, prep, C_in, n_in, m_in_row, ng, h):
    li_all, lf_all, bcum_all = prep
    lane = lax.broadcasted_iota(jnp.int32, (1, CHUNK), 1)
    row_i = lax.broadcasted_iota(jnp.int32, (CHUNK, CHUNK), 0)
    col_i = lax.broadcasted_iota(jnp.int32, (CHUNK, CHUNK), 1)
    causal = row_i >= col_i
    kk = k * (ML_DK ** -0.5)
    li = _lane_col(li_all, lane, LANE_I + h)
    lf = _lane_col(lf_all, lane, LANE_F + h)
    bc = _lane_col(bcum_all, lane, LANE_F + h)
    b_last = jnp.sum(lf, axis=0, keepdims=True)
    m_in = _lane_col(m_in_row, lane, 0)
    a = b_last - bc + li
    m_loc = jnp.max(a, axis=0, keepdims=True)
    w = jnp.exp(a - m_loc)
    C_loc = _mm_tn(w * v, kk)
    n_loc = jnp.sum(w * kk, axis=0, keepdims=True)
    m_new = jnp.maximum(b_last + m_in, m_loc)
    s_old = jnp.exp(b_last + m_in - m_new)
    s_new = jnp.exp(m_loc - m_new)
    C_out = s_old * C_in + s_new * C_loc
    n_out = s_old * n_in + s_new * n_loc
    bc_b = jnp.broadcast_to(bc, (CHUNK, CHUNK))
    li_b = jnp.broadcast_to(li, (CHUNK, CHUNK))
    D = jnp.where(causal, bc_b - bc_b.T + li_b.T, -jnp.inf)
    m_intra = jnp.max(D, axis=1, keepdims=True)
    inter_log = bc + m_in
    m_t = jnp.maximum(inter_log, m_intra)
    S = _mm_nt(q, kk) * jnp.exp(D - m_t)
    w_inter = jnp.exp(inter_log - m_t)
    num = _mm(S, v) + w_inter * _mm_nt(q, C_in)
    nq = jnp.sum(S, axis=1, keepdims=True) + w_inter * jnp.sum(q * n_in, axis=1, keepdims=True)
    den = jnp.maximum(jnp.abs(nq), jnp.exp(-m_t))
    hh = num / den
    hh = hh * lax.rsqrt(jnp.mean(hh * hh, axis=1, keepdims=True) + EPS)
    hh = hh * ng * _sigmoid(o_raw)
    return hh, C_out, n_out, jnp.broadcast_to(m_new, (1, CHUNK))


def _ml_specs(nc, rev):
    cc = (lambda c: nc - 1 - c) if rev else (lambda c: c)
    return [
        pl.BlockSpec((CHUNK, ML_HPS * ML_HEAD_COLS), lambda c, h: (cc(c), h)),
        pl.BlockSpec((CHUNK, SMALL_COLS), lambda c, h: (cc(c), 0)),
        pl.BlockSpec((1, SMALL_COLS), lambda c, h: (0, 0)),
        pl.BlockSpec((1, SMALL_COLS), lambda c, h: (0, 0)),
        pl.BlockSpec((1, ML_HPS * ML_DV), lambda c, h: (0, h)),
    ]


def _mlstm_forward(proj, small, ib, fb, ng, y_mix):
    L = proj.shape[0]
    nc = L // CHUNK

    def body(blk_ref, small_ref, ib_ref, fb_ref, ng_ref, _, y_ref, cst_ref, nm_ref, c_carry, nm_carry):
        c, h0 = pl.program_id(0), pl.program_id(1) * ML_HPS

        @pl.when(c == 0)
        def _():
            for k in range(ML_HPS):
                c_carry[h0 + k] = jnp.zeros((ML_DV, ML_DK), F32)
                nm_carry[h0 + k] = jnp.zeros((2, ML_DK), F32)

        for k in range(ML_HPS):
            cst_ref[0, k] = c_carry[h0 + k]
            nm_ref[0, k] = nm_carry[h0 + k]
        hh, C_out, n_out, m_out = _mlstm_math(
            [blk_ref[:, ML_HEAD_COLS * k:ML_HEAD_COLS * (k + 1)] for k in range(ML_HPS)], small_ref[...],
            [c_carry[h0 + k] for k in range(ML_HPS)], [nm_carry[h0 + k, 0:1, :] for k in range(ML_HPS)],
            [nm_carry[h0 + k, 1:2, :] for k in range(ML_HPS)], ib_ref[...], fb_ref[...], ng_ref[...], h0)
        for k in range(ML_HPS):
            y_ref[:, ML_DV * k:ML_DV * (k + 1)] = hh[k].astype(BF16)
            c_carry[h0 + k] = C_out[k]
            nm_carry[h0 + k, 0:1, :] = n_out[k]
            nm_carry[h0 + k, 1:2, :] = m_out[k]

    return pl.pallas_call(
        body, name="mlstm_fwd", grid=(nc, ML_HEADS // ML_HPS),
        in_specs=_ml_specs(nc, False) + [pl.BlockSpec(memory_space=pl.ANY)],
        out_specs=[pl.BlockSpec((CHUNK, ML_HPS * ML_DV), lambda c, h: (c, SSD_WIDTH // (ML_HPS * ML_DV) + h)),
                   pl.BlockSpec((1, ML_HPS, ML_DV, ML_DK), lambda c, h: (c, h, 0, 0)),
                   pl.BlockSpec((1, ML_HPS, 2, ML_DK), lambda c, h: (c, h, 0, 0))],
        out_shape=[jax.ShapeDtypeStruct(y_mix.shape, BF16),
                   jax.ShapeDtypeStruct((nc, ML_HEADS, ML_DV, ML_DK), F32),
                   jax.ShapeDtypeStruct((nc, ML_HEADS, 2, ML_DK), F32)],
        input_output_aliases={5: 0},
        scratch_shapes=[pltpu.VMEM((ML_HEADS, ML_DV, ML_DK), F32), pltpu.VMEM((ML_HEADS, 2, ML_DK), F32)],
        compiler_params=pltpu.CompilerParams(dimension_semantics=("arbitrary", "arbitrary"), vmem_limit_bytes=VMEM_LIMIT),
    )(proj, small, ib, fb, ng, y_mix)


def _mlstm_backward(proj, small, ib, fb, ng, cst, nmst, dy, dsmall_in):
    L = proj.shape[0]
    nc = L // CHUNK

    def body(blk_ref, small_ref, ib_ref, fb_ref, ng_ref, cst_ref, nm_ref, dy_ref, dsin_ref,
             dproj_ref, dsmall_ref, dib_ref, dfb_ref, dng_ref, dc_carry, dnm_carry):
        s, hb = pl.program_id(0), pl.program_id(1)
        h0 = hb * ML_HPS

        @pl.when(s == 0)
        def _():
            for k in range(ML_HPS):
                dc_carry[h0 + k] = jnp.zeros((ML_DV, ML_DK), F32)
                dnm_carry[h0 + k] = jnp.zeros((2, ML_DK), F32)
                dng_ref[h0 + k] = jnp.zeros((1, ML_DV), F32)

        @pl.when((s == 0) & (hb == 0))
        def _():
            dib_ref[...] = jnp.zeros((1, SMALL_COLS), F32)
            dfb_ref[...] = jnp.zeros((1, SMALL_COLS), F32)

        heads = range(ML_HPS)
        _, vjp = jax.vjp(functools.partial(_mlstm_math, h0=h0),
                         [blk_ref[:, ML_HEAD_COLS * k:ML_HEAD_COLS * (k + 1)] for k in heads], small_ref[...],
                         [cst_ref[0, k] for k in heads], [nm_ref[0, k, 0:1, :] for k in heads],
                         [nm_ref[0, k, 1:2, :] for k in heads], ib_ref[...], fb_ref[...], ng_ref[...])
        d_blk, d_small, dC, dn, dm, d_ib, d_fb, d_ng = vjp(
            (tuple(dy_ref[:, ML_DV * k:ML_DV * (k + 1)].astype(F32) for k in heads),
             tuple(dc_carry[h0 + k] for k in heads), tuple(dnm_carry[h0 + k, 0:1, :] for k in heads),
             tuple(dnm_carry[h0 + k, 1:2, :] for k in heads)))
        for k in heads:
            dc_carry[h0 + k] = dC[k]
            dnm_carry[h0 + k, 0:1, :] = dn[k]
            dnm_carry[h0 + k, 1:2, :] = dm[k]
            dproj_ref[:, ML_HEAD_COLS * k:ML_HEAD_COLS * (k + 1)] = d_blk[k].astype(BF16)
            dng_ref[h0 + k] += d_ng[:, ML_DV * k:ML_DV * (k + 1)]

        @pl.when(hb == 0)
        def _():
            dsmall_ref[...] = dsin_ref[...] + d_small

        @pl.when(hb != 0)
        def _():
            dsmall_ref[...] += d_small

        dib_ref[...] += d_ib
        dfb_ref[...] += d_fb

    whole = lambda shape: pl.BlockSpec(shape, lambda s, h: (0,) * len(shape))
    return pl.pallas_call(
        body, name="mlstm_bwd", grid=(nc, ML_HEADS // ML_HPS),
        in_specs=_ml_specs(nc, True) + [
            pl.BlockSpec((1, ML_HPS, ML_DV, ML_DK), lambda s, h: (nc - 1 - s, h, 0, 0)),
            pl.BlockSpec((1, ML_HPS, 2, ML_DK), lambda s, h: (nc - 1 - s, h, 0, 0)),
            pl.BlockSpec((CHUNK, ML_HPS * ML_DV), lambda s, h: (nc - 1 - s, SSD_WIDTH // (ML_HPS * ML_DV) + h)),
            pl.BlockSpec((CHUNK, SMALL_COLS), lambda s, h: (nc - 1 - s, 0))],
        out_specs=[pl.BlockSpec((CHUNK, ML_HPS * ML_HEAD_COLS), lambda s, h: (nc - 1 - s, h)),
                   pl.BlockSpec((CHUNK, SMALL_COLS), lambda s, h: (nc - 1 - s, 0)),
                   whole((1, SMALL_COLS)), whole((1, SMALL_COLS)), whole((ML_HEADS, 1, ML_DV))],
        out_shape=[jax.ShapeDtypeStruct((L, ML_HEADS * ML_HEAD_COLS), BF16), jax.ShapeDtypeStruct((L, SMALL_COLS), F32),
                   jax.ShapeDtypeStruct((1, SMALL_COLS), F32), jax.ShapeDtypeStruct((1, SMALL_COLS), F32),
                   jax.ShapeDtypeStruct((ML_HEADS, 1, ML_DV), F32)],
        scratch_shapes=[pltpu.VMEM((ML_HEADS, ML_DV, ML_DK), F32), pltpu.VMEM((ML_HEADS, 2, ML_DK), F32)],
        compiler_params=pltpu.CompilerParams(dimension_semantics=("arbitrary", "arbitrary"), vmem_limit_bytes=VMEM_LIMIT),
    )(proj, small, ib, fb, ng, cst, nmst, dy, dsmall_in)


_OFF_Z, _OFF_X, _OFF_B, _OFF_C, _OFF_DT = 0, 2048, 4096, 4608, 5120
_OFF_Q, _OFF_K, _OFF_V, _OFF_O, _OFF_I, _OFF_F, IN_WIDTH = 5152, 6176, 7200, 9248, 11296, 11304, 11312


def _ssd_segments():
    segs = []
    for g in range(SSD_GROUPS):
        segs += [(_OFF_Z + 512 * g, 512), (_OFF_X + 512 * g, 512), (_OFF_B + 128 * g, 128), (_OFF_C + 128 * g, 128)]
    return segs


def _ml_segments():
    segs = []
    for h in range(ML_HEADS):
        segs += [(_OFF_Q + 128 * h, 128), (_OFF_K + 128 * h, 128), (_OFF_V + 256 * h, 256), (_OFF_O + 256 * h, 256)]
    return segs


_SMALL_SEGMENTS = [(_OFF_DT, 32), (_OFF_I, 8), (_OFF_F, 8)]


SHARD_IN = IN_WIDTH // 4


def _take_cols(slabs, segs):
    parts = []
    for s, n in segs:
        while n > 0:
            k, lo = divmod(s, SHARD_IN)
            m = min(n, SHARD_IN - lo)
            parts.append(slabs[k][:, lo:lo + m])
            s, n = s + m, n - m
    return jnp.concatenate(parts, axis=1)


def _split_w_in(slabs):
    small = _take_cols(slabs, _SMALL_SEGMENTS)
    small = jnp.concatenate([small, jnp.zeros((small.shape[0], SMALL_COLS - small.shape[1]), small.dtype)], axis=1)
    return _take_cols(slabs, _ssd_segments()), _take_cols(slabs, _ml_segments()), small


def _merge_w_in(d_ssd, d_ml, d_small):
    pieces = []
    for arr, segs in ((d_ssd, _ssd_segments()), (d_ml, _ml_segments()), (d_small, _SMALL_SEGMENTS)):
        pos = 0
        for s, n in segs:
            while n > 0:
                lo = s % SHARD_IN
                m = min(n, SHARD_IN - lo)
                pieces.append((s, arr[:, pos:pos + m]))
                s, n, pos = s + m, n - m, pos + m
    pieces.sort(key=lambda t: t[0])
    return [jnp.concatenate([p for s, p in pieces if s // SHARD_IN == k], axis=1) for k in range(4)]


def _conv_to_groups(cw):
    return jnp.stack([jnp.concatenate([cw[:, 512 * g:512 * (g + 1)], cw[:, 2048 + 128 * g:2048 + 128 * (g + 1)],
                                       cw[:, 2560 + 128 * g:2560 + 128 * (g + 1)]], axis=1) for g in range(SSD_GROUPS)])


def _conv_from_groups(d):
    return jnp.concatenate([d[g, :, 0:512] for g in range(SSD_GROUPS)] + [d[g, :, 512:640] for g in range(SSD_GROUPS)]
                           + [d[g, :, 640:768] for g in range(SSD_GROUPS)], axis=1)


def _small_row(vec, lane0):
    n = vec.shape[1]
    return jnp.concatenate([jnp.zeros((1, lane0), F32), vec, jnp.zeros((1, SMALL_COLS - lane0 - n), F32)], axis=1)


class _Side:
    def __init__(self, operands, out_shapes, sems, start, finish):
        self.operands, self.out_shapes, self.sems, self.start, self.finish = operands, out_shapes, sems, start, finish


def _grid_edge(grid, first):
    hit = None
    for axis, n in enumerate(grid):
        here = pl.program_id(axis) == (0 if first else n - 1)
        hit = here if hit is None else hit & here
    return hit


def _matmul(a, b, mode, out_dtype, tm, tn, tk, name, addend=None, layout=None, side=None):
    M, Kd = (a.shape[1], a.shape[0]) if mode == "tn" else a.shape
    N = b.shape[0] if mode == "nt" else b.shape[1]
    if layout == "cols4":
        tm, tn = M // 2, N // 4
    elif layout == "rows4":
        tm = M // 4
    tm, tn, tk = min(tm, M), min(tn, N), min(tk, Kd)
    if mode == "nn":
        a_spec = pl.BlockSpec((tm, tk), lambda i, j, k: (i, k))
        b_spec = pl.BlockSpec((tk, tn), lambda i, j, k: (k, j))
        dims = (((1,), (0,)), ((), ()))
    elif mode == "nt":
        a_spec = pl.BlockSpec((tm, tk), lambda i, j, k: (i, k))
        b_spec = pl.BlockSpec((tn, tk), lambda i, j, k: (j, k))
        dims = (((1,), (1,)), ((), ()))
    else:
        a_spec = pl.BlockSpec((tk, tm), lambda i, j, k: (k, i))
        b_spec = pl.BlockSpec((tk, tn), lambda i, j, k: (k, j))
        dims = (((0,), (0,)), ((), ()))
    assert M % tm == 0 and N % tn == 0 and Kd % tk == 0, (name, M, N, Kd, tm, tn, tk)
    nk = Kd // tk
    has_add = addend is not None
    n_in = 2 + has_add
    s_in, s_out = (len(side.operands), len(side.out_shapes)) if side is not None else (0, 0)
    grid = (M // tm, N // tn, nk)

    def body(*refs):
        a_ref, b_ref = refs[0], refs[1]
        add_ref = refs[2] if has_add else None
        o_ref, acc_ref = refs[n_in + s_in], refs[n_in + s_in + 1 + s_out]
        k = pl.program_id(2)
        side_refs = (refs[n_in:n_in + s_in], refs[n_in + s_in + 1:n_in + s_in + 1 + s_out], refs[n_in + s_in + 2 + s_out:])
        if side is not None:
            pl.when(_grid_edge(grid, first=True))(lambda: side.start(*side_refs))
        part = lax.dot_general(a_ref[...].astype(BF16), b_ref[...].astype(BF16), dims, preferred_element_type=F32)

        @pl.when(k == 0)
        def _():
            acc_ref[...] = part

        @pl.when(k != 0)
        def _():
            acc_ref[...] += part

        @pl.when(k == nk - 1)
        def _():
            r = acc_ref[...]
            if has_add:
                r = r + add_ref[...].astype(F32)
            if layout == "cols4":
                o_ref[0, 0] = r.astype(out_dtype)
            elif layout == "rows4":
                o_ref[0, 0] = r[:tm // 2].astype(out_dtype)
                o_ref[1, 0] = r[tm // 2:].astype(out_dtype)
            else:
                o_ref[...] = r.astype(out_dtype)

        if side is not None:
            pl.when(_grid_edge(grid, first=False))(lambda: side.finish(*side_refs))

    o_spec = pl.BlockSpec((tm, tn), lambda i, j, k: (i, j))
    out_spec, out_dims = o_spec, (M, N)
    if layout == "cols4":
        out_spec, out_dims = pl.BlockSpec((1, 1, tm, tn), lambda i, j, k: (i, j, 0, 0)), (2, 4, tm, tn)
    elif layout == "rows4":
        out_spec, out_dims = pl.BlockSpec((2, 1, tm // 2, tn), lambda i, j, k: (0, i, 0, j)), (2, 4, tm // 2, N)
    out_shape = jax.ShapeDtypeStruct(out_dims, out_dtype)
    semantics = ("parallel", "parallel", "arbitrary")
    if side is None:
        in_specs, operands, out_specs, scratch = [], (), out_spec, []
    else:
        in_specs, operands, scratch = [ANY] * s_in, tuple(side.operands), list(side.sems)
        out_specs, out_shape = [out_spec] + [ANY] * s_out, [out_shape] + list(side.out_shapes)
        semantics = ("arbitrary",) * 3
    res = pl.pallas_call(
        body, name=name, grid=grid,
        in_specs=[a_spec, b_spec] + ([o_spec] if has_add else []) + in_specs, out_specs=out_specs,
        out_shape=out_shape, scratch_shapes=[pltpu.VMEM((tm, tn), F32)] + scratch,
        compiler_params=pltpu.CompilerParams(dimension_semantics=semantics, vmem_limit_bytes=VMEM_LIMIT),
    )(*((a, b) + ((addend,) if has_add else ()) + operands))
    return res if side is None else (res[0], res[1:])


ROW_TILE = 256
NORM_TILE = 512


def _rmsnorm_fwd(x, g, name, side=None):
    L, D = x.shape
    tr = min(NORM_TILE, L)
    s_in, s_out = (len(side.operands), len(side.out_shapes)) if side is not None else (0, 0)
    grid = (L // tr,)

    def body(*refs):
        x_ref, g_ref, u_ref = refs[0], refs[1], refs[2 + s_in]
        side_refs = (refs[2:2 + s_in], refs[3 + s_in:3 + s_in + s_out], refs[3 + s_in + s_out:])
        if side is not None:
            pl.when(_grid_edge(grid, first=True))(lambda: side.start(*side_refs))
        xv = x_ref[...]
        r = lax.rsqrt(jnp.mean(xv * xv, axis=1, keepdims=True) + EPS)
        u_ref[...] = (xv * r * g_ref[...]).astype(BF16)
        if side is not None:
            pl.when(_grid_edge(grid, first=False))(lambda: side.finish(*side_refs))

    row = pl.BlockSpec((tr, D), lambda i: (i, 0))
    res = pl.pallas_call(
        body, name=name, grid=grid,
        in_specs=[row, pl.BlockSpec((1, D), lambda i: (0, 0))] + [ANY] * s_in, out_specs=[row] + [ANY] * s_out,
        out_shape=[jax.ShapeDtypeStruct((L, D), BF16)] + (list(side.out_shapes) if side is not None else []),
        scratch_shapes=list(side.sems) if side is not None else [],
        compiler_params=pltpu.CompilerParams(dimension_semantics=("arbitrary",), vmem_limit_bytes=VMEM_LIMIT),
    )(x, g, *(side.operands if side is not None else ()))
    return res[0] if side is None else (res[0], res[1:])


def _rmsnorm_bwd(du, x, g, dres, name):
    L, D = x.shape

    def body(du_ref, x_ref, g_ref, dres_ref, dx_ref, dxb_ref, dg_ref):
        i = pl.program_id(0)
        xv, duv = x_ref[...], du_ref[...]
        r = lax.rsqrt(jnp.mean(xv * xv, axis=1, keepdims=True) + EPS)
        t = duv * g_ref[...]
        dx = dres_ref[...] + r * t - xv * (r * r * r) * jnp.mean(t * xv, axis=1, keepdims=True)
        dx_ref[...] = dx
        dxb_ref[...] = dx.astype(BF16)
        dg = jnp.sum(duv * xv * r, axis=0, keepdims=True)

        @pl.when(i == 0)
        def _():
            dg_ref[...] = dg

        @pl.when(i != 0)
        def _():
            dg_ref[...] += dg

    row = pl.BlockSpec((ROW_TILE, D), lambda i: (i, 0))
    vec = pl.BlockSpec((1, D), lambda i: (0, 0))
    return pl.pallas_call(
        body, name=name, grid=(L // ROW_TILE,),
        in_specs=[row, row, vec, row], out_specs=[row, row, vec],
        out_shape=[jax.ShapeDtypeStruct((L, D), F32), jax.ShapeDtypeStruct((L, D), BF16), jax.ShapeDtypeStruct((1, D), F32)],
        compiler_params=pltpu.CompilerParams(dimension_semantics=("arbitrary",), vmem_limit_bytes=VMEM_LIMIT),
    )(du, x, g, dres)


def _loss_head(h, target, g):
    L, D = h.shape

    def body(h_ref, t_ref, g_ref, dh_ref, dhb_ref, loss_ref, dg_ref):
        i = pl.program_id(0)
        hv = h_ref[...]
        r = lax.rsqrt(jnp.mean(hv * hv, axis=1, keepdims=True) + EPS)
        diff = hv * r * g_ref[...] - t_ref[...]
        part = 0.5 * jnp.sum(jnp.mean(diff * diff, axis=1, keepdims=True), axis=0, keepdims=True)
        dy = diff * (1.0 / D)
        t = dy * g_ref[...]
        dh = r * t - hv * (r * r * r) * jnp.mean(t * hv, axis=1, keepdims=True)
        dh_ref[...] = dh
        dhb_ref[...] = dh.astype(BF16)
        dg = jnp.sum(dy * hv * r, axis=0, keepdims=True)

        @pl.when(i == 0)
        def _():
            dg_ref[...] = dg
            loss_ref[...] = jnp.broadcast_to(part, (1, 128))

        @pl.when(i != 0)
        def _():
            dg_ref[...] += dg
            loss_ref[...] += jnp.broadcast_to(part, (1, 128))

    row = pl.BlockSpec((ROW_TILE, D), lambda i: (i, 0))
    vec = pl.BlockSpec((1, D), lambda i: (0, 0))
    return pl.pallas_call(
        body, name="loss_head", grid=(L // ROW_TILE,),
        in_specs=[row, row, vec], out_specs=[row, row, pl.BlockSpec((1, 128), lambda i: (0, 0)), vec],
        out_shape=[jax.ShapeDtypeStruct((L, D), F32), jax.ShapeDtypeStruct((L, D), BF16), jax.ShapeDtypeStruct((1, 128), F32),
                   jax.ShapeDtypeStruct((1, D), F32)],
        compiler_params=pltpu.CompilerParams(dimension_semantics=("arbitrary",), vmem_limit_bytes=VMEM_LIMIT),
    )(h, target, g)


FF_TILE = 1408


FF_COLS = 512


def _ffn_in(u, wg, wu, tm, side):
    L, D = u.shape
    F = wg.shape[1]
    tm = min(tm, L)
    s_in, s_out = len(side.operands), len(side.out_shapes)
    grid = (L // tm, F // FF_COLS)

    def body(*refs):
        u_ref, wg_ref, wu_ref = refs[:3]
        g_ref, up_ref, act_ref = refs[3 + s_in:6 + s_in]
        side_refs = (refs[3:3 + s_in], refs[6 + s_in:6 + s_in + s_out], refs[6 + s_in + s_out:])
        pl.when(_grid_edge(grid, first=True))(lambda: side.start(*side_refs))
        g = jnp.dot(u_ref[...], wg_ref[...], preferred_element_type=F32)
        v = jnp.dot(u_ref[...], wu_ref[...], preferred_element_type=F32)
        s = _sigmoid(g)
        g_ref[...] = (v * (s * (1.0 + g * (1.0 - s)))).astype(BF16)
        up_ref[...] = (g * s).astype(BF16)
        act_ref[...] = (g * s * v).astype(BF16)
        pl.when(_grid_edge(grid, first=False))(lambda: side.finish(*side_refs))

    w_spec = pl.BlockSpec((D, FF_COLS), lambda i, j: (0, j))
    o_spec = pl.BlockSpec((tm, FF_COLS), lambda i, j: (i, j))
    out = jax.ShapeDtypeStruct((L, F), BF16)
    res = pl.pallas_call(
        body, name="ffn_gate_up", grid=grid,
        in_specs=[pl.BlockSpec((tm, D), lambda i, j: (i, 0)), w_spec, w_spec] + [ANY] * s_in,
        out_specs=[o_spec] * 3 + [ANY] * s_out, out_shape=[out] * 3 + list(side.out_shapes),
        scratch_shapes=list(side.sems),
        compiler_params=pltpu.CompilerParams(dimension_semantics=("arbitrary", "arbitrary"), vmem_limit_bytes=VMEM_LIMIT),
    )(u, wg, wu, *side.operands)
    return res[0], res[1], res[2], res[3:]


def _ffn_back(dh, w_down, gate_factor, up_factor, tm):
    L, D = dh.shape
    F = w_down.shape[0]
    tm = min(tm, L)

    def body(dh_ref, w_ref, g_ref, u_ref, dg_ref, du_ref):
        d = lax.dot_general(dh_ref[...], w_ref[...], (((1,), (1,)), ((), ())), preferred_element_type=F32)
        dg_ref[...] = (d * g_ref[...].astype(F32)).astype(BF16)
        du_ref[...] = (d * u_ref[...].astype(F32)).astype(BF16)

    blk = pl.BlockSpec((tm, FF_COLS), lambda i, j: (i, j))
    out = jax.ShapeDtypeStruct((L, F), BF16)
    return pl.pallas_call(
        body, name="ffn_back", grid=(L // tm, F // FF_COLS),
        in_specs=[pl.BlockSpec((tm, D), lambda i, j: (i, 0)), pl.BlockSpec((FF_COLS, D), lambda i, j: (j, 0)), blk, blk],
        out_specs=[blk, blk], out_shape=[out, out],
        compiler_params=pltpu.CompilerParams(dimension_semantics=("parallel", "parallel"), vmem_limit_bytes=VMEM_LIMIT),
    )(dh, w_down, gate_factor, up_factor)


def _adamw(w, g, m, v, rows, name):
    R, C = w.shape
    assert R % rows == 0, (name, R, rows)

    def body(w_ref, g_ref, m_ref, v_ref, d_ref, nm_ref, nv_ref):
        gv = g_ref[...]
        mn = ADAM_B1 * m_ref[...] + (1.0 - ADAM_B1) * gv
        vn = ADAM_B2 * v_ref[...] + (1.0 - ADAM_B2) * (gv * gv)
        m_hat = mn / (1.0 - ADAM_B1 ** ADAM_STEP)
        v_hat = vn / (1.0 - ADAM_B2 ** ADAM_STEP)
        d_ref[...] = -ADAM_LR * (m_hat / (jnp.sqrt(v_hat) + ADAM_EPS) + ADAM_WD * w_ref[...])
        nm_ref[...] = mn
        nv_ref[...] = vn

    blk = pl.BlockSpec((rows, C), lambda i: (i, 0))
    out = jax.ShapeDtypeStruct((R, C), F32)
    return pl.pallas_call(
        body, name=name, grid=(R // rows,), in_specs=[blk] * 4, out_specs=[blk] * 3, out_shape=[out] * 3,
        compiler_params=pltpu.CompilerParams(dimension_semantics=("parallel",), vmem_limit_bytes=VMEM_LIMIT),
    )(w, g, m, v)


ANY = pl.BlockSpec(memory_space=pl.ANY)
N_DEV = 8


def _allgather_small(p, name):
    R, C = p.shape

    def body(p_ref, out_ref, send_sems, recv_sems, local_sem):
        x, y, c = lax.axis_index("x"), lax.axis_index("y"), lax.axis_index("c")
        me = 4 * x + 2 * y + c
        mine = pltpu.make_async_copy(p_ref, out_ref.at[me], local_sem)
        mine.start()

        def peer(d):
            return (x ^ ((d >> 2) & 1), y ^ ((d >> 1) & 1), c ^ (d & 1))

        def copy(d, block):
            return pltpu.make_async_remote_copy(src_ref=p_ref, dst_ref=out_ref.at[block], send_sem=send_sems.at[d - 1],
                                                recv_sem=recv_sems.at[d - 1], device_id=peer(d), device_id_type=MESH)

        sends = [copy(d, me) for d in range(1, N_DEV)]
        for cp in sends:
            cp.start()
        for d in range(1, N_DEV):
            px, py, pc = peer(d)
            copy(d, 4 * px + 2 * py + pc).wait_recv()
        for cp in sends:
            cp.wait_send()
        mine.wait()

    return pl.pallas_call(
        body, name=name, out_shape=jax.ShapeDtypeStruct((N_DEV, R, C), p.dtype),
        in_specs=[pl.BlockSpec(memory_space=pltpu.VMEM)], out_specs=pl.BlockSpec(memory_space=pltpu.VMEM),
        scratch_shapes=[pltpu.SemaphoreType.DMA((N_DEV - 1,)), pltpu.SemaphoreType.DMA((N_DEV - 1,)), pltpu.SemaphoreType.DMA],
    )(p)


def _other_chips(x, y):
    return [(1 - x, y), (x, 1 - y), (1 - x, 1 - y)]


def _run_side(side, name):
    s_in, s_out = len(side.operands), len(side.out_shapes)

    def body(*refs):
        parts = (refs[:s_in], refs[s_in:s_in + s_out], refs[s_in + s_out:])
        side.start(*parts)
        side.finish(*parts)

    return pl.pallas_call(body, name=name, out_shape=list(side.out_shapes), in_specs=[ANY] * s_in, out_specs=[ANY] * s_out,
                          scratch_shapes=list(side.sems))(*side.operands)


def _gather_side(shards):
    T = len(shards)
    halves = [s.shape[0] // 2 for s in shards]

    def tools(ins, outs, sems):
        send_sems, recv_sems = sems
        x, y, c = lax.axis_index("x"), lax.axis_index("y"), lax.axis_index("c")

        def rows(t, px, py, pc):
            return outs[t].at[2 * px + py, pl.ds(pc * halves[t], halves[t]), :]

        def copy(t, k, block, to, own=False):
            src = ins[t].at[pl.ds(c * halves[t], halves[t]), :] if own else rows(t, *block)
            return pltpu.make_async_remote_copy(src_ref=src, dst_ref=rows(t, *block), send_sem=send_sems.at[t, k],
                                                recv_sem=recv_sems.at[t, k], device_id=to, device_id_type=MESH)

        return (x, y, c), _other_chips(x, y), copy

    def start(ins, outs, sems):
        (x, y, c), chips, copy = tools(ins, outs, sems)
        for t in range(T):
            for j, chip in enumerate(chips):
                copy(t, j, (x, y, c), (*chip, c), own=True).start()

    def finish(ins, outs, sems):
        (x, y, c), chips, copy = tools(ins, outs, sems)
        for t in range(T):
            for j, chip in enumerate(chips):
                copy(t, j, (*chip, c), (x, y, c)).wait_recv()
                copy(t, 3 + j, (*chip, c), (x, y, 1 - c)).start()
        for t in range(T):
            for j, chip in enumerate(chips):
                copy(t, 3 + j, (*chip, 1 - c), (x, y, c)).wait_recv()
        for t in range(T):
            for j, chip in enumerate(chips):
                copy(t, j, (x, y, c), (*chip, c), own=True).wait_send()
                copy(t, 3 + j, (*chip, c), (x, y, 1 - c)).wait_send()

    return _Side(shards, [jax.ShapeDtypeStruct((4,) + s.shape, s.dtype) for s in shards],
                 [pltpu.SemaphoreType.DMA((T, 6)), pltpu.SemaphoreType.DMA((T, 6))], start, finish)


def _with_own_shard(gathered, shard, slot):
    return lax.dynamic_update_slice(gathered, shard[None], (slot,) + (0,) * shard.ndim)


def _pair_side(grads, chip_major=False):
    T = len(grads)

    def copies(ins, outs, sems):
        send_sems, recv_sems = sems
        x, y, c = lax.axis_index("x"), lax.axis_index("y"), lax.axis_index("c")
        theirs = (lambda r: r.at[:, 1 - c]) if chip_major else (lambda r: r.at[1 - c])
        return [pltpu.make_async_remote_copy(src_ref=theirs(ins[t]), dst_ref=outs[t], send_sem=send_sems.at[t],
                                             recv_sem=recv_sems.at[t], device_id=(x, y, 1 - c), device_id_type=MESH)
                for t in range(T)]

    def start(ins, outs, sems):
        for cp in copies(ins, outs, sems):
            cp.start()

    def finish(ins, outs, sems):
        for cp in copies(ins, outs, sems):
            cp.wait()

    return _Side(grads, [jax.ShapeDtypeStruct((4,) + g.shape[2:], g.dtype) for g in grads],
                 [pltpu.SemaphoreType.DMA((T,)), pltpu.SemaphoreType.DMA((T,))], start, finish)


def _scatter_side(parts):
    T = len(parts)

    def tools(ins, outs, sems):
        send_sems, recv_sems = sems
        x, y, c = lax.axis_index("x"), lax.axis_index("y"), lax.axis_index("c")

        def copy(t, j, src_slot, dst_slot, chip):
            return pltpu.make_async_remote_copy(src_ref=ins[t].at[src_slot], dst_ref=outs[t].at[dst_slot],
                                                send_sem=send_sems.at[t, j], recv_sem=recv_sems.at[t, j],
                                                device_id=(*chip, c), device_id_type=MESH)

        return 2 * x + y, _other_chips(x, y), copy

    def start(ins, outs, sems):
        my_chip, chips, copy = tools(ins, outs, sems)
        for t in range(T):
            for j, (px, py) in enumerate(chips):
                copy(t, j, 2 * px + py, my_chip, (px, py)).start()

    def finish(ins, outs, sems):
        my_chip, chips, copy = tools(ins, outs, sems)
        for t in range(T):
            for j, (px, py) in enumerate(chips):
                copy(t, j, my_chip, 2 * px + py, (px, py)).wait_recv()
        for t in range(T):
            for j, (px, py) in enumerate(chips):
                copy(t, j, 2 * px + py, my_chip, (px, py)).wait_send()

    return _Side(parts, [jax.ShapeDtypeStruct(p.shape, p.dtype) for p in parts],
                 [pltpu.SemaphoreType.DMA((T, 3)), pltpu.SemaphoreType.DMA((T, 3))], start, finish)


def _pair_share(fulls):
    T = len(fulls)

    def body(*refs):
        outs = refs[T:2 * T]
        send_sems, recv_sems = refs[2 * T:]
        x, y, c = lax.axis_index("x"), lax.axis_index("y"), lax.axis_index("c")

        def copy(t, half):
            return pltpu.make_async_remote_copy(src_ref=outs[t].at[half], dst_ref=outs[t].at[half], send_sem=send_sems.at[t],
                                                recv_sem=recv_sems.at[t], device_id=(x, y, 1 - c), device_id_type=MESH)

        for t in range(T):
            copy(t, c).start()
        for t in range(T):
            copy(t, 1 - c).wait_recv()
        for t in range(T):
            copy(t, c).wait_send()

    return pl.pallas_call(
        body, name="grad_pair_share", out_shape=[jax.ShapeDtypeStruct(f.shape, f.dtype) for f in fulls],
        in_specs=[ANY] * T, out_specs=[ANY] * T, input_output_aliases={t: t for t in range(T)},
        scratch_shapes=[pltpu.SemaphoreType.DMA((T,)), pltpu.SemaphoreType.DMA((T,))],
    )(*fulls)


def _row_tile(hr):
    return next(r for r in (128, 64, 32, 16, 8) if hr % r == 0)


def _pair_add(g, got, c, name, chip_major=False):
    _, _, hr, C = g.shape
    tr = _row_tile(hr)
    own = (lambda k, i, c_ref: (k, c_ref[0], i, 0)) if chip_major else (lambda k, i, c_ref: (c_ref[0], k, i, 0))

    def body(c_ref, g_ref, r_ref, o_ref):
        o_ref[...] = (g_ref[0].astype(F32) + r_ref[...].astype(F32)).astype(o_ref.dtype)

    return pl.pallas_call(
        body, name=name,
        grid_spec=pltpu.PrefetchScalarGridSpec(
            num_scalar_prefetch=1, grid=(4, hr // tr),
            in_specs=[pl.BlockSpec((1, 1, tr, C), own),
                      pl.BlockSpec((1, tr, C), lambda k, i, c_ref: (k, i, 0))],
            out_specs=pl.BlockSpec((1, tr, C), lambda k, i, c_ref: (k, i, 0))),
        out_shape=jax.ShapeDtypeStruct(got.shape, g.dtype),
        compiler_params=pltpu.CompilerParams(dimension_semantics=("parallel", "parallel"), vmem_limit_bytes=VMEM_LIMIT),
    )(jnp.reshape(c, (1,)).astype(jnp.int32), g, got)


def _chip_sum(parts, slots, chip, c, name):
    _, hr, C = parts.shape
    tr = _row_tile(hr)

    def body(chip_ref, half_ref, own_ref, a_ref, b_ref, c_ref, o_ref):
        acc = own_ref[...].astype(F32)
        for r in (a_ref, b_ref, c_ref):
            acc = acc + r[...].astype(F32)
        o_ref[...] = acc

    other = lambda j: pl.BlockSpec((1, tr, C), lambda i, chip_ref, half_ref: ((chip_ref[0] + j) % 4, i, 0))
    return pl.pallas_call(
        body, name=name,
        grid_spec=pltpu.PrefetchScalarGridSpec(
            num_scalar_prefetch=2, grid=(hr // tr,),
            in_specs=[pl.BlockSpec((1, tr, C), lambda i, chip_ref, half_ref: (chip_ref[0], i, 0)),
                      other(1), other(2), other(3)],
            out_specs=pl.BlockSpec((1, tr, C), lambda i, chip_ref, half_ref: (half_ref[0], i, 0))),
        out_shape=jax.ShapeDtypeStruct((2, hr, C), F32),
        compiler_params=pltpu.CompilerParams(dimension_semantics=("parallel",), vmem_limit_bytes=VMEM_LIMIT),
    )(jnp.reshape(chip, (1,)).astype(jnp.int32), jnp.reshape(c, (1,)).astype(jnp.int32), parts, slots, slots, slots)


def _sum_slots(parts, name):
    n, hr, C = parts.shape
    tr = _row_tile(hr)

    def body(p_ref, o_ref):
        acc = p_ref[0].astype(F32)
        for k in range(1, n):
            acc = acc + p_ref[k].astype(F32)
        o_ref[...] = acc

    return pl.pallas_call(
        body, name=name, grid=(hr // tr,),
        in_specs=[pl.BlockSpec((n, tr, C), lambda i: (0, i, 0))], out_specs=pl.BlockSpec((tr, C), lambda i: (i, 0)),
        out_shape=jax.ShapeDtypeStruct((hr, C), F32),
        compiler_params=pltpu.CompilerParams(dimension_semantics=("parallel",), vmem_limit_bytes=VMEM_LIMIT),
    )(parts)


PACK_ROWS, PACK_COLS = 16, 3072
_PACK = [("norm_mix_g", 0, 1, 2048), ("conv_w", 1, 4, 3072), ("conv_b", 5, 1, 3072), ("dt_bias", 6, 1, 32),
         ("a_log", 7, 1, 32), ("d_skip", 8, 1, 32), ("ssd_norm_g", 9, 1, 2048), ("i_bias", 10, 1, 8), ("f_bias", 11, 1, 8),
         ("mlstm_norm_g", 12, 1, 2048), ("norm_ffn_g", 13, 1, 2048), ("final_norm_g", 14, 1, 2048), ("loss", 15, 1, 1)]
_WEIGHTS = ["norm_mix_g", "w_in", "conv_w", "conv_b", "dt_bias", "a_log", "d_skip", "ssd_norm_g", "i_bias", "f_bias",
            "mlstm_norm_g", "w_out", "norm_ffn_g", "w_gate", "w_up", "w_down", "final_norm_g"]
_BIG = ["w_in", "w_out", "w_gate", "w_up", "w_down"]


def _pack(vals):
    rows = []
    for name, _, r, w in _PACK:
        v = vals.get(name)
        v = jnp.zeros((r, w), F32) if v is None else v.reshape(r, w).astype(F32)
        rows.append(jnp.concatenate([v, jnp.zeros((r, PACK_COLS - w), F32)], axis=1) if w < PACK_COLS else v)
    return jnp.concatenate(rows, axis=0)


def _unpack(p):
    return {name: p[r0:r0 + r, :w] for name, r0, r, w in _PACK}


def kernel(x, norm_mix_g, w_in, conv_w, conv_b, dt_bias, a_log, d_skip, ssd_norm_g, i_bias, f_bias, mlstm_norm_g, w_out, norm_ffn_g, w_gate, w_up, w_down, final_norm_g, loss_target, m_norm_mix_g, m_w_in, m_conv_w, m_conv_b, m_dt_bias, m_a_log, m_d_skip, m_ssd_norm_g, m_i_bias, m_f_bias, m_mlstm_norm_g, m_w_out, m_norm_ffn_g, m_w_gate, m_w_up, m_w_down, m_final_norm_g, v_norm_mix_g, v_w_in, v_conv_w, v_conv_b, v_dt_bias, v_a_log, v_d_skip, v_ssd_norm_g, v_i_bias, v_f_bias, v_mlstm_norm_g, v_w_out, v_norm_ffn_g, v_w_gate, v_w_up, v_w_down, v_final_norm_g):
    weights = dict(norm_mix_g=norm_mix_g, w_in=w_in, conv_w=conv_w, conv_b=conv_b, dt_bias=dt_bias, a_log=a_log, d_skip=d_skip,
                   ssd_norm_g=ssd_norm_g, i_bias=i_bias, f_bias=f_bias, mlstm_norm_g=mlstm_norm_g, w_out=w_out,
                   norm_ffn_g=norm_ffn_g, w_gate=w_gate, w_up=w_up, w_down=w_down, final_norm_g=final_norm_g)
    mom1 = dict(norm_mix_g=m_norm_mix_g, w_in=m_w_in, conv_w=m_conv_w, conv_b=m_conv_b, dt_bias=m_dt_bias, a_log=m_a_log,
                d_skip=m_d_skip, ssd_norm_g=m_ssd_norm_g, i_bias=m_i_bias, f_bias=m_f_bias, mlstm_norm_g=m_mlstm_norm_g,
                w_out=m_w_out, norm_ffn_g=m_norm_ffn_g, w_gate=m_w_gate, w_up=m_w_up, w_down=m_w_down,
                final_norm_g=m_final_norm_g)
    mom2 = dict(norm_mix_g=v_norm_mix_g, w_in=v_w_in, conv_w=v_conv_w, conv_b=v_conv_b, dt_bias=v_dt_bias, a_log=v_a_log,
                d_skip=v_d_skip, ssd_norm_g=v_ssd_norm_g, i_bias=v_i_bias, f_bias=v_f_bias, mlstm_norm_g=v_mlstm_norm_g,
                w_out=v_w_out, norm_ffn_g=v_norm_ffn_g, w_gate=v_w_gate, w_up=v_w_up, w_down=v_w_down,
                final_norm_g=v_final_norm_g)
    xi, yi, ci = lax.axis_index("x"), lax.axis_index("y"), lax.axis_index("c")
    chip = 2 * xi + yi
    xs, tgt = x[0], loss_target[0]

    shards = {n: weights[n][0].astype(BF16) for n in _BIG}
    gathered = lambda n, g: _with_own_shard(g, shards[n], chip)
    by_cols = lambda g, width: jnp.transpose(g, (1, 0, 2)).reshape(D_MODEL, width)
    u1, (g_in,) = _rmsnorm_fwd(xs, norm_mix_g, "norm_mix_fwd", side=_gather_side([shards["w_in"]]))
    g_in = gathered("w_in", g_in)
    W_ssd, W_ml, W_small = _split_w_in([g_in[k] for k in range(4)])
    cw_all = _allgather_small(jnp.concatenate([conv_w[0], jnp.zeros((4, 768), F32)], axis=0), "allgather_conv_w")
    conv_w_full = jnp.concatenate([cw_all[2 * k, :SSD_CONV] for k in range(4)], axis=1)
    cwg, cbg = _conv_to_groups(conv_w_full), _conv_to_groups(conv_b)
    dtb_row, al_row = _small_row(dt_bias, 0), _small_row(a_log, 0)
    ib_row, fb_row = _small_row(i_bias, LANE_I), _small_row(f_bias, LANE_F)
    dskip_lane = jnp.repeat(d_skip, 64, axis=1)
    fng = final_norm_g[None]

    p_ssd, (g_out,) = _matmul(u1, W_ssd, "nn", F32, 1024, 1024, 2048, "proj_ssd", side=_gather_side([shards["w_out"]]))
    p_ml, (g_gate,) = _matmul(u1, W_ml, "nn", F32, 1024, 1024, 2048, "proj_ml", side=_gather_side([shards["w_gate"]]))
    p_small = _matmul(u1, W_small, "nn", F32, 1024, 128, 2048, "proj_small")
    y_mix, ssd_st = _ssd_forward(p_ssd, p_small, cwg, cbg, dtb_row, al_row, dskip_lane, ssd_norm_g)
    y_mix, ml_c, ml_nm = _mlstm_forward(p_ml, p_small, ib_row, fb_row, mlstm_norm_g, y_mix)
    W_out = gathered("w_out", g_out).reshape(2 * SSD_WIDTH, D_MODEL)
    h1, (g_up,) = _matmul(y_mix, W_out, "nn", F32, 1024, 1024, 2048, "out_proj", addend=xs,
                          side=_gather_side([shards["w_up"]]))
    u2 = _rmsnorm_fwd(h1, norm_ffn_g, "norm_ffn_fwd")
    W_gate = by_cols(gathered("w_gate", g_gate), D_FF)
    W_up = by_cols(gathered("w_up", g_up), D_FF)
    gate, up, act, (g_down,) = _ffn_in(u2, W_gate, W_up, 1024, _gather_side([shards["w_down"]]))
    W_down = gathered("w_down", g_down).reshape(D_FF, D_MODEL)
    h2 = _matmul(act, W_down, "nn", F32, 1024, 1024, 2 * FF_TILE, "ffn_down", addend=h1)
    dh2, dh2_b, loss_row, d_fng = _loss_head(h2, tgt, fng)

    dW_down = _matmul(act, dh2_b, "tn", BF16, FF_TILE, 1024, 2048, "dw_down", layout="rows4")
    d_gate, d_up = _ffn_back(dh2_b, W_down, gate, up, 1024)
    du2 = _matmul(d_gate, W_gate, "nt", F32, 1024, 1024, 2 * FF_TILE, "du2_gate")
    du2 = _matmul(d_up, W_up, "nt", F32, 1024, 1024, 2 * FF_TILE, "du2_up", addend=du2)
    dW_gate = _matmul(u2, d_gate, "tn", BF16, 1024, FF_TILE, 2048, "dw_gate", layout="cols4")
    dW_up = _matmul(u2, d_up, "tn", BF16, 1024, FF_TILE, 2048, "dw_up", layout="cols4")
    dh1, dh1_b, d_ffn_g = _rmsnorm_bwd(du2, h1, norm_ffn_g, dh2, "norm_ffn_bwd")
    dW_out, ffn_got = _matmul(y_mix, dh1_b, "tn", BF16, 1024, 1024, 2048, "dw_out", layout="rows4",
                              side=_pair_side([dW_gate, dW_up, dW_down]))
    dy_mix, out_got = _matmul(dh1_b, W_out, "nt", F32, 1024, 1024, 2048, "d_mix", side=_pair_side([dW_out]))
    early = ["w_out", "w_gate", "w_up", "w_down"]
    early_g = [dW_out, dW_gate, dW_up, dW_down]
    early_got = list(out_got) + list(ffn_got)
    parts = {n: _pair_add(g, r, ci, "grad_pair_add_" + n) for g, r, n in zip(early_g, early_got, early)}
    (d_ssd, d_small, d_cw, d_cb, d_dtb, d_alog, d_dskip, d_sng), early_slots = _ssd_backward(
        p_ssd, p_small, cwg, cbg, dtb_row, al_row, dskip_lane, ssd_norm_g, ssd_st, dy_mix,
        side=_scatter_side([parts[n] for n in early]))
    slots = dict(zip(early, early_slots))
    d_ml, d_small, d_ib, d_fb, d_mng = _mlstm_backward(p_ml, p_small, ib_row, fb_row, mlstm_norm_g, ml_c, ml_nm, dy_mix, d_small)
    dW_ssd = _matmul(u1, d_ssd, "tn", BF16, 1024, 1280, 2048, "dw_ssd")
    dW_ml = _matmul(u1, d_ml, "tn", BF16, 1024, 1024, 2048, "dw_ml")
    dW_small = _matmul(u1, d_small, "tn", BF16, 1024, 128, 2048, "dw_small")
    g_w_in = jnp.stack(_merge_w_in(dW_ssd, dW_ml, dW_small)).reshape(4, 2, D_MODEL // 2, SHARD_IN)
    (got_in,) = _run_side(_pair_side([g_w_in], chip_major=True), "grad_pair_exchange_w_in")
    parts["w_in"] = _pair_add(g_w_in, got_in, ci, "grad_pair_add_w_in", chip_major=True)
    du1, (slots["w_in"],) = _matmul(d_ssd, W_ssd, "nt", F32, 1024, 1024, 2560, "du1_ssd",
                                    side=_scatter_side([parts["w_in"]]))
    du1 = _matmul(d_ml, W_ml, "nt", F32, 1024, 1024, 3072, "du1_ml", addend=du1)
    du1 = _matmul(d_small, W_small, "nt", F32, 1024, 1024, 128, "du1_small", addend=du1)
    grad_x, _, d_mix_g = _rmsnorm_bwd(du1, xs, norm_mix_g, dh1, "norm_mix_bwd")

    fulls = _pair_share([_chip_sum(parts[n], slots[n], chip, ci, "grad_chip_sum_" + n) for n in _BIG])
    grads = {n: f.reshape(weights[n].shape[1:]) for n, f in zip(_BIG, fulls)}

    small = _pack(dict(norm_mix_g=d_mix_g, conv_w=_conv_from_groups(d_cw), conv_b=_conv_from_groups(d_cb),
                       dt_bias=d_dtb[:, 0:32], a_log=d_alog[:, 0:32], d_skip=d_dskip.reshape(SSD_HEADS, 64).sum(axis=1),
                       ssd_norm_g=d_sng, i_bias=d_ib[:, LANE_I:LANE_I + 8], f_bias=d_fb[:, LANE_F:LANE_F + 8],
                       mlstm_norm_g=d_mng, norm_ffn_g=d_ffn_g, final_norm_g=d_fng, loss=loss_row[:, 0:1]))
    total = _sum_slots(_allgather_small(small, "allgather_small_grads"), "small_grad_sum")
    small_w = {n: weights[n] for n in _WEIGHTS if n not in _BIG and n != "conv_w"}
    sd, sm, sv = _adamw(_pack(small_w), total, _pack({n: mom1[n] for n in small_w}), _pack({n: mom2[n] for n in small_w}),
                        PACK_ROWS, "adamw_small")
    tot, sd, sm, sv = _unpack(total), _unpack(sd), _unpack(sm), _unpack(sv)
    for n in small_w:
        grads[n] = tot[n].reshape(weights[n].shape)
    grads["conv_w"] = lax.dynamic_slice_in_dim(tot["conv_w"], chip * 768, 768, axis=1)
    loss = tot["loss"][0, 0]

    delta, new_m, new_v = {}, {}, {}
    for n in _BIG + ["conv_w"]:
        w2 = weights[n][0]
        d, nm, nv = _adamw(w2, grads[n], mom1[n][0], mom2[n][0], _row_tile(w2.shape[0]) if n != "conv_w" else SSD_CONV,
                           "adamw_" + n)
        delta[n], new_m[n], new_v[n] = d[None], nm[None], nv[None]
        grads[n] = grads[n][None]
    for n in small_w:
        delta[n], new_m[n], new_v[n] = (t[n].reshape(weights[n].shape) for t in (sd, sm, sv))
    return (loss, grad_x[None], *[grads[n] for n in _WEIGHTS], *[delta[n] for n in _WEIGHTS],
            *[new_m[n] for n in _WEIGHTS], *[new_v[n] for n in _WEIGHTS])
```

```python
import functools

import jax
import jax.numpy as jnp
import numpy as np
from jax import lax
from jax.experimental import pallas as pl
from jax.experimental.pallas import tpu as pltpu

F32 = jnp.float32
BF16 = jnp.bfloat16

D_MODEL = 2048
SSD_HEADS = 32
SSD_GROUPS = 4
SSD_STATE = 128
SSD_WIDTH = 2048
SSD_CONV = 4
SSD_GROUP_COLS = 1280
SSD_XBC_COLS = 768
ML_HEADS = 8
ML_HEAD_COLS = 768
ML_DK = 128
ML_DV = 256
CHUNK = 128
SMALL_COLS = 128
LANE_I = 32
LANE_F = 40
D_FF = 5632
GATE_SOFTCAP = 15.0
EPS = 1e-6
ADAM_LR, ADAM_B1, ADAM_B2, ADAM_EPS, ADAM_WD, ADAM_STEP = 0.001, 0.9, 0.999, 1e-8, 0.01, 10
MESH = pl.DeviceIdType.MESH
VMEM_LIMIT = 56 * 1024 * 1024


def _dg(a, b, ca, cb):
    return lax.dot_general(a.astype(BF16), b.astype(BF16), (((ca,), (cb,)), ((), ())), preferred_element_type=F32)


@jax.custom_vjp
def _mm(a, b):
    return _dg(a, b, 1, 0)


_mm.defvjp(lambda a, b: (_dg(a, b, 1, 0), (a, b)),
           lambda r, g: (_dg(g, r[1], 1, 1), _dg(r[0], g, 0, 0)))


@jax.custom_vjp
def _mm_nt(a, b):
    return _dg(a, b, 1, 1)


_mm_nt.defvjp(lambda a, b: (_dg(a, b, 1, 1), (a, b)),
              lambda r, g: (_dg(g, r[1], 1, 0), _dg(g, r[0], 0, 0)))


@jax.custom_vjp
def _mm_tn(a, b):
    return _dg(a, b, 0, 0)


_mm_tn.defvjp(lambda a, b: (_dg(a, b, 0, 0), (a, b)),
              lambda r, g: (_dg(r[1], g, 1, 1), _dg(r[0], g, 1, 0)))


def _tri(lower):
    r = lax.broadcasted_iota(jnp.int32, (CHUNK, CHUNK), 0)
    c = lax.broadcasted_iota(jnp.int32, (CHUNK, CHUNK), 1)
    return ((r >= c) if lower else (r <= c)).astype(F32)


def _dg32(t, x):
    return lax.dot_general(t, x, (((1,), (0,)), ((), ())), precision=lax.Precision.HIGHEST, preferred_element_type=F32)


@jax.custom_vjp
def _cumsum(x):
    return _dg32(_tri(True), x)


_cumsum.defvjp(lambda x: (_dg32(_tri(True), x), None), lambda r, g: (_dg32(_tri(False), g),))


_sigmoid = jax.nn.sigmoid


def _silu(x):
    return x * _sigmoid(x)


def _softplus(x):
    return jnp.maximum(x, 0.0) + jnp.log(1.0 + jnp.exp(-jnp.abs(x)))


def _lane_col(m, lane, h):
    return jnp.sum(jnp.where(lane == h, m, 0.0), axis=1, keepdims=True)


SSD_GPS = 4


def _ssd_math(convs, zs, smallb, S_in, dtb_row, alog_row, dskip, ng, g0):
    dt_all = _softplus(smallb + dtb_row)
    a_all = dt_all * (-jnp.exp(alog_row))
    prep = (dt_all, _cumsum(a_all), jnp.sum(a_all, axis=0, keepdims=True))
    ys, S_out = [], []
    for k in range(SSD_GPS):
        y, S = _ssd_group(convs[k][:, 0:512], convs[k][:, 512:640], convs[k][:, 640:768], zs[k], prep, S_in[k],
                          dskip[:, 512 * k:512 * (k + 1)], ng[:, 512 * k:512 * (k + 1)], g0 + k)
        ys.append(y)
        S_out.append(S)
    return tuple(ys), tuple(S_out)


def _ssd_group(cx, cB, cC, z, prep, S_in, dskip, ng, g):
    dt_all, acum_all, alast_all = prep
    lane = lax.broadcasted_iota(jnp.int32, (1, CHUNK), 1)
    row_i = lax.broadcasted_iota(jnp.int32, (CHUNK, CHUNK), 0)
    col_i = lax.broadcasted_iota(jnp.int32, (CHUNK, CHUNK), 1)
    causal = row_i >= col_i
    half = lane < 64
    rhalf = lax.broadcasted_iota(jnp.int32, (CHUNK, 1), 0) < 64
    xs, Bm, Cm = _silu(cx), _silu(cB), _silu(cC)
    cb = _mm_nt(Cm, Bm)

    def lmat(ac):
        acb = jnp.broadcast_to(ac, (CHUNK, CHUNK))
        return jnp.exp(jnp.where(causal, acb - acb.T, -jnp.inf))

    ys, S_out = [], []
    for j in range(4):
        h0 = 8 * g + 2 * j
        ac0, ac1 = _lane_col(acum_all, lane, h0), _lane_col(acum_all, lane, h0 + 1)
        dt0, dt1 = _lane_col(dt_all, lane, h0), _lane_col(dt_all, lane, h0 + 1)
        al0, al1 = _lane_col(alast_all, lane, h0), _lane_col(alast_all, lane, h0 + 1)
        Xp = xs[:, 128 * j:128 * (j + 1)]
        Xd = Xp * jnp.where(half, dt0, dt1)
        ac_sel = jnp.where(half, ac0, ac1)
        al_sel = jnp.where(half, al0, al1)
        Yd = jnp.where(half, _mm(cb * lmat(ac0), Xd), _mm(cb * lmat(ac1), Xd))
        Yoff = _mm_nt(Cm, S_in[j]) * jnp.exp(ac_sel)
        ys.append(Yd + Yoff + dskip[:, 128 * j:128 * (j + 1)] * Xp)
        S_new = _mm_tn(Xd * jnp.exp(al_sel - ac_sel), Bm)
        S_out.append(S_in[j] * jnp.exp(jnp.where(rhalf, al0, al1)) + S_new)
    y = jnp.concatenate(ys, axis=1)
    y = y * _silu(z)
    y = y * lax.rsqrt(jnp.mean(y * y, axis=1, keepdims=True) + EPS) * ng
    return y, tuple(S_out)


def _ssd_window(blk_ref, halo_ref, ext, first, k):
    cols = slice(SSD_GROUP_COLS * k + 512, SSD_GROUP_COLS * (k + 1))
    halo = halo_ref[:, cols]
    ext[k, 0:8, :] = jnp.where(first, jnp.zeros_like(halo), halo)
    ext[k, 8:8 + CHUNK, :] = blk_ref[:, cols]


def _ssd_conv(blk_ref, halo_ref, cw_ref, cb_ref, ext, first, k):
    _ssd_window(blk_ref, halo_ref, ext, first, k)
    conv = jnp.broadcast_to(cb_ref[k], (CHUNK, SSD_XBC_COLS))
    for tap in range(SSD_CONV):
        conv = conv + cw_ref[k, tap:tap + 1, :] * ext[k, pl.ds(5 + tap, CHUNK), :]
    return conv


def _ssd_specs(nc, rev):
    cc = (lambda c: nc - 1 - c) if rev else (lambda c: c)
    width = SSD_GPS * SSD_GROUP_COLS
    return [
        pl.BlockSpec((CHUNK, width), lambda c, g: (cc(c), g)),
        pl.BlockSpec((8, width), lambda c, g: (jnp.maximum(cc(c) * (CHUNK // 8) - 1, 0), g)),
        pl.BlockSpec((CHUNK, SMALL_COLS), lambda c, g: (cc(c), 0)),
        pl.BlockSpec((SSD_GPS, SSD_CONV, SSD_XBC_COLS), lambda c, g: (g, 0, 0)),
        pl.BlockSpec((SSD_GPS, 1, SSD_XBC_COLS), lambda c, g: (g, 0, 0)),
        pl.BlockSpec((1, SMALL_COLS), lambda c, g: (0, 0)),
        pl.BlockSpec((1, SMALL_COLS), lambda c, g: (0, 0)),
        pl.BlockSpec((1, SSD_GPS * 512), lambda c, g: (0, g)),
        pl.BlockSpec((1, SSD_GPS * 512), lambda c, g: (0, g)),
    ]


def _ssd_forward(proj, small, cw, cb, dtb, alog, dskip, ng):
    L = proj.shape[0]
    nc = L // CHUNK

    def body(blk_ref, halo_ref, small_ref, cw_ref, cb_ref, dtb_ref, alog_ref, dskip_ref, ng_ref, y_ref, st_ref, conv_ref,
             carry, ext):
        c, g0 = pl.program_id(0), pl.program_id(1) * SSD_GPS

        @pl.when(c == 0)
        def _():
            for k in range(SSD_GPS):
                carry[g0 + k] = jnp.zeros((4, CHUNK, CHUNK), F32)

        convs = [_ssd_conv(blk_ref, halo_ref, cw_ref, cb_ref, ext, c == 0, k) for k in range(SSD_GPS)]
        for k in range(SSD_GPS):
            conv_ref[:, SSD_XBC_COLS * k:SSD_XBC_COLS * (k + 1)] = convs[k]
        zs = [blk_ref[:, SSD_GROUP_COLS * k:SSD_GROUP_COLS * k + 512] for k in range(SSD_GPS)]
        S_in = tuple(tuple(carry[g0 + k, j] for j in range(4)) for k in range(SSD_GPS))
        for k in range(SSD_GPS):
            st_ref[0, k] = carry[g0 + k]
        ys, S_out = _ssd_math(convs, zs, small_ref[...], S_in, dtb_ref[...], alog_ref[...], dskip_ref[...], ng_ref[...], g0)
        for k in range(SSD_GPS):
            y_ref[:, 512 * k:512 * (k + 1)] = ys[k].astype(BF16)
            for j in range(4):
                carry[g0 + k, j] = S_out[k][j]

    return pl.pallas_call(
        body, name="ssd_fwd", grid=(nc, SSD_GROUPS // SSD_GPS),
        in_specs=_ssd_specs(nc, False),
        out_specs=[pl.BlockSpec((CHUNK, SSD_GPS * 512), lambda c, g: (c, g)),
                   pl.BlockSpec((1, SSD_GPS, 4, CHUNK, CHUNK), lambda c, g: (c, g, 0, 0, 0)),
                   pl.BlockSpec((CHUNK, SSD_GPS * SSD_XBC_COLS), lambda c, g: (c, g))],
        out_shape=[jax.ShapeDtypeStruct((L, 2 * SSD_WIDTH), BF16),
                   jax.ShapeDtypeStruct((nc, SSD_GROUPS, 4, CHUNK, CHUNK), F32),
                   jax.ShapeDtypeStruct((L, SSD_GROUPS * SSD_XBC_COLS), F32)],
        scratch_shapes=[pltpu.VMEM((SSD_GROUPS, 4, CHUNK, CHUNK), F32), pltpu.VMEM((SSD_GPS, 8 + CHUNK, SSD_XBC_COLS), F32)],
        compiler_params=pltpu.CompilerParams(dimension_semantics=("arbitrary", "arbitrary"), vmem_limit_bytes=VMEM_LIMIT),
    )(proj, proj, small, cw, cb, dtb, alog, dskip, ng)


def _ssd_backward(proj, small, cw, cb, dtb, alog, dskip, ng, states, dy, conv, side=None):
    L = proj.shape[0]
    nc = L // CHUNK
    s_in, s_out = (len(side.operands), len(side.out_shapes)) if side is not None else (0, 0)
    grid = (nc, SSD_GROUPS // SSD_GPS)
    n_in = 12

    def body(*refs):
        (blk_ref, halo_ref, small_ref, cw_ref, cb_ref, dtb_ref, alog_ref, dskip_ref, ng_ref, st_ref, dy_ref,
         conv_ref) = refs[:n_in]
        o0 = n_in + s_in
        dproj_ref, dsmall_ref, dcw_ref, dcb_ref, ddtb_ref, dalog_ref, ddskip_ref, dng_ref = refs[o0:o0 + 8]
        dcarry, nxt, ext, dext = refs[o0 + 8 + s_out:o0 + 12 + s_out]
        side_refs = (refs[n_in:o0], refs[o0 + 8:o0 + 8 + s_out], refs[o0 + 12 + s_out:])
        s, gb = pl.program_id(0), pl.program_id(1)
        g0 = gb * SSD_GPS
        c = nc - 1 - s
        if side is not None:
            pl.when(_grid_edge(grid, first=True))(lambda: side.start(*side_refs))

        @pl.when(s == 0)
        def _():
            for k in range(SSD_GPS):
                dcarry[g0 + k] = jnp.zeros((4, CHUNK, CHUNK), F32)
                nxt[g0 + k] = jnp.zeros((8, SSD_XBC_COLS), F32)
                dcw_ref[g0 + k] = jnp.zeros((SSD_CONV, SSD_XBC_COLS), F32)
                dcb_ref[g0 + k] = jnp.zeros((1, SSD_XBC_COLS), F32)
                ddskip_ref[g0 + k] = jnp.zeros((1, 512), F32)
                dng_ref[g0 + k] = jnp.zeros((1, 512), F32)

        @pl.when((s == 0) & (gb == 0))
        def _():
            ddtb_ref[...] = jnp.zeros((1, SMALL_COLS), F32)
            dalog_ref[...] = jnp.zeros((1, SMALL_COLS), F32)

        for k in range(SSD_GPS):
            _ssd_window(blk_ref, halo_ref, ext, c == 0, k)
        convs = [conv_ref[:, SSD_XBC_COLS * k:SSD_XBC_COLS * (k + 1)] for k in range(SSD_GPS)]
        zs = [blk_ref[:, SSD_GROUP_COLS * k:SSD_GROUP_COLS * k + 512] for k in range(SSD_GPS)]
        S_in = tuple(tuple(st_ref[0, k, j] for j in range(4)) for k in range(SSD_GPS))
        dS_out = tuple(tuple(dcarry[g0 + k, j] for j in range(4)) for k in range(SSD_GPS))
        dys = tuple(dy_ref[:, 512 * k:512 * (k + 1)].astype(F32) for k in range(SSD_GPS))
        _, vjp = jax.vjp(functools.partial(_ssd_math, g0=g0), convs, zs, small_ref[...], S_in, dtb_ref[...], alog_ref[...],
                         dskip_ref[...], ng_ref[...])
        d_convs, d_zs, d_small, dS_in, d_dtb, d_alog, d_dskip, d_ng = vjp((dys, dS_out))
        for k in range(SSD_GPS):
            for j in range(4):
                dcarry[g0 + k, j] = dS_in[k][j]
            dext[k, 0:8, :] = jnp.zeros((8, SSD_XBC_COLS), F32)
            dext[k, 8:8 + CHUNK, :] = d_convs[k]
            dext[k, 8 + CHUNK:16 + CHUNK, :] = nxt[g0 + k]
            nxt[g0 + k] = dext[k, 8:16, :]
            dconv = d_convs[k]
            d_xbc = jnp.zeros((CHUNK, SSD_XBC_COLS), F32)
            for tap in range(SSD_CONV):
                d_xbc = d_xbc + cw_ref[k, tap:tap + 1, :] * dext[k, pl.ds(8 + 3 - tap, CHUNK), :]
                dcw_ref[g0 + k, tap:tap + 1, :] += jnp.sum(dconv * ext[k, pl.ds(5 + tap, CHUNK), :], axis=0, keepdims=True)
            dcb_ref[g0 + k] += jnp.sum(dconv, axis=0, keepdims=True)
            dproj_ref[:, SSD_GROUP_COLS * k:SSD_GROUP_COLS * k + 512] = d_zs[k].astype(BF16)
            dproj_ref[:, SSD_GROUP_COLS * k + 512:SSD_GROUP_COLS * (k + 1)] = d_xbc.astype(BF16)
            ddskip_ref[g0 + k] += d_dskip[:, 512 * k:512 * (k + 1)]
            dng_ref[g0 + k] += d_ng[:, 512 * k:512 * (k + 1)]

        @pl.when(gb == 0)
        def _():
            dsmall_ref[...] = d_small

        @pl.when(gb != 0)
        def _():
            dsmall_ref[...] += d_small

        ddtb_ref[...] += d_dtb
        dalog_ref[...] += d_alog
        if side is not None:
            pl.when(_grid_edge(grid, first=False))(lambda: side.finish(*side_refs))

    whole = lambda shape: pl.BlockSpec(shape, lambda s, g: (0,) * len(shape))
    side_ops = tuple(side.operands) if side is not None else ()
    res = pl.pallas_call(
        body, name="ssd_bwd", grid=grid,
        in_specs=_ssd_specs(nc, True) + [
            pl.BlockSpec((1, SSD_GPS, 4, CHUNK, CHUNK), lambda s, g: (nc - 1 - s, g, 0, 0, 0)),
            pl.BlockSpec((CHUNK, SSD_GPS * 512), lambda s, g: (nc - 1 - s, g)),
            pl.BlockSpec((CHUNK, SSD_GPS * SSD_XBC_COLS), lambda s, g: (nc - 1 - s, g))] + [ANY] * s_in,
        out_specs=[pl.BlockSpec((CHUNK, SSD_GPS * SSD_GROUP_COLS), lambda s, g: (nc - 1 - s, g)),
                   pl.BlockSpec((CHUNK, SMALL_COLS), lambda s, g: (nc - 1 - s, 0)),
                   whole((SSD_GROUPS, SSD_CONV, SSD_XBC_COLS)), whole((SSD_GROUPS, 1, SSD_XBC_COLS)),
                   whole((1, SMALL_COLS)), whole((1, SMALL_COLS)),
                   whole((SSD_GROUPS, 1, 512)), whole((SSD_GROUPS, 1, 512))] + [ANY] * s_out,
        out_shape=[jax.ShapeDtypeStruct((L, SSD_GROUPS * SSD_GROUP_COLS), BF16), jax.ShapeDtypeStruct((L, SMALL_COLS), F32),
                   jax.ShapeDtypeStruct((SSD_GROUPS, SSD_CONV, SSD_XBC_COLS), F32),
                   jax.ShapeDtypeStruct((SSD_GROUPS, 1, SSD_XBC_COLS), F32),
                   jax.ShapeDtypeStruct((1, SMALL_COLS), F32), jax.ShapeDtypeStruct((1, SMALL_COLS), F32),
                   jax.ShapeDtypeStruct((SSD_GROUPS, 1, 512), F32), jax.ShapeDtypeStruct((SSD_GROUPS, 1, 512), F32)]
        + (list(side.out_shapes) if side is not None else []),
        scratch_shapes=[pltpu.VMEM((SSD_GROUPS, 4, CHUNK, CHUNK), F32), pltpu.VMEM((SSD_GROUPS, 8, SSD_XBC_COLS), F32),
                        pltpu.VMEM((SSD_GPS, 8 + CHUNK, SSD_XBC_COLS), F32),
                        pltpu.VMEM((SSD_GPS, 16 + CHUNK, SSD_XBC_COLS), F32)]
        + (list(side.sems) if side is not None else []),
        compiler_params=pltpu.CompilerParams(dimension_semantics=("arbitrary", "arbitrary"), vmem_limit_bytes=VMEM_LIMIT),
    )(proj, proj, small, cw, cb, dtb, alog, dskip, ng, states, dy, conv, *side_ops)
    return res if side is None else (res[:8], res[8:])


ML_HPS = 8


def _mlstm_math(blks, smallb, C_in, n_in, m_in, ib_row, fb_row, ng, h0):
    li_all = GATE_SOFTCAP * jnp.tanh((smallb + ib_row) / GATE_SOFTCAP)
    lf_all = -_softplus(-(GATE_SOFTCAP * jnp.tanh((smallb + fb_row) / GATE_SOFTCAP)))
    prep = (li_all, lf_all, _cumsum(lf_all))
    out = [_mlstm_head(blks[k][:, 0:128], blks[k][:, 128:256], blks[k][:, 256:512], blks[k][:, 512:768], prep,
                       C_in[k], n_in[k], m_in[k], ng[:, ML_DV * k:ML_DV * (k + 1)], h0 + k) for k in range(ML_HPS)]
    return tuple(zip(*out))


def _mlstm_head(q, k, v, o_raw, prep, C_in, n_in, m_in_row, ng, h):
    li_all, lf_all, bcum_all = prep
    lane = lax.broadcasted_iota(jnp.int32, (1, CHUNK), 1)
    row_i = lax.broadcasted_iota(jnp.int32, (CHUNK, CHUNK), 0)
    col_i = lax.broadcasted_iota(jnp.int32, (CHUNK, CHUNK), 1)
    causal = row_i >= col_i
    kk = k * (ML_DK ** -0.5)
    li = _lane_col(li_all, lane, LANE_I + h)
    lf = _lane_col(lf_all, lane, LANE_F + h)
    bc = _lane_col(bcum_all, lane, LANE_F + h)
    b_last = jnp.sum(lf, axis=0, keepdims=True)
    m_in = _lane_col(m_in_row, lane, 0)
    a = b_last - bc + li
    m_loc = jnp.max(a, axis=0, keepdims=True)
    w = jnp.exp(a - m_loc)
    C_loc = _mm_tn(w * v, kk)
    n_loc = jnp.sum(w * kk, axis=0, keepdims=True)
    m_new = jnp.maximum(b_last + m_in, m_loc)
    s_old = jnp.exp(b_last + m_in - m_new)
    s_new = jnp.exp(m_loc - m_new)
    C_out = s_old * C_in + s_new * C_loc
    n_out = s_old * n_in + s_new * n_loc
    bc_b = jnp.broadcast_to(bc, (CHUNK, CHUNK))
    li_b = jnp.broadcast_to(li, (CHUNK, CHUNK))
    D = jnp.where(causal, bc_b - bc_b.T + li_b.T, -jnp.inf)
    m_intra = jnp.max(D, axis=1, keepdims=True)
    inter_log = bc + m_in
    m_t = jnp.maximum(inter_log, m_intra)
    S = _mm_nt(q, kk) * jnp.exp(D - m_t)
    w_inter = jnp.exp(inter_log - m_t)
    num = _mm(S, v) + w_inter * _mm_nt(q, C_in)
    nq = jnp.sum(S, axis=1, keepdims=True) + w_inter * jnp.sum(q * n_in, axis=1, keepdims=True)
    den = jnp.maximum(jnp.abs(nq), jnp.exp(-m_t))
    hh = num / den
    hh = hh * lax.rsqrt(jnp.mean(hh * hh, axis=1, keepdims=True) + EPS)
    hh = hh * ng * _sigmoid(o_raw)
    return hh, C_out, n_out, jnp.broadcast_to(m_new, (1, CHUNK))


def _ml_specs(nc, rev):
    cc = (lambda c: nc - 1 - c) if rev else (lambda c: c)
    return [
        pl.BlockSpec((CHUNK, ML_HPS * ML_HEAD_COLS), lambda c, h: (cc(c), h)),
        pl.BlockSpec((CHUNK, SMALL_COLS), lambda c, h: (cc(c), 0)),
        pl.BlockSpec((1, SMALL_COLS), lambda c, h: (0, 0)),
        pl.BlockSpec((1, SMALL_COLS), lambda c, h: (0, 0)),
        pl.BlockSpec((1, ML_HPS * ML_DV), lambda c, h: (0, h)),
    ]


def _mlstm_forward(proj, small, ib, fb, ng, y_mix):
    L = proj.shape[0]
    nc = L // CHUNK

    def body(blk_ref, small_ref, ib_ref, fb_ref, ng_ref, _, y_ref, cst_ref, nm_ref, c_carry, nm_carry):
        c, h0 = pl.program_id(0), pl.program_id(1) * ML_HPS

        @pl.when(c == 0)
        def _():
            for k in range(ML_HPS):
                c_carry[h0 + k] = jnp.zeros((ML_DV, ML_DK), F32)
                nm_carry[h0 + k] = jnp.zeros((2, ML_DK), F32)

        for k in range(ML_HPS):
            cst_ref[0, k] = c_carry[h0 + k]
            nm_ref[0, k] = nm_carry[h0 + k]
        hh, C_out, n_out, m_out = _mlstm_math(
            [blk_ref[:, ML_HEAD_COLS * k:ML_HEAD_COLS * (k + 1)] for k in range(ML_HPS)], small_ref[...],
            [c_carry[h0 + k] for k in range(ML_HPS)], [nm_carry[h0 + k, 0:1, :] for k in range(ML_HPS)],
            [nm_carry[h0 + k, 1:2, :] for k in range(ML_HPS)], ib_ref[...], fb_ref[...], ng_ref[...], h0)
        for k in range(ML_HPS):
            y_ref[:, ML_DV * k:ML_DV * (k + 1)] = hh[k].astype(BF16)
            c_carry[h0 + k] = C_out[k]
            nm_carry[h0 + k, 0:1, :] = n_out[k]
            nm_carry[h0 + k, 1:2, :] = m_out[k]

    return pl.pallas_call(
        body, name="mlstm_fwd", grid=(nc, ML_HEADS // ML_HPS),
        in_specs=_ml_specs(nc, False) + [pl.BlockSpec(memory_space=pl.ANY)],
        out_specs=[pl.BlockSpec((CHUNK, ML_HPS * ML_DV), lambda c, h: (c, SSD_WIDTH // (ML_HPS * ML_DV) + h)),
                   pl.BlockSpec((1, ML_HPS, ML_DV, ML_DK), lambda c, h: (c, h, 0, 0)),
                   pl.BlockSpec((1, ML_HPS, 2, ML_DK), lambda c, h: (c, h, 0, 0))],
        out_shape=[jax.ShapeDtypeStruct(y_mix.shape, BF16),
                   jax.ShapeDtypeStruct((nc, ML_HEADS, ML_DV, ML_DK), F32),
                   jax.ShapeDtypeStruct((nc, ML_HEADS, 2, ML_DK), F32)],
        input_output_aliases={5: 0},
        scratch_shapes=[pltpu.VMEM((ML_HEADS, ML_DV, ML_DK), F32), pltpu.VMEM((ML_HEADS, 2, ML_DK), F32)],
        compiler_params=pltpu.CompilerParams(dimension_semantics=("arbitrary", "arbitrary"), vmem_limit_bytes=VMEM_LIMIT),
    )(proj, small, ib, fb, ng, y_mix)


def _mlstm_backward(proj, small, ib, fb, ng, cst, nmst, dy, dsmall_in):
    L = proj.shape[0]
    nc = L // CHUNK

    def body(blk_ref, small_ref, ib_ref, fb_ref, ng_ref, cst_ref, nm_ref, dy_ref, dsin_ref,
             dproj_ref, dsmall_ref, dib_ref, dfb_ref, dng_ref, dc_carry, dnm_carry):
        s, hb = pl.program_id(0), pl.program_id(1)
        h0 = hb * ML_HPS

        @pl.when(s == 0)
        def _():
            for k in range(ML_HPS):
                dc_carry[h0 + k] = jnp.zeros((ML_DV, ML_DK), F32)
                dnm_carry[h0 + k] = jnp.zeros((2, ML_DK), F32)
                dng_ref[h0 + k] = jnp.zeros((1, ML_DV), F32)

        @pl.when((s == 0) & (hb == 0))
        def _():
            dib_ref[...] = jnp.zeros((1, SMALL_COLS), F32)
            dfb_ref[...] = jnp.zeros((1, SMALL_COLS), F32)

        heads = range(ML_HPS)
        _, vjp = jax.vjp(functools.partial(_mlstm_math, h0=h0),
                         [blk_ref[:, ML_HEAD_COLS * k:ML_HEAD_COLS * (k + 1)] for k in heads], small_ref[...],
                         [cst_ref[0, k] for k in heads], [nm_ref[0, k, 0:1, :] for k in heads],
                         [nm_ref[0, k, 1:2, :] for k in heads], ib_ref[...], fb_ref[...], ng_ref[...])
        d_blk, d_small, dC, dn, dm, d_ib, d_fb, d_ng = vjp(
            (tuple(dy_ref[:, ML_DV * k:ML_DV * (k + 1)].astype(F32) for k in heads),
             tuple(dc_carry[h0 + k] for k in heads), tuple(dnm_carry[h0 + k, 0:1, :] for k in heads),
             tuple(dnm_carry[h0 + k, 1:2, :] for k in heads)))
        for k in heads:
            dc_carry[h0 + k] = dC[k]
            dnm_carry[h0 + k, 0:1, :] = dn[k]
            dnm_carry[h0 + k, 1:2, :] = dm[k]
            dproj_ref[:, ML_HEAD_COLS * k:ML_HEAD_COLS * (k + 1)] = d_blk[k].astype(BF16)
            dng_ref[h0 + k] += d_ng[:, ML_DV * k:ML_DV * (k + 1)]

        @pl.when(hb == 0)
        def _():
            dsmall_ref[...] = dsin_ref[...] + d_small

        @pl.when(hb != 0)
        def _():
            dsmall_ref[...] += d_small

        dib_ref[...] += d_ib
        dfb_ref[...] += d_fb

    whole = lambda shape: pl.BlockSpec(shape, lambda s, h: (0,) * len(shape))
    return pl.pallas_call(
        body, name="mlstm_bwd", grid=(nc, ML_HEADS // ML_HPS),
        in_specs=_ml_specs(nc, True) + [
            pl.BlockSpec((1, ML_HPS, ML_DV, ML_DK), lambda s, h: (nc - 1 - s, h, 0, 0)),
            pl.BlockSpec((1, ML_HPS, 2, ML_DK), lambda s, h: (nc - 1 - s, h, 0, 0)),
            pl.BlockSpec((CHUNK, ML_HPS * ML_DV), lambda s, h: (nc - 1 - s, SSD_WIDTH // (ML_HPS * ML_DV) + h)),
            pl.BlockSpec((CHUNK, SMALL_COLS), lambda s, h: (nc - 1 - s, 0))],
        out_specs=[pl.BlockSpec((CHUNK, ML_HPS * ML_HEAD_COLS), lambda s, h: (nc - 1 - s, h)),
                   pl.BlockSpec((CHUNK, SMALL_COLS), lambda s, h: (nc - 1 - s, 0)),
                   whole((1, SMALL_COLS)), whole((1, SMALL_COLS)), whole((ML_HEADS, 1, ML_DV))],
        out_shape=[jax.ShapeDtypeStruct((L, ML_HEADS * ML_HEAD_COLS), BF16), jax.ShapeDtypeStruct((L, SMALL_COLS), F32),
                   jax.ShapeDtypeStruct((1, SMALL_COLS), F32), jax.ShapeDtypeStruct((1, SMALL_COLS), F32),
                   jax.ShapeDtypeStruct((ML_HEADS, 1, ML_DV), F32)],
        scratch_shapes=[pltpu.VMEM((ML_HEADS, ML_DV, ML_DK), F32), pltpu.VMEM((ML_HEADS, 2, ML_DK), F32)],
        compiler_params=pltpu.CompilerParams(dimension_semantics=("arbitrary", "arbitrary"), vmem_limit_bytes=VMEM_LIMIT),
    )(proj, small, ib, fb, ng, cst, nmst, dy, dsmall_in)


_OFF_Z, _OFF_X, _OFF_B, _OFF_C, _OFF_DT = 0, 2048, 4096, 4608, 5120
_OFF_Q, _OFF_K, _OFF_V, _OFF_O, _OFF_I, _OFF_F, IN_WIDTH = 5152, 6176, 7200, 9248, 11296, 11304, 11312


def _ssd_segments():
    segs = []
    for g in range(SSD_GROUPS):
        segs += [(_OFF_Z + 512 * g, 512), (_OFF_X + 512 * g, 512), (_OFF_B + 128 * g, 128), (_OFF_C + 128 * g, 128)]
    return segs


def _ml_segments():
    segs = []
    for h in range(ML_HEADS):
        segs += [(_OFF_Q + 128 * h, 128), (_OFF_K + 128 * h, 128), (_OFF_V + 256 * h, 256), (_OFF_O + 256 * h, 256)]
    return segs


_SMALL_SEGMENTS = [(_OFF_DT, 32), (_OFF_I, 8), (_OFF_F, 8)]


SHARD_IN = IN_WIDTH // 4


def _take_cols(slabs, segs):
    parts = []
    for s, n in segs:
        while n > 0:
            k, lo = divmod(s, SHARD_IN)
            m = min(n, SHARD_IN - lo)
            parts.append(slabs[k][:, lo:lo + m])
            s, n = s + m, n - m
    return jnp.concatenate(parts, axis=1)


def _split_w_in(slabs):
    small = _take_cols(slabs, _SMALL_SEGMENTS)
    small = jnp.concatenate([small, jnp.zeros((small.shape[0], SMALL_COLS - small.shape[1]), small.dtype)], axis=1)
    return _take_cols(slabs, _ssd_segments()), _take_cols(slabs, _ml_segments()), small


def _merge_w_in(d_ssd, d_ml, d_small):
    pieces = []
    for arr, segs in ((d_ssd, _ssd_segments()), (d_ml, _ml_segments()), (d_small, _SMALL_SEGMENTS)):
        pos = 0
        for s, n in segs:
            while n > 0:
                lo = s % SHARD_IN
                m = min(n, SHARD_IN - lo)
                pieces.append((s, arr[:, pos:pos + m]))
                s, n, pos = s + m, n - m, pos + m
    pieces.sort(key=lambda t: t[0])
    return [jnp.concatenate([p for s, p in pieces if s // SHARD_IN == k], axis=1) for k in range(4)]


def _conv_to_groups(cw):
    return jnp.stack([jnp.concatenate([cw[:, 512 * g:512 * (g + 1)], cw[:, 2048 + 128 * g:2048 + 128 * (g + 1)],
                                       cw[:, 2560 + 128 * g:2560 + 128 * (g + 1)]], axis=1) for g in range(SSD_GROUPS)])


def _conv_from_groups(d):
    return jnp.concatenate([d[g, :, 0:512] for g in range(SSD_GROUPS)] + [d[g, :, 512:640] for g in range(SSD_GROUPS)]
                           + [d[g, :, 640:768] for g in range(SSD_GROUPS)], axis=1)


def _small_row(vec, lane0):
    n = vec.shape[1]
    return jnp.concatenate([jnp.zeros((1, lane0), F32), vec, jnp.zeros((1, SMALL_COLS - lane0 - n), F32)], axis=1)


class _Side:
    def __init__(self, operands, out_shapes, sems, start, finish):
        self.operands, self.out_shapes, self.sems, self.start, self.finish = operands, out_shapes, sems, start, finish


def _grid_edge(grid, first):
    hit = None
    for axis, n in enumerate(grid):
        here = pl.program_id(axis) == (0 if first else n - 1)
        hit = here if hit is None else hit & here
    return hit


def _matmul(a, b, mode, out_dtype, tm, tn, tk, name, addend=None, layout=None, side=None):
    M, Kd = (a.shape[1], a.shape[0]) if mode == "tn" else a.shape
    N = b.shape[0] if mode == "nt" else b.shape[1]
    if layout == "cols4":
        tm, tn = M // 2, N // 4
    elif layout == "rows4":
        tm = M // 4
    tm, tn, tk = min(tm, M), min(tn, N), min(tk, Kd)
    if mode == "nn":
        a_spec = pl.BlockSpec((tm, tk), lambda i, j, k: (i, k))
        b_spec = pl.BlockSpec((tk, tn), lambda i, j, k: (k, j))
        dims = (((1,), (0,)), ((), ()))
    elif mode == "nt":
        a_spec = pl.BlockSpec((tm, tk), lambda i, j, k: (i, k))
        b_spec = pl.BlockSpec((tn, tk), lambda i, j, k: (j, k))
        dims = (((1,), (1,)), ((), ()))
    else:
        a_spec = pl.BlockSpec((tk, tm), lambda i, j, k: (k, i))
        b_spec = pl.BlockSpec((tk, tn), lambda i, j, k: (k, j))
        dims = (((0,), (0,)), ((), ()))
    assert M % tm == 0 and N % tn == 0 and Kd % tk == 0, (name, M, N, Kd, tm, tn, tk)
    nk = Kd // tk
    has_add = addend is not None
    n_in = 2 + has_add
    s_in, s_out = (len(side.operands), len(side.out_shapes)) if side is not None else (0, 0)
    grid = (M // tm, N // tn, nk)

    def body(*refs):
        a_ref, b_ref = refs[0], refs[1]
        add_ref = refs[2] if has_add else None
        o_ref, acc_ref = refs[n_in + s_in], refs[n_in + s_in + 1 + s_out]
        k = pl.program_id(2)
        side_refs = (refs[n_in:n_in + s_in], refs[n_in + s_in + 1:n_in + s_in + 1 + s_out], refs[n_in + s_in + 2 + s_out:])
        if side is not None:
            pl.when(_grid_edge(grid, first=True))(lambda: side.start(*side_refs))
        part = lax.dot_general(a_ref[...].astype(BF16), b_ref[...].astype(BF16), dims, preferred_element_type=F32)

        @pl.when(k == 0)
        def _():
            acc_ref[...] = part

        @pl.when(k != 0)
        def _():
            acc_ref[...] += part

        @pl.when(k == nk - 1)
        def _():
            r = acc_ref[...]
            if has_add:
                r = r + add_ref[...].astype(F32)
            if layout == "cols4":
                o_ref[0, 0] = r.astype(out_dtype)
            elif layout == "rows4":
                o_ref[0, 0] = r[:tm // 2].astype(out_dtype)
                o_ref[1, 0] = r[tm // 2:].astype(out_dtype)
            else:
                o_ref[...] = r.astype(out_dtype)

        if side is not None:
            pl.when(_grid_edge(grid, first=False))(lambda: side.finish(*side_refs))

    o_spec = pl.BlockSpec((tm, tn), lambda i, j, k: (i, j))
    out_spec, out_dims = o_spec, (M, N)
    if layout == "cols4":
        out_spec, out_dims = pl.BlockSpec((1, 1, tm, tn), lambda i, j, k: (i, j, 0, 0)), (2, 4, tm, tn)
    elif layout == "rows4":
        out_spec, out_dims = pl.BlockSpec((2, 1, tm // 2, tn), lambda i, j, k: (0, i, 0, j)), (2, 4, tm // 2, N)
    out_shape = jax.ShapeDtypeStruct(out_dims, out_dtype)
    semantics = ("parallel", "parallel", "arbitrary")
    if side is None:
        in_specs, operands, out_specs, scratch = [], (), out_spec, []
    else:
        in_specs, operands, scratch = [ANY] * s_in, tuple(side.operands), list(side.sems)
        out_specs, out_shape = [out_spec] + [ANY] * s_out, [out_shape] + list(side.out_shapes)
        semantics = ("arbitrary",) * 3
    res = pl.pallas_call(
        body, name=name, grid=grid,
        in_specs=[a_spec, b_spec] + ([o_spec] if has_add else []) + in_specs, out_specs=out_specs,
        out_shape=out_shape, scratch_shapes=[pltpu.VMEM((tm, tn), F32)] + scratch,
        compiler_params=pltpu.CompilerParams(dimension_semantics=semantics, vmem_limit_bytes=VMEM_LIMIT),
    )(*((a, b) + ((addend,) if has_add else ()) + operands))
    return res if side is None else (res[0], res[1:])


ROW_TILE = 256
NORM_TILE = 512


def _rmsnorm_fwd(x, g, name, side=None):
    L, D = x.shape
    tr = min(NORM_TILE, L)
    s_in, s_out = (len(side.operands), len(side.out_shapes)) if side is not None else (0, 0)
    grid = (L // tr,)

    def body(*refs):
        x_ref, g_ref, u_ref = refs[0], refs[1], refs[2 + s_in]
        side_refs = (refs[2:2 + s_in], refs[3 + s_in:3 + s_in + s_out], refs[3 + s_in + s_out:])
        if side is not None:
            pl.when(_grid_edge(grid, first=True))(lambda: side.start(*side_refs))
        xv = x_ref[...]
        r = lax.rsqrt(jnp.mean(xv * xv, axis=1, keepdims=True) + EPS)
        u_ref[...] = (xv * r * g_ref[...]).astype(BF16)
        if side is not None:
            pl.when(_grid_edge(grid, first=False))(lambda: side.finish(*side_refs))

    row = pl.BlockSpec((tr, D), lambda i: (i, 0))
    res = pl.pallas_call(
        body, name=name, grid=grid,
        in_specs=[row, pl.BlockSpec((1, D), lambda i: (0, 0))] + [ANY] * s_in, out_specs=[row] + [ANY] * s_out,
        out_shape=[jax.ShapeDtypeStruct((L, D), BF16)] + (list(side.out_shapes) if side is not None else []),
        scratch_shapes=list(side.sems) if side is not None else [],
        compiler_params=pltpu.CompilerParams(dimension_semantics=("arbitrary",), vmem_limit_bytes=VMEM_LIMIT),
    )(x, g, *(side.operands if side is not None else ()))
    return res[0] if side is None else (res[0], res[1:])


def _rmsnorm_bwd(du, x, g, dres, name):
    L, D = x.shape

    def body(du_ref, x_ref, g_ref, dres_ref, dx_ref, dxb_ref, dg_ref):
        i = pl.program_id(0)
        xv, duv = x_ref[...], du_ref[...]
        r = lax.rsqrt(jnp.mean(xv * xv, axis=1, keepdims=True) + EPS)
        t = duv * g_ref[...]
        dx = dres_ref[...] + r * t - xv * (r * r * r) * jnp.mean(t * xv, axis=1, keepdims=True)
        dx_ref[...] = dx
        dxb_ref[...] = dx.astype(BF16)
        dg = jnp.sum(duv * xv * r, axis=0, keepdims=True)

        @pl.when(i == 0)
        def _():
            dg_ref[...] = dg

        @pl.when(i != 0)
        def _():
            dg_ref[...] += dg

    row = pl.BlockSpec((ROW_TILE, D), lambda i: (i, 0))
    vec = pl.BlockSpec((1, D), lambda i: (0, 0))
    return pl.pallas_call(
        body, name=name, grid=(L // ROW_TILE,),
        in_specs=[row, row, vec, row], out_specs=[row, row, vec],
        out_shape=[jax.ShapeDtypeStruct((L, D), F32), jax.ShapeDtypeStruct((L, D), BF16), jax.ShapeDtypeStruct((1, D), F32)],
        compiler_params=pltpu.CompilerParams(dimension_semantics=("arbitrary",), vmem_limit_bytes=VMEM_LIMIT),
    )(du, x, g, dres)


def _loss_head(h, target, g):
    L, D = h.shape

    def body(h_ref, t_ref, g_ref, dh_ref, dhb_ref, loss_ref, dg_ref):
        i = pl.program_id(0)
        hv = h_ref[...]
        r = lax.rsqrt(jnp.mean(hv * hv, axis=1, keepdims=True) + EPS)
        diff = hv * r * g_ref[...] - t_ref[...]
        part = 0.5 * jnp.sum(jnp.mean(diff * diff, axis=1, keepdims=True), axis=0, keepdims=True)
        dy = diff * (1.0 / D)
        t = dy * g_ref[...]
        dh = r * t - hv * (r * r * r) * jnp.mean(t * hv, axis=1, keepdims=True)
        dh_ref[...] = dh
        dhb_ref[...] = dh.astype(BF16)
        dg = jnp.sum(dy * hv * r, axis=0, keepdims=True)

        @pl.when(i == 0)
        def _():
            dg_ref[...] = dg
            loss_ref[...] = jnp.broadcast_to(part, (1, 128))

        @pl.when(i != 0)
        def _():
            dg_ref[...] += dg
            loss_ref[...] += jnp.broadcast_to(part, (1, 128))

    row = pl.BlockSpec((ROW_TILE, D), lambda i: (i, 0))
    vec = pl.BlockSpec((1, D), lambda i: (0, 0))
    return pl.pallas_call(
        body, name="loss_head", grid=(L // ROW_TILE,),
        in_specs=[row, row, vec], out_specs=[row, row, pl.BlockSpec((1, 128), lambda i: (0, 0)), vec],
        out_shape=[jax.ShapeDtypeStruct((L, D), F32), jax.ShapeDtypeStruct((L, D), BF16), jax.ShapeDtypeStruct((1, 128), F32),
                   jax.ShapeDtypeStruct((1, D), F32)],
        compiler_params=pltpu.CompilerParams(dimension_semantics=("arbitrary",), vmem_limit_bytes=VMEM_LIMIT),
    )(h, target, g)


FF_TILE = 1408


FF_COLS = 512


def _ffn_in(u, wg, wu, tm, side):
    L, D = u.shape
    F = wg.shape[1]
    tm = min(tm, L)
    s_in, s_out = len(side.operands), len(side.out_shapes)
    grid = (L // tm, F // FF_COLS)

    def body(*refs):
        u_ref, wg_ref, wu_ref = refs[:3]
        g_ref, up_ref, act_ref = refs[3 + s_in:6 + s_in]
        side_refs = (refs[3:3 + s_in], refs[6 + s_in:6 + s_in + s_out], refs[6 + s_in + s_out:])
        pl.when(_grid_edge(grid, first=True))(lambda: side.start(*side_refs))
        g = jnp.dot(u_ref[...], wg_ref[...], preferred_element_type=F32)
        v = jnp.dot(u_ref[...], wu_ref[...], preferred_element_type=F32)
        s = _sigmoid(g)
        g_ref[...] = (v * (s * (1.0 + g * (1.0 - s)))).astype(BF16)
        up_ref[...] = (g * s).astype(BF16)
        act_ref[...] = (g * s * v).astype(BF16)
        pl.when(_grid_edge(grid, first=False))(lambda: side.finish(*side_refs))

    w_spec = pl.BlockSpec((D, FF_COLS), lambda i, j: (0, j))
    o_spec = pl.BlockSpec((tm, FF_COLS), lambda i, j: (i, j))
    out = jax.ShapeDtypeStruct((L, F), BF16)
    res = pl.pallas_call(
        body, name="ffn_gate_up", grid=grid,
        in_specs=[pl.BlockSpec((tm, D), lambda i, j: (i, 0)), w_spec, w_spec] + [ANY] * s_in,
        out_specs=[o_spec] * 3 + [ANY] * s_out, out_shape=[out] * 3 + list(side.out_shapes),
        scratch_shapes=list(side.sems),
        compiler_params=pltpu.CompilerParams(dimension_semantics=("arbitrary", "arbitrary"), vmem_limit_bytes=VMEM_LIMIT),
    )(u, wg, wu, *side.operands)
    return res[0], res[1], res[2], res[3:]


def _ffn_back(dh, w_down, gate_factor, up_factor, tm):
    L, D = dh.shape
    F = w_down.shape[0]
    tm = min(tm, L)

    def body(dh_ref, w_ref, g_ref, u_ref, dg_ref, du_ref):
        d = lax.dot_general(dh_ref[...], w_ref[...], (((1,), (1,)), ((), ())), preferred_element_type=F32)
        dg_ref[...] = (d * g_ref[...].astype(F32)).astype(BF16)
        du_ref[...] = (d * u_ref[...].astype(F32)).astype(BF16)

    blk = pl.BlockSpec((tm, FF_COLS), lambda i, j: (i, j))
    out = jax.ShapeDtypeStruct((L, F), BF16)
    return pl.pallas_call(
        body, name="ffn_back", grid=(L // tm, F // FF_COLS),
        in_specs=[pl.BlockSpec((tm, D), lambda i, j: (i, 0)), pl.BlockSpec((FF_COLS, D), lambda i, j: (j, 0)), blk, blk],
        out_specs=[blk, blk], out_shape=[out, out],
        compiler_params=pltpu.CompilerParams(dimension_semantics=("parallel", "parallel"), vmem_limit_bytes=VMEM_LIMIT),
    )(dh, w_down, gate_factor, up_factor)


def _adamw(w, g, m, v, rows, name):
    R, C = w.shape
    assert R % rows == 0, (name, R, rows)

    def body(w_ref, g_ref, m_ref, v_ref, d_ref, nm_ref, nv_ref):
        gv = g_ref[...]
        mn = ADAM_B1 * m_ref[...] + (1.0 - ADAM_B1) * gv
        vn = ADAM_B2 * v_ref[...] + (1.0 - ADAM_B2) * (gv * gv)
        m_hat = mn / (1.0 - ADAM_B1 ** ADAM_STEP)
        v_hat = vn / (1.0 - ADAM_B2 ** ADAM_STEP)
        d_ref[...] = -ADAM_LR * (m_hat / (jnp.sqrt(v_hat) + ADAM_EPS) + ADAM_WD * w_ref[...])
        nm_ref[...] = mn
        nv_ref[...] = vn

    blk = pl.BlockSpec((rows, C), lambda i: (i, 0))
    out = jax.ShapeDtypeStruct((R, C), F32)
    return pl.pallas_call(
        body, name=name, grid=(R // rows,), in_specs=[blk] * 4, out_specs=[blk] * 3, out_shape=[out] * 3,
        compiler_params=pltpu.CompilerParams(dimension_semantics=("parallel",), vmem_limit_bytes=VMEM_LIMIT),
    )(w, g, m, v)


ANY = pl.BlockSpec(memory_space=pl.ANY)
N_DEV = 8


def _allgather_small(p, name):
    R, C = p.shape

    def body(p_ref, out_ref, send_sems, recv_sems, local_sem):
        x, y, c = lax.axis_index("x"), lax.axis_index("y"), lax.axis_index("c")
        me = 4 * x + 2 * y + c
        mine = pltpu.make_async_copy(p_ref, out_ref.at[me], local_sem)
        mine.start()

        def peer(d):
            return (x ^ ((d >> 2) & 1), y ^ ((d >> 1) & 1), c ^ (d & 1))

        def copy(d, block):
            return pltpu.make_async_remote_copy(src_ref=p_ref, dst_ref=out_ref.at[block], send_sem=send_sems.at[d - 1],
                                                recv_sem=recv_sems.at[d - 1], device_id=peer(d), device_id_type=MESH)

        sends = [copy(d, me) for d in range(1, N_DEV)]
        for cp in sends:
            cp.start()
        for d in range(1, N_DEV):
            px, py, pc = peer(d)
            copy(d, 4 * px + 2 * py + pc).wait_recv()
        for cp in sends:
            cp.wait_send()
        mine.wait()

    return pl.pallas_call(
        body, name=name, out_shape=jax.ShapeDtypeStruct((N_DEV, R, C), p.dtype),
        in_specs=[pl.BlockSpec(memory_space=pltpu.VMEM)], out_specs=pl.BlockSpec(memory_space=pltpu.VMEM),
        scratch_shapes=[pltpu.SemaphoreType.DMA((N_DEV - 1,)), pltpu.SemaphoreType.DMA((N_DEV - 1,)), pltpu.SemaphoreType.DMA],
    )(p)


def _other_chips(x, y):
    return [(1 - x, y), (x, 1 - y), (1 - x, 1 - y)]


def _run_side(side, name):
    s_in, s_out = len(side.operands), len(side.out_shapes)

    def body(*refs):
        parts = (refs[:s_in], refs[s_in:s_in + s_out], refs[s_in + s_out:])
        side.start(*parts)
        side.finish(*parts)

    return pl.pallas_call(body, name=name, out_shape=list(side.out_shapes), in_specs=[ANY] * s_in, out_specs=[ANY] * s_out,
                          scratch_shapes=list(side.sems))(*side.operands)


def _gather_side(shards):
    T = len(shards)
    halves = [s.shape[0] // 2 for s in shards]

    def tools(ins, outs, sems):
        send_sems, recv_sems = sems
        x, y, c = lax.axis_index("x"), lax.axis_index("y"), lax.axis_index("c")

        def rows(t, px, py, pc):
            return outs[t].at[2 * px + py, pl.ds(pc * halves[t], halves[t]), :]

        def copy(t, k, block, to, own=False):
            src = ins[t].at[pl.ds(c * halves[t], halves[t]), :] if own else rows(t, *block)
            return pltpu.make_async_remote_copy(src_ref=src, dst_ref=rows(t, *block), send_sem=send_sems.at[t, k],
                                                recv_sem=recv_sems.at[t, k], device_id=to, device_id_type=MESH)

        return (x, y, c), _other_chips(x, y), copy

    def start(ins, outs, sems):
        (x, y, c), chips, copy = tools(ins, outs, sems)
        for t in range(T):
            for j, chip in enumerate(chips):
                copy(t, j, (x, y, c), (*chip, c), own=True).start()

    def finish(ins, outs, sems):
        (x, y, c), chips, copy = tools(ins, outs, sems)
        for t in range(T):
            for j, chip in enumerate(chips):
                copy(t, j, (*chip, c), (x, y, c)).wait_recv()
                copy(t, 3 + j, (*chip, c), (x, y, 1 - c)).start()
        for t in range(T):
            for j, chip in enumerate(chips):
                copy(t, 3 + j, (*chip, 1 - c), (x, y, c)).wait_recv()
        for t in range(T):
            for j, chip in enumerate(chips):
                copy(t, j, (x, y, c), (*chip, c), own=True).wait_send()
                copy(t, 3 + j, (*chip, c), (x, y, 1 - c)).wait_send()

    return _Side(shards, [jax.ShapeDtypeStruct((4,) + s.shape, s.dtype) for s in shards],
                 [pltpu.SemaphoreType.DMA((T, 6)), pltpu.SemaphoreType.DMA((T, 6))], start, finish)


def _with_own_shard(gathered, shard, slot):
    return lax.dynamic_update_slice(gathered, shard[None], (slot,) + (0,) * shard.ndim)


def _pair_side(grads, chip_major=False):
    T = len(grads)

    def copies(ins, outs, sems):
        send_sems, recv_sems = sems
        x, y, c = lax.axis_index("x"), lax.axis_index("y"), lax.axis_index("c")
        theirs = (lambda r: r.at[:, 1 - c]) if chip_major else (lambda r: r.at[1 - c])
        return [pltpu.make_async_remote_copy(src_ref=theirs(ins[t]), dst_ref=outs[t], send_sem=send_sems.at[t],
                                             recv_sem=recv_sems.at[t], device_id=(x, y, 1 - c), device_id_type=MESH)
                for t in range(T)]

    def start(ins, outs, sems):
        for cp in copies(ins, outs, sems):
            cp.start()

    def finish(ins, outs, sems):
        for cp in copies(ins, outs, sems):
            cp.wait()

    return _Side(grads, [jax.ShapeDtypeStruct((4,) + g.shape[2:], g.dtype) for g in grads],
                 [pltpu.SemaphoreType.DMA((T,)), pltpu.SemaphoreType.DMA((T,))], start, finish)


def _scatter_side(parts):
    T = len(parts)

    def tools(ins, outs, sems):
        send_sems, recv_sems = sems
        x, y, c = lax.axis_index("x"), lax.axis_index("y"), lax.axis_index("c")

        def copy(t, j, src_slot, dst_slot, chip):
            return pltpu.make_async_remote_copy(src_ref=ins[t].at[src_slot], dst_ref=outs[t].at[dst_slot],
                                                send_sem=send_sems.at[t, j], recv_sem=recv_sems.at[t, j],
                                                device_id=(*chip, c), device_id_type=MESH)

        return 2 * x + y, _other_chips(x, y), copy

    def start(ins, outs, sems):
        my_chip, chips, copy = tools(ins, outs, sems)
        for t in range(T):
            for j, (px, py) in enumerate(chips):
                copy(t, j, 2 * px + py, my_chip, (px, py)).start()

    def finish(ins, outs, sems):
        my_chip, chips, copy = tools(ins, outs, sems)
        for t in range(T):
            for j, (px, py) in enumerate(chips):
                copy(t, j, my_chip, 2 * px + py, (px, py)).wait_recv()
        for t in range(T):
            for j, (px, py) in enumerate(chips):
                copy(t, j, 2 * px + py, my_chip, (px, py)).wait_send()

    return _Side(parts, [jax.ShapeDtypeStruct(p.shape, p.dtype) for p in parts],
                 [pltpu.SemaphoreType.DMA((T, 3)), pltpu.SemaphoreType.DMA((T, 3))], start, finish)


def _pair_share(fulls):
    T = len(fulls)

    def body(*refs):
        outs = refs[T:2 * T]
        send_sems, recv_sems = refs[2 * T:]
        x, y, c = lax.axis_index("x"), lax.axis_index("y"), lax.axis_index("c")

        def copy(t, half):
            return pltpu.make_async_remote_copy(src_ref=outs[t].at[half], dst_ref=outs[t].at[half], send_sem=send_sems.at[t],
                                                recv_sem=recv_sems.at[t], device_id=(x, y, 1 - c), device_id_type=MESH)

        for t in range(T):
            copy(t, c).start()
        for t in range(T):
            copy(t, 1 - c).wait_recv()
        for t in range(T):
            copy(t, c).wait_send()

    return pl.pallas_call(
        body, name="grad_pair_share", out_shape=[jax.ShapeDtypeStruct(f.shape, f.dtype) for f in fulls],
        in_specs=[ANY] * T, out_specs=[ANY] * T, input_output_aliases={t: t for t in range(T)},
        scratch_shapes=[pltpu.SemaphoreType.DMA((T,)), pltpu.SemaphoreType.DMA((T,))],
    )(*fulls)


def _row_tile(hr):
    return next(r for r in (128, 64, 32, 16, 8) if hr % r == 0)


def _pair_add(g, got, c, name, chip_major=False):
    _, _, hr, C = g.shape
    tr = _row_tile(hr)
    own = (lambda k, i, c_ref: (k, c_ref[0], i, 0)) if chip_major else (lambda k, i, c_ref: (c_ref[0], k, i, 0))

    def body(c_ref, g_ref, r_ref, o_ref):
        o_ref[...] = (g_ref[0].astype(F32) + r_ref[...].astype(F32)).astype(o_ref.dtype)

    return pl.pallas_call(
        body, name=name,
        grid_spec=pltpu.PrefetchScalarGridSpec(
            num_scalar_prefetch=1, grid=(4, hr // tr),
            in_specs=[pl.BlockSpec((1, 1, tr, C), own),
                      pl.BlockSpec((1, tr, C), lambda k, i, c_ref: (k, i, 0))],
            out_specs=pl.BlockSpec((1, tr, C), lambda k, i, c_ref: (k, i, 0))),
        out_shape=jax.ShapeDtypeStruct(got.shape, g.dtype),
        compiler_params=pltpu.CompilerParams(dimension_semantics=("parallel", "parallel"), vmem_limit_bytes=VMEM_LIMIT),
    )(jnp.reshape(c, (1,)).astype(jnp.int32), g, got)


def _chip_sum(parts, slots, chip, c, name):
    _, hr, C = parts.shape
    tr = _row_tile(hr)

    def body(chip_ref, half_ref, own_ref, a_ref, b_ref, c_ref, o_ref):
        acc = own_ref[...].astype(F32)
        for r in (a_ref, b_ref, c_ref):
            acc = acc + r[...].astype(F32)
        o_ref[...] = acc

    other = lambda j: pl.BlockSpec((1, tr, C), lambda i, chip_ref, half_ref: ((chip_ref[0] + j) % 4, i, 0))
    return pl.pallas_call(
        body, name=name,
        grid_spec=pltpu.PrefetchScalarGridSpec(
            num_scalar_prefetch=2, grid=(hr // tr,),
            in_specs=[pl.BlockSpec((1, tr, C), lambda i, chip_ref, half_ref: (chip_ref[0], i, 0)),
                      other(1), other(2), other(3)],
            out_specs=pl.BlockSpec((1, tr, C), lambda i, chip_ref, half_ref: (half_ref[0], i, 0))),
        out_shape=jax.ShapeDtypeStruct((2, hr, C), F32),
        compiler_params=pltpu.CompilerParams(dimension_semantics=("parallel",), vmem_limit_bytes=VMEM_LIMIT),
    )(jnp.reshape(chip, (1,)).astype(jnp.int32), jnp.reshape(c, (1,)).astype(jnp.int32), parts, slots, slots, slots)


def _sum_slots(parts, name):
    n, hr, C = parts.shape
    tr = _row_tile(hr)

    def body(p_ref, o_ref):
        acc = p_ref[0].astype(F32)
        for k in range(1, n):
            acc = acc + p_ref[k].astype(F32)
        o_ref[...] = acc

    return pl.pallas_call(
        body, name=name, grid=(hr // tr,),
        in_specs=[pl.BlockSpec((n, tr, C), lambda i: (0, i, 0))], out_specs=pl.BlockSpec((tr, C), lambda i: (i, 0)),
        out_shape=jax.ShapeDtypeStruct((hr, C), F32),
        compiler_params=pltpu.CompilerParams(dimension_semantics=("parallel",), vmem_limit_bytes=VMEM_LIMIT),
    )(parts)


PACK_ROWS, PACK_COLS = 16, 3072
_PACK = [("norm_mix_g", 0, 1, 2048), ("conv_w", 1, 4, 3072), ("conv_b", 5, 1, 3072), ("dt_bias", 6, 1, 32),
         ("a_log", 7, 1, 32), ("d_skip", 8, 1, 32), ("ssd_norm_g", 9, 1, 2048), ("i_bias", 10, 1, 8), ("f_bias", 11, 1, 8),
         ("mlstm_norm_g", 12, 1, 2048), ("norm_ffn_g", 13, 1, 2048), ("final_norm_g", 14, 1, 2048), ("loss", 15, 1, 1)]
_WEIGHTS = ["norm_mix_g", "w_in", "conv_w", "conv_b", "dt_bias", "a_log", "d_skip", "ssd_norm_g", "i_bias", "f_bias",
            "mlstm_norm_g", "w_out", "norm_ffn_g", "w_gate", "w_up", "w_down", "final_norm_g"]
_BIG = ["w_in", "w_out", "w_gate", "w_up", "w_down"]


def _pack(vals):
    rows = []
    for name, _, r, w in _PACK:
        v = vals.get(name)
        v = jnp.zeros((r, w), F32) if v is None else v.reshape(r, w).astype(F32)
        rows.append(jnp.concatenate([v, jnp.zeros((r, PACK_COLS - w), F32)], axis=1) if w < PACK_COLS else v)
    return jnp.concatenate(rows, axis=0)


def _unpack(p):
    return {name: p[r0:r0 + r, :w] for name, r0, r, w in _PACK}


def kernel(x, norm_mix_g, w_in, conv_w, conv_b, dt_bias, a_log, d_skip, ssd_norm_g, i_bias, f_bias, mlstm_norm_g, w_out, norm_ffn_g, w_gate, w_up, w_down, final_norm_g, loss_target, m_norm_mix_g, m_w_in, m_conv_w, m_conv_b, m_dt_bias, m_a_log, m_d_skip, m_ssd_norm_g, m_i_bias, m_f_bias, m_mlstm_norm_g, m_w_out, m_norm_ffn_g, m_w_gate, m_w_up, m_w_down, m_final_norm_g, v_norm_mix_g, v_w_in, v_conv_w, v_conv_b, v_dt_bias, v_a_log, v_d_skip, v_ssd_norm_g, v_i_bias, v_f_bias, v_mlstm_norm_g, v_w_out, v_norm_ffn_g, v_w_gate, v_w_up, v_w_down, v_final_norm_g):
    weights = dict(norm_mix_g=norm_mix_g, w_in=w_in, conv_w=conv_w, conv_b=conv_b, dt_bias=dt_bias, a_log=a_log, d_skip=d_skip,
                   ssd_norm_g=ssd_norm_g, i_bias=i_bias, f_bias=f_bias, mlstm_norm_g=mlstm_norm_g, w_out=w_out,
                   norm_ffn_g=norm_ffn_g, w_gate=w_gate, w_up=w_up, w_down=w_down, final_norm_g=final_norm_g)
    mom1 = dict(norm_mix_g=m_norm_mix_g, w_in=m_w_in, conv_w=m_conv_w, conv_b=m_conv_b, dt_bias=m_dt_bias, a_log=m_a_log,
                d_skip=m_d_skip, ssd_norm_g=m_ssd_norm_g, i_bias=m_i_bias, f_bias=m_f_bias, mlstm_norm_g=m_mlstm_norm_g,
                w_out=m_w_out, norm_ffn_g=m_norm_ffn_g, w_gate=m_w_gate, w_up=m_w_up, w_down=m_w_down,
                final_norm_g=m_final_norm_g)
    mom2 = dict(norm_mix_g=v_norm_mix_g, w_in=v_w_in, conv_w=v_conv_w, conv_b=v_conv_b, dt_bias=v_dt_bias, a_log=v_a_log,
                d_skip=v_d_skip, ssd_norm_g=v_ssd_norm_g, i_bias=v_i_bias, f_bias=v_f_bias, mlstm_norm_g=v_mlstm_norm_g,
                w_out=v_w_out, norm_ffn_g=v_norm_ffn_g, w_gate=v_w_gate, w_up=v_w_up, w_down=v_w_down,
                final_norm_g=v_final_norm_g)
    xi, yi, ci = lax.axis_index("x"), lax.axis_index("y"), lax.axis_index("c")
    chip = 2 * xi + yi
    xs, tgt = x[0], loss_target[0]

    shards = {n: weights[n][0].astype(BF16) for n in _BIG}
    gathered = lambda n, g: _with_own_shard(g, shards[n], chip)
    by_cols = lambda g, width: jnp.transpose(g, (1, 0, 2)).reshape(D_MODEL, width)
    u1, (g_in,) = _rmsnorm_fwd(xs, norm_mix_g, "norm_mix_fwd", side=_gather_side([shards["w_in"]]))
    g_in = gathered("w_in", g_in)
    W_ssd, W_ml, W_small = _split_w_in([g_in[k] for k in range(4)])
    cw_all = _allgather_small(jnp.concatenate([conv_w[0], jnp.zeros((4, 768), F32)], axis=0), "allgather_conv_w")
    conv_w_full = jnp.concatenate([cw_all[2 * k, :SSD_CONV] for k in range(4)], axis=1)
    cwg, cbg = _conv_to_groups(conv_w_full), _conv_to_groups(conv_b)
    dtb_row, al_row = _small_row(dt_bias, 0), _small_row(a_log, 0)
    ib_row, fb_row = _small_row(i_bias, LANE_I), _small_row(f_bias, LANE_F)
    dskip_lane = jnp.repeat(d_skip, 64, axis=1)
    fng = final_norm_g[None]

    p_ssd, (g_out,) = _matmul(u1, W_ssd, "nn", F32, 1024, 1024, 2048, "proj_ssd", side=_gather_side([shards["w_out"]]))
    p_ml, (g_gate,) = _matmul(u1, W_ml, "nn", F32, 1024, 1024, 2048, "proj_ml", side=_gather_side([shards["w_gate"]]))
    p_small = _matmul(u1, W_small, "nn", F32, 1024, 128, 2048, "proj_small")
    y_mix, ssd_st, ssd_conv = _ssd_forward(p_ssd, p_small, cwg, cbg, dtb_row, al_row, dskip_lane, ssd_norm_g)
    y_mix, ml_c, ml_nm = _mlstm_forward(p_ml, p_small, ib_row, fb_row, mlstm_norm_g, y_mix)
    W_out = gathered("w_out", g_out).reshape(2 * SSD_WIDTH, D_MODEL)
    h1, (g_up,) = _matmul(y_mix, W_out, "nn", F32, 1024, 1024, 2048, "out_proj", addend=xs,
                          side=_gather_side([shards["w_up"]]))
    u2 = _rmsnorm_fwd(h1, norm_ffn_g, "norm_ffn_fwd")
    W_gate = by_cols(gathered("w_gate", g_gate), D_FF)
    W_up = by_cols(gathered("w_up", g_up), D_FF)
    gate, up, act, (g_down,) = _ffn_in(u2, W_gate, W_up, 1024, _gather_side([shards["w_down"]]))
    W_down = gathered("w_down", g_down).reshape(D_FF, D_MODEL)
    h2 = _matmul(act, W_down, "nn", F32, 1024, 1024, 2 * FF_TILE, "ffn_down", addend=h1)
    dh2, dh2_b, loss_row, d_fng = _loss_head(h2, tgt, fng)

    dW_down = _matmul(act, dh2_b, "tn", BF16, FF_TILE, 1024, 2048, "dw_down", layout="rows4")
    d_gate, d_up = _ffn_back(dh2_b, W_down, gate, up, 1024)
    du2 = _matmul(d_gate, W_gate, "nt", F32, 1024, 1024, 2 * FF_TILE, "du2_gate")
    du2 = _matmul(d_up, W_up, "nt", F32, 1024, 1024, 2 * FF_TILE, "du2_up", addend=du2)
    dW_gate = _matmul(u2, d_gate, "tn", BF16, 1024, FF_TILE, 2048, "dw_gate", layout="cols4")
    dW_up = _matmul(u2, d_up, "tn", BF16, 1024, FF_TILE, 2048, "dw_up", layout="cols4")
    dh1, dh1_b, d_ffn_g = _rmsnorm_bwd(du2, h1, norm_ffn_g, dh2, "norm_ffn_bwd")
    dW_out, ffn_got = _matmul(y_mix, dh1_b, "tn", BF16, 1024, 1024, 2048, "dw_out", layout="rows4",
                              side=_pair_side([dW_gate, dW_up, dW_down]))
    dy_mix, out_got = _matmul(dh1_b, W_out, "nt", F32, 1024, 1024, 2048, "d_mix", side=_pair_side([dW_out]))
    early = ["w_out", "w_gate", "w_up", "w_down"]
    early_g = [dW_out, dW_gate, dW_up, dW_down]
    early_got = list(out_got) + list(ffn_got)
    parts = {n: _pair_add(g, r, ci, "grad_pair_add_" + n) for g, r, n in zip(early_g, early_got, early)}
    (d_ssd, d_small, d_cw, d_cb, d_dtb, d_alog, d_dskip, d_sng), early_slots = _ssd_backward(
        p_ssd, p_small, cwg, cbg, dtb_row, al_row, dskip_lane, ssd_norm_g, ssd_st, dy_mix, ssd_conv,
        side=_scatter_side([parts[n] for n in early]))
    slots = dict(zip(early, early_slots))
    d_ml, d_small, d_ib, d_fb, d_mng = _mlstm_backward(p_ml, p_small, ib_row, fb_row, mlstm_norm_g, ml_c, ml_nm, dy_mix, d_small)
    dW_ssd = _matmul(u1, d_ssd, "tn", BF16, 1024, 1280, 2048, "dw_ssd")
    dW_ml = _matmul(u1, d_ml, "tn", BF16, 1024, 1024, 2048, "dw_ml")
    dW_small = _matmul(u1, d_small, "tn", BF16, 1024, 128, 2048, "dw_small")
    g_w_in = jnp.stack(_merge_w_in(dW_ssd, dW_ml, dW_small)).reshape(4, 2, D_MODEL // 2, SHARD_IN)
    du1, (got_in,) = _matmul(d_ssd, W_ssd, "nt", F32, 1024, 1024, 2560, "du1_ssd",
                             side=_pair_side([g_w_in], chip_major=True))
    parts["w_in"] = _pair_add(g_w_in, got_in, ci, "grad_pair_add_w_in", chip_major=True)
    du1, (slots["w_in"],) = _matmul(d_ml, W_ml, "nt", F32, 1024, 1024, 3072, "du1_ml", addend=du1,
                                    side=_scatter_side([parts["w_in"]]))
    du1 = _matmul(d_small, W_small, "nt", F32, 1024, 1024, 128, "du1_small", addend=du1)
    grad_x, _, d_mix_g = _rmsnorm_bwd(du1, xs, norm_mix_g, dh1, "norm_mix_bwd")

    fulls = _pair_share([_chip_sum(parts[n], slots[n], chip, ci, "grad_chip_sum_" + n) for n in _BIG])
    grads = {n: f.reshape(weights[n].shape[1:]) for n, f in zip(_BIG, fulls)}

    small = _pack(dict(norm_mix_g=d_mix_g, conv_w=_conv_from_groups(d_cw), conv_b=_conv_from_groups(d_cb),
                       dt_bias=d_dtb[:, 0:32], a_log=d_alog[:, 0:32], d_skip=d_dskip.reshape(SSD_HEADS, 64).sum(axis=1),
                       ssd_norm_g=d_sng, i_bias=d_ib[:, LANE_I:LANE_I + 8], f_bias=d_fb[:, LANE_F:LANE_F + 8],
                       mlstm_norm_g=d_mng, norm_ffn_g=d_ffn_g, final_norm_g=d_fng, loss=loss_row[:, 0:1]))
    total = _sum_slots(_allgather_small(small, "allgather_small_grads"), "small_grad_sum")
    small_w = {n: weights[n] for n in _WEIGHTS if n not in _BIG and n != "conv_w"}
    sd, sm, sv = _adamw(_pack(small_w), total, _pack({n: mom1[n] for n in small_w}), _pack({n: mom2[n] for n in small_w}),
                        PACK_ROWS, "adamw_small")
    tot, sd, sm, sv = _unpack(total), _unpack(sd), _unpack(sm), _unpack(sv)
    for n in small_w:
        grads[n] = tot[n].reshape(weights[n].shape)
    grads["conv_w"] = lax.dynamic_slice_in_dim(tot["conv_w"], chip * 768, 768, axis=1)
    loss = tot["loss"][0, 0]

    delta, new_m, new_v = {}, {}, {}
    for n in _BIG + ["conv_w"]:
        w2 = weights[n][0]
        d, nm, nv = _adamw(w2, grads[n], mom1[n][0], mom2[n][0], _row_tile(w2.shape[0]) if n != "conv_w" else SSD_CONV,
                           "adamw_" + n)
        delta[n], new_m[n], new_v[n] = d[None], nm[None], nv[None]
        grads[n] = grads[n][None]
    for n in small_w:
        delta[n], new_m[n], new_v[n] = (t[n].reshape(weights[n].shape) for t in (sd, sm, sv))
    return (loss, grad_x[None], *[grads[n] for n in _WEIGHTS], *[delta[n] for n in _WEIGHTS],
            *[new_m[n] for n in _WEIGHTS], *[new_v[n] for n in _WEIGHTS])
```

```python
import functools

import jax
import jax.numpy as jnp
from jax import lax
from jax.experimental import pallas as pl
from jax.experimental.pallas import tpu as pltpu

F32 = jnp.float32
BF16 = jnp.bfloat16

D_MODEL = 2048
SSD_HEADS = 32
SSD_GROUPS = 4
SSD_WIDTH = 2048
SSD_CONV = 4
SSD_GROUP_COLS = 1280
SSD_XBC_COLS = 768
ML_HEADS = 8
ML_HEAD_COLS = 768
ML_DK = 128
ML_DV = 256
CHUNK = 128
SMALL_COLS = 128
LANE_I = 32
LANE_F = 40
D_FF = 5632
GATE_SOFTCAP = 15.0
EPS = 1e-6
ADAM_LR, ADAM_B1, ADAM_B2, ADAM_EPS, ADAM_WD, ADAM_STEP = 0.001, 0.9, 0.999, 1e-8, 0.01, 10
MESH = pl.DeviceIdType.MESH
VMEM_LIMIT = 56 * 1024 * 1024


def _dg(a, b, ca, cb):
    return lax.dot_general(a.astype(BF16), b.astype(BF16), (((ca,), (cb,)), ((), ())), preferred_element_type=F32)


@jax.custom_vjp
def _mm(a, b):
    return _dg(a, b, 1, 0)


_mm.defvjp(lambda a, b: (_dg(a, b, 1, 0), (a, b)),
           lambda r, g: (_dg(g, r[1], 1, 1), _dg(r[0], g, 0, 0)))


@jax.custom_vjp
def _mm_nt(a, b):
    return _dg(a, b, 1, 1)


_mm_nt.defvjp(lambda a, b: (_dg(a, b, 1, 1), (a, b)),
              lambda r, g: (_dg(g, r[1], 1, 0), _dg(g, r[0], 0, 0)))


@jax.custom_vjp
def _mm_tn(a, b):
    return _dg(a, b, 0, 0)


_mm_tn.defvjp(lambda a, b: (_dg(a, b, 0, 0), (a, b)),
              lambda r, g: (_dg(r[1], g, 1, 1), _dg(r[0], g, 1, 0)))


def _tri(lower):
    r = lax.broadcasted_iota(jnp.int32, (CHUNK, CHUNK), 0)
    c = lax.broadcasted_iota(jnp.int32, (CHUNK, CHUNK), 1)
    return ((r >= c) if lower else (r <= c)).astype(F32)


def _dg32(t, x):
    return lax.dot_general(t, x, (((1,), (0,)), ((), ())), precision=lax.Precision.HIGHEST, preferred_element_type=F32)


@jax.custom_vjp
def _cumsum(x):
    return _dg32(_tri(True), x)


_cumsum.defvjp(lambda x: (_dg32(_tri(True), x), None), lambda r, g: (_dg32(_tri(False), g),))


_sigmoid = jax.nn.sigmoid


def _silu(x):
    return x * _sigmoid(x)


def _softplus(x):
    return jnp.maximum(x, 0.0) + jnp.log(1.0 + jnp.exp(-jnp.abs(x)))


def _lane_col(m, lane, h):
    return jnp.sum(jnp.where(lane == h, m, 0.0), axis=1, keepdims=True)


SSD_GPS = 4


def _ssd_math(convs, zs, smallb, S_in, dtb_row, alog_row, dskip, ng, g0):
    dt_all = _softplus(smallb + dtb_row)
    a_all = dt_all * (-jnp.exp(alog_row))
    prep = (dt_all, _cumsum(a_all), jnp.sum(a_all, axis=0, keepdims=True))
    ys, S_out = [], []
    for k in range(SSD_GPS):
        y, S = _ssd_group(convs[k][:, 0:512], convs[k][:, 512:640], convs[k][:, 640:768], zs[k], prep, S_in[k],
                          dskip[:, 512 * k:512 * (k + 1)], ng[:, 512 * k:512 * (k + 1)], g0 + k)
        ys.append(y)
        S_out.append(S)
    return tuple(ys), tuple(S_out)


def _ssd_group(cx, cB, cC, z, prep, S_in, dskip, ng, g):
    dt_all, acum_all, alast_all = prep
    lane = lax.broadcasted_iota(jnp.int32, (1, CHUNK), 1)
    row_i = lax.broadcasted_iota(jnp.int32, (CHUNK, CHUNK), 0)
    col_i = lax.broadcasted_iota(jnp.int32, (CHUNK, CHUNK), 1)
    causal = row_i >= col_i
    half = lane < 64
    rhalf = lax.broadcasted_iota(jnp.int32, (CHUNK, 1), 0) < 64
    xs, Bm, Cm = _silu(cx), _silu(cB), _silu(cC)
    cb = _mm_nt(Cm, Bm)

    def lmat(ac):
        acb = jnp.broadcast_to(ac, (CHUNK, CHUNK))
        return jnp.exp(jnp.where(causal, acb - acb.T, -jnp.inf))

    ys, S_out = [], []
    for j in range(4):
        h0 = 8 * g + 2 * j
        ac0, ac1 = _lane_col(acum_all, lane, h0), _lane_col(acum_all, lane, h0 + 1)
        dt0, dt1 = _lane_col(dt_all, lane, h0), _lane_col(dt_all, lane, h0 + 1)
        al0, al1 = _lane_col(alast_all, lane, h0), _lane_col(alast_all, lane, h0 + 1)
        Xp = xs[:, 128 * j:128 * (j + 1)]
        Xd = Xp * jnp.where(half, dt0, dt1)
        ac_sel = jnp.where(half, ac0, ac1)
        al_sel = jnp.where(half, al0, al1)
        Yd = jnp.where(half, _mm(cb * lmat(ac0), Xd), _mm(cb * lmat(ac1), Xd))
        Yoff = _mm_nt(Cm, S_in[j]) * jnp.exp(ac_sel)
        ys.append(Yd + Yoff + dskip[:, 128 * j:128 * (j + 1)] * Xp)
        S_new = _mm_tn(Xd * jnp.exp(al_sel - ac_sel), Bm)
        S_out.append(S_in[j] * jnp.exp(jnp.where(rhalf, al0, al1)) + S_new)
    y = jnp.concatenate(ys, axis=1)
    y = y * _silu(z)
    y = y * lax.rsqrt(jnp.mean(y * y, axis=1, keepdims=True) + EPS) * ng
    return y, tuple(S_out)


def _ssd_window(blk_ref, halo_ref, ext, first, k):
    cols = slice(SSD_GROUP_COLS * k + 512, SSD_GROUP_COLS * (k + 1))
    halo = halo_ref[:, cols]
    ext[k, 0:8, :] = jnp.where(first, jnp.zeros_like(halo), halo)
    ext[k, 8:8 + CHUNK, :] = blk_ref[:, cols]


def _ssd_conv(blk_ref, halo_ref, cw_ref, cb_ref, ext, first, k):
    _ssd_window(blk_ref, halo_ref, ext, first, k)
    conv = jnp.broadcast_to(cb_ref[k], (CHUNK, SSD_XBC_COLS))
    for tap in range(SSD_CONV):
        conv = conv + cw_ref[k, tap:tap + 1, :] * ext[k, pl.ds(5 + tap, CHUNK), :]
    return conv


def _ssd_specs(nc, rev):
    cc = (lambda c: nc - 1 - c) if rev else (lambda c: c)
    width = SSD_GPS * SSD_GROUP_COLS
    return [
        pl.BlockSpec((CHUNK, width), lambda c, g: (cc(c), g)),
        pl.BlockSpec((8, width), lambda c, g: (jnp.maximum(cc(c) * (CHUNK // 8) - 1, 0), g)),
        pl.BlockSpec((CHUNK, SMALL_COLS), lambda c, g: (cc(c), 0)),
        pl.BlockSpec((SSD_GPS, SSD_CONV, SSD_XBC_COLS), lambda c, g: (g, 0, 0)),
        pl.BlockSpec((SSD_GPS, 1, SSD_XBC_COLS), lambda c, g: (g, 0, 0)),
        pl.BlockSpec((1, SMALL_COLS), lambda c, g: (0, 0)),
        pl.BlockSpec((1, SMALL_COLS), lambda c, g: (0, 0)),
        pl.BlockSpec((1, SSD_GPS * 512), lambda c, g: (0, g)),
        pl.BlockSpec((1, SSD_GPS * 512), lambda c, g: (0, g)),
    ]


def _ssd_forward(proj, small, cw, cb, dtb, alog, dskip, ng):
    L = proj.shape[0]
    nc = L // CHUNK

    def body(blk_ref, halo_ref, small_ref, cw_ref, cb_ref, dtb_ref, alog_ref, dskip_ref, ng_ref, y_ref, st_ref, conv_ref,
             carry, ext):
        c, g0 = pl.program_id(0), pl.program_id(1) * SSD_GPS

        @pl.when(c == 0)
        def _():
            for k in range(SSD_GPS):
                carry[g0 + k] = jnp.zeros((4, CHUNK, CHUNK), F32)

        convs = [_ssd_conv(blk_ref, halo_ref, cw_ref, cb_ref, ext, c == 0, k) for k in range(SSD_GPS)]
        for k in range(SSD_GPS):
            conv_ref[:, SSD_XBC_COLS * k:SSD_XBC_COLS * (k + 1)] = convs[k]
        zs = [blk_ref[:, SSD_GROUP_COLS * k:SSD_GROUP_COLS * k + 512] for k in range(SSD_GPS)]
        S_in = tuple(tuple(carry[g0 + k, j] for j in range(4)) for k in range(SSD_GPS))
        for k in range(SSD_GPS):
            st_ref[0, k] = carry[g0 + k]
        ys, S_out = _ssd_math(convs, zs, small_ref[...], S_in, dtb_ref[...], alog_ref[...], dskip_ref[...], ng_ref[...], g0)
        for k in range(SSD_GPS):
            y_ref[:, 512 * k:512 * (k + 1)] = ys[k].astype(BF16)
            for j in range(4):
                carry[g0 + k, j] = S_out[k][j]

    return pl.pallas_call(
        body, name="ssd_fwd", grid=(nc, SSD_GROUPS // SSD_GPS),
        in_specs=_ssd_specs(nc, False),
        out_specs=[pl.BlockSpec((CHUNK, SSD_GPS * 512), lambda c, g: (c, g)),
                   pl.BlockSpec((1, SSD_GPS, 4, CHUNK, CHUNK), lambda c, g: (c, g, 0, 0, 0)),
                   pl.BlockSpec((CHUNK, SSD_GPS * SSD_XBC_COLS), lambda c, g: (c, g))],
        out_shape=[jax.ShapeDtypeStruct((L, 2 * SSD_WIDTH), BF16),
                   jax.ShapeDtypeStruct((nc, SSD_GROUPS, 4, CHUNK, CHUNK), F32),
                   jax.ShapeDtypeStruct((L, SSD_GROUPS * SSD_XBC_COLS), F32)],
        scratch_shapes=[pltpu.VMEM((SSD_GROUPS, 4, CHUNK, CHUNK), F32), pltpu.VMEM((SSD_GPS, 8 + CHUNK, SSD_XBC_COLS), F32)],
        compiler_params=pltpu.CompilerParams(dimension_semantics=("arbitrary", "arbitrary"), vmem_limit_bytes=VMEM_LIMIT),
    )(proj, proj, small, cw, cb, dtb, alog, dskip, ng)


def _ssd_backward(proj, small, cw, cb, dtb, alog, dskip, ng, states, dy, conv, side=None):
    L = proj.shape[0]
    nc = L // CHUNK
    s_in, s_out = (len(side.operands), len(side.out_shapes)) if side is not None else (0, 0)
    grid = (nc, SSD_GROUPS // SSD_GPS)
    n_in = 12

    def body(*refs):
        (blk_ref, halo_ref, small_ref, cw_ref, cb_ref, dtb_ref, alog_ref, dskip_ref, ng_ref, st_ref, dy_ref,
         conv_ref) = refs[:n_in]
        o0 = n_in + s_in
        dproj_ref, dsmall_ref, dcw_ref, dcb_ref, ddtb_ref, dalog_ref, ddskip_ref, dng_ref = refs[o0:o0 + 8]
        dcarry, nxt, ext, dext = refs[o0 + 8 + s_out:o0 + 12 + s_out]
        side_refs = (refs[n_in:o0], refs[o0 + 8:o0 + 8 + s_out], refs[o0 + 12 + s_out:])
        s, gb = pl.program_id(0), pl.program_id(1)
        g0 = gb * SSD_GPS
        c = nc - 1 - s
        if side is not None:
            pl.when(_grid_edge(grid, first=True))(lambda: side.start(*side_refs))

        @pl.when(s == 0)
        def _():
            for k in range(SSD_GPS):
                dcarry[g0 + k] = jnp.zeros((4, CHUNK, CHUNK), F32)
                nxt[g0 + k] = jnp.zeros((8, SSD_XBC_COLS), F32)
                dcw_ref[g0 + k] = jnp.zeros((SSD_CONV, SSD_XBC_COLS), F32)
                dcb_ref[g0 + k] = jnp.zeros((1, SSD_XBC_COLS), F32)
                ddskip_ref[g0 + k] = jnp.zeros((1, 512), F32)
                dng_ref[g0 + k] = jnp.zeros((1, 512), F32)

        @pl.when((s == 0) & (gb == 0))
        def _():
            ddtb_ref[...] = jnp.zeros((1, SMALL_COLS), F32)
            dalog_ref[...] = jnp.zeros((1, SMALL_COLS), F32)

        for k in range(SSD_GPS):
            _ssd_window(blk_ref, halo_ref, ext, c == 0, k)
        convs = [conv_ref[:, SSD_XBC_COLS * k:SSD_XBC_COLS * (k + 1)] for k in range(SSD_GPS)]
        zs = [blk_ref[:, SSD_GROUP_COLS * k:SSD_GROUP_COLS * k + 512] for k in range(SSD_GPS)]
        S_in = tuple(tuple(st_ref[0, k, j] for j in range(4)) for k in range(SSD_GPS))
        dS_out = tuple(tuple(dcarry[g0 + k, j] for j in range(4)) for k in range(SSD_GPS))
        dys = tuple(dy_ref[:, 512 * k:512 * (k + 1)].astype(F32) for k in range(SSD_GPS))
        _, vjp = jax.vjp(functools.partial(_ssd_math, g0=g0), convs, zs, small_ref[...], S_in, dtb_ref[...], alog_ref[...],
                         dskip_ref[...], ng_ref[...])
        d_convs, d_zs, d_small, dS_in, d_dtb, d_alog, d_dskip, d_ng = vjp((dys, dS_out))
        for k in range(SSD_GPS):
            for j in range(4):
                dcarry[g0 + k, j] = dS_in[k][j]
            dext[k, 0:8, :] = jnp.zeros((8, SSD_XBC_COLS), F32)
            dext[k, 8:8 + CHUNK, :] = d_convs[k]
            dext[k, 8 + CHUNK:16 + CHUNK, :] = nxt[g0 + k]
            nxt[g0 + k] = dext[k, 8:16, :]
            dconv = d_convs[k]
            d_xbc = jnp.zeros((CHUNK, SSD_XBC_COLS), F32)
            for tap in range(SSD_CONV):
                d_xbc = d_xbc + cw_ref[k, tap:tap + 1, :] * dext[k, pl.ds(8 + 3 - tap, CHUNK), :]
                dcw_ref[g0 + k, tap:tap + 1, :] += jnp.sum(dconv * ext[k, pl.ds(5 + tap, CHUNK), :], axis=0, keepdims=True)
            dcb_ref[g0 + k] += jnp.sum(dconv, axis=0, keepdims=True)
            dproj_ref[:, SSD_GROUP_COLS * k:SSD_GROUP_COLS * k + 512] = d_zs[k].astype(BF16)
            dproj_ref[:, SSD_GROUP_COLS * k + 512:SSD_GROUP_COLS * (k + 1)] = d_xbc.astype(BF16)
            ddskip_ref[g0 + k] += d_dskip[:, 512 * k:512 * (k + 1)]
            dng_ref[g0 + k] += d_ng[:, 512 * k:512 * (k + 1)]

        @pl.when(gb == 0)
        def _():
            dsmall_ref[...] = d_small

        @pl.when(gb != 0)
        def _():
            dsmall_ref[...] += d_small

        ddtb_ref[...] += d_dtb
        dalog_ref[...] += d_alog
        if side is not None:
            pl.when(_grid_edge(grid, first=False))(lambda: side.finish(*side_refs))

    whole = lambda shape: pl.BlockSpec(shape, lambda s, g: (0,) * len(shape))
    side_ops = tuple(side.operands) if side is not None else ()
    res = pl.pallas_call(
        body, name="ssd_bwd", grid=grid,
        in_specs=_ssd_specs(nc, True) + [
            pl.BlockSpec((1, SSD_GPS, 4, CHUNK, CHUNK), lambda s, g: (nc - 1 - s, g, 0, 0, 0)),
            pl.BlockSpec((CHUNK, SSD_GPS * 512), lambda s, g: (nc - 1 - s, g)),
            pl.BlockSpec((CHUNK, SSD_GPS * SSD_XBC_COLS), lambda s, g: (nc - 1 - s, g))] + [ANY] * s_in,
        out_specs=[pl.BlockSpec((CHUNK, SSD_GPS * SSD_GROUP_COLS), lambda s, g: (nc - 1 - s, g)),
                   pl.BlockSpec((CHUNK, SMALL_COLS), lambda s, g: (nc - 1 - s, 0)),
                   whole((SSD_GROUPS, SSD_CONV, SSD_XBC_COLS)), whole((SSD_GROUPS, 1, SSD_XBC_COLS)),
                   whole((1, SMALL_COLS)), whole((1, SMALL_COLS)),
                   whole((SSD_GROUPS, 1, 512)), whole((SSD_GROUPS, 1, 512))] + [ANY] * s_out,
        out_shape=[jax.ShapeDtypeStruct((L, SSD_GROUPS * SSD_GROUP_COLS), BF16), jax.ShapeDtypeStruct((L, SMALL_COLS), F32),
                   jax.ShapeDtypeStruct((SSD_GROUPS, SSD_CONV, SSD_XBC_COLS), F32),
                   jax.ShapeDtypeStruct((SSD_GROUPS, 1, SSD_XBC_COLS), F32),
                   jax.ShapeDtypeStruct((1, SMALL_COLS), F32), jax.ShapeDtypeStruct((1, SMALL_COLS), F32),
                   jax.ShapeDtypeStruct((SSD_GROUPS, 1, 512), F32), jax.ShapeDtypeStruct((SSD_GROUPS, 1, 512), F32)]
        + (list(side.out_shapes) if side is not None else []),
        scratch_shapes=[pltpu.VMEM((SSD_GROUPS, 4, CHUNK, CHUNK), F32), pltpu.VMEM((SSD_GROUPS, 8, SSD_XBC_COLS), F32),
                        pltpu.VMEM((SSD_GPS, 8 + CHUNK, SSD_XBC_COLS), F32),
                        pltpu.VMEM((SSD_GPS, 16 + CHUNK, SSD_XBC_COLS), F32)]
        + (list(side.sems) if side is not None else []),
        compiler_params=pltpu.CompilerParams(dimension_semantics=("arbitrary", "arbitrary"), vmem_limit_bytes=VMEM_LIMIT),
    )(proj, proj, small, cw, cb, dtb, alog, dskip, ng, states, dy, conv, *side_ops)
    return res if side is None else (res[:8], res[8:])


ML_HPS = 8


def _mlstm_math(blks, smallb, C_in, n_in, m_in, ib_row, fb_row, ng, h0):
    li_all = GATE_SOFTCAP * jnp.tanh((smallb + ib_row) / GATE_SOFTCAP)
    lf_all = -_softplus(-(GATE_SOFTCAP * jnp.tanh((smallb + fb_row) / GATE_SOFTCAP)))
    prep = (li_all, lf_all, _cumsum(lf_all))
    out = [_mlstm_head(blks[k][:, 0:128], blks[k][:, 128:256], blks[k][:, 256:512], blks[k][:, 512:768], prep,
                       C_in[k], n_in[k], m_in[k], ng[:, ML_DV * k:ML_DV * (k + 1)], h0 + k) for k in range(ML_HPS)]
    return tuple(zip(*out))


def _mlstm_head(q, k, v, o_raw, prep, C_in, n_in, m_in_row, ng, h):
    li_all, lf_all, bcum_all = prep
    lane = lax.broadcasted_iota(jnp.int32, (1, CHUNK), 1)
    row_i = lax.broadcasted_iota(jnp.int32, (CHUNK, CHUNK), 0)
    col_i = lax.broadcasted_iota(jnp.int32, (CHUNK, CHUNK), 1)
    causal = row_i >= col_i
    kk = k * (ML_DK ** -0.5)
    li = _lane_col(li_all, lane, LANE_I + h)
    lf = _lane_col(lf_all, lane, LANE_F + h)
    bc = _lane_col(bcum_all, lane, LANE_F + h)
    b_last = jnp.sum(lf, axis=0, keepdims=True)
    m_in = _lane_col(m_in_row, lane, 0)
    a = b_last - bc + li
    m_loc = jnp.max(a, axis=0, keepdims=True)
    w = jnp.exp(a - m_loc)
    C_loc = _mm_tn(w * v, kk)
    n_loc = jnp.sum(w * kk, axis=0, keepdims=True)
    m_new = jnp.maximum(b_last + m_in, m_loc)
    s_old = jnp.exp(b_last + m_in - m_new)
    s_new = jnp.exp(m_loc - m_new)
    C_out = s_old * C_in + s_new * C_loc
    n_out = s_old * n_in + s_new * n_loc
    bc_b = jnp.broadcast_to(bc, (CHUNK, CHUNK))
    li_b = jnp.broadcast_to(li, (CHUNK, CHUNK))
    D = jnp.where(causal, bc_b - bc_b.T + li_b.T, -jnp.inf)
    m_intra = jnp.max(D, axis=1, keepdims=True)
    inter_log = bc + m_in
    m_t = jnp.maximum(inter_log, m_intra)
    S = _mm_nt(q, kk) * jnp.exp(D - m_t)
    w_inter = jnp.exp(inter_log - m_t)
    num = _mm(S, v) + w_inter * _mm_nt(q, C_in)
    nq = jnp.sum(S, axis=1, keepdims=True) + w_inter * jnp.sum(q * n_in, axis=1, keepdims=True)
    den = jnp.maximum(jnp.abs(nq), jnp.exp(-m_t))
    hh = num / den
    hh = hh * lax.rsqrt(jnp.mean(hh * hh, axis=1, keepdims=True) + EPS)
    hh = hh * ng * _sigmoid(o_raw)
    return hh, C_out, n_out, jnp.broadcast_to(m_new, (1, CHUNK))


def _ml_specs(nc, rev):
    cc = (lambda c: nc - 1 - c) if rev else (lambda c: c)
    return [
        pl.BlockSpec((CHUNK, ML_HPS * ML_HEAD_COLS), lambda c, h: (cc(c), h)),
        pl.BlockSpec((CHUNK, SMALL_COLS), lambda c, h: (cc(c), 0)),
        pl.BlockSpec((1, SMALL_COLS), lambda c, h: (0, 0)),
        pl.BlockSpec((1, SMALL_COLS), lambda c, h: (0, 0)),
        pl.BlockSpec((1, ML_HPS * ML_DV), lambda c, h: (0, h)),
    ]


def _mlstm_forward(proj, small, ib, fb, ng, y_mix):
    L = proj.shape[0]
    nc = L // CHUNK

    def body(blk_ref, small_ref, ib_ref, fb_ref, ng_ref, _, y_ref, cst_ref, nm_ref, c_carry, nm_carry):
        c, h0 = pl.program_id(0), pl.program_id(1) * ML_HPS

        @pl.when(c == 0)
        def _():
            for k in range(ML_HPS):
                c_carry[h0 + k] = jnp.zeros((ML_DV, ML_DK), F32)
                nm_carry[h0 + k] = jnp.zeros((2, ML_DK), F32)

        for k in range(ML_HPS):
            cst_ref[0, k] = c_carry[h0 + k]
            nm_ref[0, k] = nm_carry[h0 + k]
        hh, C_out, n_out, m_out = _mlstm_math(
            [blk_ref[:, ML_HEAD_COLS * k:ML_HEAD_COLS * (k + 1)] for k in range(ML_HPS)], small_ref[...],
            [c_carry[h0 + k] for k in range(ML_HPS)], [nm_carry[h0 + k, 0:1, :] for k in range(ML_HPS)],
            [nm_carry[h0 + k, 1:2, :] for k in range(ML_HPS)], ib_ref[...], fb_ref[...], ng_ref[...], h0)
        for k in range(ML_HPS):
            y_ref[:, ML_DV * k:ML_DV * (k + 1)] = hh[k].astype(BF16)
            c_carry[h0 + k] = C_out[k]
            nm_carry[h0 + k, 0:1, :] = n_out[k]
            nm_carry[h0 + k, 1:2, :] = m_out[k]

    return pl.pallas_call(
        body, name="mlstm_fwd", grid=(nc, ML_HEADS // ML_HPS),
        in_specs=_ml_specs(nc, False) + [pl.BlockSpec(memory_space=pl.ANY)],
        out_specs=[pl.BlockSpec((CHUNK, ML_HPS * ML_DV), lambda c, h: (c, SSD_WIDTH // (ML_HPS * ML_DV) + h)),
                   pl.BlockSpec((1, ML_HPS, ML_DV, ML_DK), lambda c, h: (c, h, 0, 0)),
                   pl.BlockSpec((1, ML_HPS, 2, ML_DK), lambda c, h: (c, h, 0, 0))],
        out_shape=[jax.ShapeDtypeStruct(y_mix.shape, BF16),
                   jax.ShapeDtypeStruct((nc, ML_HEADS, ML_DV, ML_DK), F32),
                   jax.ShapeDtypeStruct((nc, ML_HEADS, 2, ML_DK), F32)],
        input_output_aliases={5: 0},
        scratch_shapes=[pltpu.VMEM((ML_HEADS, ML_DV, ML_DK), F32), pltpu.VMEM((ML_HEADS, 2, ML_DK), F32)],
        compiler_params=pltpu.CompilerParams(dimension_semantics=("arbitrary", "arbitrary"), vmem_limit_bytes=VMEM_LIMIT),
    )(proj, small, ib, fb, ng, y_mix)


def _mlstm_backward(proj, small, ib, fb, ng, cst, nmst, dy, dsmall_in):
    L = proj.shape[0]
    nc = L // CHUNK

    def body(blk_ref, small_ref, ib_ref, fb_ref, ng_ref, cst_ref, nm_ref, dy_ref, dsin_ref,
             dproj_ref, dsmall_ref, dib_ref, dfb_ref, dng_ref, dc_carry, dnm_carry):
        s, hb = pl.program_id(0), pl.program_id(1)
        h0 = hb * ML_HPS

        @pl.when(s == 0)
        def _():
            for k in range(ML_HPS):
                dc_carry[h0 + k] = jnp.zeros((ML_DV, ML_DK), F32)
                dnm_carry[h0 + k] = jnp.zeros((2, ML_DK), F32)
                dng_ref[h0 + k] = jnp.zeros((1, ML_DV), F32)

        @pl.when((s == 0) & (hb == 0))
        def _():
            dib_ref[...] = jnp.zeros((1, SMALL_COLS), F32)
            dfb_ref[...] = jnp.zeros((1, SMALL_COLS), F32)

        heads = range(ML_HPS)
        _, vjp = jax.vjp(functools.partial(_mlstm_math, h0=h0),
                         [blk_ref[:, ML_HEAD_COLS * k:ML_HEAD_COLS * (k + 1)] for k in heads], small_ref[...],
                         [cst_ref[0, k] for k in heads], [nm_ref[0, k, 0:1, :] for k in heads],
                         [nm_ref[0, k, 1:2, :] for k in heads], ib_ref[...], fb_ref[...], ng_ref[...])
        d_blk, d_small, dC, dn, dm, d_ib, d_fb, d_ng = vjp(
            (tuple(dy_ref[:, ML_DV * k:ML_DV * (k + 1)].astype(F32) for k in heads),
             tuple(dc_carry[h0 + k] for k in heads), tuple(dnm_carry[h0 + k, 0:1, :] for k in heads),
             tuple(dnm_carry[h0 + k, 1:2, :] for k in heads)))
        for k in heads:
            dc_carry[h0 + k] = dC[k]
            dnm_carry[h0 + k, 0:1, :] = dn[k]
            dnm_carry[h0 + k, 1:2, :] = dm[k]
            dproj_ref[:, ML_HEAD_COLS * k:ML_HEAD_COLS * (k + 1)] = d_blk[k].astype(BF16)
            dng_ref[h0 + k] += d_ng[:, ML_DV * k:ML_DV * (k + 1)]

        @pl.when(hb == 0)
        def _():
            dsmall_ref[...] = dsin_ref[...] + d_small

        @pl.when(hb != 0)
        def _():
            dsmall_ref[...] += d_small

        dib_ref[...] += d_ib
        dfb_ref[...] += d_fb

    whole = lambda shape: pl.BlockSpec(shape, lambda s, h: (0,) * len(shape))
    return pl.pallas_call(
        body, name="mlstm_bwd", grid=(nc, ML_HEADS // ML_HPS),
        in_specs=_ml_specs(nc, True) + [
            pl.BlockSpec((1, ML_HPS, ML_DV, ML_DK), lambda s, h: (nc - 1 - s, h, 0, 0)),
            pl.BlockSpec((1, ML_HPS, 2, ML_DK), lambda s, h: (nc - 1 - s, h, 0, 0)),
            pl.BlockSpec((CHUNK, ML_HPS * ML_DV), lambda s, h: (nc - 1 - s, SSD_WIDTH // (ML_HPS * ML_DV) + h)),
            pl.BlockSpec((CHUNK, SMALL_COLS), lambda s, h: (nc - 1 - s, 0))],
        out_specs=[pl.BlockSpec((CHUNK, ML_HPS * ML_HEAD_COLS), lambda s, h: (nc - 1 - s, h)),
                   pl.BlockSpec((CHUNK, SMALL_COLS), lambda s, h: (nc - 1 - s, 0)),
                   whole((1, SMALL_COLS)), whole((1, SMALL_COLS)), whole((ML_HEADS, 1, ML_DV))],
        out_shape=[jax.ShapeDtypeStruct((L, ML_HEADS * ML_HEAD_COLS), BF16), jax.ShapeDtypeStruct((L, SMALL_COLS), F32),
                   jax.ShapeDtypeStruct((1, SMALL_COLS), F32), jax.ShapeDtypeStruct((1, SMALL_COLS), F32),
                   jax.ShapeDtypeStruct((ML_HEADS, 1, ML_DV), F32)],
        scratch_shapes=[pltpu.VMEM((ML_HEADS, ML_DV, ML_DK), F32), pltpu.VMEM((ML_HEADS, 2, ML_DK), F32)],
        compiler_params=pltpu.CompilerParams(dimension_semantics=("arbitrary", "arbitrary"), vmem_limit_bytes=VMEM_LIMIT),
    )(proj, small, ib, fb, ng, cst, nmst, dy, dsmall_in)


_OFF_Z, _OFF_X, _OFF_B, _OFF_C, _OFF_DT = 0, 2048, 4096, 4608, 5120
_OFF_Q, _OFF_K, _OFF_V, _OFF_O, _OFF_I, _OFF_F, IN_WIDTH = 5152, 6176, 7200, 9248, 11296, 11304, 11312


def _ssd_segments():
    segs = []
    for g in range(SSD_GROUPS):
        segs += [(_OFF_Z + 512 * g, 512), (_OFF_X + 512 * g, 512), (_OFF_B + 128 * g, 128), (_OFF_C + 128 * g, 128)]
    return segs


def _ml_segments():
    segs = []
    for h in range(ML_HEADS):
        segs += [(_OFF_Q + 128 * h, 128), (_OFF_K + 128 * h, 128), (_OFF_V + 256 * h, 256), (_OFF_O + 256 * h, 256)]
    return segs


_SMALL_SEGMENTS = [(_OFF_DT, 32), (_OFF_I, 8), (_OFF_F, 8)]


SHARD_IN = IN_WIDTH // 4


def _take_cols(slabs, segs):
    parts = []
    for s, n in segs:
        while n > 0:
            k, lo = divmod(s, SHARD_IN)
            m = min(n, SHARD_IN - lo)
            parts.append(slabs[k][:, lo:lo + m])
            s, n = s + m, n - m
    return jnp.concatenate(parts, axis=1)


def _split_w_in(slabs):
    small = _take_cols(slabs, _SMALL_SEGMENTS)
    small = jnp.concatenate([small, jnp.zeros((small.shape[0], SMALL_COLS - small.shape[1]), small.dtype)], axis=1)
    return _take_cols(slabs, _ssd_segments()), _take_cols(slabs, _ml_segments()), small


def _merge_w_in(d_ssd, d_ml, d_small):
    pieces = []
    for arr, segs in ((d_ssd, _ssd_segments()), (d_ml, _ml_segments()), (d_small, _SMALL_SEGMENTS)):
        pos = 0
        for s, n in segs:
            while n > 0:
                lo = s % SHARD_IN
                m = min(n, SHARD_IN - lo)
                pieces.append((s, arr[:, pos:pos + m]))
                s, n, pos = s + m, n - m, pos + m
    pieces.sort(key=lambda t: t[0])
    return [jnp.concatenate([p for s, p in pieces if s // SHARD_IN == k], axis=1) for k in range(4)]


def _conv_to_groups(cw):
    return jnp.stack([jnp.concatenate([cw[:, 512 * g:512 * (g + 1)], cw[:, 2048 + 128 * g:2048 + 128 * (g + 1)],
                                       cw[:, 2560 + 128 * g:2560 + 128 * (g + 1)]], axis=1) for g in range(SSD_GROUPS)])


def _conv_from_groups(d):
    return jnp.concatenate([d[g, :, 0:512] for g in range(SSD_GROUPS)] + [d[g, :, 512:640] for g in range(SSD_GROUPS)]
                           + [d[g, :, 640:768] for g in range(SSD_GROUPS)], axis=1)


def _small_row(vec, lane0):
    n = vec.shape[1]
    return jnp.concatenate([jnp.zeros((1, lane0), F32), vec, jnp.zeros((1, SMALL_COLS - lane0 - n), F32)], axis=1)


class _Side:
    def __init__(self, operands, out_shapes, sems, start, finish):
        self.operands, self.out_shapes, self.sems, self.start, self.finish = operands, out_shapes, sems, start, finish


def _grid_edge(grid, first):
    hit = None
    for axis, n in enumerate(grid):
        here = pl.program_id(axis) == (0 if first else n - 1)
        hit = here if hit is None else hit & here
    return hit


def _matmul(a, b, mode, out_dtype, tm, tn, tk, name, addend=None, layout=None, side=None):
    M, Kd = (a.shape[1], a.shape[0]) if mode == "tn" else a.shape
    N = b.shape[0] if mode == "nt" else b.shape[1]
    if layout == "cols4":
        tm, tn = M // 2, N // 4
    elif layout == "rows4":
        tm = M // 4
    tm, tn, tk = min(tm, M), min(tn, N), min(tk, Kd)
    if mode == "nn":
        a_spec = pl.BlockSpec((tm, tk), lambda i, j, k: (i, k))
        b_spec = pl.BlockSpec((tk, tn), lambda i, j, k: (k, j))
        dims = (((1,), (0,)), ((), ()))
    elif mode == "nt":
        a_spec = pl.BlockSpec((tm, tk), lambda i, j, k: (i, k))
        b_spec = pl.BlockSpec((tn, tk), lambda i, j, k: (j, k))
        dims = (((1,), (1,)), ((), ()))
    else:
        a_spec = pl.BlockSpec((tk, tm), lambda i, j, k: (k, i))
        b_spec = pl.BlockSpec((tk, tn), lambda i, j, k: (k, j))
        dims = (((0,), (0,)), ((), ()))
    assert M % tm == 0 and N % tn == 0 and Kd % tk == 0, (name, M, N, Kd, tm, tn, tk)
    nk = Kd // tk
    has_add = addend is not None
    n_in = 2 + has_add
    s_in, s_out = (len(side.operands), len(side.out_shapes)) if side is not None else (0, 0)
    grid = (M // tm, N // tn, nk)

    def body(*refs):
        a_ref, b_ref = refs[0], refs[1]
        add_ref = refs[2] if has_add else None
        o_ref, acc_ref = refs[n_in + s_in], refs[n_in + s_in + 1 + s_out]
        k = pl.program_id(2)
        side_refs = (refs[n_in:n_in + s_in], refs[n_in + s_in + 1:n_in + s_in + 1 + s_out], refs[n_in + s_in + 2 + s_out:])
        if side is not None:
            pl.when(_grid_edge(grid, first=True))(lambda: side.start(*side_refs))
        part = lax.dot_general(a_ref[...].astype(BF16), b_ref[...].astype(BF16), dims, preferred_element_type=F32)

        @pl.when(k == 0)
        def _():
            acc_ref[...] = part

        @pl.when(k != 0)
        def _():
            acc_ref[...] += part

        @pl.when(k == nk - 1)
        def _():
            r = acc_ref[...]
            if has_add:
                r = r + add_ref[...].astype(F32)
            if layout == "cols4":
                o_ref[0, 0] = r.astype(out_dtype)
            elif layout == "rows4":
                o_ref[0, 0] = r[:tm // 2].astype(out_dtype)
                o_ref[1, 0] = r[tm // 2:].astype(out_dtype)
            else:
                o_ref[...] = r.astype(out_dtype)

        if side is not None:
            pl.when(_grid_edge(grid, first=False))(lambda: side.finish(*side_refs))

    o_spec = pl.BlockSpec((tm, tn), lambda i, j, k: (i, j))
    out_spec, out_dims = o_spec, (M, N)
    if layout == "cols4":
        out_spec, out_dims = pl.BlockSpec((1, 1, tm, tn), lambda i, j, k: (i, j, 0, 0)), (2, 4, tm, tn)
    elif layout == "rows4":
        out_spec, out_dims = pl.BlockSpec((2, 1, tm // 2, tn), lambda i, j, k: (0, i, 0, j)), (2, 4, tm // 2, N)
    out_shape = jax.ShapeDtypeStruct(out_dims, out_dtype)
    semantics = ("parallel", "parallel", "arbitrary")
    if side is None:
        in_specs, operands, out_specs, scratch = [], (), out_spec, []
    else:
        in_specs, operands, scratch = [ANY] * s_in, tuple(side.operands), list(side.sems)
        out_specs, out_shape = [out_spec] + [ANY] * s_out, [out_shape] + list(side.out_shapes)
        semantics = ("arbitrary",) * 3
    res = pl.pallas_call(
        body, name=name, grid=grid,
        in_specs=[a_spec, b_spec] + ([o_spec] if has_add else []) + in_specs, out_specs=out_specs,
        out_shape=out_shape, scratch_shapes=[pltpu.VMEM((tm, tn), F32)] + scratch,
        compiler_params=pltpu.CompilerParams(dimension_semantics=semantics, vmem_limit_bytes=VMEM_LIMIT),
    )(*((a, b) + ((addend,) if has_add else ()) + operands))
    return res if side is None else (res[0], res[1:])


ROW_TILE = 512


def _rmsnorm_fwd(x, g, name, side=None):
    L, D = x.shape
    tr = min(ROW_TILE, L)
    s_in, s_out = (len(side.operands), len(side.out_shapes)) if side is not None else (0, 0)
    grid = (L // tr,)

    def body(*refs):
        x_ref, g_ref, u_ref = refs[0], refs[1], refs[2 + s_in]
        side_refs = (refs[2:2 + s_in], refs[3 + s_in:3 + s_in + s_out], refs[3 + s_in + s_out:])
        if side is not None:
            pl.when(_grid_edge(grid, first=True))(lambda: side.start(*side_refs))
        xv = x_ref[...]
        r = lax.rsqrt(jnp.mean(xv * xv, axis=1, keepdims=True) + EPS)
        u_ref[...] = (xv * r * g_ref[...]).astype(BF16)
        if side is not None:
            pl.when(_grid_edge(grid, first=False))(lambda: side.finish(*side_refs))

    row = pl.BlockSpec((tr, D), lambda i: (i, 0))
    res = pl.pallas_call(
        body, name=name, grid=grid,
        in_specs=[row, pl.BlockSpec((1, D), lambda i: (0, 0))] + [ANY] * s_in, out_specs=[row] + [ANY] * s_out,
        out_shape=[jax.ShapeDtypeStruct((L, D), BF16)] + (list(side.out_shapes) if side is not None else []),
        scratch_shapes=list(side.sems) if side is not None else [],
        compiler_params=pltpu.CompilerParams(dimension_semantics=("arbitrary",), vmem_limit_bytes=VMEM_LIMIT),
    )(x, g, *(side.operands if side is not None else ()))
    return res[0] if side is None else (res[0], res[1:])


def _rmsnorm_bwd(du, x, g, dres, name):
    L, D = x.shape

    def body(du_ref, x_ref, g_ref, dres_ref, dx_ref, dxb_ref, dg_ref):
        i = pl.program_id(0)
        xv, duv = x_ref[...], du_ref[...]
        r = lax.rsqrt(jnp.mean(xv * xv, axis=1, keepdims=True) + EPS)
        t = duv * g_ref[...]
        dx = dres_ref[...] + r * t - xv * (r * r * r) * jnp.mean(t * xv, axis=1, keepdims=True)
        dx_ref[...] = dx
        dxb_ref[...] = dx.astype(BF16)
        dg = jnp.sum(duv * xv * r, axis=0, keepdims=True)

        @pl.when(i == 0)
        def _():
            dg_ref[...] = dg

        @pl.when(i != 0)
        def _():
            dg_ref[...] += dg

    tr = min(ROW_TILE, L)
    row = pl.BlockSpec((tr, D), lambda i: (i, 0))
    vec = pl.BlockSpec((1, D), lambda i: (0, 0))
    return pl.pallas_call(
        body, name=name, grid=(L // tr,),
        in_specs=[row, row, vec, row], out_specs=[row, row, vec],
        out_shape=[jax.ShapeDtypeStruct((L, D), F32), jax.ShapeDtypeStruct((L, D), BF16), jax.ShapeDtypeStruct((1, D), F32)],
        compiler_params=pltpu.CompilerParams(dimension_semantics=("arbitrary",), vmem_limit_bytes=VMEM_LIMIT),
    )(du, x, g, dres)


def _loss_head(h, target, g):
    L, D = h.shape

    def body(h_ref, t_ref, g_ref, dh_ref, dhb_ref, loss_ref, dg_ref):
        i = pl.program_id(0)
        hv = h_ref[...]
        r = lax.rsqrt(jnp.mean(hv * hv, axis=1, keepdims=True) + EPS)
        diff = hv * r * g_ref[...] - t_ref[...]
        part = 0.5 * jnp.sum(jnp.mean(diff * diff, axis=1, keepdims=True), axis=0, keepdims=True)
        dy = diff * (1.0 / D)
        t = dy * g_ref[...]
        dh = r * t - hv * (r * r * r) * jnp.mean(t * hv, axis=1, keepdims=True)
        dh_ref[...] = dh
        dhb_ref[...] = dh.astype(BF16)
        dg = jnp.sum(dy * hv * r, axis=0, keepdims=True)

        @pl.when(i == 0)
        def _():
            dg_ref[...] = dg
            loss_ref[...] = jnp.broadcast_to(part, (1, 128))

        @pl.when(i != 0)
        def _():
            dg_ref[...] += dg
            loss_ref[...] += jnp.broadcast_to(part, (1, 128))

    tr = min(ROW_TILE, L)
    row = pl.BlockSpec((tr, D), lambda i: (i, 0))
    vec = pl.BlockSpec((1, D), lambda i: (0, 0))
    return pl.pallas_call(
        body, name="loss_head", grid=(L // tr,),
        in_specs=[row, row, vec], out_specs=[row, row, pl.BlockSpec((1, 128), lambda i: (0, 0)), vec],
        out_shape=[jax.ShapeDtypeStruct((L, D), F32), jax.ShapeDtypeStruct((L, D), BF16), jax.ShapeDtypeStruct((1, 128), F32),
                   jax.ShapeDtypeStruct((1, D), F32)],
        compiler_params=pltpu.CompilerParams(dimension_semantics=("arbitrary",), vmem_limit_bytes=VMEM_LIMIT),
    )(h, target, g)


FF_TILE = 1408


FF_COLS = 512


def _ffn_in(u, wg, wu, tm, side):
    L, D = u.shape
    F = wg.shape[1]
    tm = min(tm, L)
    s_in, s_out = len(side.operands), len(side.out_shapes)
    grid = (L // tm, F // FF_COLS)

    def body(*refs):
        u_ref, wg_ref, wu_ref = refs[:3]
        g_ref, up_ref, act_ref = refs[3 + s_in:6 + s_in]
        side_refs = (refs[3:3 + s_in], refs[6 + s_in:6 + s_in + s_out], refs[6 + s_in + s_out:])
        pl.when(_grid_edge(grid, first=True))(lambda: side.start(*side_refs))
        g = jnp.dot(u_ref[...], wg_ref[...], preferred_element_type=F32)
        v = jnp.dot(u_ref[...], wu_ref[...], preferred_element_type=F32)
        s = _sigmoid(g)
        g_ref[...] = (v * (s * (1.0 + g * (1.0 - s)))).astype(BF16)
        up_ref[...] = (g * s).astype(BF16)
        act_ref[...] = (g * s * v).astype(BF16)
        pl.when(_grid_edge(grid, first=False))(lambda: side.finish(*side_refs))

    w_spec = pl.BlockSpec((D, FF_COLS), lambda i, j: (0, j))
    o_spec = pl.BlockSpec((tm, FF_COLS), lambda i, j: (i, j))
    out = jax.ShapeDtypeStruct((L, F), BF16)
    res = pl.pallas_call(
        body, name="ffn_gate_up", grid=grid,
        in_specs=[pl.BlockSpec((tm, D), lambda i, j: (i, 0)), w_spec, w_spec] + [ANY] * s_in,
        out_specs=[o_spec] * 3 + [ANY] * s_out, out_shape=[out] * 3 + list(side.out_shapes),
        scratch_shapes=list(side.sems),
        compiler_params=pltpu.CompilerParams(dimension_semantics=("arbitrary", "arbitrary"), vmem_limit_bytes=VMEM_LIMIT),
    )(u, wg, wu, *side.operands)
    return res[0], res[1], res[2], res[3:]


def _ffn_back(dh, w_down, gate_factor, up_factor, tm):
    L, D = dh.shape
    F = w_down.shape[0]
    tm = min(tm, L)

    def body(dh_ref, w_ref, g_ref, u_ref, dg_ref, du_ref):
        d = lax.dot_general(dh_ref[...], w_ref[...], (((1,), (1,)), ((), ())), preferred_element_type=F32)
        dg_ref[...] = (d * g_ref[...].astype(F32)).astype(BF16)
        du_ref[...] = (d * u_ref[...].astype(F32)).astype(BF16)

    blk = pl.BlockSpec((tm, FF_COLS), lambda i, j: (i, j))
    out = jax.ShapeDtypeStruct((L, F), BF16)
    return pl.pallas_call(
        body, name="ffn_back", grid=(L // tm, F // FF_COLS),
        in_specs=[pl.BlockSpec((tm, D), lambda i, j: (i, 0)), pl.BlockSpec((FF_COLS, D), lambda i, j: (j, 0)), blk, blk],
        out_specs=[blk, blk], out_shape=[out, out],
        compiler_params=pltpu.CompilerParams(dimension_semantics=("parallel", "parallel"), vmem_limit_bytes=VMEM_LIMIT),
    )(dh, w_down, gate_factor, up_factor)


def _adamw(w, g, m, v, rows, name):
    R, C = w.shape
    assert R % rows == 0, (name, R, rows)

    def body(w_ref, g_ref, m_ref, v_ref, d_ref, nm_ref, nv_ref):
        gv = g_ref[...]
        mn = ADAM_B1 * m_ref[...] + (1.0 - ADAM_B1) * gv
        vn = ADAM_B2 * v_ref[...] + (1.0 - ADAM_B2) * (gv * gv)
        m_hat = mn / (1.0 - ADAM_B1 ** ADAM_STEP)
        v_hat = vn / (1.0 - ADAM_B2 ** ADAM_STEP)
        d_ref[...] = -ADAM_LR * (m_hat / (jnp.sqrt(v_hat) + ADAM_EPS) + ADAM_WD * w_ref[...])
        nm_ref[...] = mn
        nv_ref[...] = vn

    blk = pl.BlockSpec((rows, C), lambda i: (i, 0))
    out = jax.ShapeDtypeStruct((R, C), F32)
    return pl.pallas_call(
        body, name=name, grid=(R // rows,), in_specs=[blk] * 4, out_specs=[blk] * 3, out_shape=[out] * 3,
        compiler_params=pltpu.CompilerParams(dimension_semantics=("parallel",), vmem_limit_bytes=VMEM_LIMIT),
    )(w, g, m, v)


ANY = pl.BlockSpec(memory_space=pl.ANY)
N_DEV = 8


def _allgather_small(p, name):
    R, C = p.shape

    def body(p_ref, out_ref, send_sems, recv_sems, local_sem):
        x, y, c = lax.axis_index("x"), lax.axis_index("y"), lax.axis_index("c")
        me = 4 * x + 2 * y + c
        mine = pltpu.make_async_copy(p_ref, out_ref.at[me], local_sem)
        mine.start()

        def peer(d):
            return (x ^ ((d >> 2) & 1), y ^ ((d >> 1) & 1), c ^ (d & 1))

        def copy(d, block):
            return pltpu.make_async_remote_copy(src_ref=p_ref, dst_ref=out_ref.at[block], send_sem=send_sems.at[d - 1],
                                                recv_sem=recv_sems.at[d - 1], device_id=peer(d), device_id_type=MESH)

        sends = [copy(d, me) for d in range(1, N_DEV)]
        for cp in sends:
            cp.start()
        for d in range(1, N_DEV):
            px, py, pc = peer(d)
            copy(d, 4 * px + 2 * py + pc).wait_recv()
        for cp in sends:
            cp.wait_send()
        mine.wait()

    return pl.pallas_call(
        body, name=name, out_shape=jax.ShapeDtypeStruct((N_DEV, R, C), p.dtype),
        in_specs=[pl.BlockSpec(memory_space=pltpu.VMEM)], out_specs=pl.BlockSpec(memory_space=pltpu.VMEM),
        scratch_shapes=[pltpu.SemaphoreType.DMA((N_DEV - 1,)), pltpu.SemaphoreType.DMA((N_DEV - 1,)), pltpu.SemaphoreType.DMA],
    )(p)


def _other_chips(x, y):
    return [(1 - x, y), (x, 1 - y), (1 - x, 1 - y)]


def _gather_side(shards):
    T = len(shards)
    halves = [s.shape[0] // 2 for s in shards]

    def tools(ins, outs, sems):
        send_sems, recv_sems = sems
        x, y, c = lax.axis_index("x"), lax.axis_index("y"), lax.axis_index("c")

        def rows(t, px, py, pc):
            return outs[t].at[2 * px + py, pl.ds(pc * halves[t], halves[t]), :]

        def copy(t, k, block, to, own=False):
            src = ins[t].at[pl.ds(c * halves[t], halves[t]), :] if own else rows(t, *block)
            return pltpu.make_async_remote_copy(src_ref=src, dst_ref=rows(t, *block), send_sem=send_sems.at[t, k],
                                                recv_sem=recv_sems.at[t, k], device_id=to, device_id_type=MESH)

        return (x, y, c), _other_chips(x, y), copy

    def start(ins, outs, sems):
        (x, y, c), chips, copy = tools(ins, outs, sems)
        for t in range(T):
            for j, chip in enumerate(chips):
                copy(t, j, (x, y, c), (*chip, c), own=True).start()

    def finish(ins, outs, sems):
        (x, y, c), chips, copy = tools(ins, outs, sems)
        for t in range(T):
            for j, chip in enumerate(chips):
                copy(t, j, (*chip, c), (x, y, c)).wait_recv()
                copy(t, 3 + j, (*chip, c), (x, y, 1 - c)).start()
        for t in range(T):
            for j, chip in enumerate(chips):
                copy(t, 3 + j, (*chip, 1 - c), (x, y, c)).wait_recv()
        for t in range(T):
            for j, chip in enumerate(chips):
                copy(t, j, (x, y, c), (*chip, c), own=True).wait_send()
                copy(t, 3 + j, (*chip, c), (x, y, 1 - c)).wait_send()

    return _Side(shards, [jax.ShapeDtypeStruct((4,) + s.shape, s.dtype) for s in shards],
                 [pltpu.SemaphoreType.DMA((T, 6)), pltpu.SemaphoreType.DMA((T, 6))], start, finish)


def _with_own_shard(gathered, shard, slot):
    return lax.dynamic_update_slice(gathered, shard[None], (slot,) + (0,) * shard.ndim)


def _pair_side(grads, chip_major=False):
    T = len(grads)

    def copies(ins, outs, sems):
        send_sems, recv_sems = sems
        x, y, c = lax.axis_index("x"), lax.axis_index("y"), lax.axis_index("c")
        theirs = (lambda r: r.at[:, 1 - c]) if chip_major else (lambda r: r.at[1 - c])
        return [pltpu.make_async_remote_copy(src_ref=theirs(ins[t]), dst_ref=outs[t], send_sem=send_sems.at[t],
                                             recv_sem=recv_sems.at[t], device_id=(x, y, 1 - c), device_id_type=MESH)
                for t in range(T)]

    def start(ins, outs, sems):
        for cp in copies(ins, outs, sems):
            cp.start()

    def finish(ins, outs, sems):
        for cp in copies(ins, outs, sems):
            cp.wait()

    return _Side(grads, [jax.ShapeDtypeStruct((4,) + g.shape[2:], g.dtype) for g in grads],
                 [pltpu.SemaphoreType.DMA((T,)), pltpu.SemaphoreType.DMA((T,))], start, finish)


def _scatter_side(parts):
    T = len(parts)

    def tools(ins, outs, sems):
        send_sems, recv_sems = sems
        x, y, c = lax.axis_index("x"), lax.axis_index("y"), lax.axis_index("c")

        def copy(t, j, src_slot, dst_slot, chip):
            return pltpu.make_async_remote_copy(src_ref=ins[t].at[src_slot], dst_ref=outs[t].at[dst_slot],
                                                send_sem=send_sems.at[t, j], recv_sem=recv_sems.at[t, j],
                                                device_id=(*chip, c), device_id_type=MESH)

        return 2 * x + y, _other_chips(x, y), copy

    def start(ins, outs, sems):
        my_chip, chips, copy = tools(ins, outs, sems)
        for t in range(T):
            for j, (px, py) in enumerate(chips):
                copy(t, j, 2 * px + py, my_chip, (px, py)).start()

    def finish(ins, outs, sems):
        my_chip, chips, copy = tools(ins, outs, sems)
        for t in range(T):
            for j, (px, py) in enumerate(chips):
                copy(t, j, my_chip, 2 * px + py, (px, py)).wait_recv()
        for t in range(T):
            for j, (px, py) in enumerate(chips):
                copy(t, j, 2 * px + py, my_chip, (px, py)).wait_send()

    return _Side(parts, [jax.ShapeDtypeStruct(p.shape, p.dtype) for p in parts],
                 [pltpu.SemaphoreType.DMA((T, 3)), pltpu.SemaphoreType.DMA((T, 3))], start, finish)


def _pair_share(fulls):
    T = len(fulls)

    def body(*refs):
        outs = refs[T:2 * T]
        send_sems, recv_sems = refs[2 * T:]
        x, y, c = lax.axis_index("x"), lax.axis_index("y"), lax.axis_index("c")

        def copy(t, half):
            return pltpu.make_async_remote_copy(src_ref=outs[t].at[half], dst_ref=outs[t].at[half], send_sem=send_sems.at[t],
                                                recv_sem=recv_sems.at[t], device_id=(x, y, 1 - c), device_id_type=MESH)

        for t in range(T):
            copy(t, c).start()
        for t in range(T):
            copy(t, 1 - c).wait_recv()
        for t in range(T):
            copy(t, c).wait_send()

    return pl.pallas_call(
        body, name="grad_pair_share", out_shape=[jax.ShapeDtypeStruct(f.shape, f.dtype) for f in fulls],
        in_specs=[ANY] * T, out_specs=[ANY] * T, input_output_aliases={t: t for t in range(T)},
        scratch_shapes=[pltpu.SemaphoreType.DMA((T,)), pltpu.SemaphoreType.DMA((T,))],
    )(*fulls)


def _row_tile(hr):
    return next(r for r in (128, 64, 32, 16, 8) if hr % r == 0)


def _pair_add(g, got, c, name, chip_major=False):
    _, _, hr, C = g.shape
    tr = _row_tile(hr)
    own = (lambda k, i, c_ref: (k, c_ref[0], i, 0)) if chip_major else (lambda k, i, c_ref: (c_ref[0], k, i, 0))

    def body(c_ref, g_ref, r_ref, o_ref):
        o_ref[...] = (g_ref[0].astype(F32) + r_ref[...].astype(F32)).astype(o_ref.dtype)

    return pl.pallas_call(
        body, name=name,
        grid_spec=pltpu.PrefetchScalarGridSpec(
            num_scalar_prefetch=1, grid=(4, hr // tr),
            in_specs=[pl.BlockSpec((1, 1, tr, C), own),
                      pl.BlockSpec((1, tr, C), lambda k, i, c_ref: (k, i, 0))],
            out_specs=pl.BlockSpec((1, tr, C), lambda k, i, c_ref: (k, i, 0))),
        out_shape=jax.ShapeDtypeStruct(got.shape, g.dtype),
        compiler_params=pltpu.CompilerParams(dimension_semantics=("parallel", "parallel"), vmem_limit_bytes=VMEM_LIMIT),
    )(jnp.reshape(c, (1,)).astype(jnp.int32), g, got)


def _chip_sum(parts, slots, chip, c, name):
    _, hr, C = parts.shape
    tr = _row_tile(hr)

    def body(chip_ref, half_ref, own_ref, a_ref, b_ref, c_ref, o_ref):
        acc = own_ref[...].astype(F32)
        for r in (a_ref, b_ref, c_ref):
            acc = acc + r[...].astype(F32)
        o_ref[...] = acc

    other = lambda j: pl.BlockSpec((1, tr, C), lambda i, chip_ref, half_ref: ((chip_ref[0] + j) % 4, i, 0))
    return pl.pallas_call(
        body, name=name,
        grid_spec=pltpu.PrefetchScalarGridSpec(
            num_scalar_prefetch=2, grid=(hr // tr,),
            in_specs=[pl.BlockSpec((1, tr, C), lambda i, chip_ref, half_ref: (chip_ref[0], i, 0)),
                      other(1), other(2), other(3)],
            out_specs=pl.BlockSpec((1, tr, C), lambda i, chip_ref, half_ref: (half_ref[0], i, 0))),
        out_shape=jax.ShapeDtypeStruct((2, hr, C), F32),
        compiler_params=pltpu.CompilerParams(dimension_semantics=("parallel",), vmem_limit_bytes=VMEM_LIMIT),
    )(jnp.reshape(chip, (1,)).astype(jnp.int32), jnp.reshape(c, (1,)).astype(jnp.int32), parts, slots, slots, slots)


def _sum_slots(parts, name):
    n, hr, C = parts.shape
    tr = _row_tile(hr)

    def body(p_ref, o_ref):
        acc = p_ref[0].astype(F32)
        for k in range(1, n):
            acc = acc + p_ref[k].astype(F32)
        o_ref[...] = acc

    return pl.pallas_call(
        body, name=name, grid=(hr // tr,),
        in_specs=[pl.BlockSpec((n, tr, C), lambda i: (0, i, 0))], out_specs=pl.BlockSpec((tr, C), lambda i: (i, 0)),
        out_shape=jax.ShapeDtypeStruct((hr, C), F32),
        compiler_params=pltpu.CompilerParams(dimension_semantics=("parallel",), vmem_limit_bytes=VMEM_LIMIT),
    )(parts)


PACK_ROWS, PACK_COLS = 16, 3072
_PACK = [("norm_mix_g", 0, 1, 2048), ("conv_w", 1, 4, 3072), ("conv_b", 5, 1, 3072), ("dt_bias", 6, 1, 32),
         ("a_log", 7, 1, 32), ("d_skip", 8, 1, 32), ("ssd_norm_g", 9, 1, 2048), ("i_bias", 10, 1, 8), ("f_bias", 11, 1, 8),
         ("mlstm_norm_g", 12, 1, 2048), ("norm_ffn_g", 13, 1, 2048), ("final_norm_g", 14, 1, 2048), ("loss", 15, 1, 1)]
_WEIGHTS = ["norm_mix_g", "w_in", "conv_w", "conv_b", "dt_bias", "a_log", "d_skip", "ssd_norm_g", "i_bias", "f_bias",
            "mlstm_norm_g", "w_out", "norm_ffn_g", "w_gate", "w_up", "w_down", "final_norm_g"]
_BIG = ["w_in", "w_out", "w_gate", "w_up", "w_down"]


def _pack(vals):
    rows = []
    for name, _, r, w in _PACK:
        v = vals.get(name)
        v = jnp.zeros((r, w), F32) if v is None else v.reshape(r, w).astype(F32)
        rows.append(jnp.concatenate([v, jnp.zeros((r, PACK_COLS - w), F32)], axis=1) if w < PACK_COLS else v)
    return jnp.concatenate(rows, axis=0)


def _unpack(p):
    return {name: p[r0:r0 + r, :w] for name, r0, r, w in _PACK}


def kernel(x, norm_mix_g, w_in, conv_w, conv_b, dt_bias, a_log, d_skip, ssd_norm_g, i_bias, f_bias, mlstm_norm_g, w_out, norm_ffn_g, w_gate, w_up, w_down, final_norm_g, loss_target, m_norm_mix_g, m_w_in, m_conv_w, m_conv_b, m_dt_bias, m_a_log, m_d_skip, m_ssd_norm_g, m_i_bias, m_f_bias, m_mlstm_norm_g, m_w_out, m_norm_ffn_g, m_w_gate, m_w_up, m_w_down, m_final_norm_g, v_norm_mix_g, v_w_in, v_conv_w, v_conv_b, v_dt_bias, v_a_log, v_d_skip, v_ssd_norm_g, v_i_bias, v_f_bias, v_mlstm_norm_g, v_w_out, v_norm_ffn_g, v_w_gate, v_w_up, v_w_down, v_final_norm_g):
    weights = dict(norm_mix_g=norm_mix_g, w_in=w_in, conv_w=conv_w, conv_b=conv_b, dt_bias=dt_bias, a_log=a_log, d_skip=d_skip,
                   ssd_norm_g=ssd_norm_g, i_bias=i_bias, f_bias=f_bias, mlstm_norm_g=mlstm_norm_g, w_out=w_out,
                   norm_ffn_g=norm_ffn_g, w_gate=w_gate, w_up=w_up, w_down=w_down, final_norm_g=final_norm_g)
    mom1 = dict(norm_mix_g=m_norm_mix_g, w_in=m_w_in, conv_w=m_conv_w, conv_b=m_conv_b, dt_bias=m_dt_bias, a_log=m_a_log,
                d_skip=m_d_skip, ssd_norm_g=m_ssd_norm_g, i_bias=m_i_bias, f_bias=m_f_bias, mlstm_norm_g=m_mlstm_norm_g,
                w_out=m_w_out, norm_ffn_g=m_norm_ffn_g, w_gate=m_w_gate, w_up=m_w_up, w_down=m_w_down,
                final_norm_g=m_final_norm_g)
    mom2 = dict(norm_mix_g=v_norm_mix_g, w_in=v_w_in, conv_w=v_conv_w, conv_b=v_conv_b, dt_bias=v_dt_bias, a_log=v_a_log,
                d_skip=v_d_skip, ssd_norm_g=v_ssd_norm_g, i_bias=v_i_bias, f_bias=v_f_bias, mlstm_norm_g=v_mlstm_norm_g,
                w_out=v_w_out, norm_ffn_g=v_norm_ffn_g, w_gate=v_w_gate, w_up=v_w_up, w_down=v_w_down,
                final_norm_g=v_final_norm_g)
    xi, yi, ci = lax.axis_index("x"), lax.axis_index("y"), lax.axis_index("c")
    chip = 2 * xi + yi
    xs, tgt = x[0], loss_target[0]

    shards = {n: weights[n][0].astype(BF16) for n in _BIG}
    gathered = lambda n, g: _with_own_shard(g, shards[n], chip)
    by_cols = lambda g, width: jnp.transpose(g, (1, 0, 2)).reshape(D_MODEL, width)
    u1, (g_in,) = _rmsnorm_fwd(xs, norm_mix_g, "norm_mix_fwd", side=_gather_side([shards["w_in"]]))
    g_in = gathered("w_in", g_in)
    W_ssd, W_ml, W_small = _split_w_in([g_in[k] for k in range(4)])
    cw_all = _allgather_small(jnp.concatenate([conv_w[0], jnp.zeros((4, 768), F32)], axis=0), "allgather_conv_w")
    conv_w_full = jnp.concatenate([cw_all[2 * k, :SSD_CONV] for k in range(4)], axis=1)
    cwg, cbg = _conv_to_groups(conv_w_full), _conv_to_groups(conv_b)
    dtb_row, al_row = _small_row(dt_bias, 0), _small_row(a_log, 0)
    ib_row, fb_row = _small_row(i_bias, LANE_I), _small_row(f_bias, LANE_F)
    dskip_lane = jnp.repeat(d_skip, 64, axis=1)
    fng = final_norm_g[None]

    p_ssd, (g_out,) = _matmul(u1, W_ssd, "nn", F32, 1024, 1024, 2048, "proj_ssd", side=_gather_side([shards["w_out"]]))
    p_ml, (g_gate,) = _matmul(u1, W_ml, "nn", F32, 1024, 1024, 2048, "proj_ml", side=_gather_side([shards["w_gate"]]))
    p_small = _matmul(u1, W_small, "nn", F32, 1024, 128, 2048, "proj_small")
    y_mix, ssd_st, ssd_conv = _ssd_forward(p_ssd, p_small, cwg, cbg, dtb_row, al_row, dskip_lane, ssd_norm_g)
    y_mix, ml_c, ml_nm = _mlstm_forward(p_ml, p_small, ib_row, fb_row, mlstm_norm_g, y_mix)
    W_out = gathered("w_out", g_out).reshape(2 * SSD_WIDTH, D_MODEL)
    h1, (g_up,) = _matmul(y_mix, W_out, "nn", F32, 1024, 1024, 2048, "out_proj", addend=xs,
                          side=_gather_side([shards["w_up"]]))
    u2 = _rmsnorm_fwd(h1, norm_ffn_g, "norm_ffn_fwd")
    W_gate = by_cols(gathered("w_gate", g_gate), D_FF)
    W_up = by_cols(gathered("w_up", g_up), D_FF)
    gate, up, act, (g_down,) = _ffn_in(u2, W_gate, W_up, 1024, _gather_side([shards["w_down"]]))
    W_down = gathered("w_down", g_down).reshape(D_FF, D_MODEL)
    h2 = _matmul(act, W_down, "nn", F32, 1024, 1024, 2 * FF_TILE, "ffn_down", addend=h1)
    dh2, dh2_b, loss_row, d_fng = _loss_head(h2, tgt, fng)

    dW_down = _matmul(act, dh2_b, "tn", BF16, FF_TILE, 1024, 2048, "dw_down", layout="rows4")
    d_gate, d_up = _ffn_back(dh2_b, W_down, gate, up, 1024)
    du2 = _matmul(d_gate, W_gate, "nt", F32, 1024, 1024, 2 * FF_TILE, "du2_gate")
    du2 = _matmul(d_up, W_up, "nt", F32, 1024, 1024, 2 * FF_TILE, "du2_up", addend=du2)
    dW_gate = _matmul(u2, d_gate, "tn", BF16, 1024, FF_TILE, 2048, "dw_gate", layout="cols4")
    dW_up = _matmul(u2, d_up, "tn", BF16, 1024, FF_TILE, 2048, "dw_up", layout="cols4")
    dh1, dh1_b, d_ffn_g = _rmsnorm_bwd(du2, h1, norm_ffn_g, dh2, "norm_ffn_bwd")
    dW_out, ffn_got = _matmul(y_mix, dh1_b, "tn", BF16, 1024, 1024, 2048, "dw_out", layout="rows4",
                              side=_pair_side([dW_gate, dW_up, dW_down]))
    dy_mix, out_got = _matmul(dh1_b, W_out, "nt", F32, 1024, 1024, 2048, "d_mix", side=_pair_side([dW_out]))
    early = ["w_out", "w_gate", "w_up", "w_down"]
    early_g = [dW_out, dW_gate, dW_up, dW_down]
    early_got = list(out_got) + list(ffn_got)
    parts = {n: _pair_add(g, r, ci, "grad_pair_add_" + n) for g, r, n in zip(early_g, early_got, early)}
    (d_ssd, d_small, d_cw, d_cb, d_dtb, d_alog, d_dskip, d_sng), early_slots = _ssd_backward(
        p_ssd, p_small, cwg, cbg, dtb_row, al_row, dskip_lane, ssd_norm_g, ssd_st, dy_mix, ssd_conv,
        side=_scatter_side([parts[n] for n in early]))
    slots = dict(zip(early, early_slots))
    d_ml, d_small, d_ib, d_fb, d_mng = _mlstm_backward(p_ml, p_small, ib_row, fb_row, mlstm_norm_g, ml_c, ml_nm, dy_mix, d_small)
    dW_ssd = _matmul(u1, d_ssd, "tn", BF16, 1024, 1280, 2048, "dw_ssd")
    dW_ml = _matmul(u1, d_ml, "tn", BF16, 1024, 1024, 2048, "dw_ml")
    dW_small = _matmul(u1, d_small, "tn", BF16, 1024, 128, 2048, "dw_small")
    g_w_in = jnp.stack(_merge_w_in(dW_ssd, dW_ml, dW_small)).reshape(4, 2, D_MODEL // 2, SHARD_IN)
    du1, (got_in,) = _matmul(d_ssd, W_ssd, "nt", F32, 1024, 1024, 2560, "du1_ssd",
                             side=_pair_side([g_w_in], chip_major=True))
    parts["w_in"] = _pair_add(g_w_in, got_in, ci, "grad_pair_add_w_in", chip_major=True)
    du1, (slots["w_in"],) = _matmul(d_ml, W_ml, "nt", F32, 1024, 1024, 3072, "du1_ml", addend=du1,
                                    side=_scatter_side([parts["w_in"]]))
    du1 = _matmul(d_small, W_small, "nt", F32, 1024, 1024, 128, "du1_small", addend=du1)
    grad_x, _, d_mix_g = _rmsnorm_bwd(du1, xs, norm_mix_g, dh1, "norm_mix_bwd")

    fulls = _pair_share([_chip_sum(parts[n], slots[n], chip, ci, "grad_chip_sum_" + n) for n in _BIG])
    grads = {n: f.reshape(weights[n].shape[1:]) for n, f in zip(_BIG, fulls)}

    small = _pack(dict(norm_mix_g=d_mix_g, conv_w=_conv_from_groups(d_cw), conv_b=_conv_from_groups(d_cb),
                       dt_bias=d_dtb[:, 0:32], a_log=d_alog[:, 0:32], d_skip=d_dskip.reshape(SSD_HEADS, 64).sum(axis=1),
                       ssd_norm_g=d_sng, i_bias=d_ib[:, LANE_I:LANE_I + 8], f_bias=d_fb[:, LANE_F:LANE_F + 8],
                       mlstm_norm_g=d_mng, norm_ffn_g=d_ffn_g, final_norm_g=d_fng, loss=loss_row[:, 0:1]))
    total = _sum_slots(_allgather_small(small, "allgather_small_grads"), "small_grad_sum")
    small_w = {n: weights[n] for n in _WEIGHTS if n not in _BIG and n != "conv_w"}
    sd, sm, sv = _adamw(_pack(small_w), total, _pack({n: mom1[n] for n in small_w}), _pack({n: mom2[n] for n in small_w}),
                        PACK_ROWS, "adamw_small")
    tot, sd, sm, sv = _unpack(total), _unpack(sd), _unpack(sm), _unpack(sv)
    for n in small_w:
        grads[n] = tot[n].reshape(weights[n].shape)
    grads["conv_w"] = lax.dynamic_slice_in_dim(tot["conv_w"], chip * 768, 768, axis=1)
    loss = tot["loss"][0, 0]

    delta, new_m, new_v = {}, {}, {}
    for n in _BIG + ["conv_w"]:
        w2 = weights[n][0]
        d, nm, nv = _adamw(w2, grads[n], mom1[n][0], mom2[n][0], _row_tile(w2.shape[0]) if n != "conv_w" else SSD_CONV,
                           "adamw_" + n)
        delta[n], new_m[n], new_v[n] = d[None], nm[None], nv[None]
        grads[n] = grads[n][None]
    for n in small_w:
        delta[n], new_m[n], new_v[n] = (t[n].reshape(weights[n].shape) for t in (sd, sm, sv))
    return (loss, grad_x[None], *[grads[n] for n in _WEIGHTS], *[delta[n] for n in _WEIGHTS],
            *[new_m[n] for n in _WEIGHTS], *[new_v[n] for n in _WEIGHTS])
```

```python
import functools

import jax
import jax.numpy as jnp
import numpy as np
from jax import lax
from jax.experimental import pallas as pl
from jax.experimental.pallas import tpu as pltpu

F32 = jnp.float32
BF16 = jnp.bfloat16

D_MODEL = 2048
SSD_HEADS = 32
SSD_GROUPS = 4
SSD_STATE = 128
SSD_WIDTH = 2048
SSD_CONV = 4
SSD_GROUP_COLS = 1280
SSD_XBC_COLS = 768
ML_HEADS = 8
ML_HEAD_COLS = 768
ML_DK = 128
ML_DV = 256
CHUNK = 128
SMALL_COLS = 128
LANE_I = 32
LANE_F = 40
D_FF = 5632
GATE_SOFTCAP = 15.0
EPS = 1e-6
ADAM_LR, ADAM_B1, ADAM_B2, ADAM_EPS, ADAM_WD, ADAM_STEP = 0.001, 0.9, 0.999, 1e-8, 0.01, 10
MESH = pl.DeviceIdType.MESH
VMEM_LIMIT = 56 * 1024 * 1024


def _dg(a, b, ca, cb):
    return lax.dot_general(a.astype(BF16), b.astype(BF16), (((ca,), (cb,)), ((), ())), preferred_element_type=F32)


@jax.custom_vjp
def _mm(a, b):
    return _dg(a, b, 1, 0)


_mm.defvjp(lambda a, b: (_dg(a, b, 1, 0), (a, b)),
           lambda r, g: (_dg(g, r[1], 1, 1), _dg(r[0], g, 0, 0)))


@jax.custom_vjp
def _mm_nt(a, b):
    return _dg(a, b, 1, 1)


_mm_nt.defvjp(lambda a, b: (_dg(a, b, 1, 1), (a, b)),
              lambda r, g: (_dg(g, r[1], 1, 0), _dg(g, r[0], 0, 0)))


@jax.custom_vjp
def _mm_tn(a, b):
    return _dg(a, b, 0, 0)


_mm_tn.defvjp(lambda a, b: (_dg(a, b, 0, 0), (a, b)),
              lambda r, g: (_dg(r[1], g, 1, 1), _dg(r[0], g, 1, 0)))


def _tri(lower):
    r = lax.broadcasted_iota(jnp.int32, (CHUNK, CHUNK), 0)
    c = lax.broadcasted_iota(jnp.int32, (CHUNK, CHUNK), 1)
    return ((r >= c) if lower else (r <= c)).astype(F32)


def _dg32(t, x):
    return lax.dot_general(t, x, (((1,), (0,)), ((), ())), precision=lax.Precision.HIGHEST, preferred_element_type=F32)


@jax.custom_vjp
def _cumsum(x):
    return _dg32(_tri(True), x)


_cumsum.defvjp(lambda x: (_dg32(_tri(True), x), None), lambda r, g: (_dg32(_tri(False), g),))


_sigmoid = jax.nn.sigmoid


def _silu(x):
    return x * _sigmoid(x)


def _softplus(x):
    return jnp.maximum(x, 0.0) + jnp.log(1.0 + jnp.exp(-jnp.abs(x)))


def _lane_col(m, lane, h):
    return jnp.sum(jnp.where(lane == h, m, 0.0), axis=1, keepdims=True)


SSD_GPS = 4


def _ssd_math(convs, zs, smallb, S_in, dtb_row, alog_row, dskip, ng, g0):
    dt_all = _softplus(smallb + dtb_row)
    a_all = dt_all * (-jnp.exp(alog_row))
    prep = (dt_all, _cumsum(a_all), jnp.sum(a_all, axis=0, keepdims=True))
    ys, S_out = [], []
    for k in range(SSD_GPS):
        y, S = _ssd_group(convs[k][:, 0:512], convs[k][:, 512:640], convs[k][:, 640:768], zs[k], prep, S_in[k],
                          dskip[:, 512 * k:512 * (k + 1)], ng[:, 512 * k:512 * (k + 1)], g0 + k)
        ys.append(y)
        S_out.append(S)
    return tuple(ys), tuple(S_out)


def _ssd_group(cx, cB, cC, z, prep, S_in, dskip, ng, g):
    dt_all, acum_all, alast_all = prep
    lane = lax.broadcasted_iota(jnp.int32, (1, CHUNK), 1)
    row_i = lax.broadcasted_iota(jnp.int32, (CHUNK, CHUNK), 0)
    col_i = lax.broadcasted_iota(jnp.int32, (CHUNK, CHUNK), 1)
    causal = row_i >= col_i
    half = lane < 64
    rhalf = lax.broadcasted_iota(jnp.int32, (CHUNK, 1), 0) < 64
    xs, Bm, Cm = _silu(cx), _silu(cB), _silu(cC)
    cb = _mm_nt(Cm, Bm)

    def lmat(ac):
        acb = jnp.broadcast_to(ac, (CHUNK, CHUNK))
        return jnp.exp(jnp.where(causal, acb - acb.T, -jnp.inf))

    ys, S_out = [], []
    for j in range(4):
        h0 = 8 * g + 2 * j
        ac0, ac1 = _lane_col(acum_all, lane, h0), _lane_col(acum_all, lane, h0 + 1)
        dt0, dt1 = _lane_col(dt_all, lane, h0), _lane_col(dt_all, lane, h0 + 1)
        al0, al1 = _lane_col(alast_all, lane, h0), _lane_col(alast_all, lane, h0 + 1)
        Xp = xs[:, 128 * j:128 * (j + 1)]
        Xd = Xp * jnp.where(half, dt0, dt1)
        ac_sel = jnp.where(half, ac0, ac1)
        al_sel = jnp.where(half, al0, al1)
        Yd = jnp.where(half, _mm(cb * lmat(ac0), Xd), _mm(cb * lmat(ac1), Xd))
        Yoff = _mm_nt(Cm, S_in[j]) * jnp.exp(ac_sel)
        ys.append(Yd + Yoff + dskip[:, 128 * j:128 * (j + 1)] * Xp)
        S_new = _mm_tn(Xd * jnp.exp(al_sel - ac_sel), Bm)
        S_out.append(S_in[j] * jnp.exp(jnp.where(rhalf, al0, al1)) + S_new)
    y = jnp.concatenate(ys, axis=1)
    y = y * _silu(z)
    y = y * lax.rsqrt(jnp.mean(y * y, axis=1, keepdims=True) + EPS) * ng
    return y, tuple(S_out)


def _ssd_window(blk_ref, halo_ref, ext, first, k):
    cols = slice(SSD_GROUP_COLS * k + 512, SSD_GROUP_COLS * (k + 1))
    halo = halo_ref[:, cols]
    ext[k, 0:8, :] = jnp.where(first, jnp.zeros_like(halo), halo)
    ext[k, 8:8 + CHUNK, :] = blk_ref[:, cols]


def _ssd_conv(blk_ref, halo_ref, cw_ref, cb_ref, ext, first, k):
    _ssd_window(blk_ref, halo_ref, ext, first, k)
    conv = jnp.broadcast_to(cb_ref[k], (CHUNK, SSD_XBC_COLS))
    for tap in range(SSD_CONV):
        conv = conv + cw_ref[k, tap:tap + 1, :] * ext[k, pl.ds(5 + tap, CHUNK), :]
    return conv


def _ssd_specs(nc, rev):
    cc = (lambda c: nc - 1 - c) if rev else (lambda c: c)
    width = SSD_GPS * SSD_GROUP_COLS
    return [
        pl.BlockSpec((CHUNK, width), lambda c, g: (cc(c), g)),
        pl.BlockSpec((8, width), lambda c, g: (jnp.maximum(cc(c) * (CHUNK // 8) - 1, 0), g)),
        pl.BlockSpec((CHUNK, SMALL_COLS), lambda c, g: (cc(c), 0)),
        pl.BlockSpec((SSD_GPS, SSD_CONV, SSD_XBC_COLS), lambda c, g: (g, 0, 0)),
        pl.BlockSpec((SSD_GPS, 1, SSD_XBC_COLS), lambda c, g: (g, 0, 0)),
        pl.BlockSpec((1, SMALL_COLS), lambda c, g: (0, 0)),
        pl.BlockSpec((1, SMALL_COLS), lambda c, g: (0, 0)),
        pl.BlockSpec((1, SSD_GPS * 512), lambda c, g: (0, g)),
        pl.BlockSpec((1, SSD_GPS * 512), lambda c, g: (0, g)),
    ]


def _ssd_forward(proj, small, cw, cb, dtb, alog, dskip, ng):
    L = proj.shape[0]
    nc = L // CHUNK

    def body(blk_ref, halo_ref, small_ref, cw_ref, cb_ref, dtb_ref, alog_ref, dskip_ref, ng_ref, y_ref, st_ref, conv_ref,
             carry, ext):
        c, g0 = pl.program_id(0), pl.program_id(1) * SSD_GPS

        @pl.when(c == 0)
        def _():
            for k in range(SSD_GPS):
                carry[g0 + k] = jnp.zeros((4, CHUNK, CHUNK), F32)

        convs = [_ssd_conv(blk_ref, halo_ref, cw_ref, cb_ref, ext, c == 0, k) for k in range(SSD_GPS)]
        for k in range(SSD_GPS):
            conv_ref[:, SSD_XBC_COLS * k:SSD_XBC_COLS * (k + 1)] = convs[k]
        zs = [blk_ref[:, SSD_GROUP_COLS * k:SSD_GROUP_COLS * k + 512] for k in range(SSD_GPS)]
        S_in = tuple(tuple(carry[g0 + k, j] for j in range(4)) for k in range(SSD_GPS))
        for k in range(SSD_GPS):
            st_ref[0, k] = carry[g0 + k]
        ys, S_out = _ssd_math(convs, zs, small_ref[...], S_in, dtb_ref[...], alog_ref[...], dskip_ref[...], ng_ref[...], g0)
        for k in range(SSD_GPS):
            y_ref[:, 512 * k:512 * (k + 1)] = ys[k].astype(BF16)
            for j in range(4):
                carry[g0 + k, j] = S_out[k][j]

    return pl.pallas_call(
        body, name="ssd_fwd", grid=(nc, SSD_GROUPS // SSD_GPS),
        in_specs=_ssd_specs(nc, False),
        out_specs=[pl.BlockSpec((CHUNK, SSD_GPS * 512), lambda c, g: (c, g)),
                   pl.BlockSpec((1, SSD_GPS, 4, CHUNK, CHUNK), lambda c, g: (c, g, 0, 0, 0)),
                   pl.BlockSpec((CHUNK, SSD_GPS * SSD_XBC_COLS), lambda c, g: (c, g))],
        out_shape=[jax.ShapeDtypeStruct((L, 2 * SSD_WIDTH), BF16),
                   jax.ShapeDtypeStruct((nc, SSD_GROUPS, 4, CHUNK, CHUNK), F32),
                   jax.ShapeDtypeStruct((L, SSD_GROUPS * SSD_XBC_COLS), F32)],
        scratch_shapes=[pltpu.VMEM((SSD_GROUPS, 4, CHUNK, CHUNK), F32), pltpu.VMEM((SSD_GPS, 8 + CHUNK, SSD_XBC_COLS), F32)],
        compiler_params=pltpu.CompilerParams(dimension_semantics=("arbitrary", "arbitrary"), vmem_limit_bytes=VMEM_LIMIT),
    )(proj, proj, small, cw, cb, dtb, alog, dskip, ng)


def _ssd_backward(proj, small, cw, cb, dtb, alog, dskip, ng, states, dy, conv, side=None):
    L = proj.shape[0]
    nc = L // CHUNK
    s_in, s_out = (len(side.operands), len(side.out_shapes)) if side is not None else (0, 0)
    grid = (nc, SSD_GROUPS // SSD_GPS)
    n_in = 12

    def body(*refs):
        (blk_ref, halo_ref, small_ref, cw_ref, cb_ref, dtb_ref, alog_ref, dskip_ref, ng_ref, st_ref, dy_ref,
         conv_ref) = refs[:n_in]
        o0 = n_in + s_in
        dproj_ref, dsmall_ref, dcw_ref, dcb_ref, ddtb_ref, dalog_ref, ddskip_ref, dng_ref = refs[o0:o0 + 8]
        dcarry, nxt, ext, dext = refs[o0 + 8 + s_out:o0 + 12 + s_out]
        side_refs = (refs[n_in:o0], refs[o0 + 8:o0 + 8 + s_out], refs[o0 + 12 + s_out:])
        s, gb = pl.program_id(0), pl.program_id(1)
        g0 = gb * SSD_GPS
        c = nc - 1 - s
        if side is not None:
            pl.when(_grid_edge(grid, first=True))(lambda: side.start(*side_refs))

        @pl.when(s == 0)
        def _():
            for k in range(SSD_GPS):
                dcarry[g0 + k] = jnp.zeros((4, CHUNK, CHUNK), F32)
                nxt[g0 + k] = jnp.zeros((8, SSD_XBC_COLS), F32)
                dcw_ref[g0 + k] = jnp.zeros((SSD_CONV, SSD_XBC_COLS), F32)
                dcb_ref[g0 + k] = jnp.zeros((1, SSD_XBC_COLS), F32)
                ddskip_ref[g0 + k] = jnp.zeros((1, 512), F32)
                dng_ref[g0 + k] = jnp.zeros((1, 512), F32)

        @pl.when((s == 0) & (gb == 0))
        def _():
            ddtb_ref[...] = jnp.zeros((1, SMALL_COLS), F32)
            dalog_ref[...] = jnp.zeros((1, SMALL_COLS), F32)

        for k in range(SSD_GPS):
            _ssd_window(blk_ref, halo_ref, ext, c == 0, k)
        convs = [conv_ref[:, SSD_XBC_COLS * k:SSD_XBC_COLS * (k + 1)] for k in range(SSD_GPS)]
        zs = [blk_ref[:, SSD_GROUP_COLS * k:SSD_GROUP_COLS * k + 512] for k in range(SSD_GPS)]
        S_in = tuple(tuple(st_ref[0, k, j] for j in range(4)) for k in range(SSD_GPS))
        dS_out = tuple(tuple(dcarry[g0 + k, j] for j in range(4)) for k in range(SSD_GPS))
        dys = tuple(dy_ref[:, 512 * k:512 * (k + 1)].astype(F32) for k in range(SSD_GPS))
        _, vjp = jax.vjp(functools.partial(_ssd_math, g0=g0), convs, zs, small_ref[...], S_in, dtb_ref[...], alog_ref[...],
                         dskip_ref[...], ng_ref[...])
        d_convs, d_zs, d_small, dS_in, d_dtb, d_alog, d_dskip, d_ng = vjp((dys, dS_out))
        for k in range(SSD_GPS):
            for j in range(4):
                dcarry[g0 + k, j] = dS_in[k][j]
            dext[k, 0:8, :] = jnp.zeros((8, SSD_XBC_COLS), F32)
            dext[k, 8:8 + CHUNK, :] = d_convs[k]
            dext[k, 8 + CHUNK:16 + CHUNK, :] = nxt[g0 + k]
            nxt[g0 + k] = dext[k, 8:16, :]
            dconv = d_convs[k]
            d_xbc = jnp.zeros((CHUNK, SSD_XBC_COLS), F32)
            for tap in range(SSD_CONV):
                d_xbc = d_xbc + cw_ref[k, tap:tap + 1, :] * dext[k, pl.ds(8 + 3 - tap, CHUNK), :]
                dcw_ref[g0 + k, tap:tap + 1, :] += jnp.sum(dconv * ext[k, pl.ds(5 + tap, CHUNK), :], axis=0, keepdims=True)
            dcb_ref[g0 + k] += jnp.sum(dconv, axis=0, keepdims=True)
            dproj_ref[:, SSD_GROUP_COLS * k:SSD_GROUP_COLS * k + 512] = d_zs[k].astype(BF16)
            dproj_ref[:, SSD_GROUP_COLS * k + 512:SSD_GROUP_COLS * (k + 1)] = d_xbc.astype(BF16)
            ddskip_ref[g0 + k] += d_dskip[:, 512 * k:512 * (k + 1)]
            dng_ref[g0 + k] += d_ng[:, 512 * k:512 * (k + 1)]

        @pl.when(gb == 0)
        def _():
            dsmall_ref[...] = d_small

        @pl.when(gb != 0)
        def _():
            dsmall_ref[...] += d_small

        ddtb_ref[...] += d_dtb
        dalog_ref[...] += d_alog
        if side is not None:
            pl.when(_grid_edge(grid, first=False))(lambda: side.finish(*side_refs))

    whole = lambda shape: pl.BlockSpec(shape, lambda s, g: (0,) * len(shape))
    side_ops = tuple(side.operands) if side is not None else ()
    res = pl.pallas_call(
        body, name="ssd_bwd", grid=grid,
        in_specs=_ssd_specs(nc, True) + [
            pl.BlockSpec((1, SSD_GPS, 4, CHUNK, CHUNK), lambda s, g: (nc - 1 - s, g, 0, 0, 0)),
            pl.BlockSpec((CHUNK, SSD_GPS * 512), lambda s, g: (nc - 1 - s, g)),
            pl.BlockSpec((CHUNK, SSD_GPS * SSD_XBC_COLS), lambda s, g: (nc - 1 - s, g))] + [ANY] * s_in,
        out_specs=[pl.BlockSpec((CHUNK, SSD_GPS * SSD_GROUP_COLS), lambda s, g: (nc - 1 - s, g)),
                   pl.BlockSpec((CHUNK, SMALL_COLS), lambda s, g: (nc - 1 - s, 0)),
                   whole((SSD_GROUPS, SSD_CONV, SSD_XBC_COLS)), whole((SSD_GROUPS, 1, SSD_XBC_COLS)),
                   whole((1, SMALL_COLS)), whole((1, SMALL_COLS)),
                   whole((SSD_GROUPS, 1, 512)), whole((SSD_GROUPS, 1, 512))] + [ANY] * s_out,
        out_shape=[jax.ShapeDtypeStruct((L, SSD_GROUPS * SSD_GROUP_COLS), BF16), jax.ShapeDtypeStruct((L, SMALL_COLS), F32),
                   jax.ShapeDtypeStruct((SSD_GROUPS, SSD_CONV, SSD_XBC_COLS), F32),
                   jax.ShapeDtypeStruct((SSD_GROUPS, 1, SSD_XBC_COLS), F32),
                   jax.ShapeDtypeStruct((1, SMALL_COLS), F32), jax.ShapeDtypeStruct((1, SMALL_COLS), F32),
                   jax.ShapeDtypeStruct((SSD_GROUPS, 1, 512), F32), jax.ShapeDtypeStruct((SSD_GROUPS, 1, 512), F32)]
        + (list(side.out_shapes) if side is not None else []),
        scratch_shapes=[pltpu.VMEM((SSD_GROUPS, 4, CHUNK, CHUNK), F32), pltpu.VMEM((SSD_GROUPS, 8, SSD_XBC_COLS), F32),
                        pltpu.VMEM((SSD_GPS, 8 + CHUNK, SSD_XBC_COLS), F32),
                        pltpu.VMEM((SSD_GPS, 16 + CHUNK, SSD_XBC_COLS), F32)]
        + (list(side.sems) if side is not None else []),
        compiler_params=pltpu.CompilerParams(dimension_semantics=("arbitrary", "arbitrary"), vmem_limit_bytes=VMEM_LIMIT),
    )(proj, proj, small, cw, cb, dtb, alog, dskip, ng, states, dy, conv, *side_ops)
    return res if side is None else (res[:8], res[8:])


ML_HPS = 8


def _mlstm_math(blks, smallb, C_in, n_in, m_in, ib_row, fb_row, ng, h0):
    li_all = GATE_SOFTCAP * jnp.tanh((smallb + ib_row) / GATE_SOFTCAP)
    lf_all = -_softplus(-(GATE_SOFTCAP * jnp.tanh((smallb + fb_row) / GATE_SOFTCAP)))
    prep = (li_all, lf_all, _cumsum(lf_all))
    out = [_mlstm_head(blks[k][:, 0:128], blks[k][:, 128:256], blks[k][:, 256:512], blks[k][:, 512:768], prep,
                       C_in[k], n_in[k], m_in[k], ng[:, ML_DV * k:ML_DV * (k + 1)], h0 + k) for k in range(ML_HPS)]
    return tuple(zip(*out))


def _mlstm_head(q, k, v, o_raw, prep, C_in, n_in, m_in_row, ng, h):
    li_all, lf_all, bcum_all = prep
    lane = lax.broadcasted_iota(jnp.int32, (1, CHUNK), 1)
    row_i = lax.broadcasted_iota(jnp.int32, (CHUNK, CHUNK), 0)
    col_i = lax.broadcasted_iota(jnp.int32, (CHUNK, CHUNK), 1)
    causal = row_i >= col_i
    kk = k * (ML_DK ** -0.5)
    li = _lane_col(li_all, lane, LANE_I + h)
    lf = _lane_col(lf_all, lane, LANE_F + h)
    bc = _lane_col(bcum_all, lane, LANE_F + h)
    b_last = jnp.sum(lf, axis=0, keepdims=True)
    m_in = _lane_col(m_in_row, lane, 0)
    a = b_last - bc + li
    m_loc = jnp.max(a, axis=0, keepdims=True)
    w = jnp.exp(a - m_loc)
    C_loc = _mm_tn(w * v, kk)
    n_loc = jnp.sum(w * kk, axis=0, keepdims=True)
    m_new = jnp.maximum(b_last + m_in, m_loc)
    s_old = jnp.exp(b_last + m_in - m_new)
    s_new = jnp.exp(m_loc - m_new)
    C_out = s_old * C_in + s_new * C_loc
    n_out = s_old * n_in + s_new * n_loc
    bc_b = jnp.broadcast_to(bc, (CHUNK, CHUNK))
    li_b = jnp.broadcast_to(li, (CHUNK, CHUNK))
    D = jnp.where(causal, bc_b - bc_b.T + li_b.T, -jnp.inf)
    m_intra = jnp.max(D, axis=1, keepdims=True)
    inter_log = bc + m_in
    m_t = jnp.maximum(inter_log, m_intra)
    S = _mm_nt(q, kk) * jnp.exp(D - m_t)
    w_inter = jnp.exp(inter_log - m_t)
    num = _mm(S, v) + w_inter * _mm_nt(q, C_in)
    nq = jnp.sum(S, axis=1, keepdims=True) + w_inter * jnp.sum(q * n_in, axis=1, keepdims=True)
    den = jnp.maximum(jnp.abs(nq), jnp.exp(-m_t))
    hh = num / den
    hh = hh * lax.rsqrt(jnp.mean(hh * hh, axis=1, keepdims=True) + EPS)
    hh = hh * ng * _sigmoid(o_raw)
    return hh, C_out, n_out, jnp.broadcast_to(m_new, (1, CHUNK))


def _ml_specs(nc, rev):
    cc = (lambda c: nc - 1 - c) if rev else (lambda c: c)
    return [
        pl.BlockSpec((CHUNK, ML_HPS * ML_HEAD_COLS), lambda c, h: (cc(c), h)),
        pl.BlockSpec((CHUNK, SMALL_COLS), lambda c, h: (cc(c), 0)),
        pl.BlockSpec((1, SMALL_COLS), lambda c, h: (0, 0)),
        pl.BlockSpec((1, SMALL_COLS), lambda c, h: (0, 0)),
        pl.BlockSpec((1, ML_HPS * ML_DV), lambda c, h: (0, h)),
    ]


def _mlstm_forward(proj, small, ib, fb, ng, y_mix):
    L = proj.shape[0]
    nc = L // CHUNK

    def body(blk_ref, small_ref, ib_ref, fb_ref, ng_ref, _, y_ref, cst_ref, nm_ref, c_carry, nm_carry):
        c, h0 = pl.program_id(0), pl.program_id(1) * ML_HPS

        @pl.when(c == 0)
        def _():
            for k in range(ML_HPS):
                c_carry[h0 + k] = jnp.zeros((ML_DV, ML_DK), F32)
                nm_carry[h0 + k] = jnp.zeros((2, ML_DK), F32)

        for k in range(ML_HPS):
            cst_ref[0, k] = c_carry[h0 + k]
            nm_ref[0, k] = nm_carry[h0 + k]
        hh, C_out, n_out, m_out = _mlstm_math(
            [blk_ref[:, ML_HEAD_COLS * k:ML_HEAD_COLS * (k + 1)] for k in range(ML_HPS)], small_ref[...],
            [c_carry[h0 + k] for k in range(ML_HPS)], [nm_carry[h0 + k, 0:1, :] for k in range(ML_HPS)],
            [nm_carry[h0 + k, 1:2, :] for k in range(ML_HPS)], ib_ref[...], fb_ref[...], ng_ref[...], h0)
        for k in range(ML_HPS):
            y_ref[:, ML_DV * k:ML_DV * (k + 1)] = hh[k].astype(BF16)
            c_carry[h0 + k] = C_out[k]
            nm_carry[h0 + k, 0:1, :] = n_out[k]
            nm_carry[h0 + k, 1:2, :] = m_out[k]

    return pl.pallas_call(
        body, name="mlstm_fwd", grid=(nc, ML_HEADS // ML_HPS),
        in_specs=_ml_specs(nc, False) + [pl.BlockSpec(memory_space=pl.ANY)],
        out_specs=[pl.BlockSpec((CHUNK, ML_HPS * ML_DV), lambda c, h: (c, SSD_WIDTH // (ML_HPS * ML_DV) + h)),
                   pl.BlockSpec((1, ML_HPS, ML_DV, ML_DK), lambda c, h: (c, h, 0, 0)),
                   pl.BlockSpec((1, ML_HPS, 2, ML_DK), lambda c, h: (c, h, 0, 0))],
        out_shape=[jax.ShapeDtypeStruct(y_mix.shape, BF16),
                   jax.ShapeDtypeStruct((nc, ML_HEADS, ML_DV, ML_DK), F32),
                   jax.ShapeDtypeStruct((nc, ML_HEADS, 2, ML_DK), F32)],
        input_output_aliases={5: 0},
        scratch_shapes=[pltpu.VMEM((ML_HEADS, ML_DV, ML_DK), F32), pltpu.VMEM((ML_HEADS, 2, ML_DK), F32)],
        compiler_params=pltpu.CompilerParams(dimension_semantics=("arbitrary", "arbitrary"), vmem_limit_bytes=VMEM_LIMIT),
    )(proj, small, ib, fb, ng, y_mix)


def _mlstm_backward(proj, small, ib, fb, ng, cst, nmst, dy, dsmall_in):
    L = proj.shape[0]
    nc = L // CHUNK

    def body(blk_ref, small_ref, ib_ref, fb_ref, ng_ref, cst_ref, nm_ref, dy_ref, dsin_ref,
             dproj_ref, dsmall_ref, dib_ref, dfb_ref, dng_ref, dc_carry, dnm_carry):
        s, hb = pl.program_id(0), pl.program_id(1)
        h0 = hb * ML_HPS

        @pl.when(s == 0)
        def _():
            for k in range(ML_HPS):
                dc_carry[h0 + k] = jnp.zeros((ML_DV, ML_DK), F32)
                dnm_carry[h0 + k] = jnp.zeros((2, ML_DK), F32)
                dng_ref[h0 + k] = jnp.zeros((1, ML_DV), F32)

        @pl.when((s == 0) & (hb == 0))
        def _():
            dib_ref[...] = jnp.zeros((1, SMALL_COLS), F32)
            dfb_ref[...] = jnp.zeros((1, SMALL_COLS), F32)

        heads = range(ML_HPS)
        _, vjp = jax.vjp(functools.partial(_mlstm_math, h0=h0),
                         [blk_ref[:, ML_HEAD_COLS * k:ML_HEAD_COLS * (k + 1)] for k in heads], small_ref[...],
                         [cst_ref[0, k] for k in heads], [nm_ref[0, k, 0:1, :] for k in heads],
                         [nm_ref[0, k, 1:2, :] for k in heads], ib_ref[...], fb_ref[...], ng_ref[...])
        d_blk, d_small, dC, dn, dm, d_ib, d_fb, d_ng = vjp(
            (tuple(dy_ref[:, ML_DV * k:ML_DV * (k + 1)].astype(F32) for k in heads),
             tuple(dc_carry[h0 + k] for k in heads), tuple(dnm_carry[h0 + k, 0:1, :] for k in heads),
             tuple(dnm_carry[h0 + k, 1:2, :] for k in heads)))
        for k in heads:
            dc_carry[h0 + k] = dC[k]
            dnm_carry[h0 + k, 0:1, :] = dn[k]
            dnm_carry[h0 + k, 1:2, :] = dm[k]
            dproj_ref[:, ML_HEAD_COLS * k:ML_HEAD_COLS * (k + 1)] = d_blk[k].astype(BF16)
            dng_ref[h0 + k] += d_ng[:, ML_DV * k:ML_DV * (k + 1)]

        @pl.when(hb == 0)
        def _():
            dsmall_ref[...] = dsin_ref[...] + d_small

        @pl.when(hb != 0)
        def _():
            dsmall_ref[...] += d_small

        dib_ref[...] += d_ib
        dfb_ref[...] += d_fb

    whole = lambda shape: pl.BlockSpec(shape, lambda s, h: (0,) * len(shape))
    return pl.pallas_call(
        body, name="mlstm_bwd", grid=(nc, ML_HEADS // ML_HPS),
        in_specs=_ml_specs(nc, True) + [
            pl.BlockSpec((1, ML_HPS, ML_DV, ML_DK), lambda s, h: (nc - 1 - s, h, 0, 0)),
            pl.BlockSpec((1, ML_HPS, 2, ML_DK), lambda s, h: (nc - 1 - s, h, 0, 0)),
            pl.BlockSpec((CHUNK, ML_HPS * ML_DV), lambda s, h: (nc - 1 - s, SSD_WIDTH // (ML_HPS * ML_DV) + h)),
            pl.BlockSpec((CHUNK, SMALL_COLS), lambda s, h: (nc - 1 - s, 0))],
        out_specs=[pl.BlockSpec((CHUNK, ML_HPS * ML_HEAD_COLS), lambda s, h: (nc - 1 - s, h)),
                   pl.BlockSpec((CHUNK, SMALL_COLS), lambda s, h: (nc - 1 - s, 0)),
                   whole((1, SMALL_COLS)), whole((1, SMALL_COLS)), whole((ML_HEADS, 1, ML_DV))],
        out_shape=[jax.ShapeDtypeStruct((L, ML_HEADS * ML_HEAD_COLS), BF16), jax.ShapeDtypeStruct((L, SMALL_COLS), F32),
                   jax.ShapeDtypeStruct((1, SMALL_COLS), F32), jax.ShapeDtypeStruct((1, SMALL_COLS), F32),
                   jax.ShapeDtypeStruct((ML_HEADS, 1, ML_DV), F32)],
        scratch_shapes=[pltpu.VMEM((ML_HEADS, ML_DV, ML_DK), F32), pltpu.VMEM((ML_HEADS, 2, ML_DK), F32)],
        compiler_params=pltpu.CompilerParams(dimension_semantics=("arbitrary", "arbitrary"), vmem_limit_bytes=VMEM_LIMIT),
    )(proj, small, ib, fb, ng, cst, nmst, dy, dsmall_in)


_OFF_Z, _OFF_X, _OFF_B, _OFF_C, _OFF_DT = 0, 2048, 4096, 4608, 5120
_OFF_Q, _OFF_K, _OFF_V, _OFF_O, _OFF_I, _OFF_F, IN_WIDTH = 5152, 6176, 7200, 9248, 11296, 11304, 11312


def _ssd_segments():
    segs = []
    for g in range(SSD_GROUPS):
        segs += [(_OFF_Z + 512 * g, 512), (_OFF_X + 512 * g, 512), (_OFF_B + 128 * g, 128), (_OFF_C + 128 * g, 128)]
    return segs


def _ml_segments():
    segs = []
    for h in range(ML_HEADS):
        segs += [(_OFF_Q + 128 * h, 128), (_OFF_K + 128 * h, 128), (_OFF_V + 256 * h, 256), (_OFF_O + 256 * h, 256)]
    return segs


_SMALL_SEGMENTS = [(_OFF_DT, 32), (_OFF_I, 8), (_OFF_F, 8)]


SHARD_IN = IN_WIDTH // 4


def _take_cols(slabs, segs):
    parts = []
    for s, n in segs:
        while n > 0:
            k, lo = divmod(s, SHARD_IN)
            m = min(n, SHARD_IN - lo)
            parts.append(slabs[k][:, lo:lo + m])
            s, n = s + m, n - m
    return jnp.concatenate(parts, axis=1)


def _split_w_in(slabs):
    small = _take_cols(slabs, _SMALL_SEGMENTS)
    small = jnp.concatenate([small, jnp.zeros((small.shape[0], SMALL_COLS - small.shape[1]), small.dtype)], axis=1)
    return _take_cols(slabs, _ssd_segments()), _take_cols(slabs, _ml_segments()), small


def _merge_w_in(d_ssd, d_ml, d_small):
    pieces = []
    for arr, segs in ((d_ssd, _ssd_segments()), (d_ml, _ml_segments()), (d_small, _SMALL_SEGMENTS)):
        pos = 0
        for s, n in segs:
            while n > 0:
                lo = s % SHARD_IN
                m = min(n, SHARD_IN - lo)
                pieces.append((s, arr[:, pos:pos + m]))
                s, n, pos = s + m, n - m, pos + m
    pieces.sort(key=lambda t: t[0])
    return [jnp.concatenate([p for s, p in pieces if s // SHARD_IN == k], axis=1) for k in range(4)]


def _conv_to_groups(cw):
    return jnp.stack([jnp.concatenate([cw[:, 512 * g:512 * (g + 1)], cw[:, 2048 + 128 * g:2048 + 128 * (g + 1)],
                                       cw[:, 2560 + 128 * g:2560 + 128 * (g + 1)]], axis=1) for g in range(SSD_GROUPS)])


def _conv_from_groups(d):
    return jnp.concatenate([d[g, :, 0:512] for g in range(SSD_GROUPS)] + [d[g, :, 512:640] for g in range(SSD_GROUPS)]
                           + [d[g, :, 640:768] for g in range(SSD_GROUPS)], axis=1)


def _small_row(vec, lane0):
    n = vec.shape[1]
    return jnp.concatenate([jnp.zeros((1, lane0), F32), vec, jnp.zeros((1, SMALL_COLS - lane0 - n), F32)], axis=1)


class _Side:
    def __init__(self, operands, out_shapes, sems, start, finish):
        self.operands, self.out_shapes, self.sems, self.start, self.finish = operands, out_shapes, sems, start, finish


def _grid_edge(grid, first):
    hit = None
    for axis, n in enumerate(grid):
        here = pl.program_id(axis) == (0 if first else n - 1)
        hit = here if hit is None else hit & here
    return hit


def _matmul(a, b, mode, out_dtype, tm, tn, tk, name, addend=None, layout=None, side=None):
    M, Kd = (a.shape[1], a.shape[0]) if mode == "tn" else a.shape
    N = b.shape[0] if mode == "nt" else b.shape[1]
    if layout == "cols4":
        tm, tn = M // 2, N // 4
    elif layout == "rows4":
        tm = M // 4
    tm, tn, tk = min(tm, M), min(tn, N), min(tk, Kd)
    if mode == "nn":
        a_spec = pl.BlockSpec((tm, tk), lambda i, j, k: (i, k))
        b_spec = pl.BlockSpec((tk, tn), lambda i, j, k: (k, j))
        dims = (((1,), (0,)), ((), ()))
    elif mode == "nt":
        a_spec = pl.BlockSpec((tm, tk), lambda i, j, k: (i, k))
        b_spec = pl.BlockSpec((tn, tk), lambda i, j, k: (j, k))
        dims = (((1,), (1,)), ((), ()))
    else:
        a_spec = pl.BlockSpec((tk, tm), lambda i, j, k: (k, i))
        b_spec = pl.BlockSpec((tk, tn), lambda i, j, k: (k, j))
        dims = (((0,), (0,)), ((), ()))
    assert M % tm == 0 and N % tn == 0 and Kd % tk == 0, (name, M, N, Kd, tm, tn, tk)
    nk = Kd // tk
    has_add = addend is not None
    n_in = 2 + has_add
    s_in, s_out = (len(side.operands), len(side.out_shapes)) if side is not None else (0, 0)
    grid = (M // tm, N // tn, nk)

    def body(*refs):
        a_ref, b_ref = refs[0], refs[1]
        add_ref = refs[2] if has_add else None
        o_ref, acc_ref = refs[n_in + s_in], refs[n_in + s_in + 1 + s_out]
        k = pl.program_id(2)
        side_refs = (refs[n_in:n_in + s_in], refs[n_in + s_in + 1:n_in + s_in + 1 + s_out], refs[n_in + s_in + 2 + s_out:])
        if side is not None:
            pl.when(_grid_edge(grid, first=True))(lambda: side.start(*side_refs))
        part = lax.dot_general(a_ref[...].astype(BF16), b_ref[...].astype(BF16), dims, preferred_element_type=F32)

        @pl.when(k == 0)
        def _():
            acc_ref[...] = part

        @pl.when(k != 0)
        def _():
            acc_ref[...] += part

        @pl.when(k == nk - 1)
        def _():
            r = acc_ref[...]
            if has_add:
                r = r + add_ref[...].astype(F32)
            if layout == "cols4":
                o_ref[0, 0] = r.astype(out_dtype)
            elif layout == "rows4":
                o_ref[0, 0] = r[:tm // 2].astype(out_dtype)
                o_ref[1, 0] = r[tm // 2:].astype(out_dtype)
            else:
                o_ref[...] = r.astype(out_dtype)

        if side is not None:
            pl.when(_grid_edge(grid, first=False))(lambda: side.finish(*side_refs))

    o_spec = pl.BlockSpec((tm, tn), lambda i, j, k: (i, j))
    out_spec, out_dims = o_spec, (M, N)
    if layout == "cols4":
        out_spec, out_dims = pl.BlockSpec((1, 1, tm, tn), lambda i, j, k: (i, j, 0, 0)), (2, 4, tm, tn)
    elif layout == "rows4":
        out_spec, out_dims = pl.BlockSpec((2, 1, tm // 2, tn), lambda i, j, k: (0, i, 0, j)), (2, 4, tm // 2, N)
    out_shape = jax.ShapeDtypeStruct(out_dims, out_dtype)
    semantics = ("parallel", "parallel", "arbitrary")
    if side is None:
        in_specs, operands, out_specs, scratch = [], (), out_spec, []
    else:
        in_specs, operands, scratch = [ANY] * s_in, tuple(side.operands), list(side.sems)
        out_specs, out_shape = [out_spec] + [ANY] * s_out, [out_shape] + list(side.out_shapes)
        semantics = ("arbitrary",) * 3
    res = pl.pallas_call(
        body, name=name, grid=grid,
        in_specs=[a_spec, b_spec] + ([o_spec] if has_add else []) + in_specs, out_specs=out_specs,
        out_shape=out_shape, scratch_shapes=[pltpu.VMEM((tm, tn), F32)] + scratch,
        compiler_params=pltpu.CompilerParams(dimension_semantics=semantics, vmem_limit_bytes=VMEM_LIMIT),
    )(*((a, b) + ((addend,) if has_add else ()) + operands))
    return res if side is None else (res[0], res[1:])


ROW_TILE = 256
NORM_TILE = 512


def _rmsnorm_fwd(x, g, name, side=None):
    L, D = x.shape
    tr = min(NORM_TILE, L)
    s_in, s_out = (len(side.operands), len(side.out_shapes)) if side is not None else (0, 0)
    grid = (L // tr,)

    def body(*refs):
        x_ref, g_ref, u_ref = refs[0], refs[1], refs[2 + s_in]
        side_refs = (refs[2:2 + s_in], refs[3 + s_in:3 + s_in + s_out], refs[3 + s_in + s_out:])
        if side is not None:
            pl.when(_grid_edge(grid, first=True))(lambda: side.start(*side_refs))
        xv = x_ref[...]
        r = lax.rsqrt(jnp.mean(xv * xv, axis=1, keepdims=True) + EPS)
        u_ref[...] = (xv * r * g_ref[...]).astype(BF16)
        if side is not None:
            pl.when(_grid_edge(grid, first=False))(lambda: side.finish(*side_refs))

    row = pl.BlockSpec((tr, D), lambda i: (i, 0))
    res = pl.pallas_call(
        body, name=name, grid=grid,
        in_specs=[row, pl.BlockSpec((1, D), lambda i: (0, 0))] + [ANY] * s_in, out_specs=[row] + [ANY] * s_out,
        out_shape=[jax.ShapeDtypeStruct((L, D), BF16)] + (list(side.out_shapes) if side is not None else []),
        scratch_shapes=list(side.sems) if side is not None else [],
        compiler_params=pltpu.CompilerParams(dimension_semantics=("arbitrary",), vmem_limit_bytes=VMEM_LIMIT),
    )(x, g, *(side.operands if side is not None else ()))
    return res[0] if side is None else (res[0], res[1:])


def _rmsnorm_bwd(du, x, g, dres, name):
    L, D = x.shape

    def body(du_ref, x_ref, g_ref, dres_ref, dx_ref, dxb_ref, dg_ref):
        i = pl.program_id(0)
        xv, duv = x_ref[...], du_ref[...].astype(F32)
        r = lax.rsqrt(jnp.mean(xv * xv, axis=1, keepdims=True) + EPS)
        t = duv * g_ref[...]
        dx = dres_ref[...] + r * t - xv * (r * r * r) * jnp.mean(t * xv, axis=1, keepdims=True)
        dx_ref[...] = dx
        dxb_ref[...] = dx.astype(BF16)
        dg = jnp.sum(duv * xv * r, axis=0, keepdims=True)

        @pl.when(i == 0)
        def _():
            dg_ref[...] = dg

        @pl.when(i != 0)
        def _():
            dg_ref[...] += dg

    row = pl.BlockSpec((ROW_TILE, D), lambda i: (i, 0))
    vec = pl.BlockSpec((1, D), lambda i: (0, 0))
    return pl.pallas_call(
        body, name=name, grid=(L // ROW_TILE,),
        in_specs=[row, row, vec, row], out_specs=[row, row, vec],
        out_shape=[jax.ShapeDtypeStruct((L, D), F32), jax.ShapeDtypeStruct((L, D), BF16), jax.ShapeDtypeStruct((1, D), F32)],
        compiler_params=pltpu.CompilerParams(dimension_semantics=("arbitrary",), vmem_limit_bytes=VMEM_LIMIT),
    )(du, x, g, dres)


def _loss_head(h, target, g):
    L, D = h.shape

    def body(h_ref, t_ref, g_ref, dh_ref, dhb_ref, loss_ref, dg_ref):
        i = pl.program_id(0)
        hv = h_ref[...]
        r = lax.rsqrt(jnp.mean(hv * hv, axis=1, keepdims=True) + EPS)
        diff = hv * r * g_ref[...] - t_ref[...]
        part = 0.5 * jnp.sum(jnp.mean(diff * diff, axis=1, keepdims=True), axis=0, keepdims=True)
        dy = diff * (1.0 / D)
        t = dy * g_ref[...]
        dh = r * t - hv * (r * r * r) * jnp.mean(t * hv, axis=1, keepdims=True)
        dh_ref[...] = dh
        dhb_ref[...] = dh.astype(BF16)
        dg = jnp.sum(dy * hv * r, axis=0, keepdims=True)

        @pl.when(i == 0)
        def _():
            dg_ref[...] = dg
            loss_ref[...] = jnp.broadcast_to(part, (1, 128))

        @pl.when(i != 0)
        def _():
            dg_ref[...] += dg
            loss_ref[...] += jnp.broadcast_to(part, (1, 128))

    row = pl.BlockSpec((ROW_TILE, D), lambda i: (i, 0))
    vec = pl.BlockSpec((1, D), lambda i: (0, 0))
    return pl.pallas_call(
        body, name="loss_head", grid=(L // ROW_TILE,),
        in_specs=[row, row, vec], out_specs=[row, row, pl.BlockSpec((1, 128), lambda i: (0, 0)), vec],
        out_shape=[jax.ShapeDtypeStruct((L, D), F32), jax.ShapeDtypeStruct((L, D), BF16), jax.ShapeDtypeStruct((1, 128), F32),
                   jax.ShapeDtypeStruct((1, D), F32)],
        compiler_params=pltpu.CompilerParams(dimension_semantics=("arbitrary",), vmem_limit_bytes=VMEM_LIMIT),
    )(h, target, g)


FF_TILE = 1408


FF_COLS = 512


def _ffn_in(u, wg, wu, tm, side):
    L, D = u.shape
    F = wg.shape[1]
    tm = min(tm, L)
    s_in, s_out = len(side.operands), len(side.out_shapes)
    grid = (L // tm, F // FF_COLS)

    def body(*refs):
        u_ref, wg_ref, wu_ref = refs[:3]
        g_ref, up_ref, act_ref = refs[3 + s_in:6 + s_in]
        side_refs = (refs[3:3 + s_in], refs[6 + s_in:6 + s_in + s_out], refs[6 + s_in + s_out:])
        pl.when(_grid_edge(grid, first=True))(lambda: side.start(*side_refs))
        g = jnp.dot(u_ref[...], wg_ref[...], preferred_element_type=F32)
        v = jnp.dot(u_ref[...], wu_ref[...], preferred_element_type=F32)
        s = _sigmoid(g)
        g_ref[...] = (v * (s * (1.0 + g * (1.0 - s)))).astype(BF16)
        up_ref[...] = (g * s).astype(BF16)
        act_ref[...] = (g * s * v).astype(BF16)
        pl.when(_grid_edge(grid, first=False))(lambda: side.finish(*side_refs))

    w_spec = pl.BlockSpec((D, FF_COLS), lambda i, j: (0, j))
    o_spec = pl.BlockSpec((tm, FF_COLS), lambda i, j: (i, j))
    out = jax.ShapeDtypeStruct((L, F), BF16)
    res = pl.pallas_call(
        body, name="ffn_gate_up", grid=grid,
        in_specs=[pl.BlockSpec((tm, D), lambda i, j: (i, 0)), w_spec, w_spec] + [ANY] * s_in,
        out_specs=[o_spec] * 3 + [ANY] * s_out, out_shape=[out] * 3 + list(side.out_shapes),
        scratch_shapes=list(side.sems),
        compiler_params=pltpu.CompilerParams(dimension_semantics=("arbitrary", "arbitrary"), vmem_limit_bytes=VMEM_LIMIT),
    )(u, wg, wu, *side.operands)
    return res[0], res[1], res[2], res[3:]


def _ffn_back(dh, w_down, gate_factor, up_factor, tm):
    L, D = dh.shape
    F = w_down.shape[0]
    tm = min(tm, L)

    def body(dh_ref, w_ref, g_ref, u_ref, dg_ref, du_ref):
        d = lax.dot_general(dh_ref[...], w_ref[...], (((1,), (1,)), ((), ())), preferred_element_type=F32)
        dg_ref[...] = (d * g_ref[...].astype(F32)).astype(BF16)
        du_ref[...] = (d * u_ref[...].astype(F32)).astype(BF16)

    blk = pl.BlockSpec((tm, FF_COLS), lambda i, j: (i, j))
    out = jax.ShapeDtypeStruct((L, F), BF16)
    return pl.pallas_call(
        body, name="ffn_back", grid=(L // tm, F // FF_COLS),
        in_specs=[pl.BlockSpec((tm, D), lambda i, j: (i, 0)), pl.BlockSpec((FF_COLS, D), lambda i, j: (j, 0)), blk, blk],
        out_specs=[blk, blk], out_shape=[out, out],
        compiler_params=pltpu.CompilerParams(dimension_semantics=("parallel", "parallel"), vmem_limit_bytes=VMEM_LIMIT),
    )(dh, w_down, gate_factor, up_factor)


def _adamw(w, g, m, v, rows, name):
    R, C = w.shape
    assert R % rows == 0, (name, R, rows)

    def body(w_ref, g_ref, m_ref, v_ref, d_ref, nm_ref, nv_ref):
        gv = g_ref[...]
        mn = ADAM_B1 * m_ref[...] + (1.0 - ADAM_B1) * gv
        vn = ADAM_B2 * v_ref[...] + (1.0 - ADAM_B2) * (gv * gv)
        m_hat = mn / (1.0 - ADAM_B1 ** ADAM_STEP)
        v_hat = vn / (1.0 - ADAM_B2 ** ADAM_STEP)
        d_ref[...] = -ADAM_LR * (m_hat / (jnp.sqrt(v_hat) + ADAM_EPS) + ADAM_WD * w_ref[...])
        nm_ref[...] = mn
        nv_ref[...] = vn

    blk = pl.BlockSpec((rows, C), lambda i: (i, 0))
    out = jax.ShapeDtypeStruct((R, C), F32)
    return pl.pallas_call(
        body, name=name, grid=(R // rows,), in_specs=[blk] * 4, out_specs=[blk] * 3, out_shape=[out] * 3,
        compiler_params=pltpu.CompilerParams(dimension_semantics=("parallel",), vmem_limit_bytes=VMEM_LIMIT),
    )(w, g, m, v)


ANY = pl.BlockSpec(memory_space=pl.ANY)
N_DEV = 8


def _allgather_small(p, name):
    R, C = p.shape

    def body(p_ref, out_ref, send_sems, recv_sems, local_sem):
        x, y, c = lax.axis_index("x"), lax.axis_index("y"), lax.axis_index("c")
        me = 4 * x + 2 * y + c
        mine = pltpu.make_async_copy(p_ref, out_ref.at[me], local_sem)
        mine.start()

        def peer(d):
            return (x ^ ((d >> 2) & 1), y ^ ((d >> 1) & 1), c ^ (d & 1))

        def copy(d, block):
            return pltpu.make_async_remote_copy(src_ref=p_ref, dst_ref=out_ref.at[block], send_sem=send_sems.at[d - 1],
                                                recv_sem=recv_sems.at[d - 1], device_id=peer(d), device_id_type=MESH)

        sends = [copy(d, me) for d in range(1, N_DEV)]
        for cp in sends:
            cp.start()
        for d in range(1, N_DEV):
            px, py, pc = peer(d)
            copy(d, 4 * px + 2 * py + pc).wait_recv()
        for cp in sends:
            cp.wait_send()
        mine.wait()

    return pl.pallas_call(
        body, name=name, out_shape=jax.ShapeDtypeStruct((N_DEV, R, C), p.dtype),
        in_specs=[pl.BlockSpec(memory_space=pltpu.VMEM)], out_specs=pl.BlockSpec(memory_space=pltpu.VMEM),
        scratch_shapes=[pltpu.SemaphoreType.DMA((N_DEV - 1,)), pltpu.SemaphoreType.DMA((N_DEV - 1,)), pltpu.SemaphoreType.DMA],
    )(p)


def _other_chips(x, y):
    return [(1 - x, y), (x, 1 - y), (1 - x, 1 - y)]


def _run_side(side, name):
    s_in, s_out = len(side.operands), len(side.out_shapes)

    def body(*refs):
        parts = (refs[:s_in], refs[s_in:s_in + s_out], refs[s_in + s_out:])
        side.start(*parts)
        side.finish(*parts)

    return pl.pallas_call(body, name=name, out_shape=list(side.out_shapes), in_specs=[ANY] * s_in, out_specs=[ANY] * s_out,
                          scratch_shapes=list(side.sems))(*side.operands)


def _gather_side(shards):
    T = len(shards)
    halves = [s.shape[0] // 2 for s in shards]

    def tools(ins, outs, sems):
        send_sems, recv_sems = sems
        x, y, c = lax.axis_index("x"), lax.axis_index("y"), lax.axis_index("c")

        def rows(t, px, py, pc):
            return outs[t].at[2 * px + py, pl.ds(pc * halves[t], halves[t]), :]

        def copy(t, k, block, to, own=False):
            src = ins[t].at[pl.ds(c * halves[t], halves[t]), :] if own else rows(t, *block)
            return pltpu.make_async_remote_copy(src_ref=src, dst_ref=rows(t, *block), send_sem=send_sems.at[t, k],
                                                recv_sem=recv_sems.at[t, k], device_id=to, device_id_type=MESH)

        return (x, y, c), _other_chips(x, y), copy

    def start(ins, outs, sems):
        (x, y, c), chips, copy = tools(ins, outs, sems)
        for t in range(T):
            for j, chip in enumerate(chips):
                copy(t, j, (x, y, c), (*chip, c), own=True).start()

    def finish(ins, outs, sems):
        (x, y, c), chips, copy = tools(ins, outs, sems)
        for t in range(T):
            for j, chip in enumerate(chips):
                copy(t, j, (*chip, c), (x, y, c)).wait_recv()
                copy(t, 3 + j, (*chip, c), (x, y, 1 - c)).start()
        for t in range(T):
            for j, chip in enumerate(chips):
                copy(t, 3 + j, (*chip, 1 - c), (x, y, c)).wait_recv()
        for t in range(T):
            for j, chip in enumerate(chips):
                copy(t, j, (x, y, c), (*chip, c), own=True).wait_send()
                copy(t, 3 + j, (*chip, c), (x, y, 1 - c)).wait_send()

    return _Side(shards, [jax.ShapeDtypeStruct((4,) + s.shape, s.dtype) for s in shards],
                 [pltpu.SemaphoreType.DMA((T, 6)), pltpu.SemaphoreType.DMA((T, 6))], start, finish)


def _with_own_shard(gathered, shard, slot):
    return lax.dynamic_update_slice(gathered, shard[None], (slot,) + (0,) * shard.ndim)


def _pair_side(grads, chip_major=False):
    T = len(grads)

    def copies(ins, outs, sems):
        send_sems, recv_sems = sems
        x, y, c = lax.axis_index("x"), lax.axis_index("y"), lax.axis_index("c")
        theirs = (lambda r: r.at[:, 1 - c]) if chip_major else (lambda r: r.at[1 - c])
        return [pltpu.make_async_remote_copy(src_ref=theirs(ins[t]), dst_ref=outs[t], send_sem=send_sems.at[t],
                                             recv_sem=recv_sems.at[t], device_id=(x, y, 1 - c), device_id_type=MESH)
                for t in range(T)]

    def start(ins, outs, sems):
        for cp in copies(ins, outs, sems):
            cp.start()

    def finish(ins, outs, sems):
        for cp in copies(ins, outs, sems):
            cp.wait()

    return _Side(grads, [jax.ShapeDtypeStruct((4,) + g.shape[2:], g.dtype) for g in grads],
                 [pltpu.SemaphoreType.DMA((T,)), pltpu.SemaphoreType.DMA((T,))], start, finish)


def _scatter_side(parts):
    T = len(parts)

    def tools(ins, outs, sems):
        send_sems, recv_sems = sems
        x, y, c = lax.axis_index("x"), lax.axis_index("y"), lax.axis_index("c")

        def copy(t, j, src_slot, dst_slot, chip):
            return pltpu.make_async_remote_copy(src_ref=ins[t].at[src_slot], dst_ref=outs[t].at[dst_slot],
                                                send_sem=send_sems.at[t, j], recv_sem=recv_sems.at[t, j],
                                                device_id=(*chip, c), device_id_type=MESH)

        return 2 * x + y, _other_chips(x, y), copy

    def start(ins, outs, sems):
        my_chip, chips, copy = tools(ins, outs, sems)
        for t in range(T):
            for j, (px, py) in enumerate(chips):
                copy(t, j, 2 * px + py, my_chip, (px, py)).start()

    def finish(ins, outs, sems):
        my_chip, chips, copy = tools(ins, outs, sems)
        for t in range(T):
            for j, (px, py) in enumerate(chips):
                copy(t, j, my_chip, 2 * px + py, (px, py)).wait_recv()
        for t in range(T):
            for j, (px, py) in enumerate(chips):
                copy(t, j, 2 * px + py, my_chip, (px, py)).wait_send()

    return _Side(parts, [jax.ShapeDtypeStruct(p.shape, p.dtype) for p in parts],
                 [pltpu.SemaphoreType.DMA((T, 3)), pltpu.SemaphoreType.DMA((T, 3))], start, finish)


def _pair_share(fulls):
    T = len(fulls)

    def body(*refs):
        outs = refs[T:2 * T]
        send_sems, recv_sems = refs[2 * T:]
        x, y, c = lax.axis_index("x"), lax.axis_index("y"), lax.axis_index("c")

        def copy(t, half):
            return pltpu.make_async_remote_copy(src_ref=outs[t].at[half], dst_ref=outs[t].at[half], send_sem=send_sems.at[t],
                                                recv_sem=recv_sems.at[t], device_id=(x, y, 1 - c), device_id_type=MESH)

        for t in range(T):
            copy(t, c).start()
        for t in range(T):
            copy(t, 1 - c).wait_recv()
        for t in range(T):
            copy(t, c).wait_send()

    return pl.pallas_call(
        body, name="grad_pair_share", out_shape=[jax.ShapeDtypeStruct(f.shape, f.dtype) for f in fulls],
        in_specs=[ANY] * T, out_specs=[ANY] * T, input_output_aliases={t: t for t in range(T)},
        scratch_shapes=[pltpu.SemaphoreType.DMA((T,)), pltpu.SemaphoreType.DMA((T,))],
    )(*fulls)


def _row_tile(hr):
    return next(r for r in (128, 64, 32, 16, 8) if hr % r == 0)


def _pair_add(g, got, c, name, chip_major=False):
    _, _, hr, C = g.shape
    tr = _row_tile(hr)
    own = (lambda k, i, c_ref: (k, c_ref[0], i, 0)) if chip_major else (lambda k, i, c_ref: (c_ref[0], k, i, 0))

    def body(c_ref, g_ref, r_ref, o_ref):
        o_ref[...] = (g_ref[0].astype(F32) + r_ref[...].astype(F32)).astype(o_ref.dtype)

    return pl.pallas_call(
        body, name=name,
        grid_spec=pltpu.PrefetchScalarGridSpec(
            num_scalar_prefetch=1, grid=(4, hr // tr),
            in_specs=[pl.BlockSpec((1, 1, tr, C), own),
                      pl.BlockSpec((1, tr, C), lambda k, i, c_ref: (k, i, 0))],
            out_specs=pl.BlockSpec((1, tr, C), lambda k, i, c_ref: (k, i, 0))),
        out_shape=jax.ShapeDtypeStruct(got.shape, g.dtype),
        compiler_params=pltpu.CompilerParams(dimension_semantics=("parallel", "parallel"), vmem_limit_bytes=VMEM_LIMIT),
    )(jnp.reshape(c, (1,)).astype(jnp.int32), g, got)


def _chip_sum(parts, slots, chip, c, name):
    _, hr, C = parts.shape
    tr = _row_tile(hr)

    def body(chip_ref, half_ref, own_ref, a_ref, b_ref, c_ref, o_ref):
        acc = own_ref[...].astype(F32)
        for r in (a_ref, b_ref, c_ref):
            acc = acc + r[...].astype(F32)
        o_ref[...] = acc

    other = lambda j: pl.BlockSpec((1, tr, C), lambda i, chip_ref, half_ref: ((chip_ref[0] + j) % 4, i, 0))
    return pl.pallas_call(
        body, name=name,
        grid_spec=pltpu.PrefetchScalarGridSpec(
            num_scalar_prefetch=2, grid=(hr // tr,),
            in_specs=[pl.BlockSpec((1, tr, C), lambda i, chip_ref, half_ref: (chip_ref[0], i, 0)),
                      other(1), other(2), other(3)],
            out_specs=pl.BlockSpec((1, tr, C), lambda i, chip_ref, half_ref: (half_ref[0], i, 0))),
        out_shape=jax.ShapeDtypeStruct((2, hr, C), F32),
        compiler_params=pltpu.CompilerParams(dimension_semantics=("parallel",), vmem_limit_bytes=VMEM_LIMIT),
    )(jnp.reshape(chip, (1,)).astype(jnp.int32), jnp.reshape(c, (1,)).astype(jnp.int32), parts, slots, slots, slots)


def _sum_slots(parts, name):
    n, hr, C = parts.shape
    tr = _row_tile(hr)

    def body(p_ref, o_ref):
        acc = p_ref[0].astype(F32)
        for k in range(1, n):
            acc = acc + p_ref[k].astype(F32)
        o_ref[...] = acc

    return pl.pallas_call(
        body, name=name, grid=(hr // tr,),
        in_specs=[pl.BlockSpec((n, tr, C), lambda i: (0, i, 0))], out_specs=pl.BlockSpec((tr, C), lambda i: (i, 0)),
        out_shape=jax.ShapeDtypeStruct((hr, C), F32),
        compiler_params=pltpu.CompilerParams(dimension_semantics=("parallel",), vmem_limit_bytes=VMEM_LIMIT),
    )(parts)


PACK_ROWS, PACK_COLS = 16, 3072
_PACK = [("norm_mix_g", 0, 1, 2048), ("conv_w", 1, 4, 3072), ("conv_b", 5, 1, 3072), ("dt_bias", 6, 1, 32),
         ("a_log", 7, 1, 32), ("d_skip", 8, 1, 32), ("ssd_norm_g", 9, 1, 2048), ("i_bias", 10, 1, 8), ("f_bias", 11, 1, 8),
         ("mlstm_norm_g", 12, 1, 2048), ("norm_ffn_g", 13, 1, 2048), ("final_norm_g", 14, 1, 2048), ("loss", 15, 1, 1)]
_WEIGHTS = ["norm_mix_g", "w_in", "conv_w", "conv_b", "dt_bias", "a_log", "d_skip", "ssd_norm_g", "i_bias", "f_bias",
            "mlstm_norm_g", "w_out", "norm_ffn_g", "w_gate", "w_up", "w_down", "final_norm_g"]
_BIG = ["w_in", "w_out", "w_gate", "w_up", "w_down"]


def _pack(vals):
    rows = []
    for name, _, r, w in _PACK:
        v = vals.get(name)
        v = jnp.zeros((r, w), F32) if v is None else v.reshape(r, w).astype(F32)
        rows.append(jnp.concatenate([v, jnp.zeros((r, PACK_COLS - w), F32)], axis=1) if w < PACK_COLS else v)
    return jnp.concatenate(rows, axis=0)


def _unpack(p):
    return {name: p[r0:r0 + r, :w] for name, r0, r, w in _PACK}


def kernel(x, norm_mix_g, w_in, conv_w, conv_b, dt_bias, a_log, d_skip, ssd_norm_g, i_bias, f_bias, mlstm_norm_g, w_out, norm_ffn_g, w_gate, w_up, w_down, final_norm_g, loss_target, m_norm_mix_g, m_w_in, m_conv_w, m_conv_b, m_dt_bias, m_a_log, m_d_skip, m_ssd_norm_g, m_i_bias, m_f_bias, m_mlstm_norm_g, m_w_out, m_norm_ffn_g, m_w_gate, m_w_up, m_w_down, m_final_norm_g, v_norm_mix_g, v_w_in, v_conv_w, v_conv_b, v_dt_bias, v_a_log, v_d_skip, v_ssd_norm_g, v_i_bias, v_f_bias, v_mlstm_norm_g, v_w_out, v_norm_ffn_g, v_w_gate, v_w_up, v_w_down, v_final_norm_g):
    weights = dict(norm_mix_g=norm_mix_g, w_in=w_in, conv_w=conv_w, conv_b=conv_b, dt_bias=dt_bias, a_log=a_log, d_skip=d_skip,
                   ssd_norm_g=ssd_norm_g, i_bias=i_bias, f_bias=f_bias, mlstm_norm_g=mlstm_norm_g, w_out=w_out,
                   norm_ffn_g=norm_ffn_g, w_gate=w_gate, w_up=w_up, w_down=w_down, final_norm_g=final_norm_g)
    mom1 = dict(norm_mix_g=m_norm_mix_g, w_in=m_w_in, conv_w=m_conv_w, conv_b=m_conv_b, dt_bias=m_dt_bias, a_log=m_a_log,
                d_skip=m_d_skip, ssd_norm_g=m_ssd_norm_g, i_bias=m_i_bias, f_bias=m_f_bias, mlstm_norm_g=m_mlstm_norm_g,
                w_out=m_w_out, norm_ffn_g=m_norm_ffn_g, w_gate=m_w_gate, w_up=m_w_up, w_down=m_w_down,
                final_norm_g=m_final_norm_g)
    mom2 = dict(norm_mix_g=v_norm_mix_g, w_in=v_w_in, conv_w=v_conv_w, conv_b=v_conv_b, dt_bias=v_dt_bias, a_log=v_a_log,
                d_skip=v_d_skip, ssd_norm_g=v_ssd_norm_g, i_bias=v_i_bias, f_bias=v_f_bias, mlstm_norm_g=v_mlstm_norm_g,
                w_out=v_w_out, norm_ffn_g=v_norm_ffn_g, w_gate=v_w_gate, w_up=v_w_up, w_down=v_w_down,
                final_norm_g=v_final_norm_g)
    xi, yi, ci = lax.axis_index("x"), lax.axis_index("y"), lax.axis_index("c")
    chip = 2 * xi + yi
    xs, tgt = x[0], loss_target[0]

    shards = {n: weights[n][0].astype(BF16) for n in _BIG}
    gathered = lambda n, g: _with_own_shard(g, shards[n], chip)
    by_cols = lambda g, width: jnp.transpose(g, (1, 0, 2)).reshape(D_MODEL, width)
    u1, (g_in,) = _rmsnorm_fwd(xs, norm_mix_g, "norm_mix_fwd", side=_gather_side([shards["w_in"]]))
    g_in = gathered("w_in", g_in)
    W_ssd, W_ml, W_small = _split_w_in([g_in[k] for k in range(4)])
    cw_all = _allgather_small(jnp.concatenate([conv_w[0], jnp.zeros((4, 768), F32)], axis=0), "allgather_conv_w")
    conv_w_full = jnp.concatenate([cw_all[2 * k, :SSD_CONV] for k in range(4)], axis=1)
    cwg, cbg = _conv_to_groups(conv_w_full), _conv_to_groups(conv_b)
    dtb_row, al_row = _small_row(dt_bias, 0), _small_row(a_log, 0)
    ib_row, fb_row = _small_row(i_bias, LANE_I), _small_row(f_bias, LANE_F)
    dskip_lane = jnp.repeat(d_skip, 64, axis=1)
    fng = final_norm_g[None]

    p_ssd, (g_out,) = _matmul(u1, W_ssd, "nn", F32, 1024, 1024, 2048, "proj_ssd", side=_gather_side([shards["w_out"]]))
    p_ml, (g_gate,) = _matmul(u1, W_ml, "nn", F32, 1024, 1024, 2048, "proj_ml", side=_gather_side([shards["w_gate"]]))
    p_small = _matmul(u1, W_small, "nn", F32, 1024, 128, 2048, "proj_small")
    y_mix, ssd_st, ssd_conv = _ssd_forward(p_ssd, p_small, cwg, cbg, dtb_row, al_row, dskip_lane, ssd_norm_g)
    y_mix, ml_c, ml_nm = _mlstm_forward(p_ml, p_small, ib_row, fb_row, mlstm_norm_g, y_mix)
    W_out = gathered("w_out", g_out).reshape(2 * SSD_WIDTH, D_MODEL)
    h1, (g_up,) = _matmul(y_mix, W_out, "nn", F32, 1024, 1024, 2048, "out_proj", addend=xs,
                          side=_gather_side([shards["w_up"]]))
    u2 = _rmsnorm_fwd(h1, norm_ffn_g, "norm_ffn_fwd")
    W_gate = by_cols(gathered("w_gate", g_gate), D_FF)
    W_up = by_cols(gathered("w_up", g_up), D_FF)
    gate, up, act, (g_down,) = _ffn_in(u2, W_gate, W_up, 1024, _gather_side([shards["w_down"]]))
    W_down = gathered("w_down", g_down).reshape(D_FF, D_MODEL)
    h2 = _matmul(act, W_down, "nn", F32, 1024, 1024, 2 * FF_TILE, "ffn_down", addend=h1)
    dh2, dh2_b, loss_row, d_fng = _loss_head(h2, tgt, fng)

    dW_down = _matmul(act, dh2_b, "tn", BF16, FF_TILE, 1024, 2048, "dw_down", layout="rows4")
    d_gate, d_up = _ffn_back(dh2_b, W_down, gate, up, 1024)
    du2 = _matmul(d_gate, W_gate, "nt", F32, 1024, 1024, 2 * FF_TILE, "du2_gate")
    du2 = _matmul(d_up, W_up, "nt", BF16, 1024, 1024, 2 * FF_TILE, "du2_up", addend=du2)
    dW_gate = _matmul(u2, d_gate, "tn", BF16, 1024, FF_TILE, 2048, "dw_gate", layout="cols4")
    dW_up = _matmul(u2, d_up, "tn", BF16, 1024, FF_TILE, 2048, "dw_up", layout="cols4")
    dh1, dh1_b, d_ffn_g = _rmsnorm_bwd(du2, h1, norm_ffn_g, dh2, "norm_ffn_bwd")
    dW_out, ffn_got = _matmul(y_mix, dh1_b, "tn", BF16, 1024, 1024, 2048, "dw_out", layout="rows4",
                              side=_pair_side([dW_gate, dW_up, dW_down]))
    dy_mix, out_got = _matmul(dh1_b, W_out, "nt", F32, 1024, 1024, 2048, "d_mix", side=_pair_side([dW_out]))
    early = ["w_out", "w_gate", "w_up", "w_down"]
    early_g = [dW_out, dW_gate, dW_up, dW_down]
    early_got = list(out_got) + list(ffn_got)
    parts = {n: _pair_add(g, r, ci, "grad_pair_add_" + n) for g, r, n in zip(early_g, early_got, early)}
    (d_ssd, d_small, d_cw, d_cb, d_dtb, d_alog, d_dskip, d_sng), early_slots = _ssd_backward(
        p_ssd, p_small, cwg, cbg, dtb_row, al_row, dskip_lane, ssd_norm_g, ssd_st, dy_mix, ssd_conv,
        side=_scatter_side([parts[n] for n in early]))
    slots = dict(zip(early, early_slots))
    d_ml, d_small, d_ib, d_fb, d_mng = _mlstm_backward(p_ml, p_small, ib_row, fb_row, mlstm_norm_g, ml_c, ml_nm, dy_mix, d_small)
    dW_ssd = _matmul(u1, d_ssd, "tn", BF16, 1024, 1280, 2048, "dw_ssd")
    dW_ml = _matmul(u1, d_ml, "tn", BF16, 1024, 1024, 2048, "dw_ml")
    dW_small = _matmul(u1, d_small, "tn", BF16, 1024, 128, 2048, "dw_small")
    g_w_in = jnp.stack(_merge_w_in(dW_ssd, dW_ml, dW_small)).reshape(4, 2, D_MODEL // 2, SHARD_IN)
    du1, (got_in,) = _matmul(d_ssd, W_ssd, "nt", F32, 1024, 1024, 2560, "du1_ssd",
                             side=_pair_side([g_w_in], chip_major=True))
    parts["w_in"] = _pair_add(g_w_in, got_in, ci, "grad_pair_add_w_in", chip_major=True)
    du1, (slots["w_in"],) = _matmul(d_ml, W_ml, "nt", F32, 1024, 1024, 3072, "du1_ml", addend=du1,
                                    side=_scatter_side([parts["w_in"]]))
    du1 = _matmul(d_small, W_small, "nt", BF16, 1024, 1024, 128, "du1_small", addend=du1)
    grad_x, _, d_mix_g = _rmsnorm_bwd(du1, xs, norm_mix_g, dh1, "norm_mix_bwd")

    fulls = _pair_share([_chip_sum(parts[n], slots[n], chip, ci, "grad_chip_sum_" + n) for n in _BIG])
    grads = {n: f.reshape(weights[n].shape[1:]) for n, f in zip(_BIG, fulls)}

    small = _pack(dict(norm_mix_g=d_mix_g, conv_w=_conv_from_groups(d_cw), conv_b=_conv_from_groups(d_cb),
                       dt_bias=d_dtb[:, 0:32], a_log=d_alog[:, 0:32], d_skip=d_dskip.reshape(SSD_HEADS, 64).sum(axis=1),
                       ssd_norm_g=d_sng, i_bias=d_ib[:, LANE_I:LANE_I + 8], f_bias=d_fb[:, LANE_F:LANE_F + 8],
                       mlstm_norm_g=d_mng, norm_ffn_g=d_ffn_g, final_norm_g=d_fng, loss=loss_row[:, 0:1]))
    total = _sum_slots(_allgather_small(small, "allgather_small_grads"), "small_grad_sum")
    small_w = {n: weights[n] for n in _WEIGHTS if n not in _BIG and n != "conv_w"}
    sd, sm, sv = _adamw(_pack(small_w), total, _pack({n: mom1[n] for n in small_w}), _pack({n: mom2[n] for n in small_w}),
                        PACK_ROWS, "adamw_small")
    tot, sd, sm, sv = _unpack(total), _unpack(sd), _unpack(sm), _unpack(sv)
    for n in small_w:
        grads[n] = tot[n].reshape(weights[n].shape)
    grads["conv_w"] = lax.dynamic_slice_in_dim(tot["conv_w"], chip * 768, 768, axis=1)
    loss = tot["loss"][0, 0]

    delta, new_m, new_v = {}, {}, {}
    for n in _BIG + ["conv_w"]:
        w2 = weights[n][0]
        d, nm, nv = _adamw(w2, grads[n], mom1[n][0], mom2[n][0], _row_tile(w2.shape[0]) if n != "conv_w" else SSD_CONV,
                           "adamw_" + n)
        delta[n], new_m[n], new_v[n] = d[None], nm[None], nv[None]
        grads[n] = grads[n][None]
    for n in small_w:
        delta[n], new_m[n], new_v[n] = (t[n].reshape(weights[n].shape) for t in (sd, sm, sv))
    return (loss, grad_x[None], *[grads[n] for n in _WEIGHTS], *[delta[n] for n in _WEIGHTS],
            *[new_m[n] for n in _WEIGHTS], *[new_v[n] for n in _WEIGHTS])
```
